```python
import jax, jax.numpy as jnp
from jax import lax
import numpy as np

D_MODEL = 1024
BATCH = 8
SEQ = 4096
DEPTH = 4

D_MIX = D_MODEL
HEAD_DIM = 64
N_ATT_HEADS = 8
D_ATT = N_ATT_HEADS * HEAD_DIM
N_GM_GROUPS = 8
GM_GROUP_DIM = 64
D_GM = N_GM_GROUPS * GM_GROUP_DIM
BLOCK = 128
CHUNK = 128
D_IN = 3 * D_ATT + N_ATT_HEADS + 2 * D_GM
D_FF = ((8 * D_MODEL // 3 + 255) // 256) * 256
PLE_DIM = 256
EPS = 1e-6
NEG_INF = -1e30

kernel_name = "hybrid_fox_gmlp_sandwich_ple"


def rmsnorm(x, gain=None):
    xf = x.astype(jnp.float32)
    y = xf * lax.rsqrt(jnp.mean(xf * xf, axis=-1, keepdims=True) + EPS)
    if gain is not None:
        y = y * gain.astype(jnp.float32)
    return y.astype(x.dtype)


def fox_attention(q, k, v, log_f):
    S = q.shape[1]
    dh = q.shape[-1]
    scale = dh ** -0.5
    c = jnp.cumsum(log_f.astype(jnp.float32), axis=1).transpose(0, 2, 1)
    q_idx = jnp.arange(BLOCK)
    outs = []
    for i in range(S // BLOCK):
        lo, hi = i * BLOCK, (i + 1) * BLOCK
        s = jnp.einsum('bqhd,bkhd->bhqk', q[:, lo:hi], k[:, :hi],
                       preferred_element_type=jnp.float32) * scale
        s = s + c[:, :, lo:hi, None] - c[:, :, None, :hi]
        causal = (lo + q_idx)[:, None] >= jnp.arange(hi)[None, :]
        s = jnp.where(causal, s, NEG_INF)
        w = jax.nn.softmax(s, axis=-1).astype(v.dtype)
        outs.append(jnp.einsum('bhqk,bkhd->bqhd', w, v[:, :hi]))
    return jnp.concatenate(outs, axis=1)


def chunked_spatial_gating(g, w_s, b_s, v_gain):
    B, S, _ = g.shape
    u, vv = jnp.split(jax.nn.gelu(g), 2, axis=-1)
    shp = (B, S // CHUNK, CHUNK, N_GM_GROUPS, GM_GROUP_DIM)
    vf = vv.reshape(shp).astype(jnp.float32)
    mu = jnp.mean(vf, axis=-1, keepdims=True)
    var = jnp.mean(jnp.square(vf - mu), axis=-1, keepdims=True)
    vn = ((vf - mu) * lax.rsqrt(var + EPS) * v_gain.reshape(N_GM_GROUPS, GM_GROUP_DIM).astype(jnp.float32)).astype(g.dtype)
    tril = jnp.tril(jnp.ones((CHUNK, CHUNK), dtype=w_s.dtype))
    w = w_s * tril[None]
    mixed = jnp.einsum('gts,bnsgd->bntgd', w, vn) + b_s.T[:, :, None]
    return (u.reshape(shp) * mixed).reshape(B, S, D_GM)


def _fwd_setup_inputs(seed: int = 0) -> dict:
    key = jax.random.key(seed)
    ks = jax.random.split(key, 20)
    L = DEPTH
    f32 = jnp.float32
    def nrm(k, shape, scale):
        return jax.random.normal(k, shape, f32) * scale
    def gain(k, shape):
        return 1.0 + 0.05 * jax.random.normal(k, shape, f32)
    return {
        "x": jax.random.normal(ks[0], (BATCH, SEQ, D_MODEL), f32),
        "p": jax.random.normal(ks[1], (DEPTH, BATCH, SEQ, PLE_DIM), f32),
        "mix_pre_norm": gain(ks[2], (L, D_MODEL)),
        "mix_post_norm": gain(ks[3], (L, D_MODEL)),
        "w_in": nrm(ks[4], (L, D_MODEL, D_IN), D_MODEL ** -0.5),
        "b_forget": jax.random.uniform(ks[5], (L, N_ATT_HEADS), f32, 2.0, 6.0),
        "gm_v_norm": gain(ks[6], (L, D_GM)),
        "gm_w_s": nrm(ks[7], (L, N_GM_GROUPS, CHUNK, CHUNK), 0.5 * CHUNK ** -0.5),
        "gm_b_s": 1.0 + 0.05 * jax.random.normal(ks[8], (L, N_GM_GROUPS, CHUNK), f32),
        "mix_out_norm": gain(ks[9], (L, D_MIX)),
        "w_out": nrm(ks[10], (L, D_MIX, D_MODEL), D_MIX ** -0.5),
        "ffn_pre_norm": gain(ks[11], (L, D_MODEL)),
        "ffn_post_norm": gain(ks[12], (L, D_MODEL)),
        "w_ffn_in": nrm(ks[13], (L, D_MODEL, 2 * D_FF), D_MODEL ** -0.5),
        "w_ffn_out": nrm(ks[14], (L, D_FF, D_MODEL), D_FF ** -0.5),
        "w_ple": nrm(ks[15], (L, PLE_DIM, D_MODEL), PLE_DIM ** -0.5),
        "ple_norm": gain(ks[16], (L, D_MODEL)),
        "w_ple_gate": nrm(ks[17], (L, D_MODEL, D_MODEL), D_MODEL ** -0.5),
    }


def _fwd_reference(x, p, mix_pre_norm, mix_post_norm, w_in, b_forget, gm_v_norm, gm_w_s,
              gm_b_s, mix_out_norm, w_out, ffn_pre_norm, ffn_post_norm, w_ffn_in,
              w_ffn_out, w_ple, ple_norm, w_ple_gate):
    B, S, _ = x.shape
    h = x
    for i in range(DEPTH):
        hn = rmsnorm(h, mix_pre_norm[i])
        z = hn @ w_in[i]
        q, k, v, f_logit, g = jnp.split(
            z, [D_ATT, 2 * D_ATT, 3 * D_ATT, 3 * D_ATT + N_ATT_HEADS], axis=-1)
        q = q.reshape(B, S, N_ATT_HEADS, HEAD_DIM)
        k = k.reshape(B, S, N_ATT_HEADS, HEAD_DIM)
        v = v.reshape(B, S, N_ATT_HEADS, HEAD_DIM)
        log_f = jax.nn.log_sigmoid(f_logit.astype(jnp.float32) + b_forget[i].astype(jnp.float32))
        att = fox_attention(q, k, v, log_f).reshape(B, S, D_ATT)
        gm = chunked_spatial_gating(g, gm_w_s[i], gm_b_s[i], gm_v_norm[i])
        g_att, g_gm = jnp.split(mix_out_norm[i], [D_ATT])
        mixed = jnp.concatenate([rmsnorm(att, g_att), rmsnorm(gm, g_gm)], axis=-1)
        h = h + rmsnorm(mixed @ w_out[i], mix_post_norm[i])
        hn = rmsnorm(h, ffn_pre_norm[i])
        a, b = jnp.split(hn @ w_ffn_in[i], 2, axis=-1)
        h = h + rmsnorm((jax.nn.silu(a) * b) @ w_ffn_out[i], ffn_post_norm[i])
        e = rmsnorm(p[i] @ w_ple[i], ple_norm[i])
        gate = jax.nn.sigmoid(rmsnorm(h) @ w_ple_gate[i])
        h = h + gate * e
    return h


import jax as _jax
import jax.numpy as _jnp

TWIN_FORMAT = 'train_step'
FWD_PARAMS = ['x', 'p', 'mix_pre_norm', 'mix_post_norm', 'w_in', 'b_forget', 'gm_v_norm', 'gm_w_s', 'gm_b_s', 'mix_out_norm', 'w_out', 'ffn_pre_norm', 'ffn_post_norm', 'w_ffn_in', 'w_ffn_out', 'w_ple', 'ple_norm', 'w_ple_gate']
TWIN_WEIGHTS = ['mix_pre_norm', 'mix_post_norm', 'w_in', 'b_forget', 'gm_v_norm', 'gm_w_s', 'gm_b_s', 'mix_out_norm', 'w_out', 'ffn_pre_norm', 'ffn_post_norm', 'w_ffn_in', 'w_ffn_out', 'w_ple', 'ple_norm', 'w_ple_gate']
TWIN_DIFF_INPUT = 'x'
TWIN_INPUTS = ['x', 'p', 'mix_pre_norm', 'mix_post_norm', 'w_in', 'b_forget', 'gm_v_norm', 'gm_w_s', 'gm_b_s', 'mix_out_norm', 'w_out', 'ffn_pre_norm', 'ffn_post_norm', 'w_ffn_in', 'w_ffn_out', 'w_ple', 'ple_norm', 'w_ple_gate', 'loss_target', 'm_mix_pre_norm', 'm_mix_post_norm', 'm_w_in', 'm_b_forget', 'm_gm_v_norm', 'm_gm_w_s', 'm_gm_b_s', 'm_mix_out_norm', 'm_w_out', 'm_ffn_pre_norm', 'm_ffn_post_norm', 'm_w_ffn_in', 'm_w_ffn_out', 'm_w_ple', 'm_ple_norm', 'm_w_ple_gate', 'v_mix_pre_norm', 'v_mix_post_norm', 'v_w_in', 'v_b_forget', 'v_gm_v_norm', 'v_gm_w_s', 'v_gm_b_s', 'v_mix_out_norm', 'v_w_out', 'v_ffn_pre_norm', 'v_ffn_post_norm', 'v_w_ffn_in', 'v_w_ffn_out', 'v_w_ple', 'v_ple_norm', 'v_w_ple_gate']
TWIN_OUTPUTS = ['loss', 'grad_x', 'grad_mix_pre_norm', 'grad_mix_post_norm', 'grad_w_in', 'grad_b_forget', 'grad_gm_v_norm', 'grad_gm_w_s', 'grad_gm_b_s', 'grad_mix_out_norm', 'grad_w_out', 'grad_ffn_pre_norm', 'grad_ffn_post_norm', 'grad_w_ffn_in', 'grad_w_ffn_out', 'grad_w_ple', 'grad_ple_norm', 'grad_w_ple_gate', 'delta_mix_pre_norm', 'delta_mix_post_norm', 'delta_w_in', 'delta_b_forget', 'delta_gm_v_norm', 'delta_gm_w_s', 'delta_gm_b_s', 'delta_mix_out_norm', 'delta_w_out', 'delta_ffn_pre_norm', 'delta_ffn_post_norm', 'delta_w_ffn_in', 'delta_w_ffn_out', 'delta_w_ple', 'delta_ple_norm', 'delta_w_ple_gate', 'new_m_mix_pre_norm', 'new_m_mix_post_norm', 'new_m_w_in', 'new_m_b_forget', 'new_m_gm_v_norm', 'new_m_gm_w_s', 'new_m_gm_b_s', 'new_m_mix_out_norm', 'new_m_w_out', 'new_m_ffn_pre_norm', 'new_m_ffn_post_norm', 'new_m_w_ffn_in', 'new_m_w_ffn_out', 'new_m_w_ple', 'new_m_ple_norm', 'new_m_w_ple_gate', 'new_v_mix_pre_norm', 'new_v_mix_post_norm', 'new_v_w_in', 'new_v_b_forget', 'new_v_gm_v_norm', 'new_v_gm_w_s', 'new_v_gm_b_s', 'new_v_mix_out_norm', 'new_v_w_out', 'new_v_ffn_pre_norm', 'new_v_ffn_post_norm', 'new_v_w_ffn_in', 'new_v_w_ffn_out', 'new_v_w_ple', 'new_v_ple_norm', 'new_v_w_ple_gate']
TWIN_LEAF_KINDS = {'loss': 'loss', 'grad_x': 'grad_x', 'grad_mix_pre_norm': 'grad_w', 'grad_mix_post_norm': 'grad_w', 'grad_w_in': 'grad_w', 'grad_b_forget': 'grad_w', 'grad_gm_v_norm': 'grad_w', 'grad_gm_w_s': 'grad_w', 'grad_gm_b_s': 'grad_w', 'grad_mix_out_norm': 'grad_w', 'grad_w_out': 'grad_w', 'grad_ffn_pre_norm': 'grad_w', 'grad_ffn_post_norm': 'grad_w', 'grad_w_ffn_in': 'grad_w', 'grad_w_ffn_out': 'grad_w', 'grad_w_ple': 'grad_w', 'grad_ple_norm': 'grad_w', 'grad_w_ple_gate': 'grad_w', 'delta_mix_pre_norm': 'delta_w', 'delta_mix_post_norm': 'delta_w', 'delta_w_in': 'delta_w', 'delta_b_forget': 'delta_w', 'delta_gm_v_norm': 'delta_w', 'delta_gm_w_s': 'delta_w', 'delta_gm_b_s': 'delta_w', 'delta_mix_out_norm': 'delta_w', 'delta_w_out': 'delta_w', 'delta_ffn_pre_norm': 'delta_w', 'delta_ffn_post_norm': 'delta_w', 'delta_w_ffn_in': 'delta_w', 'delta_w_ffn_out': 'delta_w', 'delta_w_ple': 'delta_w', 'delta_ple_norm': 'delta_w', 'delta_w_ple_gate': 'delta_w', 'new_m_mix_pre_norm': 'new_m', 'new_m_mix_post_norm': 'new_m', 'new_m_w_in': 'new_m', 'new_m_b_forget': 'new_m', 'new_m_gm_v_norm': 'new_m', 'new_m_gm_w_s': 'new_m', 'new_m_gm_b_s': 'new_m', 'new_m_mix_out_norm': 'new_m', 'new_m_w_out': 'new_m', 'new_m_ffn_pre_norm': 'new_m', 'new_m_ffn_post_norm': 'new_m', 'new_m_w_ffn_in': 'new_m', 'new_m_w_ffn_out': 'new_m', 'new_m_w_ple': 'new_m', 'new_m_ple_norm': 'new_m', 'new_m_w_ple_gate': 'new_m', 'new_v_mix_pre_norm': 'new_v', 'new_v_mix_post_norm': 'new_v', 'new_v_w_in': 'new_v', 'new_v_b_forget': 'new_v', 'new_v_gm_v_norm': 'new_v', 'new_v_gm_w_s': 'new_v', 'new_v_gm_b_s': 'new_v', 'new_v_mix_out_norm': 'new_v', 'new_v_w_out': 'new_v', 'new_v_ffn_pre_norm': 'new_v', 'new_v_ffn_post_norm': 'new_v', 'new_v_w_ffn_in': 'new_v', 'new_v_w_ffn_out': 'new_v', 'new_v_w_ple': 'new_v', 'new_v_ple_norm': 'new_v', 'new_v_w_ple_gate': 'new_v'}


def _forward(args):
    return _fwd_reference(*[args[k] for k in FWD_PARAMS])


def _output_shape():
    out = _jax.eval_shape(lambda: _forward(_fwd_setup_inputs(0)))
    return out.shape, out.dtype

N_MICROBATCH = 1
ADAM_LR = 0.001
ADAM_B1 = 0.9
ADAM_B2 = 0.999
ADAM_EPS = 1e-08
ADAM_WD = 0.01
ADAM_STEP = 10
PER_EXAMPLE_BATCH_AXIS = {'x': 0, 'p': 1, 'loss_target': 0}
SHARED_INPUTS = []
_WEIGHT_DTYPES = {'mix_pre_norm': _jnp.float32, 'mix_post_norm': _jnp.float32, 'w_in': _jnp.float32, 'b_forget': _jnp.float32, 'gm_v_norm': _jnp.float32, 'gm_w_s': _jnp.float32, 'gm_b_s': _jnp.float32, 'mix_out_norm': _jnp.float32, 'w_out': _jnp.float32, 'ffn_pre_norm': _jnp.float32, 'ffn_post_norm': _jnp.float32, 'w_ffn_in': _jnp.float32, 'w_ffn_out': _jnp.float32, 'w_ple': _jnp.float32, 'ple_norm': _jnp.float32, 'w_ple_gate': _jnp.float32}
MOMENT_SCALE = {'mix_pre_norm': 1.058899e+01, 'mix_post_norm': 3.758811e+01, 'w_in': 6.572162e+00, 'b_forget': 1.465246e+01, 'gm_v_norm': 4.155010e-01, 'gm_w_s': 5.577752e-01, 'gm_b_s': 9.644265e-01, 'mix_out_norm': 1.816001e+01, 'w_out': 1.755958e+01, 'ffn_pre_norm': 4.783847e+00, 'ffn_post_norm': 3.226070e+01, 'w_ffn_in': 1.985494e+00, 'w_ffn_out': 4.051144e+00, 'w_ple': 5.274043e-01, 'ple_norm': 9.359905e+00, 'w_ple_gate': 6.338333e-01}


def _to_microbatches(a, axis):
    t = _jnp.moveaxis(a, axis, 0)
    t = t.reshape((N_MICROBATCH, t.shape[0] // N_MICROBATCH) + t.shape[1:])
    return _jnp.moveaxis(t, 1, axis + 1)


def setup_inputs(seed: int = 0) -> dict:
    inp = _fwd_setup_inputs(seed)
    key = _jax.random.fold_in(_jax.random.key(seed), 7919)
    shape, _ = _output_shape()
    out = dict(inp)
    out["loss_target"] = _jax.random.normal(_jax.random.fold_in(key, 0), shape, _jnp.float32)
    for i, name in enumerate(TWIN_WEIGHTS):
        w = inp[name].astype(_jnp.float32)
        if MOMENT_SCALE is None:
            s = _jnp.sqrt(_jnp.mean(_jnp.square(w)) + 1e-30)
        else:
            s = MOMENT_SCALE[name]
        km, kv = _jax.random.split(_jax.random.fold_in(key, i + 1))
        out[name] = w
        out["m_" + name] = s * _jax.random.normal(km, w.shape, _jnp.float32)
        out["v_" + name] = (s * s) * _jax.random.uniform(kv, w.shape, _jnp.float32, 0.5, 1.5)
    if N_MICROBATCH > 1:
        for name, axis in PER_EXAMPLE_BATCH_AXIS.items():
            out[name] = _to_microbatches(out[name], axis)
    return {'x': out['x'], 'p': out['p'], 'mix_pre_norm': out['mix_pre_norm'], 'mix_post_norm': out['mix_post_norm'], 'w_in': out['w_in'], 'b_forget': out['b_forget'], 'gm_v_norm': out['gm_v_norm'], 'gm_w_s': out['gm_w_s'], 'gm_b_s': out['gm_b_s'], 'mix_out_norm': out['mix_out_norm'], 'w_out': out['w_out'], 'ffn_pre_norm': out['ffn_pre_norm'], 'ffn_post_norm': out['ffn_post_norm'], 'w_ffn_in': out['w_ffn_in'], 'w_ffn_out': out['w_ffn_out'], 'w_ple': out['w_ple'], 'ple_norm': out['ple_norm'], 'w_ple_gate': out['w_ple_gate'], 'loss_target': out['loss_target'], 'm_mix_pre_norm': out['m_mix_pre_norm'], 'm_mix_post_norm': out['m_mix_post_norm'], 'm_w_in': out['m_w_in'], 'm_b_forget': out['m_b_forget'], 'm_gm_v_norm': out['m_gm_v_norm'], 'm_gm_w_s': out['m_gm_w_s'], 'm_gm_b_s': out['m_gm_b_s'], 'm_mix_out_norm': out['m_mix_out_norm'], 'm_w_out': out['m_w_out'], 'm_ffn_pre_norm': out['m_ffn_pre_norm'], 'm_ffn_post_norm': out['m_ffn_post_norm'], 'm_w_ffn_in': out['m_w_ffn_in'], 'm_w_ffn_out': out['m_w_ffn_out'], 'm_w_ple': out['m_w_ple'], 'm_ple_norm': out['m_ple_norm'], 'm_w_ple_gate': out['m_w_ple_gate'], 'v_mix_pre_norm': out['v_mix_pre_norm'], 'v_mix_post_norm': out['v_mix_post_norm'], 'v_w_in': out['v_w_in'], 'v_b_forget': out['v_b_forget'], 'v_gm_v_norm': out['v_gm_v_norm'], 'v_gm_w_s': out['v_gm_w_s'], 'v_gm_b_s': out['v_gm_b_s'], 'v_mix_out_norm': out['v_mix_out_norm'], 'v_w_out': out['v_w_out'], 'v_ffn_pre_norm': out['v_ffn_pre_norm'], 'v_ffn_post_norm': out['v_ffn_post_norm'], 'v_w_ffn_in': out['v_w_ffn_in'], 'v_w_ffn_out': out['v_w_ffn_out'], 'v_w_ple': out['v_w_ple'], 'v_ple_norm': out['v_ple_norm'], 'v_w_ple_gate': out['v_w_ple_gate']}


def _loss(weights, diff, rest, loss_target):
    with _jax.named_scope("forward"):
        args = {**rest, TWIN_DIFF_INPUT: diff, **{k: w.astype(_WEIGHT_DTYPES[k]) for k, w in weights.items()}}
        y = _forward(args)
    with _jax.named_scope("loss_head"):
        err = _jnp.square(y.astype(_jnp.float32) - loss_target)
        return 0.5 * _jnp.sum(_jnp.mean(err, axis=-1)) if err.ndim else 0.5 * err


def _adamw(w, g, m, v):
    m = ADAM_B1 * m + (1.0 - ADAM_B1) * g
    v = ADAM_B2 * v + (1.0 - ADAM_B2) * _jnp.square(g)
    m_hat = m / (1.0 - ADAM_B1 ** ADAM_STEP)
    v_hat = v / (1.0 - ADAM_B2 ** ADAM_STEP)
    delta = -ADAM_LR * (m_hat / (_jnp.sqrt(v_hat) + ADAM_EPS) + ADAM_WD * w)
    return delta, m, v


def reference(x, p, mix_pre_norm, mix_post_norm, w_in, b_forget, gm_v_norm, gm_w_s, gm_b_s, mix_out_norm, w_out, ffn_pre_norm, ffn_post_norm, w_ffn_in, w_ffn_out, w_ple, ple_norm, w_ple_gate, loss_target, m_mix_pre_norm, m_mix_post_norm, m_w_in, m_b_forget, m_gm_v_norm, m_gm_w_s, m_gm_b_s, m_mix_out_norm, m_w_out, m_ffn_pre_norm, m_ffn_post_norm, m_w_ffn_in, m_w_ffn_out, m_w_ple, m_ple_norm, m_w_ple_gate, v_mix_pre_norm, v_mix_post_norm, v_w_in, v_b_forget, v_gm_v_norm, v_gm_w_s, v_gm_b_s, v_mix_out_norm, v_w_out, v_ffn_pre_norm, v_ffn_post_norm, v_w_ffn_in, v_w_ffn_out, v_w_ple, v_ple_norm, v_w_ple_gate):
    given = dict(x=x, p=p, mix_pre_norm=mix_pre_norm, mix_post_norm=mix_post_norm, w_in=w_in, b_forget=b_forget, gm_v_norm=gm_v_norm, gm_w_s=gm_w_s, gm_b_s=gm_b_s, mix_out_norm=mix_out_norm, w_out=w_out, ffn_pre_norm=ffn_pre_norm, ffn_post_norm=ffn_post_norm, w_ffn_in=w_ffn_in, w_ffn_out=w_ffn_out, w_ple=w_ple, ple_norm=ple_norm, w_ple_gate=w_ple_gate, loss_target=loss_target, m_mix_pre_norm=m_mix_pre_norm, m_mix_post_norm=m_mix_post_norm, m_w_in=m_w_in, m_b_forget=m_b_forget, m_gm_v_norm=m_gm_v_norm, m_gm_w_s=m_gm_w_s, m_gm_b_s=m_gm_b_s, m_mix_out_norm=m_mix_out_norm, m_w_out=m_w_out, m_ffn_pre_norm=m_ffn_pre_norm, m_ffn_post_norm=m_ffn_post_norm, m_w_ffn_in=m_w_ffn_in, m_w_ffn_out=m_w_ffn_out, m_w_ple=m_w_ple, m_ple_norm=m_ple_norm, m_w_ple_gate=m_w_ple_gate, v_mix_pre_norm=v_mix_pre_norm, v_mix_post_norm=v_mix_post_norm, v_w_in=v_w_in, v_b_forget=v_b_forget, v_gm_v_norm=v_gm_v_norm, v_gm_w_s=v_gm_w_s, v_gm_b_s=v_gm_b_s, v_mix_out_norm=v_mix_out_norm, v_w_out=v_w_out, v_ffn_pre_norm=v_ffn_pre_norm, v_ffn_post_norm=v_ffn_post_norm, v_w_ffn_in=v_w_ffn_in, v_w_ffn_out=v_w_ffn_out, v_w_ple=v_w_ple, v_ple_norm=v_ple_norm, v_w_ple_gate=v_w_ple_gate)
    weights = {n: given[n] for n in TWIN_WEIGHTS}
    shared = {n: given[n] for n in SHARED_INPUTS}
    per_example = {n: given[n] for n in ['x', 'p']}
    grad_fn = _jax.value_and_grad(_loss, argnums=(0, 1))

    def one_microbatch(ex, loss_target):
        ex = dict(ex)
        diff = ex.pop(TWIN_DIFF_INPUT)
        return grad_fn(weights, diff, {**shared, **ex}, loss_target)

    if N_MICROBATCH == 1:
        loss, (grad_w, grad_x) = one_microbatch(per_example, given["loss_target"])
    else:
        def body(carry, xs):
            loss_sum, grad_sum = carry
            l_k, (gw_k, gx_k) = one_microbatch(xs[0], xs[1])
            with _jax.named_scope("update"):
                return (loss_sum + l_k, _jax.tree.map(_jnp.add, grad_sum, gw_k)), gx_k

        init = (_jnp.zeros((), _jnp.float32), _jax.tree.map(_jnp.zeros_like, weights))
        (loss, grad_w), grad_x = _jax.lax.scan(body, init, (per_example, given["loss_target"]))
    with _jax.named_scope("update"):
        delta_w, new_m, new_v = {}, {}, {}
        for n in TWIN_WEIGHTS:
            delta_w[n], new_m[n], new_v[n] = _adamw(weights[n], grad_w[n], given["m_" + n], given["v_" + n])
    return (loss, grad_x, *[grad_w[n] for n in TWIN_WEIGHTS], *[delta_w[n] for n in TWIN_WEIGHTS],
            *[new_m[n] for n in TWIN_WEIGHTS], *[new_v[n] for n in TWIN_WEIGHTS])
```

```python
import functools

import jax
import jax.numpy as jnp
from jax import lax
from jax.experimental import pallas as pl
from jax.experimental.pallas import tpu as pltpu

F32 = jnp.float32
BF16 = jnp.bfloat16
MESH = pl.DeviceIdType.MESH
HIGHEST = lax.Precision.HIGHEST

EPS = 1e-6
NEG_INF = -1e30
N_HEADS = 8
HEAD_DIM = 64
D_ATT = N_HEADS * HEAD_DIM
D_GM = 512
CHUNK = 128
LANES = 128
N_CHIPS = 4
ADAM_LR = 0.001
ADAM_B1 = 0.9
ADAM_B2 = 0.999
ADAM_EPS = 1e-08
ADAM_WD = 0.01
ADAM_STEP = 10
VMEM_LIMIT = 56 * 1024 * 1024


def _cp(sem=None):
    return pltpu.CompilerParams(dimension_semantics=sem, vmem_limit_bytes=VMEM_LIMIT)


def _full(shape):
    return pl.BlockSpec(shape, lambda *_: (0,) * len(shape))


def _rows(tm, width, col_block=0):
    return pl.BlockSpec((tm, width), lambda i: (i, col_block))


def _dot(a, b, dims, precision=None):
    return lax.dot_general(a, b, (dims, ((), ())), preferred_element_type=F32, precision=precision)


NN = ((1,), (0,))
NT = ((1,), (1,))
TN = ((0,), (0,))


def _pick(n, cap):
    best = None
    for t in range(LANES, min(n, cap) + 1, LANES):
        if n % t == 0:
            best = t
    assert best is not None, (n, cap)
    return best


def _mm(a, b, mode, out_dtype, name, tm=512, tn_cap=1024, add=None, chip_split=False):
    dims = {"nn": NN, "nt": NT, "tn": TN}[mode]
    if mode == "tn":
        k, m = a.shape
    else:
        m, k = a.shape
    n = b.shape[0] if mode == "nt" else b.shape[1]
    tm = min(tm, m)
    tn = n // N_CHIPS if chip_split else _pick(n, tn_cap)
    assert m % tm == 0 and n % tn == 0

    def body(*refs):
        a_ref, b_ref = refs[0], refs[1]
        o_ref = refs[-1]
        acc = _dot(a_ref[...].astype(BF16), b_ref[...].astype(BF16), dims)
        if add is not None:
            acc = acc + refs[2][...]
        o_ref[...] = acc.astype(out_dtype)

    a_spec = pl.BlockSpec((k, tm), lambda i, j: (0, i)) if mode == "tn" else pl.BlockSpec((tm, k), lambda i, j: (i, 0))
    b_spec = pl.BlockSpec((tn, k), lambda i, j: (j, 0)) if mode == "nt" else pl.BlockSpec((k, tn), lambda i, j: (0, j))
    in_specs = [a_spec, b_spec]
    args = [a, b]
    if add is not None:
        in_specs.append(pl.BlockSpec((tm, tn), lambda i, j: (i, j)))
        args.append(add)
    if chip_split:
        out_shape = jax.ShapeDtypeStruct((N_CHIPS, m, tn), out_dtype)
        out_spec = pl.BlockSpec((None, tm, tn), lambda i, j: (j, i, 0))
    else:
        out_shape = jax.ShapeDtypeStruct((m, n), out_dtype)
        out_spec = pl.BlockSpec((tm, tn), lambda i, j: (i, j))
    return pl.pallas_call(body, out_shape=out_shape, grid=(m // tm, n // tn), in_specs=in_specs, out_specs=out_spec,
                          compiler_params=_cp(("arbitrary", "arbitrary")), name=name)(*args)


def _rms_inv(x):
    return lax.rsqrt(jnp.mean(x * x, axis=-1, keepdims=True) + EPS)


def _rms_bwd(x, gain, dy):
    inv = _rms_inv(x)
    xhat = x * inv
    dxn = dy if gain is None else dy * gain
    dx = inv * (dxn - xhat * jnp.mean(dxn * xhat, axis=-1, keepdims=True))
    return dx, dy * xhat


def _acc_rows(ref, val):
    s = jnp.sum(val, axis=0, keepdims=True)

    @pl.when(pl.program_id(0) == 0)
    def _():
        ref[...] = s

    @pl.when(pl.program_id(0) > 0)
    def _():
        ref[...] += s


def _norm_cast(h, gain, name, tm=512):
    t, d = h.shape

    def body(h_ref, g_ref, o_ref):
        x = h_ref[...]
        o_ref[...] = (x * _rms_inv(x) * g_ref[...]).astype(BF16)

    return pl.pallas_call(body, out_shape=jax.ShapeDtypeStruct((t, d), BF16), grid=(t // tm,),
                          in_specs=[_rows(tm, d), _full((1, d))], out_specs=_rows(tm, d),
                          compiler_params=_cp(("arbitrary",)), name=name)(h, gain)


def _resid_norm(h, o, g_post, g_next, name, tm=512):
    t, d = h.shape
    has_gain = g_next is not None

    def body(*refs):
        h_ref, o_ref, gp_ref = refs[:3]
        h1_ref, hn_ref = refs[-2:]
        ov = o_ref[...]
        h1 = h_ref[...] + ov * _rms_inv(ov) * gp_ref[...]
        h1_ref[...] = h1
        hn = h1 * _rms_inv(h1)
        if has_gain:
            hn = hn * refs[3][...]
        hn_ref[...] = hn.astype(BF16)

    args = [h, o, g_post] + ([g_next] if has_gain else [])
    in_specs = [_rows(tm, d), _rows(tm, d), _full((1, d))] + ([_full((1, d))] if has_gain else [])
    return pl.pallas_call(body, out_shape=(jax.ShapeDtypeStruct((t, d), F32), jax.ShapeDtypeStruct((t, d), BF16)),
                          grid=(t // tm,), in_specs=in_specs, out_specs=(_rows(tm, d), _rows(tm, d)),
                          compiler_params=_cp(("arbitrary",)), name=name)(*args)


def _ple_fwd(h2, pe, gl, g_ple, g_next, name, tm=512):
    t, d = h2.shape

    def body(h_ref, pe_ref, gl_ref, gp_ref, gn_ref, h3_ref, hn_ref):
        pv = pe_ref[...]
        e = pv * _rms_inv(pv) * gp_ref[...]
        h3 = h_ref[...] + jax.nn.sigmoid(gl_ref[...]) * e
        h3_ref[...] = h3
        hn_ref[...] = (h3 * _rms_inv(h3) * gn_ref[...]).astype(BF16)

    return pl.pallas_call(body, out_shape=(jax.ShapeDtypeStruct((t, d), F32), jax.ShapeDtypeStruct((t, d), BF16)),
                          grid=(t // tm,), in_specs=[_rows(tm, d)] * 3 + [_full((1, d))] * 2,
                          out_specs=(_rows(tm, d), _rows(tm, d)), compiler_params=_cp(("arbitrary",)),
                          name=name)(h2, pe, gl, g_ple, g_next)


def _loss_head(y, target, name, tm=512):
    t, d = y.shape

    def body(y_ref, t_ref, dy_ref, loss_ref):
        diff = y_ref[...] - t_ref[...]
        dy_ref[...] = diff * (1.0 / d)
        part = 0.5 * jnp.sum(jnp.mean(diff * diff, axis=-1, keepdims=True), axis=0, keepdims=True)
        part = jnp.broadcast_to(part, (8, LANES))

        @pl.when(pl.program_id(0) == 0)
        def _():
            loss_ref[...] = part

        @pl.when(pl.program_id(0) > 0)
        def _():
            loss_ref[...] += part

    return pl.pallas_call(body, out_shape=(jax.ShapeDtypeStruct((t, d), F32), jax.ShapeDtypeStruct((8, LANES), F32)),
                          grid=(t // tm,), in_specs=[_rows(tm, d)] * 2, out_specs=(_rows(tm, d), _full((8, LANES))),
                          compiler_params=_cp(("arbitrary",)), name=name)(y, target)


def _ple_bwd(dh3, pe, gl, g_ple, name, tm=512):
    t, d = dh3.shape

    def body(dh_ref, pe_ref, gl_ref, gp_ref, dgl_ref, dpe_ref, dg_ref):
        dh = dh_ref[...]
        pv = pe_ref[...]
        gp = gp_ref[...]
        gate = jax.nn.sigmoid(gl_ref[...])
        e = pv * _rms_inv(pv) * gp
        dgl_ref[...] = (dh * e * gate * (1.0 - gate)).astype(BF16)
        dpe, dg_rows = _rms_bwd(pv, gp, dh * gate)
        dpe_ref[...] = dpe.astype(BF16)
        _acc_rows(dg_ref, dg_rows)

    return pl.pallas_call(body, out_shape=(jax.ShapeDtypeStruct((t, d), BF16), jax.ShapeDtypeStruct((t, d), BF16),
                                           jax.ShapeDtypeStruct((1, d), F32)),
                          grid=(t // tm,), in_specs=[_rows(tm, d)] * 3 + [_full((1, d))],
                          out_specs=(_rows(tm, d), _rows(tm, d), _full((1, d))),
                          compiler_params=_cp(("arbitrary",)), name=name)(dh3, pe, gl, g_ple)


def _join(d_res, x_a, gain_a, d_a, x_b, gain_b, name, tm=512):
    t, d = d_res.shape
    has_ga = gain_a is not None
    has_b = x_b is not None

    def body(*refs):
        it = iter(refs)
        dres_ref, xa_ref = next(it), next(it)
        ga_ref = next(it) if has_ga else None
        da_ref = next(it)
        xb_ref, gb_ref = (next(it), next(it)) if has_b else (None, None)
        dout_ref = next(it)
        dga_ref = next(it) if has_ga else None
        db_ref, dgb_ref = (next(it), next(it)) if has_b else (None, None)
        dx, dg_rows = _rms_bwd(xa_ref[...], ga_ref[...] if has_ga else None, da_ref[...])
        dout = dres_ref[...] + dx
        dout_ref[...] = dout
        if has_ga:
            _acc_rows(dga_ref, dg_rows)
        if has_b:
            db, dgb_rows = _rms_bwd(xb_ref[...], gb_ref[...], dout)
            db_ref[...] = db.astype(BF16)
            _acc_rows(dgb_ref, dgb_rows)

    args, in_specs = [d_res, x_a], [_rows(tm, d), _rows(tm, d)]
    if has_ga:
        args.append(gain_a)
        in_specs.append(_full((1, d)))
    args.append(d_a)
    in_specs.append(_rows(tm, d))
    if has_b:
        args += [x_b, gain_b]
        in_specs += [_rows(tm, d), _full((1, d))]
    out_shape, out_specs = [jax.ShapeDtypeStruct((t, d), F32)], [_rows(tm, d)]
    if has_ga:
        out_shape.append(jax.ShapeDtypeStruct((1, d), F32))
        out_specs.append(_full((1, d)))
    if has_b:
        out_shape += [jax.ShapeDtypeStruct((t, d), BF16), jax.ShapeDtypeStruct((1, d), F32)]
        out_specs += [_rows(tm, d), _full((1, d))]
    return pl.pallas_call(body, out_shape=tuple(out_shape), grid=(t // tm,), in_specs=in_specs,
                          out_specs=tuple(out_specs), compiler_params=_cp(("arbitrary",)), name=name)(*args)


def _swiglu_fwd(ab, name, tm=256):
    t, n2 = ab.shape
    n = n2 // 2

    def body(a_ref, b_ref, s_ref):
        a = a_ref[...]
        s_ref[...] = (a * jax.nn.sigmoid(a) * b_ref[...]).astype(BF16)

    return pl.pallas_call(body, out_shape=jax.ShapeDtypeStruct((t, n), BF16), grid=(t // tm,),
                          in_specs=[_rows(tm, n, 0), _rows(tm, n, 1)], out_specs=_rows(tm, n),
                          compiler_params=_cp(("arbitrary",)), name=name)(ab, ab)


def _swiglu_bwd(ab, ds, name, tm=256):
    t, n2 = ab.shape
    n = n2 // 2

    def body(a_ref, b_ref, ds_ref, o_ref):
        a = a_ref[...]
        dsv = ds_ref[...]
        sg = jax.nn.sigmoid(a)
        silu = a * sg
        o_ref[:, :n] = (dsv * b_ref[...] * (sg + silu * (1.0 - sg))).astype(BF16)
        o_ref[:, n:] = (dsv * silu).astype(BF16)

    return pl.pallas_call(body, out_shape=jax.ShapeDtypeStruct((t, n2), BF16), grid=(t // tm,),
                          in_specs=[_rows(tm, n, 0), _rows(tm, n, 1), _rows(tm, n)], out_specs=_rows(tm, n2),
                          compiler_params=_cp(("arbitrary",)), name=name)(ab, ab, ds)


def _mixnorm_fwd(att, gm, g_out, name, tm=512):
    t, w = att.shape

    def body(a_ref, m_ref, g_ref, o_ref):
        a, m, g = a_ref[...], m_ref[...], g_ref[...]
        o_ref[:, :w] = (a * _rms_inv(a) * g[:, :w]).astype(BF16)
        o_ref[:, w:] = (m * _rms_inv(m) * g[:, w:]).astype(BF16)

    return pl.pallas_call(body, out_shape=jax.ShapeDtypeStruct((t, 2 * w), BF16), grid=(t // tm,),
                          in_specs=[_rows(tm, w), _rows(tm, w), _full((1, 2 * w))], out_specs=_rows(tm, 2 * w),
                          compiler_params=_cp(("arbitrary",)), name=name)(att, gm, g_out)


def _mixnorm_bwd(att, gm, g_out, dmixed, name, tm=512):
    t, w = att.shape

    def body(a_ref, m_ref, g_ref, d_ref, da_ref, dm_ref, dg_ref):
        g, d = g_ref[...], d_ref[...]
        da, dga = _rms_bwd(a_ref[...], g[:, :w], d[:, :w])
        dm, dgm = _rms_bwd(m_ref[...], g[:, w:], d[:, w:])
        da_ref[...] = da
        dm_ref[...] = dm
        _acc_rows(dg_ref, jnp.concatenate([dga, dgm], axis=1))

    return pl.pallas_call(body, out_shape=(jax.ShapeDtypeStruct((t, w), F32), jax.ShapeDtypeStruct((t, w), F32),
                                           jax.ShapeDtypeStruct((1, 2 * w), F32)),
                          grid=(t // tm,), in_specs=[_rows(tm, w), _rows(tm, w), _full((1, 2 * w)), _rows(tm, 2 * w)],
                          out_specs=(_rows(tm, w), _rows(tm, w), _full((1, 2 * w))),
                          compiler_params=_cp(("arbitrary",)), name=name)(att, gm, g_out, dmixed)


SCAN_BLOCK = 256


def _gate_fwd(zgf, b_pad, name):
    t = zgf.shape[0]
    fcol = zgf.shape[1] // LANES - 1
    nb = t // SCAN_BLOCK

    def body(f_ref, b_ref, c_ref, ct_ref):
        r = lax.broadcasted_iota(jnp.int32, (SCAN_BLOCK, SCAN_BLOCK), 0)
        s = lax.broadcasted_iota(jnp.int32, (SCAN_BLOCK, SCAN_BLOCK), 1)
        tril = (r >= s).astype(F32)
        head = lax.broadcasted_iota(jnp.int32, (SCAN_BLOCK, LANES), 1) < N_HEADS
        carry = jnp.zeros((1, LANES), F32)
        for blk in range(nb):
            rows = pl.ds(blk * SCAN_BLOCK, SCAN_BLOCK)
            x = f_ref[rows, :] + b_ref[...]
            lf = jnp.minimum(x, 0.0) - jnp.log1p(jnp.exp(-jnp.abs(x)))
            lf = jnp.where(head, lf, 0.0)
            cs = _dot(tril, lf, NN, HIGHEST) + carry
            c_ref[rows, :] = cs
            ct_ref[:, rows] = cs.T
            carry = carry + jnp.sum(lf, axis=0, keepdims=True)

    return pl.pallas_call(body, out_shape=(jax.ShapeDtypeStruct((t, LANES), F32), jax.ShapeDtypeStruct((LANES, t), F32)),
                          grid=(1,), in_specs=[pl.BlockSpec((t, LANES), lambda i: (0, fcol)), _full((1, LANES))],
                          out_specs=(_full((t, LANES)), _full((LANES, t))), compiler_params=_cp(("arbitrary",)),
                          name=name)(zgf, b_pad)


def _gate_bwd(dct_pad, dcq, zgf, b_pad, name):
    t = zgf.shape[0]
    fcol = zgf.shape[1] // LANES - 1
    nb = t // SCAN_BLOCK

    def body(dct_ref, dcq_ref, f_ref, b_ref, dfl_ref, db_ref):
        r = lax.broadcasted_iota(jnp.int32, (SCAN_BLOCK, SCAN_BLOCK), 0)
        s = lax.broadcasted_iota(jnp.int32, (SCAN_BLOCK, SCAN_BLOCK), 1)
        triu = (s >= r).astype(F32)
        head = lax.broadcasted_iota(jnp.int32, (SCAN_BLOCK, LANES), 1) < N_HEADS
        carry = jnp.zeros((1, LANES), F32)
        db = jnp.zeros((1, LANES), F32)
        for blk in reversed(range(nb)):
            rows = pl.ds(blk * SCAN_BLOCK, SCAN_BLOCK)
            dc = dct_ref[:, rows].T + dcq_ref[rows, :]
            dlf = _dot(triu, dc, NN, HIGHEST) + carry
            x = f_ref[rows, :] + b_ref[...]
            dfl = jnp.where(head, dlf * jax.nn.sigmoid(-x), 0.0)
            dfl_ref[rows, :] = dfl.astype(BF16)
            db = db + jnp.sum(dfl, axis=0, keepdims=True)
            carry = carry + jnp.sum(dc, axis=0, keepdims=True)
        db_ref[...] = db

    return pl.pallas_call(body, out_shape=(jax.ShapeDtypeStruct((t, LANES), BF16), jax.ShapeDtypeStruct((1, LANES), F32)),
                          grid=(1,), in_specs=[_full((LANES, t)), _full((t, LANES)),
                                               pl.BlockSpec((t, LANES), lambda i: (0, fcol)), _full((1, LANES))],
                          out_specs=(_full((t, LANES)), _full((1, LANES))), compiler_params=_cp(("arbitrary",)),
                          name=name)(dct_pad, dcq, zgf, b_pad)


ATT_BLOCK = 512
PAIRS = N_HEADS // 2


def _pick_col(x, idx):
    lane = lax.broadcasted_iota(jnp.int32, x.shape, 1)
    return jnp.sum(jnp.where(lane == idx, x, 0.0), axis=1, keepdims=True)


def _pick_row(x, idx):
    sub = lax.broadcasted_iota(jnp.int32, x.shape, 0)
    return jnp.sum(jnp.where(sub == idx, x, 0.0), axis=0, keepdims=True)


def _attn_fwd(qkv, c, ct, name):
    t = qkv.shape[0]
    tb = min(ATT_BLOCK, t)
    nb = t // tb
    scale = HEAD_DIM ** -0.5

    def body(q_ref, k_ref, v_ref, cq_ref, ck_ref, o_ref, lse_ref, m_s, l_s, acc_s):
        i, j, kb = pl.program_id(0), pl.program_id(1), pl.program_id(2)

        @pl.when(kb == 0)
        def _():
            m_s[...] = jnp.full(m_s.shape, NEG_INF, F32)
            l_s[...] = jnp.zeros(l_s.shape, F32)
            acc_s[...] = jnp.zeros(acc_s.shape, F32)

        @pl.when((kb == 0) & (j == 0))
        def _():
            lse_ref[...] = jnp.zeros(lse_ref.shape, F32)

        @pl.when(kb <= i)
        def _():
            q2, k2, v2 = q_ref[...], k_ref[...], v_ref[...]
            first = lax.broadcasted_iota(jnp.int32, (tb, LANES), 1) < HEAD_DIM
            row = i * tb + lax.broadcasted_iota(jnp.int32, (tb, tb), 0)
            col = kb * tb + lax.broadcasted_iota(jnp.int32, (tb, tb), 1)
            causal = row >= col
            cq, ck = cq_ref[...], ck_ref[...]
            pv, alpha = [], []
            for hh in range(2):
                qx = jnp.where(first if hh == 0 else ~first, q2, jnp.zeros_like(q2))
                sc = _dot(qx, k2, NT) * scale + (_pick_col(cq, 2 * j + hh) - _pick_row(ck, 2 * j + hh))
                sc = jnp.where(causal, sc, NEG_INF)
                m_prev = m_s[hh]
                m_new = jnp.maximum(m_prev, jnp.max(sc, axis=1, keepdims=True))
                a = jnp.exp(m_prev - m_new)
                p = jnp.exp(sc - m_new)
                l_s[hh] = a * l_s[hh] + jnp.sum(p, axis=1, keepdims=True)
                m_s[hh] = m_new
                pv.append(_dot(p.astype(BF16), v2, NN))
                alpha.append(a)
            acc_s[...] = jnp.where(first, alpha[0], alpha[1]) * acc_s[...] + jnp.where(first, pv[0], pv[1])

        @pl.when(kb == i)
        def _():
            first = lax.broadcasted_iota(jnp.int32, (tb, LANES), 1) < HEAD_DIM
            o_ref[...] = acc_s[...] / jnp.where(first, l_s[0], l_s[1])
            lane8 = lax.broadcasted_iota(jnp.int32, (tb, N_HEADS), 1)
            lse_ref[...] += (jnp.where(lane8 == 2 * j, m_s[0] + jnp.log(l_s[0]), 0.0)
                             + jnp.where(lane8 == 2 * j + 1, m_s[1] + jnp.log(l_s[1]), 0.0))

    kv_idx = lambda i, j, kb: jnp.minimum(kb, i)
    return pl.pallas_call(
        body, out_shape=(jax.ShapeDtypeStruct((t, D_ATT), F32), jax.ShapeDtypeStruct((t, N_HEADS), F32)),
        grid=(nb, PAIRS, nb),
        in_specs=[pl.BlockSpec((tb, LANES), lambda i, j, kb: (i, j)),
                  pl.BlockSpec((tb, LANES), lambda i, j, kb: (kv_idx(i, j, kb), PAIRS + j)),
                  pl.BlockSpec((tb, LANES), lambda i, j, kb: (kv_idx(i, j, kb), 2 * PAIRS + j)),
                  pl.BlockSpec((tb, LANES), lambda i, j, kb: (i, 0)),
                  pl.BlockSpec((N_HEADS, tb), lambda i, j, kb: (0, kv_idx(i, j, kb)))],
        out_specs=(pl.BlockSpec((tb, LANES), lambda i, j, kb: (i, j)),
                   pl.BlockSpec((tb, N_HEADS), lambda i, j, kb: (i, 0))),
        scratch_shapes=[pltpu.VMEM((2, tb, 1), F32), pltpu.VMEM((2, tb, 1), F32), pltpu.VMEM((tb, LANES), F32)],
        compiler_params=_cp(("arbitrary",) * 3), name=name)(qkv, qkv, qkv, c, ct)


def _attn_bwd(qkv, c, ct, att, lse, datt, name):
    t = qkv.shape[0]
    tb = min(ATT_BLOCK, t)
    nb = t // tb
    scale = HEAD_DIM ** -0.5

    def body(q_ref, k_ref, v_ref, do_ref, o_ref, lse_ref, cq_ref, ck_ref, dq_ref, dk_ref, dv_ref, dck_ref, dcq_ref,
             dk_s, dv_s, dck_s):
        j, kb, qi = pl.program_id(0), pl.program_id(1), pl.program_id(2)

        @pl.when((j == 0) & (kb == 0) & (qi == 0))
        def _():
            dcq_ref[...] = jnp.zeros(dcq_ref.shape, F32)

        @pl.when(qi == kb)
        def _():
            dk_s[...] = jnp.zeros(dk_s.shape, F32)
            dv_s[...] = jnp.zeros(dv_s.shape, F32)
            dck_s[...] = jnp.zeros(dck_s.shape, F32)

        @pl.when(qi >= kb)
        def _():
            q2, k2, v2 = q_ref[...], k_ref[...], v_ref[...]
            do = do_ref[...]
            do_b = do.astype(BF16)
            prod = do * o_ref[...]
            first = lax.broadcasted_iota(jnp.int32, (tb, LANES), 1) < HEAD_DIM
            row = qi * tb + lax.broadcasted_iota(jnp.int32, (tb, tb), 0)
            col = kb * tb + lax.broadcasted_iota(jnp.int32, (tb, tb), 1)
            causal = row >= col
            cq, ck, lse8 = cq_ref[...], ck_ref[...], lse_ref[...]
            sub8 = lax.broadcasted_iota(jnp.int32, (N_HEADS, tb), 0)
            dvs, dks, dqs = [], [], []
            dck = jnp.zeros((N_HEADS, tb), F32)
            dcq = jnp.zeros((tb, LANES), F32)
            lane = lax.broadcasted_iota(jnp.int32, (tb, LANES), 1)
            for hh in range(2):
                sel = first if hh == 0 else ~first
                qx = jnp.where(sel, q2, jnp.zeros_like(q2))
                dox = jnp.where(sel, do_b, jnp.zeros_like(do_b))
                delta = jnp.sum(jnp.where(sel, prod, 0.0), axis=1, keepdims=True)
                sc = _dot(qx, k2, NT) * scale + (_pick_col(cq, 2 * j + hh) - _pick_row(ck, 2 * j + hh))
                sc = jnp.where(causal, sc, NEG_INF)
                p = jnp.exp(sc - _pick_col(lse8, 2 * j + hh))
                dp = _dot(dox, v2, NT)
                ds = p * (dp - delta)
                ds_b = ds.astype(BF16)
                dvs.append(_dot(p.astype(BF16), do_b, TN))
                dks.append(_dot(ds_b, q2, TN))
                dqs.append(_dot(ds_b, k2, NN))
                dck = dck + jnp.where(sub8 == hh, -jnp.sum(ds, axis=0, keepdims=True), 0.0)
                dcq = dcq + jnp.where(lane == 2 * j + hh, jnp.sum(ds, axis=1, keepdims=True), 0.0)
            dv_s[...] += jnp.where(first, dvs[0], dvs[1])
            dk_s[...] += jnp.where(first, dks[0], dks[1]) * scale
            dck_s[...] += dck
            dq_new = jnp.where(first, dqs[0], dqs[1]) * scale
            rows = pl.ds(pl.multiple_of(qi * tb, tb), tb)
            dcq_ref[rows, :] += dcq

            @pl.when(kb == 0)
            def _():
                dq_ref[rows, :] = dq_new

            @pl.when(kb > 0)
            def _():
                dq_ref[rows, :] += dq_new

        @pl.when(qi == nb - 1)
        def _():
            dk_ref[...] = dk_s[...].astype(BF16)
            dv_ref[...] = dv_s[...].astype(BF16)
            dck_ref[...] = dck_s[...]

    qe = lambda j, kb, qi: jnp.maximum(qi, kb)
    return pl.pallas_call(
        body,
        out_shape=(jax.ShapeDtypeStruct((t, D_ATT), F32), jax.ShapeDtypeStruct((t, D_ATT), BF16),
                   jax.ShapeDtypeStruct((t, D_ATT), BF16), jax.ShapeDtypeStruct((PAIRS, N_HEADS, t), F32),
                   jax.ShapeDtypeStruct((t, LANES), F32)),
        grid=(PAIRS, nb, nb),
        in_specs=[pl.BlockSpec((tb, LANES), lambda j, kb, qi: (qe(j, kb, qi), j)),
                  pl.BlockSpec((tb, LANES), lambda j, kb, qi: (kb, PAIRS + j)),
                  pl.BlockSpec((tb, LANES), lambda j, kb, qi: (kb, 2 * PAIRS + j)),
                  pl.BlockSpec((tb, LANES), lambda j, kb, qi: (qe(j, kb, qi), j)),
                  pl.BlockSpec((tb, LANES), lambda j, kb, qi: (qe(j, kb, qi), j)),
                  pl.BlockSpec((tb, N_HEADS), lambda j, kb, qi: (qe(j, kb, qi), 0)),
                  pl.BlockSpec((tb, LANES), lambda j, kb, qi: (qe(j, kb, qi), 0)),
                  pl.BlockSpec((N_HEADS, tb), lambda j, kb, qi: (0, kb))],
        out_specs=(pl.BlockSpec((t, LANES), lambda j, kb, qi: (0, j)),
                   pl.BlockSpec((tb, LANES), lambda j, kb, qi: (kb, j)),
                   pl.BlockSpec((tb, LANES), lambda j, kb, qi: (kb, j)),
                   pl.BlockSpec((None, N_HEADS, tb), lambda j, kb, qi: (j, 0, kb)),
                   pl.BlockSpec((t, LANES), lambda j, kb, qi: (0, 0))),
        scratch_shapes=[pltpu.VMEM((tb, LANES), F32), pltpu.VMEM((tb, LANES), F32), pltpu.VMEM((N_HEADS, tb), F32)],
        compiler_params=_cp(("arbitrary",) * 3), name=name)(qkv, qkv, qkv, datt, att, lse, c, ct)


GELU_K = 0.7978845608028654
GELU_A = 0.044715


def _gelu(x):
    th = jnp.tanh(GELU_K * (x + GELU_A * x * x * x))
    return 0.5 * x * (1.0 + th), th


def _group_mean_matrix():
    r = jnp.arange(D_GM)[:, None] // HEAD_DIM
    s = jnp.arange(D_GM)[None, :] // HEAD_DIM
    return jnp.where(r == s, 1.0 / HEAD_DIM, 0.0).astype(F32)


def _gm_forward_parts(g, w_ref, bias, gain, mean_mat):
    gel, _ = _gelu(g)
    u, vv = gel[:, :D_GM], gel[:, D_GM:]
    mu = _dot(vv, mean_mat, NN, HIGHEST)
    d = vv - mu
    rstd = lax.rsqrt(_dot(d * d, mean_mat, NN, HIGHEST) + EPS)
    xhat = d * rstd
    vn = (xhat * gain).astype(BF16)
    first = lax.broadcasted_iota(jnp.int32, (CHUNK, LANES), 1) < HEAD_DIM
    tri = lax.broadcasted_iota(jnp.int32, (CHUNK, CHUNK), 0) >= lax.broadcasted_iota(jnp.int32, (CHUNK, CHUNK), 1)
    parts = []
    for jp in range(D_GM // LANES):
        vp = vn[:, jp * LANES:(jp + 1) * LANES]
        wa = jnp.where(tri, w_ref[2 * jp], 0.0).astype(BF16)
        wb = jnp.where(tri, w_ref[2 * jp + 1], 0.0).astype(BF16)
        parts.append(jnp.where(first, _dot(wa, vp, NN), _dot(wb, vp, NN)))
    mixed = jnp.concatenate(parts, axis=1) + bias
    return u, xhat, rstd, vn, mixed


def _gmlp_fwd(zgf, w_s, bias_full, gain, mean_mat, name):
    t = zgf.shape[0]

    def body(g_ref, w_ref, b_ref, gain_ref, mm_ref, o_ref):
        u, _, _, _, mixed = _gm_forward_parts(g_ref[...], w_ref, b_ref[...], gain_ref[...], mm_ref[...])
        o_ref[...] = u * mixed

    return pl.pallas_call(body, out_shape=jax.ShapeDtypeStruct((t, D_GM), F32), grid=(t // CHUNK,),
                          in_specs=[_rows(CHUNK, 2 * D_GM), _full(w_s.shape), _full((CHUNK, D_GM)), _full((1, D_GM)),
                                    _full((D_GM, D_GM))],
                          out_specs=_rows(CHUNK, D_GM), compiler_params=_cp(("arbitrary",)),
                          name=name)(zgf, w_s, bias_full, gain, mean_mat)


def _gmlp_bwd(zgf, dgm, w_s, bias_full, gain, mean_mat, name):
    t = zgf.shape[0]

    def body(g_ref, d_ref, w_ref, b_ref, gain_ref, mm_ref, dg_ref, dw_ref, dmix_ref, dgain_ref):
        g, gain, mean_mat = g_ref[...], gain_ref[...], mm_ref[...]
        u, xhat, rstd, vn, mixed = _gm_forward_parts(g, w_ref, b_ref[...], gain, mean_mat)
        dgm_v = d_ref[...]
        du = dgm_v * mixed
        dmixed = dgm_v * u
        dm_b = dmixed.astype(BF16)
        first = lax.broadcasted_iota(jnp.int32, (CHUNK, LANES), 1) < HEAD_DIM
        tri = lax.broadcasted_iota(jnp.int32, (CHUNK, CHUNK), 0) >= lax.broadcasted_iota(jnp.int32, (CHUNK, CHUNK), 1)

        @pl.when(pl.program_id(0) == 0)
        def _():
            dw_ref[...] = jnp.zeros(dw_ref.shape, F32)
            dmix_ref[...] = jnp.zeros(dmix_ref.shape, F32)
            dgain_ref[...] = jnp.zeros(dgain_ref.shape, F32)

        dvn_parts = []
        for jp in range(D_GM // LANES):
            vp = vn[:, jp * LANES:(jp + 1) * LANES]
            dmp = dm_b[:, jp * LANES:(jp + 1) * LANES]
            halves = []
            for hh in range(2):
                sel = first if hh == 0 else ~first
                dw = jnp.where(tri, _dot(jnp.where(sel, dmp, jnp.zeros_like(dmp)), vp, NT), 0.0)
                grp = 2 * jp + hh
                dw_ref[grp] += dw
                wm = jnp.where(tri, w_ref[grp], 0.0).astype(BF16)
                halves.append(_dot(wm, dmp, TN))
            dvn_parts.append(jnp.where(first, halves[0], halves[1]))
        dvn = jnp.concatenate(dvn_parts, axis=1)
        dmix_ref[...] += dmixed
        dgain_ref[...] += jnp.sum(dvn * xhat, axis=0, keepdims=True)
        dxhat = dvn * gain
        m1 = _dot(dxhat, mean_mat, NN, HIGHEST)
        m2 = _dot(dxhat * xhat, mean_mat, NN, HIGHEST)
        dvv = rstd * (dxhat - m1 - xhat * m2)
        gel, th = _gelu(g)
        dgel = 0.5 * (1.0 + th) + 0.5 * g * (1.0 - th * th) * GELU_K * (1.0 + 3.0 * GELU_A * g * g)
        dg_ref[...] = (jnp.concatenate([du, dvv], axis=1) * dgel).astype(BF16)

    return pl.pallas_call(
        body, out_shape=(jax.ShapeDtypeStruct((t, 2 * D_GM), BF16), jax.ShapeDtypeStruct(w_s.shape, F32),
                         jax.ShapeDtypeStruct((CHUNK, D_GM), F32), jax.ShapeDtypeStruct((1, D_GM), F32)),
        grid=(t // CHUNK,),
        in_specs=[_rows(CHUNK, 2 * D_GM), _rows(CHUNK, D_GM), _full(w_s.shape), _full((CHUNK, D_GM)), _full((1, D_GM)),
                  _full((D_GM, D_GM))],
        out_specs=(_rows(CHUNK, 2 * D_GM), _full(w_s.shape), _full((CHUNK, D_GM)), _full((1, D_GM))),
        compiler_params=_cp(("arbitrary",)), name=name)(zgf, dgm, w_s, bias_full, gain, mean_mat)


def _row_tile(r, c, budget=1 << 19):
    best = None
    for tr in range(8, r + 1, 8):
        if r % tr == 0 and tr * c <= budget:
            best = tr
    return best if best is not None else r


def _adamw(w, g, m, v, name):
    nl, r, c = w.shape
    tr = _row_tile(r, c, 1 << 18)
    c1 = 1.0 - ADAM_B1 ** ADAM_STEP
    c2 = 1.0 - ADAM_B2 ** ADAM_STEP

    def body(w_ref, g_ref, m_ref, v_ref, d_ref, mo_ref, vo_ref):
        gv = g_ref[...]
        mn = ADAM_B1 * m_ref[...] + (1.0 - ADAM_B1) * gv
        vn = ADAM_B2 * v_ref[...] + (1.0 - ADAM_B2) * jnp.square(gv)
        mo_ref[...] = mn
        vo_ref[...] = vn
        d_ref[...] = -ADAM_LR * ((mn / c1) / (jnp.sqrt(vn / c2) + ADAM_EPS) + ADAM_WD * w_ref[...])

    spec = pl.BlockSpec((None, tr, c), lambda l, i: (l, i, 0))
    shp = jax.ShapeDtypeStruct(w.shape, F32)
    return pl.pallas_call(body, out_shape=(shp, shp, shp), grid=(nl, r // tr), in_specs=[spec] * 4,
                          out_specs=(spec, spec, spec), compiler_params=_cp(("arbitrary",) * 2), name=name)(w, g, m, v)


def _add_sibling(g, recv, c_idx, name):
    nj, _, h, c = g.shape
    tr = _row_tile(h, c)

    def body(c_ref, g_ref, r_ref, o_ref):
        o_ref[...] = g_ref[...] + r_ref[...]

    grid_spec = pltpu.PrefetchScalarGridSpec(
        num_scalar_prefetch=1, grid=(nj, h // tr),
        in_specs=[pl.BlockSpec((None, None, tr, c), lambda j, i, c_ref: (j, c_ref[0], i, 0)),
                  pl.BlockSpec((None, tr, c), lambda j, i, c_ref: (j, i, 0))],
        out_specs=pl.BlockSpec((None, tr, c), lambda j, i, c_ref: (j, i, 0)))
    return pl.pallas_call(body, out_shape=jax.ShapeDtypeStruct((nj, h, c), F32), grid_spec=grid_spec,
                          compiler_params=_cp(("arbitrary",) * 2), name=name)(c_idx, g, recv)


def _add_chips(parts, name):
    _, h, c = parts.shape
    tr = _row_tile(h, c)

    def body(p_ref, o_ref):
        o_ref[...] = ((p_ref[0] + p_ref[1]) + p_ref[2]) + p_ref[3]

    return pl.pallas_call(body, out_shape=jax.ShapeDtypeStruct((h, c), F32), grid=(h // tr,),
                          in_specs=[pl.BlockSpec((N_CHIPS, tr, c), lambda i: (0, i, 0))],
                          out_specs=pl.BlockSpec((tr, c), lambda i: (i, 0)),
                          compiler_params=_cp(("arbitrary",)), name=name)(parts)


HBM_SPEC = pl.BlockSpec(memory_space=pltpu.HBM)


def _place():
    x, y, c = lax.axis_index("x"), lax.axis_index("y"), lax.axis_index("c")
    chips = [(1 - x, y), (x, 1 - y), (1 - x, 1 - y)]
    return x, y, c, 2 * x + y, chips, [2 * px + py for px, py in chips]


def _remote(src, dst, send_sem, recv_sem, dev):
    return pltpu.make_async_remote_copy(src_ref=src, dst_ref=dst, send_sem=send_sem, recv_sem=recv_sem,
                                        device_id=dev, device_id_type=MESH)


def _comm_call(body, arrays, out_shapes, sems, name):
    n = len(arrays)
    return pl.pallas_call(
        body, out_shape=tuple(out_shapes), in_specs=[HBM_SPEC] * n, out_specs=tuple([HBM_SPEC] * len(out_shapes)),
        scratch_shapes=[pltpu.SemaphoreType.DMA(s) for s in sems],
        compiler_params=pltpu.CompilerParams(has_side_effects=True), name=name)(*arrays)


def _gather_weights(shards, name):
    n = len(shards)

    def body(*refs):
        ins, outs = refs[:n], refs[n:2 * n]
        send, recv, loc = refs[2 * n:]
        x, y, c, me, chips, cidx = _place()
        sib = (x, y, 1 - c)
        local = [pltpu.make_async_copy(ins[a], outs[a].at[me], loc.at[a]) for a in range(n)]
        for cp in local:
            cp.start()
        first = []
        for a in range(n):
            for k, (px, py) in enumerate(chips):
                cp = _remote(ins[a].at[c], outs[a].at[me, c], send.at[a, k], recv.at[a, k], (px, py, c))
                cp.start()
                first.append(cp)
        passed = []
        for a in range(n):
            for k in range(3):
                blk = outs[a].at[cidx[k], c]
                _remote(blk, blk, send.at[a, k], recv.at[a, k], sib).wait_recv()
                fw = _remote(blk, blk, send.at[a, 3 + k], recv.at[a, 3 + k], sib)
                fw.start()
                passed.append(fw)
        for a in range(n):
            for k in range(3):
                blk = outs[a].at[cidx[k], 1 - c]
                _remote(blk, blk, send.at[a, 3 + k], recv.at[a, 3 + k], sib).wait_recv()
        for cp in first + passed:
            cp.wait_send()
        for cp in local:
            cp.wait()

    out_shapes = [jax.ShapeDtypeStruct((N_CHIPS,) + s.shape, s.dtype) for s in shards]
    return _comm_call(body, shards, out_shapes, [(n, 6), (n, 6), (n,)], name)


def _send_sibling_halves(grads, name):
    n = len(grads)

    def body(*refs):
        ins, outs = refs[:n], refs[n:2 * n]
        send, recv = refs[2 * n:]
        x, y, c, _, _, _ = _place()
        sib = (x, y, 1 - c)
        cps = []
        for a in range(n):
            for j in range(N_CHIPS):
                cp = _remote(ins[a].at[j, 1 - c], outs[a].at[j], send.at[a, j], recv.at[a, j], sib)
                cp.start()
                cps.append(cp)
        for cp in cps:
            cp.wait_recv()
        for cp in cps:
            cp.wait_send()

    out_shapes = [jax.ShapeDtypeStruct((g.shape[0],) + g.shape[2:], g.dtype) for g in grads]
    return _comm_call(body, grads, out_shapes, [(n, N_CHIPS), (n, N_CHIPS)], name)


def _scatter_to_chips(sums, name):
    n = len(sums)

    def body(*refs):
        ins, outs = refs[:n], refs[n:2 * n]
        send, recv, loc = refs[2 * n:]
        x, y, c, me, chips, cidx = _place()
        local = [pltpu.make_async_copy(ins[a].at[me], outs[a].at[me], loc.at[a]) for a in range(n)]
        for cp in local:
            cp.start()
        cps = []
        for a in range(n):
            for k, (px, py) in enumerate(chips):
                cp = _remote(ins[a].at[cidx[k]], outs[a].at[me], send.at[a, k], recv.at[a, k], (px, py, c))
                cp.start()
                cps.append(cp)
        for a in range(n):
            for k in range(3):
                blk = outs[a].at[cidx[k]]
                _remote(blk, blk, send.at[a, k], recv.at[a, k], (x, y, 1 - c)).wait_recv()
        for cp in cps:
            cp.wait_send()
        for cp in local:
            cp.wait()

    out_shapes = [jax.ShapeDtypeStruct(s.shape, s.dtype) for s in sums]
    return _comm_call(body, sums, out_shapes, [(n, 3), (n, 3), (n,)], name)


def _exchange_halves(halves, name):
    n = len(halves)

    def body(*refs):
        ins, outs = refs[:n], refs[n:2 * n]
        send, recv, loc = refs[2 * n:]
        x, y, c, _, _, _ = _place()
        sib = (x, y, 1 - c)
        local = [pltpu.make_async_copy(ins[a], outs[a].at[c], loc.at[a]) for a in range(n)]
        for cp in local:
            cp.start()
        cps = []
        for a in range(n):
            cp = _remote(ins[a], outs[a].at[c], send.at[a], recv.at[a], sib)
            cp.start()
            cps.append(cp)
        for a in range(n):
            blk = outs[a].at[1 - c]
            _remote(blk, blk, send.at[a], recv.at[a], sib).wait_recv()
        for cp in cps:
            cp.wait_send()
        for cp in local:
            cp.wait()

    out_shapes = [jax.ShapeDtypeStruct((2,) + s.shape, s.dtype) for s in halves]
    return _comm_call(body, halves, out_shapes, [(n,), (n,), (n,)], name)


def _gather_chips(slices, name):
    n = len(slices)

    def body(*refs):
        ins, outs = refs[:n], refs[n:2 * n]
        send, recv, loc = refs[2 * n:]
        x, y, c, me, chips, cidx = _place()
        local = [pltpu.make_async_copy(ins[a], outs[a].at[me], loc.at[a]) for a in range(n)]
        for cp in local:
            cp.start()
        cps = []
        for a in range(n):
            for k, (px, py) in enumerate(chips):
                cp = _remote(ins[a], outs[a].at[me], send.at[a, k], recv.at[a, k], (px, py, c))
                cp.start()
                cps.append(cp)
        for a in range(n):
            for k in range(3):
                blk = outs[a].at[cidx[k]]
                _remote(blk, blk, send.at[a, k], recv.at[a, k], (x, y, 1 - c)).wait_recv()
        for cp in cps:
            cp.wait_send()
        for cp in local:
            cp.wait()

    out_shapes = [jax.ShapeDtypeStruct((N_CHIPS,) + s.shape, s.dtype) for s in slices]
    return _comm_call(body, slices, out_shapes, [(n, 3), (n, 3), (n,)], name)


def _reduce_scatter(grads, c_idx, tag):
    split = [g.reshape(g.shape[0], 2, g.shape[1] // 2, g.shape[2]) for g in grads]
    recv = _send_sibling_halves(split, "rs_sibling_" + tag)
    sums = [_add_sibling(g, r, c_idx, "rs_add_sibling_" + tag) for g, r in zip(split, recv)]
    parts = _scatter_to_chips(sums, "rs_chips_" + tag)
    halves = [_add_chips(p, "rs_add_chips_" + tag) for p in parts]
    both = _exchange_halves(halves, "rs_halves_" + tag)
    return [b.reshape(b.shape[0] * b.shape[1], b.shape[2]) for b in both]


SMALL_ORDER = ("gm_w_s", "mix_pre_norm", "mix_post_norm", "mix_out_norm", "ffn_pre_norm", "ffn_post_norm", "ple_norm",
               "gm_v_norm", "gm_b_s", "b_forget")
SMALL_ROWS_MULTIPLE = 64


def _pack_small(parts):
    flat = []
    for nme in SMALL_ORDER:
        v = parts[nme].reshape(-1)
        pad = (-v.shape[0]) % LANES
        flat.append(jnp.pad(v, (0, pad)) if pad else v)
    v = jnp.concatenate(flat)
    rows = v.shape[0] // LANES
    pad_rows = (-rows) % SMALL_ROWS_MULTIPLE
    return jnp.pad(v.reshape(rows, LANES), ((0, pad_rows), (0, 0)))


def _unpack_small(packed, shapes):
    flat = packed.reshape(-1)
    out, off = {}, 0
    for nme in SMALL_ORDER:
        size = 1
        for s in shapes[nme]:
            size *= s
        out[nme] = flat[off:off + size].reshape(shapes[nme])
        off += size + ((-size) % LANES)
    return out


def kernel(x, p, mix_pre_norm, mix_post_norm, w_in, b_forget, gm_v_norm, gm_w_s, gm_b_s, mix_out_norm, w_out, ffn_pre_norm, ffn_post_norm, w_ffn_in, w_ffn_out, w_ple, ple_norm, w_ple_gate, loss_target, m_mix_pre_norm, m_mix_post_norm, m_w_in, m_b_forget, m_gm_v_norm, m_gm_w_s, m_gm_b_s, m_mix_out_norm, m_w_out, m_ffn_pre_norm, m_ffn_post_norm, m_w_ffn_in, m_w_ffn_out, m_w_ple, m_ple_norm, m_w_ple_gate, v_mix_pre_norm, v_mix_post_norm, v_w_in, v_b_forget, v_gm_v_norm, v_gm_w_s, v_gm_b_s, v_mix_out_norm, v_w_out, v_ffn_pre_norm, v_ffn_post_norm, v_w_ffn_in, v_w_ffn_out, v_w_ple, v_ple_norm, v_w_ple_gate):
    weights = dict(mix_pre_norm=mix_pre_norm, mix_post_norm=mix_post_norm, w_in=w_in, b_forget=b_forget,
                   gm_v_norm=gm_v_norm, gm_w_s=gm_w_s, gm_b_s=gm_b_s, mix_out_norm=mix_out_norm, w_out=w_out,
                   ffn_pre_norm=ffn_pre_norm, ffn_post_norm=ffn_post_norm, w_ffn_in=w_ffn_in, w_ffn_out=w_ffn_out,
                   w_ple=w_ple, ple_norm=ple_norm, w_ple_gate=w_ple_gate)
    mom_m = dict(mix_pre_norm=m_mix_pre_norm, mix_post_norm=m_mix_post_norm, w_in=m_w_in, b_forget=m_b_forget,
                 gm_v_norm=m_gm_v_norm, gm_w_s=m_gm_w_s, gm_b_s=m_gm_b_s, mix_out_norm=m_mix_out_norm, w_out=m_w_out,
                 ffn_pre_norm=m_ffn_pre_norm, ffn_post_norm=m_ffn_post_norm, w_ffn_in=m_w_ffn_in,
                 w_ffn_out=m_w_ffn_out, w_ple=m_w_ple, ple_norm=m_ple_norm, w_ple_gate=m_w_ple_gate)
    mom_v = dict(mix_pre_norm=v_mix_pre_norm, mix_post_norm=v_mix_post_norm, w_in=v_w_in, b_forget=v_b_forget,
                 gm_v_norm=v_gm_v_norm, gm_w_s=v_gm_w_s, gm_b_s=v_gm_b_s, mix_out_norm=v_mix_out_norm, w_out=v_w_out,
                 ffn_pre_norm=v_ffn_pre_norm, ffn_post_norm=v_ffn_post_norm, w_ffn_in=v_w_ffn_in,
                 w_ffn_out=v_w_ffn_out, w_ple=v_w_ple, ple_norm=v_ple_norm, w_ple_gate=v_w_ple_gate)
    big = ("w_in", "w_out", "w_ffn_in", "w_ffn_out", "w_ple", "w_ple_gate")
    depth = w_in.shape[0]
    t, d = x.shape[1], x.shape[2]
    d_ff = w_ffn_out.shape[1] * N_CHIPS
    c_idx = lax.axis_index("c").astype(jnp.int32).reshape(1)
    h = x[0]
    target = loss_target[0]
    mean_mat = _group_mean_matrix()

    def row(a, i):
        return a[i].reshape(1, -1)

    saved = []
    hn = _norm_cast(h, row(mix_pre_norm, 0), "norm_first")
    for i in range(depth):
        shards = [weights[nme][i].astype(BF16) for nme in big]
        shards = [s.reshape(2, s.shape[0] // 2, s.shape[1]) for s in shards]
        gathered = _gather_weights(shards, "gather_weights")
        gathered = [g.reshape(N_CHIPS, g.shape[2] * 2, g.shape[3]) for g in gathered]
        by_cols = lambda g: g.transpose(1, 0, 2).reshape(g.shape[1], N_CHIPS * g.shape[2])
        by_rows = lambda g: g.reshape(N_CHIPS * g.shape[1], g.shape[2])
        w_in_f = by_cols(gathered[0])
        w_qkv = w_in_f[:, :3 * D_ATT]
        w_gf = jnp.concatenate([w_in_f[:, 3 * D_ATT + N_HEADS:], w_in_f[:, 3 * D_ATT:3 * D_ATT + N_HEADS],
                                jnp.zeros((d, LANES - N_HEADS), BF16)], axis=1)
        w_out_f = by_rows(gathered[1])
        w_fi_f = by_cols(gathered[2])
        w_fo_f = by_rows(gathered[3])
        w_ple_f = by_cols(gathered[4])
        w_pg_f = by_rows(gathered[5])
        b_pad = jnp.pad(b_forget[i], (0, LANES - N_HEADS)).reshape(1, LANES)
        bias_full = jnp.repeat(gm_b_s[i].T, HEAD_DIM, axis=1)
        gain_v = row(gm_v_norm, i)

        qkv = _mm(hn, w_qkv, "nn", BF16, "mm_qkv")
        zgf = _mm(hn, w_gf, "nn", F32, "mm_gf", tn_cap=384)
        c_col, c_row = _gate_fwd(zgf, b_pad, "gate_fwd")
        att, lse = _attn_fwd(qkv, c_col, c_row, "attn_fwd")
        gm = _gmlp_fwd(zgf, gm_w_s[i], bias_full, gain_v, mean_mat, "gmlp_fwd")
        mixed = _mixnorm_fwd(att, gm, row(mix_out_norm, i), "mixnorm_fwd")
        o = _mm(mixed, w_out_f, "nn", F32, "mm_out")
        h1, hn2 = _resid_norm(h, o, row(mix_post_norm, i), row(ffn_pre_norm, i), "resid_mix")
        ab = _mm(hn2, w_fi_f, "nn", F32, "mm_ffn_in", tn_cap=512)
        s = _swiglu_fwd(ab, "swiglu_fwd")
        o2 = _mm(s, w_fo_f, "nn", F32, "mm_ffn_out")
        h2, hr = _resid_norm(h1, o2, row(ffn_post_norm, i), None, "resid_ffn")
        pe = _mm(p[i, 0], w_ple_f, "nn", F32, "mm_ple")
        gl = _mm(hr, w_pg_f, "nn", F32, "mm_ple_gate")
        g_next = row(mix_pre_norm, i + 1) if i + 1 < depth else row(mix_pre_norm, 0)
        h3, hn_next = _ple_fwd(h2, pe, gl, row(ple_norm, i), g_next, "ple_fwd")
        saved.append(dict(h=h, hn=hn, qkv=qkv, zgf=zgf, c_col=c_col, c_row=c_row, att=att, lse=lse, gm=gm, mixed=mixed,
                          o=o, h1=h1, hn2=hn2, ab=ab, s=s, o2=o2, h2=h2, hr=hr, pe=pe, gl=gl, w_qkv=w_qkv, w_gf=w_gf,
                          w_out=w_out_f, w_fi=w_fi_f, w_fo=w_fo_f, w_pg=w_pg_f, b_pad=b_pad, bias_full=bias_full,
                          gain_v=gain_v))
        h, hn = h3, hn_next

    dh, loss_blk = _loss_head(h, target, "loss_head")
    loss = lax.psum(loss_blk[0, 0], ("x", "y", "c"))

    small = {nme: [None] * depth for nme in SMALL_ORDER}
    big_grads = {nme: [None] * depth for nme in big}
    for i in reversed(range(depth)):
        sv = saved[i]
        dgl, dpe, small["ple_norm"][i] = _ple_bwd(dh, sv["pe"], sv["gl"], row(ple_norm, i), "ple_bwd")
        dhr = _mm(dgl, sv["w_pg"], "nt", F32, "mm_d_hr")
        g_pg = _mm(sv["hr"], dgl, "tn", F32, "mm_dw_ple_gate").reshape(N_CHIPS, -1, d)
        g_ple = _mm(p[i, 0], dpe, "tn", F32, "mm_dw_ple", chip_split=True)
        dh2, do2, small["ffn_post_norm"][i] = _join(dh, sv["h2"], None, dhr, sv["o2"], row(ffn_post_norm, i), "join_ple")
        ds = _mm(do2, sv["w_fo"], "nt", F32, "mm_d_s", tn_cap=1408)
        g_fo = _mm(sv["s"], do2, "tn", F32, "mm_dw_ffn_out", tm=256).reshape(N_CHIPS, -1, d)
        dab = _swiglu_bwd(sv["ab"], ds, "swiglu_bwd")
        dhn2 = _mm(dab, sv["w_fi"], "nt", F32, "mm_d_hn2", tn_cap=512)
        g_fi = _mm(sv["hn2"], dab, "tn", F32, "mm_dw_ffn_in", tm=256, chip_split=True)
        dh1, small["ffn_pre_norm"][i], do, small["mix_post_norm"][i] = _join(
            dh2, sv["h1"], row(ffn_pre_norm, i), dhn2, sv["o"], row(mix_post_norm, i), "join_ffn")
        dmixed = _mm(do, sv["w_out"], "nt", F32, "mm_d_mixed")
        g_out = _mm(sv["mixed"], do, "tn", F32, "mm_dw_out").reshape(N_CHIPS, -1, d)
        datt, dgm, small["mix_out_norm"][i] = _mixnorm_bwd(sv["att"], sv["gm"], row(mix_out_norm, i), dmixed, "mixnorm_bwd")
        dg, small["gm_w_s"][i], dmix_sum, small["gm_v_norm"][i] = _gmlp_bwd(
            sv["zgf"], dgm, gm_w_s[i], sv["bias_full"], sv["gain_v"], mean_mat, "gmlp_bwd")
        small["gm_b_s"][i] = dmix_sum.reshape(CHUNK, N_HEADS, HEAD_DIM).sum(-1).T
        dq, dk, dv, dck, dcq = _attn_bwd(sv["qkv"], sv["c_col"], sv["c_row"], sv["att"], sv["lse"], datt, "attn_bwd")
        dct_pad = jnp.pad(dck[:, :2, :].reshape(N_HEADS, t), ((0, LANES - N_HEADS), (0, 0)))
        dfl, db = _gate_bwd(dct_pad, dcq, sv["zgf"], sv["b_pad"], "gate_bwd")
        small["b_forget"][i] = db[0, :N_HEADS]
        dqkv = jnp.concatenate([dq.astype(BF16), dk, dv], axis=1)
        dgf = jnp.concatenate([dg, dfl], axis=1)
        dhn = _mm(dqkv, sv["w_qkv"], "nt", F32, "mm_d_hn_qkv")
        dhn = _mm(dgf, sv["w_gf"], "nt", F32, "mm_d_hn_gf", add=dhn)
        g_qkv = _mm(sv["hn"], dqkv, "tn", F32, "mm_dw_qkv")
        g_gf = _mm(sv["hn"], dgf, "tn", F32, "mm_dw_gf", tn_cap=384)
        g_in = jnp.concatenate([g_qkv, g_gf[:, 2 * D_GM:2 * D_GM + N_HEADS], g_gf[:, :2 * D_GM]], axis=1)
        g_in = g_in.reshape(d, N_CHIPS, -1).transpose(1, 0, 2)
        dh, small["mix_pre_norm"][i] = _join(dh1, sv["h"], row(mix_pre_norm, i), dhn, None, None, "join_mix")
        reduced = _reduce_scatter([g_in, g_out, g_fi, g_fo, g_ple, g_pg], c_idx, "layer")
        for nme, g in zip(big, reduced):
            big_grads[nme][i] = g
    grad_x = dh.reshape(1, t, d)

    small_shapes = {nme: weights[nme].shape for nme in SMALL_ORDER}
    small_part = _pack_small({nme: jnp.stack([g.reshape(small_shapes[nme][1:]) for g in small[nme]])
                              for nme in SMALL_ORDER})
    rows_small = small_part.shape[0]
    small_slice = _reduce_scatter([small_part.reshape(N_CHIPS, rows_small // N_CHIPS, LANES)], c_idx, "small")[0]
    small_all = _gather_chips([small_slice], "gather_small")[0].reshape(1, rows_small, LANES)
    sd, sm, sv_ = _adamw(_pack_small({n_: weights[n_] for n_ in SMALL_ORDER})[None], small_all,
                         _pack_small({n_: mom_m[n_] for n_ in SMALL_ORDER})[None],
                         _pack_small({n_: mom_v[n_] for n_ in SMALL_ORDER})[None], "adamw_small")
    grads = _unpack_small(small_all[0], small_shapes)
    deltas = _unpack_small(sd[0], small_shapes)
    new_m = _unpack_small(sm[0], small_shapes)
    new_v = _unpack_small(sv_[0], small_shapes)

    for nme in big:
        g = jnp.stack(big_grads[nme]).reshape(weights[nme].shape)
        grads[nme] = g
        deltas[nme], new_m[nme], new_v[nme] = _adamw(weights[nme], g, mom_m[nme], mom_v[nme], "adamw_" + nme)

    order = ("mix_pre_norm", "mix_post_norm", "w_in", "b_forget", "gm_v_norm", "gm_w_s", "gm_b_s", "mix_out_norm",
             "w_out", "ffn_pre_norm", "ffn_post_norm", "w_ffn_in", "w_ffn_out", "w_ple", "ple_norm", "w_ple_gate")
    return (loss, grad_x, *[grads[n_] for n_ in order], *[deltas[n_] for n_ in order], *[new_m[n_] for n_ in order],
            *[new_v[n_] for n_ in order])
```

```python
import functools

import jax
import jax.numpy as jnp
from jax import lax
from jax.experimental import pallas as pl
from jax.experimental.pallas import tpu as pltpu

F32 = jnp.float32
BF16 = jnp.bfloat16
MESH = pl.DeviceIdType.MESH
HIGHEST = lax.Precision.HIGHEST

EPS = 1e-6
NEG_INF = -1e30
N_HEADS = 8
HEAD_DIM = 64
D_ATT = N_HEADS * HEAD_DIM
D_GM = 512
CHUNK = 128
LANES = 128
N_CHIPS = 4
ADAM_LR = 0.001
ADAM_B1 = 0.9
ADAM_B2 = 0.999
ADAM_EPS = 1e-08
ADAM_WD = 0.01
ADAM_STEP = 10
VMEM_LIMIT = 56 * 1024 * 1024


def _cp(sem=None):
    return pltpu.CompilerParams(dimension_semantics=sem, vmem_limit_bytes=VMEM_LIMIT)


def _full(shape):
    return pl.BlockSpec(shape, lambda *_: (0,) * len(shape))


def _rows(tm, width, col_block=0):
    return pl.BlockSpec((tm, width), lambda i: (i, col_block))


def _dot(a, b, dims, precision=None):
    return lax.dot_general(a, b, (dims, ((), ())), preferred_element_type=F32, precision=precision)


NN = ((1,), (0,))
NT = ((1,), (1,))
TN = ((0,), (0,))


def _pick(n, cap):
    best = None
    for t in range(LANES, min(n, cap) + 1, LANES):
        if n % t == 0:
            best = t
    assert best is not None, (n, cap)
    return best


def _mm(a, b, mode, out_dtype, name, tm=512, tn_cap=1024, add=None, chip_split=False):
    dims = {"nn": NN, "nt": NT, "tn": TN}[mode]
    if mode == "tn":
        k, m = a.shape
    else:
        m, k = a.shape
    n = b.shape[0] if mode == "nt" else b.shape[1]
    tm = min(tm, m)
    tn = n // N_CHIPS if chip_split else _pick(n, tn_cap)
    assert m % tm == 0 and n % tn == 0

    def body(*refs):
        a_ref, b_ref = refs[0], refs[1]
        o_ref = refs[-1]
        acc = _dot(a_ref[...].astype(BF16), b_ref[...].astype(BF16), dims)
        if add is not None:
            acc = acc + refs[2][...]
        o_ref[...] = acc.astype(out_dtype)

    a_spec = pl.BlockSpec((k, tm), lambda i, j: (0, i)) if mode == "tn" else pl.BlockSpec((tm, k), lambda i, j: (i, 0))
    b_spec = pl.BlockSpec((tn, k), lambda i, j: (j, 0)) if mode == "nt" else pl.BlockSpec((k, tn), lambda i, j: (0, j))
    in_specs = [a_spec, b_spec]
    args = [a, b]
    if add is not None:
        in_specs.append(pl.BlockSpec((tm, tn), lambda i, j: (i, j)))
        args.append(add)
    if chip_split:
        out_shape = jax.ShapeDtypeStruct((N_CHIPS, m, tn), out_dtype)
        out_spec = pl.BlockSpec((None, tm, tn), lambda i, j: (j, i, 0))
    else:
        out_shape = jax.ShapeDtypeStruct((m, n), out_dtype)
        out_spec = pl.BlockSpec((tm, tn), lambda i, j: (i, j))
    return pl.pallas_call(body, out_shape=out_shape, grid=(m // tm, n // tn), in_specs=in_specs, out_specs=out_spec,
                          compiler_params=_cp(("arbitrary", "arbitrary")), name=name)(*args)


def _rms_inv(x):
    return lax.rsqrt(jnp.mean(x * x, axis=-1, keepdims=True) + EPS)


def _rms_bwd(x, gain, dy):
    inv = _rms_inv(x)
    xhat = x * inv
    dxn = dy if gain is None else dy * gain
    dx = inv * (dxn - xhat * jnp.mean(dxn * xhat, axis=-1, keepdims=True))
    return dx, dy * xhat


def _acc_rows(ref, val):
    s = jnp.sum(val, axis=0, keepdims=True)

    @pl.when(pl.program_id(0) == 0)
    def _():
        ref[...] = s

    @pl.when(pl.program_id(0) > 0)
    def _():
        ref[...] += s


def _norm_cast(h, gain, name, tm=512):
    t, d = h.shape

    def body(h_ref, g_ref, o_ref):
        x = h_ref[...]
        o_ref[...] = (x * _rms_inv(x) * g_ref[...]).astype(BF16)

    return pl.pallas_call(body, out_shape=jax.ShapeDtypeStruct((t, d), BF16), grid=(t // tm,),
                          in_specs=[_rows(tm, d), _full((1, d))], out_specs=_rows(tm, d),
                          compiler_params=_cp(("arbitrary",)), name=name)(h, gain)


def _resid_norm(h, o, g_post, g_next, name, tm=512):
    t, d = h.shape
    has_gain = g_next is not None

    def body(*refs):
        h_ref, o_ref, gp_ref = refs[:3]
        h1_ref, hn_ref = refs[-2:]
        ov = o_ref[...]
        h1 = h_ref[...] + ov * _rms_inv(ov) * gp_ref[...]
        h1_ref[...] = h1
        hn = h1 * _rms_inv(h1)
        if has_gain:
            hn = hn * refs[3][...]
        hn_ref[...] = hn.astype(BF16)

    args = [h, o, g_post] + ([g_next] if has_gain else [])
    in_specs = [_rows(tm, d), _rows(tm, d), _full((1, d))] + ([_full((1, d))] if has_gain else [])
    return pl.pallas_call(body, out_shape=(jax.ShapeDtypeStruct((t, d), F32), jax.ShapeDtypeStruct((t, d), BF16)),
                          grid=(t // tm,), in_specs=in_specs, out_specs=(_rows(tm, d), _rows(tm, d)),
                          compiler_params=_cp(("arbitrary",)), name=name)(*args)


def _ple_fwd(h2, pe, gl, g_ple, g_next, name, tm=512):
    t, d = h2.shape

    def body(h_ref, pe_ref, gl_ref, gp_ref, gn_ref, h3_ref, hn_ref):
        pv = pe_ref[...]
        e = pv * _rms_inv(pv) * gp_ref[...]
        h3 = h_ref[...] + jax.nn.sigmoid(gl_ref[...]) * e
        h3_ref[...] = h3
        hn_ref[...] = (h3 * _rms_inv(h3) * gn_ref[...]).astype(BF16)

    return pl.pallas_call(body, out_shape=(jax.ShapeDtypeStruct((t, d), F32), jax.ShapeDtypeStruct((t, d), BF16)),
                          grid=(t // tm,), in_specs=[_rows(tm, d)] * 3 + [_full((1, d))] * 2,
                          out_specs=(_rows(tm, d), _rows(tm, d)), compiler_params=_cp(("arbitrary",)),
                          name=name)(h2, pe, gl, g_ple, g_next)


def _loss_head(y, target, name, tm=512):
    t, d = y.shape

    def body(y_ref, t_ref, dy_ref, loss_ref):
        diff = y_ref[...] - t_ref[...]
        dy_ref[...] = diff * (1.0 / d)
        part = 0.5 * jnp.sum(jnp.mean(diff * diff, axis=-1, keepdims=True), axis=0, keepdims=True)
        part = jnp.broadcast_to(part, (8, LANES))

        @pl.when(pl.program_id(0) == 0)
        def _():
            loss_ref[...] = part

        @pl.when(pl.program_id(0) > 0)
        def _():
            loss_ref[...] += part

    return pl.pallas_call(body, out_shape=(jax.ShapeDtypeStruct((t, d), F32), jax.ShapeDtypeStruct((8, LANES), F32)),
                          grid=(t // tm,), in_specs=[_rows(tm, d)] * 2, out_specs=(_rows(tm, d), _full((8, LANES))),
                          compiler_params=_cp(("arbitrary",)), name=name)(y, target)


def _ple_bwd(dh3, pe, gl, g_ple, name, tm=512):
    t, d = dh3.shape

    def body(dh_ref, pe_ref, gl_ref, gp_ref, dgl_ref, dpe_ref, dg_ref):
        dh = dh_ref[...]
        pv = pe_ref[...]
        gp = gp_ref[...]
        gate = jax.nn.sigmoid(gl_ref[...])
        e = pv * _rms_inv(pv) * gp
        dgl_ref[...] = (dh * e * gate * (1.0 - gate)).astype(BF16)
        dpe, dg_rows = _rms_bwd(pv, gp, dh * gate)
        dpe_ref[...] = dpe.astype(BF16)
        _acc_rows(dg_ref, dg_rows)

    return pl.pallas_call(body, out_shape=(jax.ShapeDtypeStruct((t, d), BF16), jax.ShapeDtypeStruct((t, d), BF16),
                                           jax.ShapeDtypeStruct((1, d), F32)),
                          grid=(t // tm,), in_specs=[_rows(tm, d)] * 3 + [_full((1, d))],
                          out_specs=(_rows(tm, d), _rows(tm, d), _full((1, d))),
                          compiler_params=_cp(("arbitrary",)), name=name)(dh3, pe, gl, g_ple)


def _join(d_res, x_a, gain_a, d_a, x_b, gain_b, name, tm=512):
    t, d = d_res.shape
    has_ga = gain_a is not None
    has_b = x_b is not None

    def body(*refs):
        it = iter(refs)
        dres_ref, xa_ref = next(it), next(it)
        ga_ref = next(it) if has_ga else None
        da_ref = next(it)
        xb_ref, gb_ref = (next(it), next(it)) if has_b else (None, None)
        dout_ref = next(it)
        dga_ref = next(it) if has_ga else None
        db_ref, dgb_ref = (next(it), next(it)) if has_b else (None, None)
        dx, dg_rows = _rms_bwd(xa_ref[...], ga_ref[...] if has_ga else None, da_ref[...])
        dout = dres_ref[...] + dx
        dout_ref[...] = dout
        if has_ga:
            _acc_rows(dga_ref, dg_rows)
        if has_b:
            db, dgb_rows = _rms_bwd(xb_ref[...], gb_ref[...], dout)
            db_ref[...] = db.astype(BF16)
            _acc_rows(dgb_ref, dgb_rows)

    args, in_specs = [d_res, x_a], [_rows(tm, d), _rows(tm, d)]
    if has_ga:
        args.append(gain_a)
        in_specs.append(_full((1, d)))
    args.append(d_a)
    in_specs.append(_rows(tm, d))
    if has_b:
        args += [x_b, gain_b]
        in_specs += [_rows(tm, d), _full((1, d))]
    out_shape, out_specs = [jax.ShapeDtypeStruct((t, d), F32)], [_rows(tm, d)]
    if has_ga:
        out_shape.append(jax.ShapeDtypeStruct((1, d), F32))
        out_specs.append(_full((1, d)))
    if has_b:
        out_shape += [jax.ShapeDtypeStruct((t, d), BF16), jax.ShapeDtypeStruct((1, d), F32)]
        out_specs += [_rows(tm, d), _full((1, d))]
    return pl.pallas_call(body, out_shape=tuple(out_shape), grid=(t // tm,), in_specs=in_specs,
                          out_specs=tuple(out_specs), compiler_params=_cp(("arbitrary",)), name=name)(*args)


def _swiglu_fwd(ab, name, tm=256):
    t, n2 = ab.shape
    n = n2 // 2

    def body(a_ref, b_ref, s_ref):
        a = a_ref[...]
        s_ref[...] = (a * jax.nn.sigmoid(a) * b_ref[...]).astype(BF16)

    return pl.pallas_call(body, out_shape=jax.ShapeDtypeStruct((t, n), BF16), grid=(t // tm,),
                          in_specs=[_rows(tm, n, 0), _rows(tm, n, 1)], out_specs=_rows(tm, n),
                          compiler_params=_cp(("arbitrary",)), name=name)(ab, ab)


def _swiglu_bwd(ab, ds, name, tm=256):
    t, n2 = ab.shape
    n = n2 // 2

    def body(a_ref, b_ref, ds_ref, o_ref):
        a = a_ref[...]
        dsv = ds_ref[...]
        sg = jax.nn.sigmoid(a)
        silu = a * sg
        o_ref[:, :n] = (dsv * b_ref[...] * (sg + silu * (1.0 - sg))).astype(BF16)
        o_ref[:, n:] = (dsv * silu).astype(BF16)

    return pl.pallas_call(body, out_shape=jax.ShapeDtypeStruct((t, n2), BF16), grid=(t // tm,),
                          in_specs=[_rows(tm, n, 0), _rows(tm, n, 1), _rows(tm, n)], out_specs=_rows(tm, n2),
                          compiler_params=_cp(("arbitrary",)), name=name)(ab, ab, ds)


def _mixnorm_fwd(att, gm, g_out, name, tm=512):
    t, w = att.shape

    def body(a_ref, m_ref, g_ref, o_ref):
        a, m, g = a_ref[...], m_ref[...], g_ref[...]
        o_ref[:, :w] = (a * _rms_inv(a) * g[:, :w]).astype(BF16)
        o_ref[:, w:] = (m * _rms_inv(m) * g[:, w:]).astype(BF16)

    return pl.pallas_call(body, out_shape=jax.ShapeDtypeStruct((t, 2 * w), BF16), grid=(t // tm,),
                          in_specs=[_rows(tm, w), _rows(tm, w), _full((1, 2 * w))], out_specs=_rows(tm, 2 * w),
                          compiler_params=_cp(("arbitrary",)), name=name)(att, gm, g_out)


def _mixnorm_bwd(att, gm, g_out, dmixed, name, tm=512):
    t, w = att.shape

    def body(a_ref, m_ref, g_ref, d_ref, da_ref, dm_ref, dg_ref):
        g, d = g_ref[...], d_ref[...]
        da, dga = _rms_bwd(a_ref[...], g[:, :w], d[:, :w])
        dm, dgm = _rms_bwd(m_ref[...], g[:, w:], d[:, w:])
        da_ref[...] = da
        dm_ref[...] = dm
        _acc_rows(dg_ref, jnp.concatenate([dga, dgm], axis=1))

    return pl.pallas_call(body, out_shape=(jax.ShapeDtypeStruct((t, w), F32), jax.ShapeDtypeStruct((t, w), F32),
                                           jax.ShapeDtypeStruct((1, 2 * w), F32)),
                          grid=(t // tm,), in_specs=[_rows(tm, w), _rows(tm, w), _full((1, 2 * w)), _rows(tm, 2 * w)],
                          out_specs=(_rows(tm, w), _rows(tm, w), _full((1, 2 * w))),
                          compiler_params=_cp(("arbitrary",)), name=name)(att, gm, g_out, dmixed)


SCAN_BLOCK = 256


def _gate_fwd(zgf, b_pad, name):
    t = zgf.shape[0]
    fcol = zgf.shape[1] // LANES - 1
    nb = t // SCAN_BLOCK

    def body(f_ref, b_ref, c_ref):
        r = lax.broadcasted_iota(jnp.int32, (SCAN_BLOCK, SCAN_BLOCK), 0)
        s = lax.broadcasted_iota(jnp.int32, (SCAN_BLOCK, SCAN_BLOCK), 1)
        tril = (r >= s).astype(F32)
        head = lax.broadcasted_iota(jnp.int32, (SCAN_BLOCK, LANES), 1) < N_HEADS
        carry = jnp.zeros((1, LANES), F32)
        for blk in range(nb):
            rows = pl.ds(blk * SCAN_BLOCK, SCAN_BLOCK)
            x = f_ref[rows, :] + b_ref[...]
            lf = jnp.minimum(x, 0.0) - jnp.log1p(jnp.exp(-jnp.abs(x)))
            lf = jnp.where(head, lf, 0.0)
            cs = _dot(tril, lf, NN, HIGHEST) + carry
            c_ref[rows, :] = cs
            carry = carry + jnp.sum(lf, axis=0, keepdims=True)

    return pl.pallas_call(body, out_shape=jax.ShapeDtypeStruct((t, LANES), F32),
                          grid=(1,), in_specs=[pl.BlockSpec((t, LANES), lambda i: (0, fcol)), _full((1, LANES))],
                          out_specs=_full((t, LANES)), compiler_params=_cp(("arbitrary",)),
                          name=name)(zgf, b_pad)


def _gate_bwd(dc, zgf, b_pad, name):
    t = zgf.shape[0]
    fcol = zgf.shape[1] // LANES - 1
    nb = t // SCAN_BLOCK

    def body(dc_ref, f_ref, b_ref, dfl_ref, db_ref):
        r = lax.broadcasted_iota(jnp.int32, (SCAN_BLOCK, SCAN_BLOCK), 0)
        s = lax.broadcasted_iota(jnp.int32, (SCAN_BLOCK, SCAN_BLOCK), 1)
        triu = (s >= r).astype(F32)
        head = lax.broadcasted_iota(jnp.int32, (SCAN_BLOCK, LANES), 1) < N_HEADS
        carry = jnp.zeros((1, LANES), F32)
        db = jnp.zeros((1, LANES), F32)
        for blk in reversed(range(nb)):
            rows = pl.ds(blk * SCAN_BLOCK, SCAN_BLOCK)
            dc = dc_ref[rows, :]
            dlf = _dot(triu, dc, NN, HIGHEST) + carry
            x = f_ref[rows, :] + b_ref[...]
            dfl = jnp.where(head, dlf * jax.nn.sigmoid(-x), 0.0)
            dfl_ref[rows, :] = dfl.astype(BF16)
            db = db + jnp.sum(dfl, axis=0, keepdims=True)
            carry = carry + jnp.sum(dc, axis=0, keepdims=True)
        db_ref[...] = db

    return pl.pallas_call(body, out_shape=(jax.ShapeDtypeStruct((t, LANES), BF16), jax.ShapeDtypeStruct((1, LANES), F32)),
                          grid=(1,), in_specs=[_full((t, LANES)), pl.BlockSpec((t, LANES), lambda i: (0, fcol)),
                                               _full((1, LANES))],
                          out_specs=(_full((t, LANES)), _full((1, LANES))), compiler_params=_cp(("arbitrary",)),
                          name=name)(dc, zgf, b_pad)


ATT_BLOCK = 512
PAIRS = N_HEADS // 2
CQ_LANE = HEAD_DIM
CK_LANE = HEAD_DIM + 3
LSE_LANE = HEAD_DIM + 6


def _pick_col(x, idx):
    lane = lax.broadcasted_iota(jnp.int32, x.shape, 1)
    return jnp.sum(jnp.where(lane == idx, x, 0.0), axis=1, keepdims=True)


def _split3(x):
    hi = x.astype(BF16)
    r1 = x - hi.astype(F32)
    mid = r1.astype(BF16)
    lo = (r1 - mid.astype(F32)).astype(BF16)
    return hi, mid, lo


def _lanes_put(base, lane, start, vals):
    out = base
    for n, v in enumerate(vals):
        out = jnp.where(lane == start + n, v, out)
    return out


def _to_first_half(x, hh):
    return x if hh == 0 else pltpu.roll(x, HEAD_DIM, 1)


def _attn_prep(qkv, c, name, tm=512):
    t = qkv.shape[0]
    tm = min(tm, t)

    def body(q_ref, k_ref, v_ref, c_ref, qa_ref, ka_ref, va_ref):
        j = pl.program_id(1)
        lane = lax.broadcasted_iota(jnp.int32, (tm, LANES), 1)
        first = lane < HEAD_DIM
        q2, k2, v2 = q_ref[...].astype(F32), k_ref[...].astype(F32), v_ref[...].astype(F32)
        cc = c_ref[...]
        one = jnp.ones((tm, 1), F32)
        for hh in range(2):
            chi, cmid, clo = [v.astype(F32) for v in _split3(_pick_col(cc, 2 * j + hh))]
            qh = jnp.where(first, _to_first_half(q2, hh) * (HEAD_DIM ** -0.5), 0.0)
            kh = jnp.where(first, _to_first_half(k2, hh), 0.0)
            vh = jnp.where(first, _to_first_half(v2, hh), 0.0)
            qa = _lanes_put(qh, lane, CQ_LANE, [chi, cmid, clo, one, one, one])
            ka = _lanes_put(kh, lane, CQ_LANE, [one, one, one, -chi, -cmid, -clo, one, one, one])
            va = _lanes_put(vh, lane, CQ_LANE, [one, one, one])
            cols = slice(hh * LANES, (hh + 1) * LANES)
            qa_ref[:, cols] = qa.astype(BF16)
            ka_ref[:, cols] = ka.astype(BF16)
            va_ref[:, cols] = va.astype(BF16)

    blk = lambda off: pl.BlockSpec((tm, LANES), lambda i, j: (i, off + j))
    out = pl.BlockSpec((tm, 2 * LANES), lambda i, j: (i, j))
    shp = jax.ShapeDtypeStruct((t, N_HEADS * LANES), BF16)
    return pl.pallas_call(body, out_shape=(shp, shp, shp), grid=(t // tm, PAIRS),
                          in_specs=[blk(0), blk(PAIRS), blk(2 * PAIRS), pl.BlockSpec((tm, LANES), lambda i, j: (i, 0))],
                          out_specs=(out, out, out), compiler_params=_cp(("arbitrary",) * 2), name=name)(qkv, qkv, qkv, c)


def _causal_block(tb):
    return lax.broadcasted_iota(jnp.int32, (tb, tb), 0) >= lax.broadcasted_iota(jnp.int32, (tb, tb), 1)


def _attn_fwd(qa, ka, va, name):
    t = qa.shape[0]
    tb = min(ATT_BLOCK, t)
    nb = t // tb

    def body(q_ref, k_ref, v_ref, o_ref, lse_ref, m_s, acc_s):
        i, h, kb = pl.program_id(0), pl.program_id(1), pl.program_id(2)

        @pl.when(kb == 0)
        def _():
            m_s[...] = jnp.full(m_s.shape, NEG_INF, F32)
            acc_s[...] = jnp.zeros(acc_s.shape, F32)

        @pl.when((kb == 0) & (h == 0))
        def _():
            lse_ref[...] = jnp.zeros(lse_ref.shape, F32)

        def step(diagonal):
            sc = _dot(q_ref[...], k_ref[...], NT)
            if diagonal:
                sc = jnp.where(_causal_block(tb), sc, NEG_INF)
            m_prev = m_s[...]
            m_new = jnp.maximum(m_prev, jnp.max(sc, axis=1, keepdims=True))
            p = jnp.exp(sc - m_new)
            acc_s[...] = jnp.exp(m_prev - m_new) * acc_s[...] + _dot(p.astype(BF16), v_ref[...], NN)
            m_s[...] = m_new

        @pl.when(kb < i)
        def _():
            step(False)

        @pl.when(kb == i)
        def _():
            step(True)
            acc = acc_s[...]
            l = _pick_col(acc, CQ_LANE)
            o = acc / l
            first = lax.broadcasted_iota(jnp.int32, (tb, LANES), 1) < HEAD_DIM

            @pl.when(h % 2 == 0)
            def _():
                o_ref[...] = o

            @pl.when(h % 2 == 1)
            def _():
                o_ref[...] = jnp.where(first, o_ref[...], pltpu.roll(o, HEAD_DIM, 1))

            lane8 = lax.broadcasted_iota(jnp.int32, (tb, N_HEADS), 1)
            lse_ref[...] += jnp.where(lane8 == h, m_s[...] + jnp.log(l), 0.0)

    kv = lambda i, h, kb: (jnp.minimum(kb, i), h)
    return pl.pallas_call(
        body, out_shape=(jax.ShapeDtypeStruct((t, D_ATT), F32), jax.ShapeDtypeStruct((t, N_HEADS), F32)),
        grid=(nb, N_HEADS, nb),
        in_specs=[pl.BlockSpec((tb, LANES), lambda i, h, kb: (i, h)), pl.BlockSpec((tb, LANES), kv),
                  pl.BlockSpec((tb, LANES), kv)],
        out_specs=(pl.BlockSpec((tb, LANES), lambda i, h, kb: (i, h // 2)),
                   pl.BlockSpec((tb, N_HEADS), lambda i, h, kb: (i, 0))),
        scratch_shapes=[pltpu.VMEM((tb, 1), F32), pltpu.VMEM((tb, LANES), F32)],
        compiler_params=_cp(("arbitrary",) * 3), name=name)(qa, ka, va)


def _attn_bwd_prep(qa, att, lse, datt, name, tm=512):
    t = qa.shape[0]
    tm = min(tm, t)

    def body(qa_ref, o_ref, lse_ref, do_ref, qb_ref, doa_ref):
        j = pl.program_id(1)
        lane = lax.broadcasted_iota(jnp.int32, (tm, LANES), 1)
        first = lane < HEAD_DIM
        do = do_ref[...]
        prod = do * o_ref[...]
        lse8 = lse_ref[...]
        for hh in range(2):
            cols = slice(hh * LANES, (hh + 1) * LANES)
            delta = jnp.sum(jnp.where(first if hh == 0 else ~first, prod, 0.0), axis=1, keepdims=True)
            doh = jnp.where(first, _to_first_half(do, hh), 0.0)
            doa_ref[:, cols] = _lanes_put(doh, lane, CQ_LANE, [v.astype(F32) for v in _split3(-delta)]).astype(BF16)
            nl = [v.astype(F32) for v in _split3(-_pick_col(lse8, 2 * j + hh))]
            qb_ref[:, cols] = _lanes_put(qa_ref[:, cols].astype(F32), lane, LSE_LANE, nl).astype(BF16)

    wide = pl.BlockSpec((tm, 2 * LANES), lambda i, j: (i, j))
    pair = pl.BlockSpec((tm, LANES), lambda i, j: (i, j))
    shp = jax.ShapeDtypeStruct((t, N_HEADS * LANES), BF16)
    return pl.pallas_call(body, out_shape=(shp, shp), grid=(t // tm, PAIRS),
                          in_specs=[wide, pair, pl.BlockSpec((tm, N_HEADS), lambda i, j: (i, 0)), pair],
                          out_specs=(wide, wide), compiler_params=_cp(("arbitrary",) * 2), name=name)(qa, att, lse, datt)


def _attn_bwd(qb, ka, va, doa, name):
    t = qb.shape[0]
    tb = min(ATT_BLOCK, t)
    nb = t // tb

    def body(q_ref, k_ref, v_ref, do_ref, dq_ref, dk_ref, dv_ref, dk_s, dv_s):
        kb, qi = pl.program_id(1), pl.program_id(2)

        @pl.when(qi == kb)
        def _():
            dk_s[...] = jnp.zeros(dk_s.shape, F32)
            dv_s[...] = jnp.zeros(dv_s.shape, F32)

        def step(diagonal):
            q, k, do = q_ref[...], k_ref[...], do_ref[...]
            sc = _dot(q, k, NT)
            if diagonal:
                sc = jnp.where(_causal_block(tb), sc, NEG_INF)
            p = jnp.exp(sc)
            ds = (p * _dot(do, v_ref[...], NT)).astype(BF16)
            dv_s[...] += _dot(p.astype(BF16), do, TN)
            dk_s[...] += _dot(ds, q, TN)
            dq_new = _dot(ds, k, NN)
            rows = pl.ds(pl.multiple_of(qi * tb, tb), tb)

            @pl.when(kb == 0)
            def _():
                dq_ref[rows, :] = dq_new

            @pl.when(kb > 0)
            def _():
                dq_ref[rows, :] += dq_new

        @pl.when(qi == kb)
        def _():
            step(True)

        @pl.when(qi > kb)
        def _():
            step(False)

        @pl.when(qi == nb - 1)
        def _():
            dk_ref[...] = dk_s[...]
            dv_ref[...] = dv_s[...].astype(BF16)

    qrow = lambda h, kb, qi: (jnp.maximum(qi, kb), h)
    krow = lambda h, kb, qi: (kb, h)
    wide = (t, N_HEADS * LANES)
    return pl.pallas_call(
        body, out_shape=(jax.ShapeDtypeStruct(wide, F32), jax.ShapeDtypeStruct(wide, F32), jax.ShapeDtypeStruct(wide, BF16)),
        grid=(N_HEADS, nb, nb),
        in_specs=[pl.BlockSpec((tb, LANES), qrow), pl.BlockSpec((tb, LANES), krow), pl.BlockSpec((tb, LANES), krow),
                  pl.BlockSpec((tb, LANES), qrow)],
        out_specs=(pl.BlockSpec((t, LANES), lambda h, kb, qi: (0, h)), pl.BlockSpec((tb, LANES), krow),
                   pl.BlockSpec((tb, LANES), krow)),
        scratch_shapes=[pltpu.VMEM((tb, LANES), F32), pltpu.VMEM((tb, LANES), F32)],
        compiler_params=_cp(("arbitrary",) * 3), name=name)(qb, ka, va, doa)


def _attn_bwd_post(dqa, dka, dva, name, tm=256):
    t = dqa.shape[0]
    tm = min(tm, t)

    def body(dq_ref, dk_ref, dv_ref, o_ref, dc_ref):
        lane = lax.broadcasted_iota(jnp.int32, (tm, LANES), 1)
        first = lane < HEAD_DIM
        dc = jnp.zeros((tm, LANES), F32)
        for j in range(PAIRS):
            packed = []
            for ref, gain in ((dq_ref, HEAD_DIM ** -0.5), (dk_ref, 1.0), (dv_ref, 1.0)):
                even = ref[:, 2 * j * LANES:(2 * j + 1) * LANES].astype(F32)
                odd = ref[:, (2 * j + 1) * LANES:(2 * j + 2) * LANES].astype(F32)
                packed.append((jnp.where(first, even, pltpu.roll(odd, HEAD_DIM, 1)) * gain).astype(BF16))
                if ref is dq_ref:
                    dc = dc + jnp.where(lane == 2 * j, _pick_col(even, CQ_LANE), 0.0)
                    dc = dc + jnp.where(lane == 2 * j + 1, _pick_col(odd, CQ_LANE), 0.0)
                if ref is dk_ref:
                    dc = dc - jnp.where(lane == 2 * j, _pick_col(even, CK_LANE), 0.0)
                    dc = dc - jnp.where(lane == 2 * j + 1, _pick_col(odd, CK_LANE), 0.0)
            for part, val in enumerate(packed):
                o_ref[:, (part * PAIRS + j) * LANES:(part * PAIRS + j + 1) * LANES] = val
        dc_ref[...] = dc

    wide = _rows(tm, N_HEADS * LANES)
    return pl.pallas_call(body, out_shape=(jax.ShapeDtypeStruct((t, 3 * D_ATT), BF16), jax.ShapeDtypeStruct((t, LANES), F32)),
                          grid=(t // tm,), in_specs=[wide, wide, wide], out_specs=(_rows(tm, 3 * D_ATT), _rows(tm, LANES)),
                          compiler_params=_cp(("arbitrary",)), name=name)(dqa, dka, dva)


GELU_K = 0.7978845608028654
GELU_A = 0.044715


def _gelu(x):
    th = jnp.tanh(GELU_K * (x + GELU_A * x * x * x))
    return 0.5 * x * (1.0 + th), th


def _group_mean_matrix():
    r = jnp.arange(D_GM)[:, None] // HEAD_DIM
    s = jnp.arange(D_GM)[None, :] // HEAD_DIM
    return jnp.where(r == s, 1.0 / HEAD_DIM, 0.0).astype(F32)


def _gm_forward_parts(g, w_ref, bias, gain, mean_mat):
    gel, _ = _gelu(g)
    u, vv = gel[:, :D_GM], gel[:, D_GM:]
    mu = _dot(vv, mean_mat, NN, HIGHEST)
    d = vv - mu
    rstd = lax.rsqrt(_dot(d * d, mean_mat, NN, HIGHEST) + EPS)
    xhat = d * rstd
    vn = (xhat * gain).astype(BF16)
    first = lax.broadcasted_iota(jnp.int32, (CHUNK, LANES), 1) < HEAD_DIM
    tri = lax.broadcasted_iota(jnp.int32, (CHUNK, CHUNK), 0) >= lax.broadcasted_iota(jnp.int32, (CHUNK, CHUNK), 1)
    parts = []
    for jp in range(D_GM // LANES):
        vp = vn[:, jp * LANES:(jp + 1) * LANES]
        wa = jnp.where(tri, w_ref[2 * jp], 0.0).astype(BF16)
        wb = jnp.where(tri, w_ref[2 * jp + 1], 0.0).astype(BF16)
        parts.append(jnp.where(first, _dot(wa, vp, NN), _dot(wb, vp, NN)))
    mixed = jnp.concatenate(parts, axis=1) + bias
    return u, xhat, rstd, vn, mixed


def _gmlp_fwd(zgf, w_s, bias_full, gain, mean_mat, name):
    t = zgf.shape[0]

    def body(g_ref, w_ref, b_ref, gain_ref, mm_ref, o_ref):
        u, _, _, _, mixed = _gm_forward_parts(g_ref[...], w_ref, b_ref[...], gain_ref[...], mm_ref[...])
        o_ref[...] = u * mixed

    return pl.pallas_call(body, out_shape=jax.ShapeDtypeStruct((t, D_GM), F32), grid=(t // CHUNK,),
                          in_specs=[_rows(CHUNK, 2 * D_GM), _full(w_s.shape), _full((CHUNK, D_GM)), _full((1, D_GM)),
                                    _full((D_GM, D_GM))],
                          out_specs=_rows(CHUNK, D_GM), compiler_params=_cp(("arbitrary",)),
                          name=name)(zgf, w_s, bias_full, gain, mean_mat)


def _gmlp_bwd(zgf, dgm, w_s, bias_full, gain, mean_mat, name):
    t = zgf.shape[0]

    def body(g_ref, d_ref, w_ref, b_ref, gain_ref, mm_ref, dg_ref, dw_ref, dmix_ref, dgain_ref):
        g, gain, mean_mat = g_ref[...], gain_ref[...], mm_ref[...]
        u, xhat, rstd, vn, mixed = _gm_forward_parts(g, w_ref, b_ref[...], gain, mean_mat)
        dgm_v = d_ref[...]
        du = dgm_v * mixed
        dmixed = dgm_v * u
        dm_b = dmixed.astype(BF16)
        first = lax.broadcasted_iota(jnp.int32, (CHUNK, LANES), 1) < HEAD_DIM
        tri = lax.broadcasted_iota(jnp.int32, (CHUNK, CHUNK), 0) >= lax.broadcasted_iota(jnp.int32, (CHUNK, CHUNK), 1)

        @pl.when(pl.program_id(0) == 0)
        def _():
            dw_ref[...] = jnp.zeros(dw_ref.shape, F32)
            dmix_ref[...] = jnp.zeros(dmix_ref.shape, F32)
            dgain_ref[...] = jnp.zeros(dgain_ref.shape, F32)

        dvn_parts = []
        for jp in range(D_GM // LANES):
            vp = vn[:, jp * LANES:(jp + 1) * LANES]
            dmp = dm_b[:, jp * LANES:(jp + 1) * LANES]
            halves = []
            for hh in range(2):
                sel = first if hh == 0 else ~first
                dw = jnp.where(tri, _dot(jnp.where(sel, dmp, jnp.zeros_like(dmp)), vp, NT), 0.0)
                grp = 2 * jp + hh
                dw_ref[grp] += dw
                wm = jnp.where(tri, w_ref[grp], 0.0).astype(BF16)
                halves.append(_dot(wm, dmp, TN))
            dvn_parts.append(jnp.where(first, halves[0], halves[1]))
        dvn = jnp.concatenate(dvn_parts, axis=1)
        dmix_ref[...] += dmixed
        dgain_ref[...] += jnp.sum(dvn * xhat, axis=0, keepdims=True)
        dxhat = dvn * gain
        m1 = _dot(dxhat, mean_mat, NN, HIGHEST)
        m2 = _dot(dxhat * xhat, mean_mat, NN, HIGHEST)
        dvv = rstd * (dxhat - m1 - xhat * m2)
        gel, th = _gelu(g)
        dgel = 0.5 * (1.0 + th) + 0.5 * g * (1.0 - th * th) * GELU_K * (1.0 + 3.0 * GELU_A * g * g)
        dg_ref[...] = (jnp.concatenate([du, dvv], axis=1) * dgel).astype(BF16)

    return pl.pallas_call(
        body, out_shape=(jax.ShapeDtypeStruct((t, 2 * D_GM), BF16), jax.ShapeDtypeStruct(w_s.shape, F32),
                         jax.ShapeDtypeStruct((CHUNK, D_GM), F32), jax.ShapeDtypeStruct((1, D_GM), F32)),
        grid=(t // CHUNK,),
        in_specs=[_rows(CHUNK, 2 * D_GM), _rows(CHUNK, D_GM), _full(w_s.shape), _full((CHUNK, D_GM)), _full((1, D_GM)),
                  _full((D_GM, D_GM))],
        out_specs=(_rows(CHUNK, 2 * D_GM), _full(w_s.shape), _full((CHUNK, D_GM)), _full((1, D_GM))),
        compiler_params=_cp(("arbitrary",)), name=name)(zgf, dgm, w_s, bias_full, gain, mean_mat)


def _row_tile(r, c, budget=1 << 19):
    best = None
    for tr in range(8, r + 1, 8):
        if r % tr == 0 and tr * c <= budget:
            best = tr
    return best if best is not None else r


def _adamw(w, g, m, v, name):
    nl, r, c = w.shape
    tr = _row_tile(r, c, 1 << 18)
    c1 = 1.0 - ADAM_B1 ** ADAM_STEP
    c2 = 1.0 - ADAM_B2 ** ADAM_STEP

    def body(w_ref, g_ref, m_ref, v_ref, d_ref, mo_ref, vo_ref):
        gv = g_ref[...]
        mn = ADAM_B1 * m_ref[...] + (1.0 - ADAM_B1) * gv
        vn = ADAM_B2 * v_ref[...] + (1.0 - ADAM_B2) * jnp.square(gv)
        mo_ref[...] = mn
        vo_ref[...] = vn
        d_ref[...] = -ADAM_LR * ((mn / c1) / (jnp.sqrt(vn / c2) + ADAM_EPS) + ADAM_WD * w_ref[...])

    spec = pl.BlockSpec((None, tr, c), lambda l, i: (l, i, 0))
    shp = jax.ShapeDtypeStruct(w.shape, F32)
    return pl.pallas_call(body, out_shape=(shp, shp, shp), grid=(nl, r // tr), in_specs=[spec] * 4,
                          out_specs=(spec, spec, spec), compiler_params=_cp(("arbitrary",) * 2), name=name)(w, g, m, v)


def _add_sibling(g, recv, c_idx, wire_dtype, name):
    nj, _, h, c = g.shape
    tr = _row_tile(h, c)

    def body(c_ref, g_ref, r_ref, o_ref):
        o_ref[...] = (g_ref[...] + r_ref[...]).astype(wire_dtype)

    grid_spec = pltpu.PrefetchScalarGridSpec(
        num_scalar_prefetch=1, grid=(nj, h // tr),
        in_specs=[pl.BlockSpec((None, None, tr, c), lambda j, i, c_ref: (j, c_ref[0], i, 0)),
                  pl.BlockSpec((None, tr, c), lambda j, i, c_ref: (j, i, 0))],
        out_specs=pl.BlockSpec((None, tr, c), lambda j, i, c_ref: (j, i, 0)))
    return pl.pallas_call(body, out_shape=jax.ShapeDtypeStruct((nj, h, c), wire_dtype), grid_spec=grid_spec,
                          compiler_params=_cp(("arbitrary",) * 2), name=name)(c_idx, g, recv)


def _add_chips(own, parts, place, name):
    _, h, c = own.shape
    tr = _row_tile(h, c)

    def body(p_ref, o_ref, a_ref, b_ref, c_ref, out_ref):
        out_ref[...] = ((o_ref[...].astype(F32) + a_ref[...].astype(F32)) + b_ref[...].astype(F32)) + c_ref[...].astype(F32)

    def other(k):
        return pl.BlockSpec((None, tr, c), lambda i, p_ref: (jnp.bitwise_xor(p_ref[0], k), i, 0))

    grid_spec = pltpu.PrefetchScalarGridSpec(
        num_scalar_prefetch=1, grid=(h // tr,),
        in_specs=[pl.BlockSpec((None, tr, c), lambda i, p_ref: (p_ref[0], i, 0)), other(1), other(2), other(3)],
        out_specs=pl.BlockSpec((None, tr, c), lambda i, p_ref: (p_ref[1], i, 0)))
    return pl.pallas_call(body, out_shape=jax.ShapeDtypeStruct((2, h, c), F32), grid_spec=grid_spec,
                          compiler_params=_cp(("arbitrary",)), name=name)(place, own, parts, parts, parts)


HBM_SPEC = pl.BlockSpec(memory_space=pltpu.HBM)


def _place():
    x, y, c = lax.axis_index("x"), lax.axis_index("y"), lax.axis_index("c")
    chips = [(1 - x, y), (x, 1 - y), (1 - x, 1 - y)]
    return x, y, c, 2 * x + y, chips, [2 * px + py for px, py in chips]


def _remote(src, dst, send_sem, recv_sem, dev):
    return pltpu.make_async_remote_copy(src_ref=src, dst_ref=dst, send_sem=send_sem, recv_sem=recv_sem,
                                        device_id=dev, device_id_type=MESH)


def _comm_call(body, arrays, out_shapes, sems, name, aliases=None):
    n = len(arrays)
    return pl.pallas_call(
        body, out_shape=tuple(out_shapes), in_specs=[HBM_SPEC] * n, out_specs=tuple([HBM_SPEC] * len(out_shapes)),
        scratch_shapes=[pltpu.SemaphoreType.DMA(s) for s in sems], input_output_aliases=aliases or {},
        compiler_params=pltpu.CompilerParams(has_side_effects=True), name=name)(*arrays)


def _gather_weights(shards, name):
    n = len(shards)

    def body(*refs):
        ins, outs = refs[:n], refs[n:2 * n]
        send, recv = refs[2 * n:]
        x, y, c, me, chips, cidx = _place()
        sib = (x, y, 1 - c)
        first = []
        for a in range(n):
            for k, (px, py) in enumerate(chips):
                cp = _remote(ins[a].at[c], outs[a].at[me, c], send.at[a, k], recv.at[a, k], (px, py, c))
                cp.start()
                first.append(cp)
        for a in range(n):
            cp = _remote(ins[a], outs[a].at[me], send.at[a, 6], recv.at[a, 6], sib)
            cp.start()
            first.append(cp)
        passed = []
        for a in range(n):
            for k in range(3):
                blk = outs[a].at[cidx[k], c]
                _remote(blk, blk, send.at[a, k], recv.at[a, k], sib).wait_recv()
                fw = _remote(blk, blk, send.at[a, 3 + k], recv.at[a, 3 + k], sib)
                fw.start()
                passed.append(fw)
        for a in range(n):
            for k in range(3):
                blk = outs[a].at[cidx[k], 1 - c]
                _remote(blk, blk, send.at[a, 3 + k], recv.at[a, 3 + k], sib).wait_recv()
            blk = outs[a].at[me]
            _remote(blk, blk, send.at[a, 6], recv.at[a, 6], sib).wait_recv()
        for cp in first + passed:
            cp.wait_send()

    out_shapes = [jax.ShapeDtypeStruct((N_CHIPS,) + s.shape, s.dtype) for s in shards]
    return _comm_call(body, shards, out_shapes, [(n, 7), (n, 7)], name)


def _send_sibling_halves(grads, name):
    n = len(grads)

    def body(*refs):
        ins, outs = refs[:n], refs[n:2 * n]
        send, recv = refs[2 * n:]
        x, y, c, _, _, _ = _place()
        sib = (x, y, 1 - c)
        cps = []
        for a in range(n):
            for j in range(N_CHIPS):
                cp = _remote(ins[a].at[j, 1 - c], outs[a].at[j], send.at[a, j], recv.at[a, j], sib)
                cp.start()
                cps.append(cp)
        for cp in cps:
            cp.wait_recv()
        for cp in cps:
            cp.wait_send()

    out_shapes = [jax.ShapeDtypeStruct((g.shape[0],) + g.shape[2:], g.dtype) for g in grads]
    return _comm_call(body, grads, out_shapes, [(n, N_CHIPS), (n, N_CHIPS)], name)


def _scatter_to_chips(sums, name):
    n = len(sums)

    def body(*refs):
        ins, outs = refs[:n], refs[n:2 * n]
        send, recv = refs[2 * n:]
        x, y, c, me, chips, cidx = _place()
        cps = []
        for a in range(n):
            for k, (px, py) in enumerate(chips):
                cp = _remote(ins[a].at[cidx[k]], outs[a].at[me], send.at[a, k], recv.at[a, k], (px, py, c))
                cp.start()
                cps.append(cp)
        for a in range(n):
            for k in range(3):
                blk = outs[a].at[cidx[k]]
                _remote(blk, blk, send.at[a, k], recv.at[a, k], (x, y, 1 - c)).wait_recv()
        for cp in cps:
            cp.wait_send()

    out_shapes = [jax.ShapeDtypeStruct(s.shape, s.dtype) for s in sums]
    return _comm_call(body, sums, out_shapes, [(n, 3), (n, 3)], name)


def _exchange_halves(halves, name):
    n = len(halves)

    def body(*refs):
        ins, bufs = refs[:n], refs[n:2 * n]
        send, recv = refs[2 * n:]
        x, y, c, _, _, _ = _place()
        sib = (x, y, 1 - c)
        cps = []
        for a in range(n):
            cp = _remote(ins[a].at[c], bufs[a].at[c], send.at[a], recv.at[a], sib)
            cp.start()
            cps.append(cp)
        for a in range(n):
            blk = bufs[a].at[1 - c]
            _remote(blk, blk, send.at[a], recv.at[a], sib).wait_recv()
        for cp in cps:
            cp.wait_send()

    out_shapes = [jax.ShapeDtypeStruct(s.shape, s.dtype) for s in halves]
    return _comm_call(body, halves, out_shapes, [(n,), (n,)], name, aliases={a: a for a in range(n)})


def _gather_chips(slices, name):
    n = len(slices)

    def body(*refs):
        ins, outs = refs[:n], refs[n:2 * n]
        send, recv = refs[2 * n:]
        x, y, c, me, chips, cidx = _place()
        sib = (x, y, 1 - c)
        cps = []
        for a in range(n):
            for k, dev in enumerate([(px, py, c) for px, py in chips] + [sib]):
                cp = _remote(ins[a], outs[a].at[me], send.at[a, k], recv.at[a, k], dev)
                cp.start()
                cps.append(cp)
        for a in range(n):
            for k, slot in enumerate(cidx + [me]):
                blk = outs[a].at[slot]
                _remote(blk, blk, send.at[a, k], recv.at[a, k], sib).wait_recv()
        for cp in cps:
            cp.wait_send()

    out_shapes = [jax.ShapeDtypeStruct((N_CHIPS,) + s.shape, s.dtype) for s in slices]
    return _comm_call(body, slices, out_shapes, [(n, 4), (n, 4)], name)


def _reduce_scatter(grads, c_idx, place, wire_dtype, tag):
    split = [g.reshape(g.shape[0], 2, g.shape[1] // 2, g.shape[2]) for g in grads]
    recv = _send_sibling_halves(split, "rs_sibling_" + tag)
    sums = [_add_sibling(g, r, c_idx, wire_dtype, "rs_add_sibling_" + tag) for g, r in zip(split, recv)]
    parts = _scatter_to_chips(sums, "rs_chips_" + tag)
    halves = [_add_chips(s, p, place, "rs_add_chips_" + tag) for s, p in zip(sums, parts)]
    both = _exchange_halves(halves, "rs_halves_" + tag)
    return [b.reshape(b.shape[0] * b.shape[1], b.shape[2]) for b in both]


SMALL_ORDER = ("gm_w_s", "mix_pre_norm", "mix_post_norm", "mix_out_norm", "ffn_pre_norm", "ffn_post_norm", "ple_norm",
               "gm_v_norm", "gm_b_s", "b_forget")
SMALL_ROWS_MULTIPLE = 64


def _pack_small(parts):
    flat = []
    for nme in SMALL_ORDER:
        v = parts[nme].reshape(-1)
        pad = (-v.shape[0]) % LANES
        flat.append(jnp.pad(v, (0, pad)) if pad else v)
    v = jnp.concatenate(flat)
    rows = v.shape[0] // LANES
    pad_rows = (-rows) % SMALL_ROWS_MULTIPLE
    return jnp.pad(v.reshape(rows, LANES), ((0, pad_rows), (0, 0)))


def _unpack_small(packed, shapes):
    flat = packed.reshape(-1)
    out, off = {}, 0
    for nme in SMALL_ORDER:
        size = 1
        for s in shapes[nme]:
            size *= s
        out[nme] = flat[off:off + size].reshape(shapes[nme])
        off += size + ((-size) % LANES)
    return out


def kernel(x, p, mix_pre_norm, mix_post_norm, w_in, b_forget, gm_v_norm, gm_w_s, gm_b_s, mix_out_norm, w_out, ffn_pre_norm, ffn_post_norm, w_ffn_in, w_ffn_out, w_ple, ple_norm, w_ple_gate, loss_target, m_mix_pre_norm, m_mix_post_norm, m_w_in, m_b_forget, m_gm_v_norm, m_gm_w_s, m_gm_b_s, m_mix_out_norm, m_w_out, m_ffn_pre_norm, m_ffn_post_norm, m_w_ffn_in, m_w_ffn_out, m_w_ple, m_ple_norm, m_w_ple_gate, v_mix_pre_norm, v_mix_post_norm, v_w_in, v_b_forget, v_gm_v_norm, v_gm_w_s, v_gm_b_s, v_mix_out_norm, v_w_out, v_ffn_pre_norm, v_ffn_post_norm, v_w_ffn_in, v_w_ffn_out, v_w_ple, v_ple_norm, v_w_ple_gate):
    weights = dict(mix_pre_norm=mix_pre_norm, mix_post_norm=mix_post_norm, w_in=w_in, b_forget=b_forget,
                   gm_v_norm=gm_v_norm, gm_w_s=gm_w_s, gm_b_s=gm_b_s, mix_out_norm=mix_out_norm, w_out=w_out,
                   ffn_pre_norm=ffn_pre_norm, ffn_post_norm=ffn_post_norm, w_ffn_in=w_ffn_in, w_ffn_out=w_ffn_out,
                   w_ple=w_ple, ple_norm=ple_norm, w_ple_gate=w_ple_gate)
    mom_m = dict(mix_pre_norm=m_mix_pre_norm, mix_post_norm=m_mix_post_norm, w_in=m_w_in, b_forget=m_b_forget,
                 gm_v_norm=m_gm_v_norm, gm_w_s=m_gm_w_s, gm_b_s=m_gm_b_s, mix_out_norm=m_mix_out_norm, w_out=m_w_out,
                 ffn_pre_norm=m_ffn_pre_norm, ffn_post_norm=m_ffn_post_norm, w_ffn_in=m_w_ffn_in,
                 w_ffn_out=m_w_ffn_out, w_ple=m_w_ple, ple_norm=m_ple_norm, w_ple_gate=m_w_ple_gate)
    mom_v = dict(mix_pre_norm=v_mix_pre_norm, mix_post_norm=v_mix_post_norm, w_in=v_w_in, b_forget=v_b_forget,
                 gm_v_norm=v_gm_v_norm, gm_w_s=v_gm_w_s, gm_b_s=v_gm_b_s, mix_out_norm=v_mix_out_norm, w_out=v_w_out,
                 ffn_pre_norm=v_ffn_pre_norm, ffn_post_norm=v_ffn_post_norm, w_ffn_in=v_w_ffn_in,
                 w_ffn_out=v_w_ffn_out, w_ple=v_w_ple, ple_norm=v_ple_norm, w_ple_gate=v_w_ple_gate)
    big = ("w_in", "w_out", "w_ffn_in", "w_ffn_out", "w_ple", "w_ple_gate")
    depth = w_in.shape[0]
    t, d = x.shape[1], x.shape[2]
    d_ff = w_ffn_out.shape[1] * N_CHIPS
    c_idx = lax.axis_index("c").astype(jnp.int32).reshape(1)
    place = jnp.stack([2 * lax.axis_index("x") + lax.axis_index("y"), lax.axis_index("c")]).astype(jnp.int32)
    h = x[0]
    target = loss_target[0]
    mean_mat = _group_mean_matrix()

    def row(a, i):
        return a[i].reshape(1, -1)

    saved = []
    hn = _norm_cast(h, row(mix_pre_norm, 0), "norm_first")
    for i in range(depth):
        shards = [weights[nme][i].astype(BF16) for nme in big]
        shards = [s.reshape(2, s.shape[0] // 2, s.shape[1]) for s in shards]
        gathered = _gather_weights(shards, "gather_weights")
        gathered = [g.reshape(N_CHIPS, g.shape[2] * 2, g.shape[3]) for g in gathered]
        by_cols = lambda g: g.transpose(1, 0, 2).reshape(g.shape[1], N_CHIPS * g.shape[2])
        by_rows = lambda g: g.reshape(N_CHIPS * g.shape[1], g.shape[2])
        w_in_f = by_cols(gathered[0])
        w_qkv = w_in_f[:, :3 * D_ATT]
        w_gf = jnp.concatenate([w_in_f[:, 3 * D_ATT + N_HEADS:], w_in_f[:, 3 * D_ATT:3 * D_ATT + N_HEADS],
                                jnp.zeros((d, LANES - N_HEADS), BF16)], axis=1)
        w_out_f = by_rows(gathered[1])
        w_fi_f = by_cols(gathered[2])
        w_fo_f = by_rows(gathered[3])
        w_ple_f = by_cols(gathered[4])
        w_pg_f = by_rows(gathered[5])
        b_pad = jnp.pad(b_forget[i], (0, LANES - N_HEADS)).reshape(1, LANES)
        bias_full = jnp.repeat(gm_b_s[i].T, HEAD_DIM, axis=1)
        gain_v = row(gm_v_norm, i)

        qkv = _mm(hn, w_qkv, "nn", BF16, "mm_qkv")
        zgf = _mm(hn, w_gf, "nn", F32, "mm_gf", tn_cap=384)
        qa, ka, va = _attn_prep(qkv, _gate_fwd(zgf, b_pad, "gate_fwd"), "attn_prep")
        att, lse = _attn_fwd(qa, ka, va, "attn_fwd")
        gm = _gmlp_fwd(zgf, gm_w_s[i], bias_full, gain_v, mean_mat, "gmlp_fwd")
        mixed = _mixnorm_fwd(att, gm, row(mix_out_norm, i), "mixnorm_fwd")
        o = _mm(mixed, w_out_f, "nn", F32, "mm_out")
        h1, hn2 = _resid_norm(h, o, row(mix_post_norm, i), row(ffn_pre_norm, i), "resid_mix")
        ab = _mm(hn2, w_fi_f, "nn", F32, "mm_ffn_in", tn_cap=512)
        s = _swiglu_fwd(ab, "swiglu_fwd")
        o2 = _mm(s, w_fo_f, "nn", F32, "mm_ffn_out")
        h2, hr = _resid_norm(h1, o2, row(ffn_post_norm, i), None, "resid_ffn")
        pe = _mm(p[i, 0], w_ple_f, "nn", F32, "mm_ple")
        gl = _mm(hr, w_pg_f, "nn", F32, "mm_ple_gate")
        g_next = row(mix_pre_norm, i + 1) if i + 1 < depth else row(mix_pre_norm, 0)
        h3, hn_next = _ple_fwd(h2, pe, gl, row(ple_norm, i), g_next, "ple_fwd")
        saved.append(dict(h=h, hn=hn, qa=qa, ka=ka, va=va, zgf=zgf, att=att, lse=lse, gm=gm, mixed=mixed,
                          o=o, h1=h1, hn2=hn2, ab=ab, s=s, o2=o2, h2=h2, hr=hr, pe=pe, gl=gl, w_qkv=w_qkv, w_gf=w_gf,
                          w_out=w_out_f, w_fi=w_fi_f, w_fo=w_fo_f, w_pg=w_pg_f, b_pad=b_pad, bias_full=bias_full,
                          gain_v=gain_v))
        h, hn = h3, hn_next

    dh, loss_blk = _loss_head(h, target, "loss_head")
    loss = lax.psum(loss_blk[0, 0], ("x", "y", "c"))

    small = {nme: [None] * depth for nme in SMALL_ORDER}
    big_grads = {nme: [None] * depth for nme in big}
    for i in reversed(range(depth)):
        sv = saved[i]
        dgl, dpe, small["ple_norm"][i] = _ple_bwd(dh, sv["pe"], sv["gl"], row(ple_norm, i), "ple_bwd")
        dhr = _mm(dgl, sv["w_pg"], "nt", F32, "mm_d_hr")
        g_pg = _mm(sv["hr"], dgl, "tn", F32, "mm_dw_ple_gate").reshape(N_CHIPS, -1, d)
        g_ple = _mm(p[i, 0], dpe, "tn", F32, "mm_dw_ple", chip_split=True)
        dh2, do2, small["ffn_post_norm"][i] = _join(dh, sv["h2"], None, dhr, sv["o2"], row(ffn_post_norm, i), "join_ple")
        ds = _mm(do2, sv["w_fo"], "nt", F32, "mm_d_s", tn_cap=1408)
        g_fo = _mm(sv["s"], do2, "tn", F32, "mm_dw_ffn_out", tm=256).reshape(N_CHIPS, -1, d)
        dab = _swiglu_bwd(sv["ab"], ds, "swiglu_bwd")
        dhn2 = _mm(dab, sv["w_fi"], "nt", F32, "mm_d_hn2", tn_cap=512)
        g_fi = _mm(sv["hn2"], dab, "tn", F32, "mm_dw_ffn_in", tm=256, chip_split=True)
        dh1, small["ffn_pre_norm"][i], do, small["mix_post_norm"][i] = _join(
            dh2, sv["h1"], row(ffn_pre_norm, i), dhn2, sv["o"], row(mix_post_norm, i), "join_ffn")
        dmixed = _mm(do, sv["w_out"], "nt", F32, "mm_d_mixed")
        g_out = _mm(sv["mixed"], do, "tn", F32, "mm_dw_out").reshape(N_CHIPS, -1, d)
        datt, dgm, small["mix_out_norm"][i] = _mixnorm_bwd(sv["att"], sv["gm"], row(mix_out_norm, i), dmixed, "mixnorm_bwd")
        dg, small["gm_w_s"][i], dmix_sum, small["gm_v_norm"][i] = _gmlp_bwd(
            sv["zgf"], dgm, gm_w_s[i], sv["bias_full"], sv["gain_v"], mean_mat, "gmlp_bwd")
        small["gm_b_s"][i] = dmix_sum.reshape(CHUNK, N_HEADS, HEAD_DIM).sum(-1).T
        qb, doa = _attn_bwd_prep(sv["qa"], sv["att"], sv["lse"], datt, "attn_bwd_prep")
        dqa, dka, dva = _attn_bwd(qb, sv["ka"], sv["va"], doa, "attn_bwd")
        dqkv, dc = _attn_bwd_post(dqa, dka, dva, "attn_bwd_post")
        dfl, db = _gate_bwd(dc, sv["zgf"], sv["b_pad"], "gate_bwd")
        small["b_forget"][i] = db[0, :N_HEADS]
        dgf = jnp.concatenate([dg, dfl], axis=1)
        dhn = _mm(dqkv, sv["w_qkv"], "nt", F32, "mm_d_hn_qkv")
        dhn = _mm(dgf, sv["w_gf"], "nt", F32, "mm_d_hn_gf", add=dhn)
        g_qkv = _mm(sv["hn"], dqkv, "tn", F32, "mm_dw_qkv")
        g_gf = _mm(sv["hn"], dgf, "tn", F32, "mm_dw_gf", tn_cap=384)
        g_in = jnp.concatenate([g_qkv, g_gf[:, 2 * D_GM:2 * D_GM + N_HEADS], g_gf[:, :2 * D_GM]], axis=1)
        g_in = g_in.reshape(d, N_CHIPS, -1).transpose(1, 0, 2)
        dh, small["mix_pre_norm"][i] = _join(dh1, sv["h"], row(mix_pre_norm, i), dhn, None, None, "join_mix")
        reduced = _reduce_scatter([g_in, g_out, g_fi, g_fo, g_ple, g_pg], c_idx, place, BF16, "layer")
        for nme, g in zip(big, reduced):
            big_grads[nme][i] = g
    grad_x = dh.reshape(1, t, d)

    small_shapes = {nme: weights[nme].shape for nme in SMALL_ORDER}
    small_part = _pack_small({nme: jnp.stack([g.reshape(small_shapes[nme][1:]) for g in small[nme]])
                              for nme in SMALL_ORDER})
    rows_small = small_part.shape[0]
    small_slice = _reduce_scatter([small_part.reshape(N_CHIPS, rows_small // N_CHIPS, LANES)], c_idx, place, F32, "small")[0]
    small_all = _gather_chips([small_slice], "gather_small")[0].reshape(1, rows_small, LANES)
    sd, sm, sv_ = _adamw(_pack_small({n_: weights[n_] for n_ in SMALL_ORDER})[None], small_all,
                         _pack_small({n_: mom_m[n_] for n_ in SMALL_ORDER})[None],
                         _pack_small({n_: mom_v[n_] for n_ in SMALL_ORDER})[None], "adamw_small")
    grads = _unpack_small(small_all[0], small_shapes)
    deltas = _unpack_small(sd[0], small_shapes)
    new_m = _unpack_small(sm[0], small_shapes)
    new_v = _unpack_small(sv_[0], small_shapes)

    for nme in big:
        g = jnp.stack(big_grads[nme]).reshape(weights[nme].shape)
        grads[nme] = g
        deltas[nme], new_m[nme], new_v[nme] = _adamw(weights[nme], g, mom_m[nme], mom_v[nme], "adamw_" + nme)

    order = ("mix_pre_norm", "mix_post_norm", "w_in", "b_forget", "gm_v_norm", "gm_w_s", "gm_b_s", "mix_out_norm",
             "w_out", "ffn_pre_norm", "ffn_post_norm", "w_ffn_in", "w_ffn_out", "w_ple", "ple_norm", "w_ple_gate")
    return (loss, grad_x, *[grads[n_] for n_ in order], *[deltas[n_] for n_ in order], *[new_m[n_] for n_ in order],
            *[new_v[n_] for n_ in order])
```

```python
import functools

import jax
import jax.numpy as jnp
from jax import lax
from jax.experimental import pallas as pl
from jax.experimental.pallas import tpu as pltpu

F32 = jnp.float32
BF16 = jnp.bfloat16
MESH = pl.DeviceIdType.MESH
HIGHEST = lax.Precision.HIGHEST

EPS = 1e-6
NEG_INF = -1e30
N_HEADS = 8
HEAD_DIM = 64
D_ATT = N_HEADS * HEAD_DIM
D_GM = 512
CHUNK = 128
LANES = 128
N_CHIPS = 4
ADAM_LR = 0.001
ADAM_B1 = 0.9
ADAM_B2 = 0.999
ADAM_EPS = 1e-08
ADAM_WD = 0.01
ADAM_STEP = 10
VMEM_LIMIT = 56 * 1024 * 1024


def _cp(sem=None):
    return pltpu.CompilerParams(dimension_semantics=sem, vmem_limit_bytes=VMEM_LIMIT)


def _full(shape):
    return pl.BlockSpec(shape, lambda *_: (0,) * len(shape))


def _rows(tm, width, col_block=0):
    return pl.BlockSpec((tm, width), lambda i: (i, col_block))


def _dot(a, b, dims, precision=None):
    return lax.dot_general(a, b, (dims, ((), ())), preferred_element_type=F32, precision=precision)


NN = ((1,), (0,))
NT = ((1,), (1,))
TN = ((0,), (0,))


def _pick(n, cap):
    best = None
    for t in range(LANES, min(n, cap) + 1, LANES):
        if n % t == 0:
            best = t
    assert best is not None, (n, cap)
    return best


def _mm(a, b, mode, out_dtype, name, tm=512, tn_cap=1024, add=None, chip_split=False):
    dims = {"nn": NN, "nt": NT, "tn": TN}[mode]
    if mode == "tn":
        k, m = a.shape
    else:
        m, k = a.shape
    n = b.shape[0] if mode == "nt" else b.shape[1]
    tm = min(tm, m)
    tn = n // N_CHIPS if chip_split else _pick(n, tn_cap)
    assert m % tm == 0 and n % tn == 0

    def body(*refs):
        a_ref, b_ref = refs[0], refs[1]
        o_ref = refs[-1]
        acc = _dot(a_ref[...].astype(BF16), b_ref[...].astype(BF16), dims)
        if add is not None:
            acc = acc + refs[2][...]
        o_ref[...] = acc.astype(out_dtype)

    a_spec = pl.BlockSpec((k, tm), lambda i, j: (0, i)) if mode == "tn" else pl.BlockSpec((tm, k), lambda i, j: (i, 0))
    b_spec = pl.BlockSpec((tn, k), lambda i, j: (j, 0)) if mode == "nt" else pl.BlockSpec((k, tn), lambda i, j: (0, j))
    in_specs = [a_spec, b_spec]
    args = [a, b]
    if add is not None:
        in_specs.append(pl.BlockSpec((tm, tn), lambda i, j: (i, j)))
        args.append(add)
    if chip_split:
        out_shape = jax.ShapeDtypeStruct((N_CHIPS, m, tn), out_dtype)
        out_spec = pl.BlockSpec((None, tm, tn), lambda i, j: (j, i, 0))
    else:
        out_shape = jax.ShapeDtypeStruct((m, n), out_dtype)
        out_spec = pl.BlockSpec((tm, tn), lambda i, j: (i, j))
    return pl.pallas_call(body, out_shape=out_shape, grid=(m // tm, n // tn), in_specs=in_specs, out_specs=out_spec,
                          compiler_params=_cp(("arbitrary", "arbitrary")), name=name)(*args)


def _rms_inv(x):
    return lax.rsqrt(jnp.mean(x * x, axis=-1, keepdims=True) + EPS)


def _rms_bwd(x, gain, dy):
    inv = _rms_inv(x)
    xhat = x * inv
    dxn = dy if gain is None else dy * gain
    dx = inv * (dxn - xhat * jnp.mean(dxn * xhat, axis=-1, keepdims=True))
    return dx, dy * xhat


def _acc_rows(ref, val):
    s = jnp.sum(val, axis=0, keepdims=True)

    @pl.when(pl.program_id(0) == 0)
    def _():
        ref[...] = s

    @pl.when(pl.program_id(0) > 0)
    def _():
        ref[...] += s


def _norm_cast(h, gain, name, tm=512):
    t, d = h.shape

    def body(h_ref, g_ref, o_ref):
        x = h_ref[...]
        o_ref[...] = (x * _rms_inv(x) * g_ref[...]).astype(BF16)

    return pl.pallas_call(body, out_shape=jax.ShapeDtypeStruct((t, d), BF16), grid=(t // tm,),
                          in_specs=[_rows(tm, d), _full((1, d))], out_specs=_rows(tm, d),
                          compiler_params=_cp(("arbitrary",)), name=name)(h, gain)


def _resid_norm(h, o, g_post, g_next, name, tm=512):
    t, d = h.shape
    has_gain = g_next is not None

    def body(*refs):
        h_ref, o_ref, gp_ref = refs[:3]
        h1_ref, hn_ref = refs[-2:]
        ov = o_ref[...]
        h1 = h_ref[...] + ov * _rms_inv(ov) * gp_ref[...]
        h1_ref[...] = h1
        hn = h1 * _rms_inv(h1)
        if has_gain:
            hn = hn * refs[3][...]
        hn_ref[...] = hn.astype(BF16)

    args = [h, o, g_post] + ([g_next] if has_gain else [])
    in_specs = [_rows(tm, d), _rows(tm, d), _full((1, d))] + ([_full((1, d))] if has_gain else [])
    return pl.pallas_call(body, out_shape=(jax.ShapeDtypeStruct((t, d), F32), jax.ShapeDtypeStruct((t, d), BF16)),
                          grid=(t // tm,), in_specs=in_specs, out_specs=(_rows(tm, d), _rows(tm, d)),
                          compiler_params=_cp(("arbitrary",)), name=name)(*args)


def _ple_fwd(h2, pe, gl, g_ple, g_next, name, tm=512):
    t, d = h2.shape

    def body(h_ref, pe_ref, gl_ref, gp_ref, gn_ref, h3_ref, hn_ref):
        pv = pe_ref[...]
        e = pv * _rms_inv(pv) * gp_ref[...]
        h3 = h_ref[...] + jax.nn.sigmoid(gl_ref[...]) * e
        h3_ref[...] = h3
        hn_ref[...] = (h3 * _rms_inv(h3) * gn_ref[...]).astype(BF16)

    return pl.pallas_call(body, out_shape=(jax.ShapeDtypeStruct((t, d), F32), jax.ShapeDtypeStruct((t, d), BF16)),
                          grid=(t // tm,), in_specs=[_rows(tm, d)] * 3 + [_full((1, d))] * 2,
                          out_specs=(_rows(tm, d), _rows(tm, d)), compiler_params=_cp(("arbitrary",)),
                          name=name)(h2, pe, gl, g_ple, g_next)


def _loss_head(y, target, name, tm=512):
    t, d = y.shape

    def body(y_ref, t_ref, dy_ref, loss_ref):
        diff = y_ref[...] - t_ref[...]
        dy_ref[...] = diff * (1.0 / d)
        part = 0.5 * jnp.sum(jnp.mean(diff * diff, axis=-1, keepdims=True), axis=0, keepdims=True)
        part = jnp.broadcast_to(part, (8, LANES))

        @pl.when(pl.program_id(0) == 0)
        def _():
            loss_ref[...] = part

        @pl.when(pl.program_id(0) > 0)
        def _():
            loss_ref[...] += part

    return pl.pallas_call(body, out_shape=(jax.ShapeDtypeStruct((t, d), F32), jax.ShapeDtypeStruct((8, LANES), F32)),
                          grid=(t // tm,), in_specs=[_rows(tm, d)] * 2, out_specs=(_rows(tm, d), _full((8, LANES))),
                          compiler_params=_cp(("arbitrary",)), name=name)(y, target)


def _ple_bwd(dh3, pe, gl, g_ple, name, tm=512):
    t, d = dh3.shape

    def body(dh_ref, pe_ref, gl_ref, gp_ref, dgl_ref, dpe_ref, dg_ref):
        dh = dh_ref[...]
        pv = pe_ref[...]
        gp = gp_ref[...]
        gate = jax.nn.sigmoid(gl_ref[...])
        e = pv * _rms_inv(pv) * gp
        dgl_ref[...] = (dh * e * gate * (1.0 - gate)).astype(BF16)
        dpe, dg_rows = _rms_bwd(pv, gp, dh * gate)
        dpe_ref[...] = dpe.astype(BF16)
        _acc_rows(dg_ref, dg_rows)

    return pl.pallas_call(body, out_shape=(jax.ShapeDtypeStruct((t, d), BF16), jax.ShapeDtypeStruct((t, d), BF16),
                                           jax.ShapeDtypeStruct((1, d), F32)),
                          grid=(t // tm,), in_specs=[_rows(tm, d)] * 3 + [_full((1, d))],
                          out_specs=(_rows(tm, d), _rows(tm, d), _full((1, d))),
                          compiler_params=_cp(("arbitrary",)), name=name)(dh3, pe, gl, g_ple)


def _join(d_res, x_a, gain_a, d_a, x_b, gain_b, name, tm=512):
    t, d = d_res.shape
    has_ga = gain_a is not None
    has_b = x_b is not None

    def body(*refs):
        it = iter(refs)
        dres_ref, xa_ref = next(it), next(it)
        ga_ref = next(it) if has_ga else None
        da_ref = next(it)
        xb_ref, gb_ref = (next(it), next(it)) if has_b else (None, None)
        dout_ref = next(it)
        dga_ref = next(it) if has_ga else None
        db_ref, dgb_ref = (next(it), next(it)) if has_b else (None, None)
        dx, dg_rows = _rms_bwd(xa_ref[...], ga_ref[...] if has_ga else None, da_ref[...])
        dout = dres_ref[...] + dx
        dout_ref[...] = dout
        if has_ga:
            _acc_rows(dga_ref, dg_rows)
        if has_b:
            db, dgb_rows = _rms_bwd(xb_ref[...], gb_ref[...], dout)
            db_ref[...] = db.astype(BF16)
            _acc_rows(dgb_ref, dgb_rows)

    args, in_specs = [d_res, x_a], [_rows(tm, d), _rows(tm, d)]
    if has_ga:
        args.append(gain_a)
        in_specs.append(_full((1, d)))
    args.append(d_a)
    in_specs.append(_rows(tm, d))
    if has_b:
        args += [x_b, gain_b]
        in_specs += [_rows(tm, d), _full((1, d))]
    out_shape, out_specs = [jax.ShapeDtypeStruct((t, d), F32)], [_rows(tm, d)]
    if has_ga:
        out_shape.append(jax.ShapeDtypeStruct((1, d), F32))
        out_specs.append(_full((1, d)))
    if has_b:
        out_shape += [jax.ShapeDtypeStruct((t, d), BF16), jax.ShapeDtypeStruct((1, d), F32)]
        out_specs += [_rows(tm, d), _full((1, d))]
    return pl.pallas_call(body, out_shape=tuple(out_shape), grid=(t // tm,), in_specs=in_specs,
                          out_specs=tuple(out_specs), compiler_params=_cp(("arbitrary",)), name=name)(*args)


def _ffn_in(hn, w, name, tm=1024, tn=256):
    t, d = hn.shape
    f = w.shape[1] // 2
    tm = min(tm, t)
    nj = f // tn

    def body(h_ref, wa_ref, wb_ref, s_ref, ab_ref):
        h = h_ref[...]
        a = _dot(h, wa_ref[...], NN)
        b = _dot(h, wb_ref[...], NN)
        ab_ref[0] = a.astype(BF16)
        ab_ref[1] = b.astype(BF16)
        s_ref[...] = (a * jax.nn.sigmoid(a) * b).astype(BF16)

    return pl.pallas_call(
        body, out_shape=(jax.ShapeDtypeStruct((t, f), BF16), jax.ShapeDtypeStruct((2, t, f), BF16)), grid=(t // tm, nj),
        in_specs=[pl.BlockSpec((tm, d), lambda i, j: (i, 0)), pl.BlockSpec((d, tn), lambda i, j: (0, j)),
                  pl.BlockSpec((d, tn), lambda i, j: (0, nj + j))],
        out_specs=(pl.BlockSpec((tm, tn), lambda i, j: (i, j)), pl.BlockSpec((2, tm, tn), lambda i, j: (0, i, j))),
        compiler_params=_cp(("arbitrary",) * 2), name=name)(hn, w, w)


def _ffn_bwd(do, w_out, ab, name, tm=1024, tn=256):
    t, d = do.shape
    f = w_out.shape[0]
    tm = min(tm, t)

    def body(d_ref, w_ref, ab_ref, o_ref):
        ds = _dot(d_ref[...], w_ref[...], NT)
        a = ab_ref[0].astype(F32)
        sg = jax.nn.sigmoid(a)
        silu = a * sg
        o_ref[0] = (ds * ab_ref[1].astype(F32) * (sg + silu * (1.0 - sg))).astype(BF16)
        o_ref[1] = (ds * silu).astype(BF16)

    both = pl.BlockSpec((2, tm, tn), lambda i, j: (0, i, j))
    return pl.pallas_call(
        body, out_shape=jax.ShapeDtypeStruct((2, t, f), BF16), grid=(t // tm, f // tn),
        in_specs=[pl.BlockSpec((tm, d), lambda i, j: (i, 0)), pl.BlockSpec((tn, d), lambda i, j: (j, 0)), both],
        out_specs=both, compiler_params=_cp(("arbitrary",) * 2), name=name)(do, w_out, ab)


def _mm_halves_nt(x2, w, name, tm=512, tn=512):
    _, t, f = x2.shape
    n = w.shape[0]
    tm = min(tm, t)

    def body(x_ref, w_ref, o_ref, acc):
        part = _dot(x_ref[...], w_ref[...], NT)

        @pl.when(pl.program_id(2) == 0)
        def _():
            acc[...] = part

        @pl.when(pl.program_id(2) == 1)
        def _():
            o_ref[...] = acc[...] + part

    return pl.pallas_call(
        body, out_shape=jax.ShapeDtypeStruct((t, n), F32), grid=(t // tm, n // tn, 2),
        in_specs=[pl.BlockSpec((None, tm, f), lambda i, j, k: (k, i, 0)), pl.BlockSpec((tn, f), lambda i, j, k: (j, k))],
        out_specs=pl.BlockSpec((tm, tn), lambda i, j, k: (i, j)), scratch_shapes=[pltpu.VMEM((tm, tn), F32)],
        compiler_params=_cp(("arbitrary",) * 3), name=name)(x2, w)


def _mm_halves_tn(h, x2, name, tm=256):
    t, d = h.shape
    f = x2.shape[2]
    tn = 2 * f // N_CHIPS

    def body(h_ref, x_ref, o_ref):
        o_ref[...] = _dot(h_ref[...], x_ref[...], TN)

    return pl.pallas_call(
        body, out_shape=jax.ShapeDtypeStruct((N_CHIPS, d, tn), F32), grid=(d // tm, N_CHIPS),
        in_specs=[pl.BlockSpec((t, tm), lambda i, j: (0, i)), pl.BlockSpec((None, t, tn), lambda i, j: (j // 2, 0, j % 2))],
        out_specs=pl.BlockSpec((None, tm, tn), lambda i, j: (j, i, 0)),
        compiler_params=_cp(("arbitrary",) * 2), name=name)(h, x2)


def _mixnorm_fwd(att, gm, g_out, name, tm=512):
    t, w = att.shape

    def body(a_ref, m_ref, g_ref, o_ref):
        a, m, g = a_ref[...], m_ref[...], g_ref[...]
        o_ref[:, :w] = (a * _rms_inv(a) * g[:, :w]).astype(BF16)
        o_ref[:, w:] = (m * _rms_inv(m) * g[:, w:]).astype(BF16)

    return pl.pallas_call(body, out_shape=jax.ShapeDtypeStruct((t, 2 * w), BF16), grid=(t // tm,),
                          in_specs=[_rows(tm, w), _rows(tm, w), _full((1, 2 * w))], out_specs=_rows(tm, 2 * w),
                          compiler_params=_cp(("arbitrary",)), name=name)(att, gm, g_out)


def _mixnorm_bwd(att, gm, g_out, dmixed, name, tm=512):
    t, w = att.shape

    def body(a_ref, m_ref, g_ref, d_ref, da_ref, dm_ref, dg_ref):
        g, d = g_ref[...], d_ref[...]
        da, dga = _rms_bwd(a_ref[...], g[:, :w], d[:, :w])
        dm, dgm = _rms_bwd(m_ref[...], g[:, w:], d[:, w:])
        da_ref[...] = da
        dm_ref[...] = dm
        _acc_rows(dg_ref, jnp.concatenate([dga, dgm], axis=1))

    return pl.pallas_call(body, out_shape=(jax.ShapeDtypeStruct((t, w), F32), jax.ShapeDtypeStruct((t, w), F32),
                                           jax.ShapeDtypeStruct((1, 2 * w), F32)),
                          grid=(t // tm,), in_specs=[_rows(tm, w), _rows(tm, w), _full((1, 2 * w)), _rows(tm, 2 * w)],
                          out_specs=(_rows(tm, w), _rows(tm, w), _full((1, 2 * w))),
                          compiler_params=_cp(("arbitrary",)), name=name)(att, gm, g_out, dmixed)


SCAN_BLOCK = 256


def _gate_fwd(zgf, b_pad, name):
    t = zgf.shape[0]
    fcol = zgf.shape[1] // LANES - 1
    nb = t // SCAN_BLOCK

    def body(f_ref, b_ref, c_ref):
        r = lax.broadcasted_iota(jnp.int32, (SCAN_BLOCK, SCAN_BLOCK), 0)
        s = lax.broadcasted_iota(jnp.int32, (SCAN_BLOCK, SCAN_BLOCK), 1)
        tril = (r >= s).astype(F32)
        head = lax.broadcasted_iota(jnp.int32, (SCAN_BLOCK, LANES), 1) < N_HEADS
        carry = jnp.zeros((1, LANES), F32)
        for blk in range(nb):
            rows = pl.ds(blk * SCAN_BLOCK, SCAN_BLOCK)
            x = f_ref[rows, :] + b_ref[...]
            lf = jnp.minimum(x, 0.0) - jnp.log1p(jnp.exp(-jnp.abs(x)))
            lf = jnp.where(head, lf, 0.0)
            cs = _dot(tril, lf, NN, HIGHEST) + carry
            c_ref[rows, :] = cs
            carry = carry + jnp.sum(lf, axis=0, keepdims=True)

    return pl.pallas_call(body, out_shape=jax.ShapeDtypeStruct((t, LANES), F32),
                          grid=(1,), in_specs=[pl.BlockSpec((t, LANES), lambda i: (0, fcol)), _full((1, LANES))],
                          out_specs=_full((t, LANES)), compiler_params=_cp(("arbitrary",)),
                          name=name)(zgf, b_pad)


def _gate_bwd(dc, zgf, b_pad, name):
    t = zgf.shape[0]
    fcol = zgf.shape[1] // LANES - 1
    nb = t // SCAN_BLOCK

    def body(dc_ref, f_ref, b_ref, dfl_ref, db_ref):
        r = lax.broadcasted_iota(jnp.int32, (SCAN_BLOCK, SCAN_BLOCK), 0)
        s = lax.broadcasted_iota(jnp.int32, (SCAN_BLOCK, SCAN_BLOCK), 1)
        triu = (s >= r).astype(F32)
        head = lax.broadcasted_iota(jnp.int32, (SCAN_BLOCK, LANES), 1) < N_HEADS
        carry = jnp.zeros((1, LANES), F32)
        db = jnp.zeros((1, LANES), F32)
        for blk in reversed(range(nb)):
            rows = pl.ds(blk * SCAN_BLOCK, SCAN_BLOCK)
            dc = dc_ref[rows, :]
            dlf = _dot(triu, dc, NN, HIGHEST) + carry
            x = f_ref[rows, :] + b_ref[...]
            dfl = jnp.where(head, dlf * jax.nn.sigmoid(-x), 0.0)
            dfl_ref[rows, :] = dfl.astype(BF16)
            db = db + jnp.sum(dfl, axis=0, keepdims=True)
            carry = carry + jnp.sum(dc, axis=0, keepdims=True)
        db_ref[...] = db

    return pl.pallas_call(body, out_shape=(jax.ShapeDtypeStruct((t, LANES), BF16), jax.ShapeDtypeStruct((1, LANES), F32)),
                          grid=(1,), in_specs=[_full((t, LANES)), pl.BlockSpec((t, LANES), lambda i: (0, fcol)),
                                               _full((1, LANES))],
                          out_specs=(_full((t, LANES)), _full((1, LANES))), compiler_params=_cp(("arbitrary",)),
                          name=name)(dc, zgf, b_pad)


ATT_BLOCK = 512
PAIRS = N_HEADS // 2
CQ_LANE = HEAD_DIM
CK_LANE = HEAD_DIM + 3
LSE_LANE = HEAD_DIM + 6


def _pick_col(x, idx):
    lane = lax.broadcasted_iota(jnp.int32, x.shape, 1)
    return jnp.sum(jnp.where(lane == idx, x, 0.0), axis=1, keepdims=True)


def _split3(x):
    hi = x.astype(BF16)
    r1 = x - hi.astype(F32)
    mid = r1.astype(BF16)
    lo = (r1 - mid.astype(F32)).astype(BF16)
    return hi, mid, lo


def _lanes_put(base, lane, start, vals):
    out = base
    for n, v in enumerate(vals):
        out = jnp.where(lane == start + n, v, out)
    return out


def _to_first_half(x, hh):
    return x if hh == 0 else pltpu.roll(x, HEAD_DIM, 1)


def _attn_prep(qkv, c, name, tm=512):
    t = qkv.shape[0]
    tm = min(tm, t)

    def body(q_ref, k_ref, v_ref, c_ref, qa_ref, ka_ref, va_ref):
        j = pl.program_id(1)
        lane = lax.broadcasted_iota(jnp.int32, (tm, LANES), 1)
        first = lane < HEAD_DIM
        q2, k2, v2 = q_ref[...].astype(F32), k_ref[...].astype(F32), v_ref[...].astype(F32)
        cc = c_ref[...]
        one = jnp.ones((tm, 1), F32)
        for hh in range(2):
            chi, cmid, clo = [v.astype(F32) for v in _split3(_pick_col(cc, 2 * j + hh))]
            qh = jnp.where(first, _to_first_half(q2, hh) * (HEAD_DIM ** -0.5), 0.0)
            kh = jnp.where(first, _to_first_half(k2, hh), 0.0)
            vh = jnp.where(first, _to_first_half(v2, hh), 0.0)
            qa = _lanes_put(qh, lane, CQ_LANE, [chi, cmid, clo, one, one, one])
            ka = _lanes_put(kh, lane, CQ_LANE, [one, one, one, -chi, -cmid, -clo, one, one, one])
            va = _lanes_put(vh, lane, CQ_LANE, [one, one, one])
            cols = slice(hh * LANES, (hh + 1) * LANES)
            qa_ref[:, cols] = qa.astype(BF16)
            ka_ref[:, cols] = ka.astype(BF16)
            va_ref[:, cols] = va.astype(BF16)

    blk = lambda off: pl.BlockSpec((tm, LANES), lambda i, j: (i, off + j))
    out = pl.BlockSpec((tm, 2 * LANES), lambda i, j: (i, j))
    shp = jax.ShapeDtypeStruct((t, N_HEADS * LANES), BF16)
    return pl.pallas_call(body, out_shape=(shp, shp, shp), grid=(t // tm, PAIRS),
                          in_specs=[blk(0), blk(PAIRS), blk(2 * PAIRS), pl.BlockSpec((tm, LANES), lambda i, j: (i, 0))],
                          out_specs=(out, out, out), compiler_params=_cp(("arbitrary",) * 2), name=name)(qkv, qkv, qkv, c)


def _causal_block(tb):
    return lax.broadcasted_iota(jnp.int32, (tb, tb), 0) >= lax.broadcasted_iota(jnp.int32, (tb, tb), 1)


def _causal_pairs(nb, key_major):
    pairs = [(q, k) for q in range(nb) for k in range(q + 1)]
    if key_major:
        pairs.sort(key=lambda qk: (qk[1], qk[0]))
    return (jnp.array([q for q, _ in pairs], jnp.int32), jnp.array([k for _, k in pairs], jnp.int32))


def _attn_fwd(qa, ka, va, name):
    t = qa.shape[0]
    tb = min(ATT_BLOCK, t)
    q_tab, k_tab = _causal_pairs(t // tb, key_major=False)

    def body(q_tab_ref, k_tab_ref, q_ref, k_ref, v_ref, o_ref, lse_ref, m_s, acc_s):
        qi, kb = q_tab_ref[pl.program_id(1)], k_tab_ref[pl.program_id(1)]

        @pl.when(kb == 0)
        def _():
            m_s[...] = jnp.full(m_s.shape, NEG_INF, F32)
            acc_s[...] = jnp.zeros(acc_s.shape, F32)

        def step(diagonal):
            for hh in range(2):
                cols = slice(hh * LANES, (hh + 1) * LANES)
                sc = _dot(q_ref[:, cols], k_ref[:, cols], NT)
                if diagonal:
                    sc = jnp.where(_causal_block(tb), sc, NEG_INF)
                m_prev = m_s[hh]
                m_new = jnp.maximum(m_prev, jnp.max(sc, axis=1, keepdims=True))
                p = jnp.exp(sc - m_new)
                acc_s[hh] = jnp.exp(m_prev - m_new) * acc_s[hh] + _dot(p.astype(BF16), v_ref[:, cols], NN)
                m_s[hh] = m_new

        @pl.when(kb < qi)
        def _():
            step(False)

        @pl.when(kb == qi)
        def _():
            step(True)
            lane = lax.broadcasted_iota(jnp.int32, (tb, LANES), 1)
            outs, lses = [], []
            for hh in range(2):
                acc = acc_s[hh]
                l = _pick_col(acc, CQ_LANE)
                outs.append(acc / l)
                lses.append(m_s[hh] + jnp.log(l))
            o_ref[...] = jnp.where(lane < HEAD_DIM, outs[0], pltpu.roll(outs[1], HEAD_DIM, 1))
            lse_ref[...] = jnp.where(lane == 0, lses[0], jnp.where(lane == 1, lses[1], 0.0))

    qrow = lambda j, s, qt, kt: (qt[s], j)
    krow = lambda j, s, qt, kt: (kt[s], j)
    grid_spec = pltpu.PrefetchScalarGridSpec(
        num_scalar_prefetch=2, grid=(PAIRS, q_tab.shape[0]),
        in_specs=[pl.BlockSpec((tb, 2 * LANES), qrow), pl.BlockSpec((tb, 2 * LANES), krow),
                  pl.BlockSpec((tb, 2 * LANES), krow)],
        out_specs=(pl.BlockSpec((tb, LANES), qrow), pl.BlockSpec((tb, LANES), qrow)),
        scratch_shapes=[pltpu.VMEM((2, tb, 1), F32), pltpu.VMEM((2, tb, LANES), F32)])
    return pl.pallas_call(
        body, out_shape=(jax.ShapeDtypeStruct((t, D_ATT), F32), jax.ShapeDtypeStruct((t, PAIRS * LANES), F32)),
        grid_spec=grid_spec, compiler_params=_cp(("arbitrary",) * 2), name=name)(q_tab, k_tab, qa, ka, va)


def _attn_bwd_prep(qa, att, lse, datt, name, tm=512):
    t = qa.shape[0]
    tm = min(tm, t)

    def body(qa_ref, o_ref, lse_ref, do_ref, qb_ref, doa_ref):
        j = pl.program_id(1)
        lane = lax.broadcasted_iota(jnp.int32, (tm, LANES), 1)
        first = lane < HEAD_DIM
        do = do_ref[...]
        prod = do * o_ref[...]
        lse2 = lse_ref[...]
        for hh in range(2):
            cols = slice(hh * LANES, (hh + 1) * LANES)
            delta = jnp.sum(jnp.where(first if hh == 0 else ~first, prod, 0.0), axis=1, keepdims=True)
            doh = jnp.where(first, _to_first_half(do, hh), 0.0)
            doa_ref[:, cols] = _lanes_put(doh, lane, CQ_LANE, [v.astype(F32) for v in _split3(-delta)]).astype(BF16)
            nl = [v.astype(F32) for v in _split3(-_pick_col(lse2, hh))]
            qb_ref[:, cols] = _lanes_put(qa_ref[:, cols].astype(F32), lane, LSE_LANE, nl).astype(BF16)

    wide = pl.BlockSpec((tm, 2 * LANES), lambda i, j: (i, j))
    pair = pl.BlockSpec((tm, LANES), lambda i, j: (i, j))
    shp = jax.ShapeDtypeStruct((t, N_HEADS * LANES), BF16)
    return pl.pallas_call(body, out_shape=(shp, shp), grid=(t // tm, PAIRS), in_specs=[wide, pair, pair, pair],
                          out_specs=(wide, wide), compiler_params=_cp(("arbitrary",) * 2), name=name)(qa, att, lse, datt)


def _attn_bwd(qb, ka, va, doa, name):
    t = qb.shape[0]
    tb = min(ATT_BLOCK, t)
    nb = t // tb
    q_tab, k_tab = _causal_pairs(nb, key_major=True)

    def body(q_tab_ref, k_tab_ref, q_ref, k_ref, v_ref, do_ref, dq_ref, dk_ref, dv_ref, dk_s, dv_s):
        qi, kb = q_tab_ref[pl.program_id(1)], k_tab_ref[pl.program_id(1)]

        @pl.when(qi == kb)
        def _():
            dk_s[...] = jnp.zeros(dk_s.shape, F32)
            dv_s[...] = jnp.zeros(dv_s.shape, F32)

        def step(diagonal):
            rows = pl.ds(pl.multiple_of(qi * tb, tb), tb)
            for hh in range(2):
                cols = slice(hh * LANES, (hh + 1) * LANES)
                q, k, do = q_ref[:, cols], k_ref[:, cols], do_ref[:, cols]
                sc = _dot(q, k, NT)
                if diagonal:
                    sc = jnp.where(_causal_block(tb), sc, NEG_INF)
                p = jnp.exp(sc)
                ds = (p * _dot(do, v_ref[:, cols], NT)).astype(BF16)
                dv_s[:, cols] += _dot(p.astype(BF16), do, TN)
                dk_s[:, cols] += _dot(ds, q, TN)
                dq_new = _dot(ds, k, NN)

                @pl.when(kb == 0)
                def _():
                    dq_ref[rows, cols] = dq_new

                @pl.when(kb > 0)
                def _():
                    dq_ref[rows, cols] += dq_new

        @pl.when(qi == kb)
        def _():
            step(True)

        @pl.when(qi > kb)
        def _():
            step(False)

        @pl.when(qi == nb - 1)
        def _():
            dk_ref[...] = dk_s[...]
            dv_ref[...] = dv_s[...].astype(BF16)

    qrow = lambda j, s, qt, kt: (qt[s], j)
    krow = lambda j, s, qt, kt: (kt[s], j)
    blk = (tb, 2 * LANES)
    grid_spec = pltpu.PrefetchScalarGridSpec(
        num_scalar_prefetch=2, grid=(PAIRS, q_tab.shape[0]),
        in_specs=[pl.BlockSpec(blk, qrow), pl.BlockSpec(blk, krow), pl.BlockSpec(blk, krow), pl.BlockSpec(blk, qrow)],
        out_specs=(pl.BlockSpec((t, 2 * LANES), lambda j, s, qt, kt: (0, j)), pl.BlockSpec(blk, krow),
                   pl.BlockSpec(blk, krow)),
        scratch_shapes=[pltpu.VMEM(blk, F32), pltpu.VMEM(blk, F32)])
    wide = (t, N_HEADS * LANES)
    return pl.pallas_call(
        body, out_shape=(jax.ShapeDtypeStruct(wide, F32), jax.ShapeDtypeStruct(wide, F32), jax.ShapeDtypeStruct(wide, BF16)),
        grid_spec=grid_spec, compiler_params=_cp(("arbitrary",) * 2), name=name)(q_tab, k_tab, qb, ka, va, doa)


def _attn_bwd_post(dqa, dka, dva, name, tm=256):
    t = dqa.shape[0]
    tm = min(tm, t)

    def body(dq_ref, dk_ref, dv_ref, o_ref, dc_ref):
        lane = lax.broadcasted_iota(jnp.int32, (tm, LANES), 1)
        first = lane < HEAD_DIM
        dc = jnp.zeros((tm, LANES), F32)
        for j in range(PAIRS):
            packed = []
            for ref, gain in ((dq_ref, HEAD_DIM ** -0.5), (dk_ref, 1.0), (dv_ref, 1.0)):
                even = ref[:, 2 * j * LANES:(2 * j + 1) * LANES].astype(F32)
                odd = ref[:, (2 * j + 1) * LANES:(2 * j + 2) * LANES].astype(F32)
                packed.append((jnp.where(first, even, pltpu.roll(odd, HEAD_DIM, 1)) * gain).astype(BF16))
                if ref is dq_ref:
                    dc = dc + jnp.where(lane == 2 * j, _pick_col(even, CQ_LANE), 0.0)
                    dc = dc + jnp.where(lane == 2 * j + 1, _pick_col(odd, CQ_LANE), 0.0)
                if ref is dk_ref:
                    dc = dc - jnp.where(lane == 2 * j, _pick_col(even, CK_LANE), 0.0)
                    dc = dc - jnp.where(lane == 2 * j + 1, _pick_col(odd, CK_LANE), 0.0)
            for part, val in enumerate(packed):
                o_ref[:, (part * PAIRS + j) * LANES:(part * PAIRS + j + 1) * LANES] = val
        dc_ref[...] = dc

    wide = _rows(tm, N_HEADS * LANES)
    return pl.pallas_call(body, out_shape=(jax.ShapeDtypeStruct((t, 3 * D_ATT), BF16), jax.ShapeDtypeStruct((t, LANES), F32)),
                          grid=(t // tm,), in_specs=[wide, wide, wide], out_specs=(_rows(tm, 3 * D_ATT), _rows(tm, LANES)),
                          compiler_params=_cp(("arbitrary",)), name=name)(dqa, dka, dva)


GELU_K = 0.7978845608028654
GELU_A = 0.044715


def _gelu(x):
    th = jnp.tanh(GELU_K * (x + GELU_A * x * x * x))
    return 0.5 * x * (1.0 + th), th


def _group_mean_matrix():
    r = jnp.arange(D_GM)[:, None] // HEAD_DIM
    s = jnp.arange(D_GM)[None, :] // HEAD_DIM
    return jnp.where(r == s, 1.0 / HEAD_DIM, 0.0).astype(F32)


def _gm_forward_parts(g, w_ref, bias, gain, mean_mat):
    gel, _ = _gelu(g)
    u, vv = gel[:, :D_GM], gel[:, D_GM:]
    mu = _dot(vv, mean_mat, NN, HIGHEST)
    d = vv - mu
    rstd = lax.rsqrt(_dot(d * d, mean_mat, NN, HIGHEST) + EPS)
    xhat = d * rstd
    vn = (xhat * gain).astype(BF16)
    first = lax.broadcasted_iota(jnp.int32, (CHUNK, LANES), 1) < HEAD_DIM
    tri = lax.broadcasted_iota(jnp.int32, (CHUNK, CHUNK), 0) >= lax.broadcasted_iota(jnp.int32, (CHUNK, CHUNK), 1)
    parts = []
    for jp in range(D_GM // LANES):
        vp = vn[:, jp * LANES:(jp + 1) * LANES]
        wa = jnp.where(tri, w_ref[2 * jp], 0.0).astype(BF16)
        wb = jnp.where(tri, w_ref[2 * jp + 1], 0.0).astype(BF16)
        parts.append(jnp.where(first, _dot(wa, vp, NN), _dot(wb, vp, NN)))
    mixed = jnp.concatenate(parts, axis=1) + bias
    return u, xhat, rstd, vn, mixed


def _gmlp_fwd(zgf, w_s, bias_full, gain, mean_mat, name):
    t = zgf.shape[0]

    def body(g_ref, w_ref, b_ref, gain_ref, mm_ref, o_ref):
        u, _, _, _, mixed = _gm_forward_parts(g_ref[...], w_ref, b_ref[...], gain_ref[...], mm_ref[...])
        o_ref[...] = u * mixed

    return pl.pallas_call(body, out_shape=jax.ShapeDtypeStruct((t, D_GM), F32), grid=(t // CHUNK,),
                          in_specs=[_rows(CHUNK, 2 * D_GM), _full(w_s.shape), _full((CHUNK, D_GM)), _full((1, D_GM)),
                                    _full((D_GM, D_GM))],
                          out_specs=_rows(CHUNK, D_GM), compiler_params=_cp(("arbitrary",)),
                          name=name)(zgf, w_s, bias_full, gain, mean_mat)


def _gmlp_bwd(zgf, dgm, w_s, bias_full, gain, mean_mat, name):
    t = zgf.shape[0]

    def body(g_ref, d_ref, w_ref, b_ref, gain_ref, mm_ref, dg_ref, dw_ref, dmix_ref, dgain_ref):
        g, gain, mean_mat = g_ref[...], gain_ref[...], mm_ref[...]
        u, xhat, rstd, vn, mixed = _gm_forward_parts(g, w_ref, b_ref[...], gain, mean_mat)
        dgm_v = d_ref[...]
        du = dgm_v * mixed
        dmixed = dgm_v * u
        dm_b = dmixed.astype(BF16)
        first = lax.broadcasted_iota(jnp.int32, (CHUNK, LANES), 1) < HEAD_DIM
        tri = lax.broadcasted_iota(jnp.int32, (CHUNK, CHUNK), 0) >= lax.broadcasted_iota(jnp.int32, (CHUNK, CHUNK), 1)

        @pl.when(pl.program_id(0) == 0)
        def _():
            dw_ref[...] = jnp.zeros(dw_ref.shape, F32)
            dmix_ref[...] = jnp.zeros(dmix_ref.shape, F32)
            dgain_ref[...] = jnp.zeros(dgain_ref.shape, F32)

        dvn_parts = []
        for jp in range(D_GM // LANES):
            vp = vn[:, jp * LANES:(jp + 1) * LANES]
            dmp = dm_b[:, jp * LANES:(jp + 1) * LANES]
            halves = []
            for hh in range(2):
                sel = first if hh == 0 else ~first
                dw = jnp.where(tri, _dot(jnp.where(sel, dmp, jnp.zeros_like(dmp)), vp, NT), 0.0)
                grp = 2 * jp + hh
                dw_ref[grp] += dw
                wm = jnp.where(tri, w_ref[grp], 0.0).astype(BF16)
                halves.append(_dot(wm, dmp, TN))
            dvn_parts.append(jnp.where(first, halves[0], halves[1]))
        dvn = jnp.concatenate(dvn_parts, axis=1)
        dmix_ref[...] += dmixed
        dgain_ref[...] += jnp.sum(dvn * xhat, axis=0, keepdims=True)
        dxhat = dvn * gain
        m1 = _dot(dxhat, mean_mat, NN, HIGHEST)
        m2 = _dot(dxhat * xhat, mean_mat, NN, HIGHEST)
        dvv = rstd * (dxhat - m1 - xhat * m2)
        gel, th = _gelu(g)
        dgel = 0.5 * (1.0 + th) + 0.5 * g * (1.0 - th * th) * GELU_K * (1.0 + 3.0 * GELU_A * g * g)
        dg_ref[...] = (jnp.concatenate([du, dvv], axis=1) * dgel).astype(BF16)

    return pl.pallas_call(
        body, out_shape=(jax.ShapeDtypeStruct((t, 2 * D_GM), BF16), jax.ShapeDtypeStruct(w_s.shape, F32),
                         jax.ShapeDtypeStruct((CHUNK, D_GM), F32), jax.ShapeDtypeStruct((1, D_GM), F32)),
        grid=(t // CHUNK,),
        in_specs=[_rows(CHUNK, 2 * D_GM), _rows(CHUNK, D_GM), _full(w_s.shape), _full((CHUNK, D_GM)), _full((1, D_GM)),
                  _full((D_GM, D_GM))],
        out_specs=(_rows(CHUNK, 2 * D_GM), _full(w_s.shape), _full((CHUNK, D_GM)), _full((1, D_GM))),
        compiler_params=_cp(("arbitrary",)), name=name)(zgf, dgm, w_s, bias_full, gain, mean_mat)


def _row_tile(r, c, budget=1 << 19):
    best = None
    for tr in range(8, r + 1, 8):
        if r % tr == 0 and tr * c <= budget:
            best = tr
    return best if best is not None else r


def _adamw(w, g, m, v, name):
    nl, r, c = w.shape
    tr = _row_tile(r, c, 1 << 18)
    c1 = 1.0 - ADAM_B1 ** ADAM_STEP
    c2 = 1.0 - ADAM_B2 ** ADAM_STEP

    def body(w_ref, g_ref, m_ref, v_ref, d_ref, mo_ref, vo_ref):
        gv = g_ref[...]
        mn = ADAM_B1 * m_ref[...] + (1.0 - ADAM_B1) * gv
        vn = ADAM_B2 * v_ref[...] + (1.0 - ADAM_B2) * jnp.square(gv)
        mo_ref[...] = mn
        vo_ref[...] = vn
        d_ref[...] = -ADAM_LR * ((mn / c1) / (jnp.sqrt(vn / c2) + ADAM_EPS) + ADAM_WD * w_ref[...])

    spec = pl.BlockSpec((None, tr, c), lambda l, i: (l, i, 0))
    shp = jax.ShapeDtypeStruct(w.shape, F32)
    return pl.pallas_call(body, out_shape=(shp, shp, shp), grid=(nl, r // tr), in_specs=[spec] * 4,
                          out_specs=(spec, spec, spec), compiler_params=_cp(("arbitrary",) * 2), name=name)(w, g, m, v)


def _add_sibling(g, recv, c_idx, wire_dtype, name):
    nj, _, h, c = g.shape
    tr = _row_tile(h, c)

    def body(c_ref, g_ref, r_ref, o_ref):
        o_ref[...] = (g_ref[...] + r_ref[...]).astype(wire_dtype)

    grid_spec = pltpu.PrefetchScalarGridSpec(
        num_scalar_prefetch=1, grid=(nj, h // tr),
        in_specs=[pl.BlockSpec((None, None, tr, c), lambda j, i, c_ref: (j, c_ref[0], i, 0)),
                  pl.BlockSpec((None, tr, c), lambda j, i, c_ref: (j, i, 0))],
        out_specs=pl.BlockSpec((None, tr, c), lambda j, i, c_ref: (j, i, 0)))
    return pl.pallas_call(body, out_shape=jax.ShapeDtypeStruct((nj, h, c), wire_dtype), grid_spec=grid_spec,
                          compiler_params=_cp(("arbitrary",) * 2), name=name)(c_idx, g, recv)


def _add_chips(own, parts, place, name):
    _, h, c = own.shape
    tr = _row_tile(h, c)

    def body(p_ref, o_ref, a_ref, b_ref, c_ref, out_ref):
        out_ref[...] = ((o_ref[...].astype(F32) + a_ref[...].astype(F32)) + b_ref[...].astype(F32)) + c_ref[...].astype(F32)

    def other(k):
        return pl.BlockSpec((None, tr, c), lambda i, p_ref: (jnp.bitwise_xor(p_ref[0], k), i, 0))

    grid_spec = pltpu.PrefetchScalarGridSpec(
        num_scalar_prefetch=1, grid=(h // tr,),
        in_specs=[pl.BlockSpec((None, tr, c), lambda i, p_ref: (p_ref[0], i, 0)), other(1), other(2), other(3)],
        out_specs=pl.BlockSpec((None, tr, c), lambda i, p_ref: (p_ref[1], i, 0)))
    return pl.pallas_call(body, out_shape=jax.ShapeDtypeStruct((2, h, c), F32), grid_spec=grid_spec,
                          compiler_params=_cp(("arbitrary",)), name=name)(place, own, parts, parts, parts)


HBM_SPEC = pl.BlockSpec(memory_space=pltpu.HBM)


def _place():
    x, y, c = lax.axis_index("x"), lax.axis_index("y"), lax.axis_index("c")
    chips = [(1 - x, y), (x, 1 - y), (1 - x, 1 - y)]
    return x, y, c, 2 * x + y, chips, [2 * px + py for px, py in chips]


def _remote(src, dst, send_sem, recv_sem, dev):
    return pltpu.make_async_remote_copy(src_ref=src, dst_ref=dst, send_sem=send_sem, recv_sem=recv_sem,
                                        device_id=dev, device_id_type=MESH)


def _comm_call(body, arrays, out_shapes, sems, name, aliases=None):
    n = len(arrays)
    return pl.pallas_call(
        body, out_shape=tuple(out_shapes), in_specs=[HBM_SPEC] * n, out_specs=tuple([HBM_SPEC] * len(out_shapes)),
        scratch_shapes=[pltpu.SemaphoreType.DMA(s) for s in sems], input_output_aliases=aliases or {},
        compiler_params=pltpu.CompilerParams(has_side_effects=True), name=name)(*arrays)


def _gather_weights(shards, name):
    n = len(shards)

    def body(*refs):
        ins, outs = refs[:n], refs[n:2 * n]
        send, recv = refs[2 * n:]
        x, y, c, me, chips, cidx = _place()
        sib = (x, y, 1 - c)
        first = []
        for a in range(n):
            for k, (px, py) in enumerate(chips):
                cp = _remote(ins[a].at[c], outs[a].at[me, c], send.at[a, k], recv.at[a, k], (px, py, c))
                cp.start()
                first.append(cp)
        for a in range(n):
            cp = _remote(ins[a], outs[a].at[me], send.at[a, 6], recv.at[a, 6], sib)
            cp.start()
            first.append(cp)
        passed = []
        for a in range(n):
            for k in range(3):
                blk = outs[a].at[cidx[k], c]
                _remote(blk, blk, send.at[a, k], recv.at[a, k], sib).wait_recv()
                fw = _remote(blk, blk, send.at[a, 3 + k], recv.at[a, 3 + k], sib)
                fw.start()
                passed.append(fw)
        for a in range(n):
            for k in range(3):
                blk = outs[a].at[cidx[k], 1 - c]
                _remote(blk, blk, send.at[a, 3 + k], recv.at[a, 3 + k], sib).wait_recv()
            blk = outs[a].at[me]
            _remote(blk, blk, send.at[a, 6], recv.at[a, 6], sib).wait_recv()
        for cp in first + passed:
            cp.wait_send()

    out_shapes = [jax.ShapeDtypeStruct((N_CHIPS,) + s.shape, s.dtype) for s in shards]
    return _comm_call(body, shards, out_shapes, [(n, 7), (n, 7)], name)


def _send_sibling_halves(grads, name):
    n = len(grads)

    def body(*refs):
        ins, outs = refs[:n], refs[n:2 * n]
        send, recv = refs[2 * n:]
        x, y, c, _, _, _ = _place()
        sib = (x, y, 1 - c)
        cps = []
        for a in range(n):
            for j in range(N_CHIPS):
                cp = _remote(ins[a].at[j, 1 - c], outs[a].at[j], send.at[a, j], recv.at[a, j], sib)
                cp.start()
                cps.append(cp)
        for cp in cps:
            cp.wait_recv()
        for cp in cps:
            cp.wait_send()

    out_shapes = [jax.ShapeDtypeStruct((g.shape[0],) + g.shape[2:], g.dtype) for g in grads]
    return _comm_call(body, grads, out_shapes, [(n, N_CHIPS), (n, N_CHIPS)], name)


def _scatter_to_chips(sums, name):
    n = len(sums)

    def body(*refs):
        ins, outs = refs[:n], refs[n:2 * n]
        send, recv = refs[2 * n:]
        x, y, c, me, chips, cidx = _place()
        cps = []
        for a in range(n):
            for k, (px, py) in enumerate(chips):
                cp = _remote(ins[a].at[cidx[k]], outs[a].at[me], send.at[a, k], recv.at[a, k], (px, py, c))
                cp.start()
                cps.append(cp)
        for a in range(n):
            for k in range(3):
                blk = outs[a].at[cidx[k]]
                _remote(blk, blk, send.at[a, k], recv.at[a, k], (x, y, 1 - c)).wait_recv()
        for cp in cps:
            cp.wait_send()

    out_shapes = [jax.ShapeDtypeStruct(s.shape, s.dtype) for s in sums]
    return _comm_call(body, sums, out_shapes, [(n, 3), (n, 3)], name)


def _exchange_halves(halves, name):
    n = len(halves)

    def body(*refs):
        ins, bufs = refs[:n], refs[n:2 * n]
        send, recv = refs[2 * n:]
        x, y, c, _, _, _ = _place()
        sib = (x, y, 1 - c)
        cps = []
        for a in range(n):
            cp = _remote(ins[a].at[c], bufs[a].at[c], send.at[a], recv.at[a], sib)
            cp.start()
            cps.append(cp)
        for a in range(n):
            blk = bufs[a].at[1 - c]
            _remote(blk, blk, send.at[a], recv.at[a], sib).wait_recv()
        for cp in cps:
            cp.wait_send()

    out_shapes = [jax.ShapeDtypeStruct(s.shape, s.dtype) for s in halves]
    return _comm_call(body, halves, out_shapes, [(n,), (n,)], name, aliases={a: a for a in range(n)})


def _gather_chips(slices, name):
    n = len(slices)

    def body(*refs):
        ins, outs = refs[:n], refs[n:2 * n]
        send, recv = refs[2 * n:]
        x, y, c, me, chips, cidx = _place()
        sib = (x, y, 1 - c)
        cps = []
        for a in range(n):
            for k, dev in enumerate([(px, py, c) for px, py in chips] + [sib]):
                cp = _remote(ins[a], outs[a].at[me], send.at[a, k], recv.at[a, k], dev)
                cp.start()
                cps.append(cp)
        for a in range(n):
            for k, slot in enumerate(cidx + [me]):
                blk = outs[a].at[slot]
                _remote(blk, blk, send.at[a, k], recv.at[a, k], sib).wait_recv()
        for cp in cps:
            cp.wait_send()

    out_shapes = [jax.ShapeDtypeStruct((N_CHIPS,) + s.shape, s.dtype) for s in slices]
    return _comm_call(body, slices, out_shapes, [(n, 4), (n, 4)], name)


def _reduce_scatter(grads, c_idx, place, wire_dtype, tag):
    split = [g.reshape(g.shape[0], 2, g.shape[1] // 2, g.shape[2]) for g in grads]
    recv = _send_sibling_halves(split, "rs_sibling_" + tag)
    sums = [_add_sibling(g, r, c_idx, wire_dtype, "rs_add_sibling_" + tag) for g, r in zip(split, recv)]
    parts = _scatter_to_chips(sums, "rs_chips_" + tag)
    halves = [_add_chips(s, p, place, "rs_add_chips_" + tag) for s, p in zip(sums, parts)]
    both = _exchange_halves(halves, "rs_halves_" + tag)
    return [b.reshape(b.shape[0] * b.shape[1], b.shape[2]) for b in both]


SMALL_ORDER = ("gm_w_s", "mix_pre_norm", "mix_post_norm", "mix_out_norm", "ffn_pre_norm", "ffn_post_norm", "ple_norm",
               "gm_v_norm", "gm_b_s", "b_forget")
SMALL_ROWS_MULTIPLE = 64


def _pack_small(parts):
    flat = []
    for nme in SMALL_ORDER:
        v = parts[nme].reshape(-1)
        pad = (-v.shape[0]) % LANES
        flat.append(jnp.pad(v, (0, pad)) if pad else v)
    v = jnp.concatenate(flat)
    rows = v.shape[0] // LANES
    pad_rows = (-rows) % SMALL_ROWS_MULTIPLE
    return jnp.pad(v.reshape(rows, LANES), ((0, pad_rows), (0, 0)))


def _unpack_small(packed, shapes):
    flat = packed.reshape(-1)
    out, off = {}, 0
    for nme in SMALL_ORDER:
        size = 1
        for s in shapes[nme]:
            size *= s
        out[nme] = flat[off:off + size].reshape(shapes[nme])
        off += size + ((-size) % LANES)
    return out


def kernel(x, p, mix_pre_norm, mix_post_norm, w_in, b_forget, gm_v_norm, gm_w_s, gm_b_s, mix_out_norm, w_out, ffn_pre_norm, ffn_post_norm, w_ffn_in, w_ffn_out, w_ple, ple_norm, w_ple_gate, loss_target, m_mix_pre_norm, m_mix_post_norm, m_w_in, m_b_forget, m_gm_v_norm, m_gm_w_s, m_gm_b_s, m_mix_out_norm, m_w_out, m_ffn_pre_norm, m_ffn_post_norm, m_w_ffn_in, m_w_ffn_out, m_w_ple, m_ple_norm, m_w_ple_gate, v_mix_pre_norm, v_mix_post_norm, v_w_in, v_b_forget, v_gm_v_norm, v_gm_w_s, v_gm_b_s, v_mix_out_norm, v_w_out, v_ffn_pre_norm, v_ffn_post_norm, v_w_ffn_in, v_w_ffn_out, v_w_ple, v_ple_norm, v_w_ple_gate):
    weights = dict(mix_pre_norm=mix_pre_norm, mix_post_norm=mix_post_norm, w_in=w_in, b_forget=b_forget,
                   gm_v_norm=gm_v_norm, gm_w_s=gm_w_s, gm_b_s=gm_b_s, mix_out_norm=mix_out_norm, w_out=w_out,
                   ffn_pre_norm=ffn_pre_norm, ffn_post_norm=ffn_post_norm, w_ffn_in=w_ffn_in, w_ffn_out=w_ffn_out,
                   w_ple=w_ple, ple_norm=ple_norm, w_ple_gate=w_ple_gate)
    mom_m = dict(mix_pre_norm=m_mix_pre_norm, mix_post_norm=m_mix_post_norm, w_in=m_w_in, b_forget=m_b_forget,
                 gm_v_norm=m_gm_v_norm, gm_w_s=m_gm_w_s, gm_b_s=m_gm_b_s, mix_out_norm=m_mix_out_norm, w_out=m_w_out,
                 ffn_pre_norm=m_ffn_pre_norm, ffn_post_norm=m_ffn_post_norm, w_ffn_in=m_w_ffn_in,
                 w_ffn_out=m_w_ffn_out, w_ple=m_w_ple, ple_norm=m_ple_norm, w_ple_gate=m_w_ple_gate)
    mom_v = dict(mix_pre_norm=v_mix_pre_norm, mix_post_norm=v_mix_post_norm, w_in=v_w_in, b_forget=v_b_forget,
                 gm_v_norm=v_gm_v_norm, gm_w_s=v_gm_w_s, gm_b_s=v_gm_b_s, mix_out_norm=v_mix_out_norm, w_out=v_w_out,
                 ffn_pre_norm=v_ffn_pre_norm, ffn_post_norm=v_ffn_post_norm, w_ffn_in=v_w_ffn_in,
                 w_ffn_out=v_w_ffn_out, w_ple=v_w_ple, ple_norm=v_ple_norm, w_ple_gate=v_w_ple_gate)
    big = ("w_in", "w_out", "w_ffn_in", "w_ffn_out", "w_ple", "w_ple_gate")
    depth = w_in.shape[0]
    t, d = x.shape[1], x.shape[2]
    d_ff = w_ffn_out.shape[1] * N_CHIPS
    c_idx = lax.axis_index("c").astype(jnp.int32).reshape(1)
    place = jnp.stack([2 * lax.axis_index("x") + lax.axis_index("y"), lax.axis_index("c")]).astype(jnp.int32)
    h = x[0]
    target = loss_target[0]
    mean_mat = _group_mean_matrix()

    def row(a, i):
        return a[i].reshape(1, -1)

    saved = []
    hn = _norm_cast(h, row(mix_pre_norm, 0), "norm_first")
    for i in range(depth):
        shards = [weights[nme][i].astype(BF16) for nme in big]
        shards = [s.reshape(2, s.shape[0] // 2, s.shape[1]) for s in shards]
        gathered = _gather_weights(shards, "gather_weights")
        gathered = [g.reshape(N_CHIPS, g.shape[2] * 2, g.shape[3]) for g in gathered]
        by_cols = lambda g: g.transpose(1, 0, 2).reshape(g.shape[1], N_CHIPS * g.shape[2])
        by_rows = lambda g: g.reshape(N_CHIPS * g.shape[1], g.shape[2])
        w_in_f = by_cols(gathered[0])
        w_qkv = w_in_f[:, :3 * D_ATT]
        w_gf = jnp.concatenate([w_in_f[:, 3 * D_ATT + N_HEADS:], w_in_f[:, 3 * D_ATT:3 * D_ATT + N_HEADS],
                                jnp.zeros((d, LANES - N_HEADS), BF16)], axis=1)
        w_out_f = by_rows(gathered[1])
        w_fi_f = by_cols(gathered[2])
        w_fo_f = by_rows(gathered[3])
        w_ple_f = by_cols(gathered[4])
        w_pg_f = by_rows(gathered[5])
        b_pad = jnp.pad(b_forget[i], (0, LANES - N_HEADS)).reshape(1, LANES)
        bias_full = jnp.repeat(gm_b_s[i].T, HEAD_DIM, axis=1)
        gain_v = row(gm_v_norm, i)

        qkv = _mm(hn, w_qkv, "nn", BF16, "mm_qkv")
        zgf = _mm(hn, w_gf, "nn", F32, "mm_gf", tn_cap=384)
        qa, ka, va = _attn_prep(qkv, _gate_fwd(zgf, b_pad, "gate_fwd"), "attn_prep")
        att, lse = _attn_fwd(qa, ka, va, "attn_fwd")
        gm = _gmlp_fwd(zgf, gm_w_s[i], bias_full, gain_v, mean_mat, "gmlp_fwd")
        mixed = _mixnorm_fwd(att, gm, row(mix_out_norm, i), "mixnorm_fwd")
        o = _mm(mixed, w_out_f, "nn", F32, "mm_out")
        h1, hn2 = _resid_norm(h, o, row(mix_post_norm, i), row(ffn_pre_norm, i), "resid_mix")
        s, ab = _ffn_in(hn2, w_fi_f, "ffn_in")
        o2 = _mm(s, w_fo_f, "nn", F32, "mm_ffn_out")
        h2, hr = _resid_norm(h1, o2, row(ffn_post_norm, i), None, "resid_ffn")
        pe = _mm(p[i, 0], w_ple_f, "nn", F32, "mm_ple")
        gl = _mm(hr, w_pg_f, "nn", F32, "mm_ple_gate")
        g_next = row(mix_pre_norm, i + 1) if i + 1 < depth else row(mix_pre_norm, 0)
        h3, hn_next = _ple_fwd(h2, pe, gl, row(ple_norm, i), g_next, "ple_fwd")
        saved.append(dict(h=h, hn=hn, qa=qa, ka=ka, va=va, zgf=zgf, att=att, lse=lse, gm=gm, mixed=mixed,
                          o=o, h1=h1, hn2=hn2, ab=ab, s=s, o2=o2, h2=h2, hr=hr, pe=pe, gl=gl, w_qkv=w_qkv, w_gf=w_gf,
                          w_out=w_out_f, w_fi=w_fi_f, w_fo=w_fo_f, w_pg=w_pg_f, b_pad=b_pad, bias_full=bias_full,
                          gain_v=gain_v))
        h, hn = h3, hn_next

    dh, loss_blk = _loss_head(h, target, "loss_head")
    loss = lax.psum(loss_blk[0, 0], ("x", "y", "c"))

    small = {nme: [None] * depth for nme in SMALL_ORDER}
    big_grads = {nme: [None] * depth for nme in big}
    for i in reversed(range(depth)):
        sv = saved[i]
        dgl, dpe, small["ple_norm"][i] = _ple_bwd(dh, sv["pe"], sv["gl"], row(ple_norm, i), "ple_bwd")
        dhr = _mm(dgl, sv["w_pg"], "nt", F32, "mm_d_hr")
        g_pg = _mm(sv["hr"], dgl, "tn", F32, "mm_dw_ple_gate").reshape(N_CHIPS, -1, d)
        g_ple = _mm(p[i, 0], dpe, "tn", F32, "mm_dw_ple", chip_split=True)
        dh2, do2, small["ffn_post_norm"][i] = _join(dh, sv["h2"], None, dhr, sv["o2"], row(ffn_post_norm, i), "join_ple")
        dab = _ffn_bwd(do2, sv["w_fo"], sv["ab"], "ffn_bwd")
        g_fo = _mm(sv["s"], do2, "tn", F32, "mm_dw_ffn_out", tm=256).reshape(N_CHIPS, -1, d)
        dhn2 = _mm_halves_nt(dab, sv["w_fi"], "mm_d_hn2")
        g_fi = _mm_halves_tn(sv["hn2"], dab, "mm_dw_ffn_in")
        dh1, small["ffn_pre_norm"][i], do, small["mix_post_norm"][i] = _join(
            dh2, sv["h1"], row(ffn_pre_norm, i), dhn2, sv["o"], row(mix_post_norm, i), "join_ffn")
        dmixed = _mm(do, sv["w_out"], "nt", F32, "mm_d_mixed")
        g_out = _mm(sv["mixed"], do, "tn", F32, "mm_dw_out").reshape(N_CHIPS, -1, d)
        datt, dgm, small["mix_out_norm"][i] = _mixnorm_bwd(sv["att"], sv["gm"], row(mix_out_norm, i), dmixed, "mixnorm_bwd")
        dg, small["gm_w_s"][i], dmix_sum, small["gm_v_norm"][i] = _gmlp_bwd(
            sv["zgf"], dgm, gm_w_s[i], sv["bias_full"], sv["gain_v"], mean_mat, "gmlp_bwd")
        small["gm_b_s"][i] = dmix_sum.reshape(CHUNK, N_HEADS, HEAD_DIM).sum(-1).T
        qb, doa = _attn_bwd_prep(sv["qa"], sv["att"], sv["lse"], datt, "attn_bwd_prep")
        dqa, dka, dva = _attn_bwd(qb, sv["ka"], sv["va"], doa, "attn_bwd")
        dqkv, dc = _attn_bwd_post(dqa, dka, dva, "attn_bwd_post")
        dfl, db = _gate_bwd(dc, sv["zgf"], sv["b_pad"], "gate_bwd")
        small["b_forget"][i] = db[0, :N_HEADS]
        dgf = jnp.concatenate([dg, dfl], axis=1)
        dhn = _mm(dqkv, sv["w_qkv"], "nt", F32, "mm_d_hn_qkv")
        dhn = _mm(dgf, sv["w_gf"], "nt", F32, "mm_d_hn_gf", add=dhn)
        g_qkv = _mm(sv["hn"], dqkv, "tn", F32, "mm_dw_qkv")
        g_gf = _mm(sv["hn"], dgf, "tn", F32, "mm_dw_gf", tn_cap=384)
        g_in = jnp.concatenate([g_qkv, g_gf[:, 2 * D_GM:2 * D_GM + N_HEADS], g_gf[:, :2 * D_GM]], axis=1)
        g_in = g_in.reshape(d, N_CHIPS, -1).transpose(1, 0, 2)
        dh, small["mix_pre_norm"][i] = _join(dh1, sv["h"], row(mix_pre_norm, i), dhn, None, None, "join_mix")
        reduced = _reduce_scatter([g_in, g_out, g_fi, g_fo, g_ple, g_pg], c_idx, place, BF16, "layer")
        for nme, g in zip(big, reduced):
            big_grads[nme][i] = g
    grad_x = dh.reshape(1, t, d)

    small_shapes = {nme: weights[nme].shape for nme in SMALL_ORDER}
    small_part = _pack_small({nme: jnp.stack([g.reshape(small_shapes[nme][1:]) for g in small[nme]])
                              for nme in SMALL_ORDER})
    rows_small = small_part.shape[0]
    small_slice = _reduce_scatter([small_part.reshape(N_CHIPS, rows_small // N_CHIPS, LANES)], c_idx, place, F32, "small")[0]
    small_all = _gather_chips([small_slice], "gather_small")[0].reshape(1, rows_small, LANES)
    sd, sm, sv_ = _adamw(_pack_small({n_: weights[n_] for n_ in SMALL_ORDER})[None], small_all,
                         _pack_small({n_: mom_m[n_] for n_ in SMALL_ORDER})[None],
                         _pack_small({n_: mom_v[n_] for n_ in SMALL_ORDER})[None], "adamw_small")
    grads = _unpack_small(small_all[0], small_shapes)
    deltas = _unpack_small(sd[0], small_shapes)
    new_m = _unpack_small(sm[0], small_shapes)
    new_v = _unpack_small(sv_[0], small_shapes)

    for nme in big:
        g = jnp.stack(big_grads[nme]).reshape(weights[nme].shape)
        grads[nme] = g
        deltas[nme], new_m[nme], new_v[nme] = _adamw(weights[nme], g, mom_m[nme], mom_v[nme], "adamw_" + nme)

    order = ("mix_pre_norm", "mix_post_norm", "w_in", "b_forget", "gm_v_norm", "gm_w_s", "gm_b_s", "mix_out_norm",
             "w_out", "ffn_pre_norm", "ffn_post_norm", "w_ffn_in", "w_ffn_out", "w_ple", "ple_norm", "w_ple_gate")
    return (loss, grad_x, *[grads[n_] for n_ in order], *[deltas[n_] for n_ in order], *[new_m[n_] for n_ in order],
            *[new_v[n_] for n_ in order])
```

```python
import functools

import jax
import jax.numpy as jnp
from jax import lax
from jax.experimental import pallas as pl
from jax.experimental.pallas import tpu as pltpu

F32 = jnp.float32
BF16 = jnp.bfloat16
MESH = pl.DeviceIdType.MESH
HIGHEST = lax.Precision.HIGHEST

EPS = 1e-6
NEG_INF = -1e30
N_HEADS = 8
HEAD_DIM = 64
D_ATT = N_HEADS * HEAD_DIM
D_GM = 512
CHUNK = 128
LANES = 128
N_CHIPS = 4
ADAM_LR = 0.001
ADAM_B1 = 0.9
ADAM_B2 = 0.999
ADAM_EPS = 1e-08
ADAM_WD = 0.01
ADAM_STEP = 10
VMEM_LIMIT = 56 * 1024 * 1024


def _cp(sem=None):
    return pltpu.CompilerParams(dimension_semantics=sem, vmem_limit_bytes=VMEM_LIMIT)


def _full(shape):
    return pl.BlockSpec(shape, lambda *_: (0,) * len(shape))


def _rows(tm, width, col_block=0):
    return pl.BlockSpec((tm, width), lambda i: (i, col_block))


def _dot(a, b, dims, precision=None):
    return lax.dot_general(a, b, (dims, ((), ())), preferred_element_type=F32, precision=precision)


NN = ((1,), (0,))
NT = ((1,), (1,))
TN = ((0,), (0,))


def _pick(n, cap):
    best = None
    for t in range(LANES, min(n, cap) + 1, LANES):
        if n % t == 0:
            best = t
    assert best is not None, (n, cap)
    return best


def _mm(a, b, mode, out_dtype, name, tm=512, tn_cap=1024, add=None, chip_split=False):
    dims = {"nn": NN, "nt": NT, "tn": TN}[mode]
    if mode == "tn":
        k, m = a.shape
    else:
        m, k = a.shape
    n = b.shape[0] if mode == "nt" else b.shape[1]
    tm = min(tm, m)
    tn = n // N_CHIPS if chip_split else _pick(n, tn_cap)
    assert m % tm == 0 and n % tn == 0

    def body(*refs):
        a_ref, b_ref = refs[0], refs[1]
        o_ref = refs[-1]
        acc = _dot(a_ref[...].astype(BF16), b_ref[...].astype(BF16), dims)
        if add is not None:
            acc = acc + refs[2][...]
        o_ref[...] = acc.astype(out_dtype)

    a_spec = pl.BlockSpec((k, tm), lambda i, j: (0, i)) if mode == "tn" else pl.BlockSpec((tm, k), lambda i, j: (i, 0))
    b_spec = pl.BlockSpec((tn, k), lambda i, j: (j, 0)) if mode == "nt" else pl.BlockSpec((k, tn), lambda i, j: (0, j))
    in_specs = [a_spec, b_spec]
    args = [a, b]
    if add is not None:
        in_specs.append(pl.BlockSpec((tm, tn), lambda i, j: (i, j)))
        args.append(add)
    if chip_split:
        out_shape = jax.ShapeDtypeStruct((N_CHIPS, m, tn), out_dtype)
        out_spec = pl.BlockSpec((None, tm, tn), lambda i, j: (j, i, 0))
    else:
        out_shape = jax.ShapeDtypeStruct((m, n), out_dtype)
        out_spec = pl.BlockSpec((tm, tn), lambda i, j: (i, j))
    return pl.pallas_call(body, out_shape=out_shape, grid=(m // tm, n // tn), in_specs=in_specs, out_specs=out_spec,
                          compiler_params=_cp(("arbitrary", "arbitrary")), name=name)(*args)


def _rms_inv(x):
    return lax.rsqrt(jnp.mean(x * x, axis=-1, keepdims=True) + EPS)


def _rms_bwd(x, gain, dy):
    inv = _rms_inv(x)
    xhat = x * inv
    dxn = dy if gain is None else dy * gain
    dx = inv * (dxn - xhat * jnp.mean(dxn * xhat, axis=-1, keepdims=True))
    return dx, dy * xhat


def _acc_rows(ref, val):
    s = jnp.sum(val, axis=0, keepdims=True)

    @pl.when(pl.program_id(0) == 0)
    def _():
        ref[...] = s

    @pl.when(pl.program_id(0) > 0)
    def _():
        ref[...] += s


def _norm_cast(h, gain, name, tm=512):
    t, d = h.shape

    def body(h_ref, g_ref, o_ref):
        x = h_ref[...]
        o_ref[...] = (x * _rms_inv(x) * g_ref[...]).astype(BF16)

    return pl.pallas_call(body, out_shape=jax.ShapeDtypeStruct((t, d), BF16), grid=(t // tm,),
                          in_specs=[_rows(tm, d), _full((1, d))], out_specs=_rows(tm, d),
                          compiler_params=_cp(("arbitrary",)), name=name)(h, gain)


def _resid_norm(h, o, g_post, g_next, name, tm=512):
    t, d = h.shape
    has_gain = g_next is not None

    def body(*refs):
        h_ref, o_ref, gp_ref = refs[:3]
        h1_ref, hn_ref = refs[-2:]
        ov = o_ref[...]
        h1 = h_ref[...] + ov * _rms_inv(ov) * gp_ref[...]
        h1_ref[...] = h1
        hn = h1 * _rms_inv(h1)
        if has_gain:
            hn = hn * refs[3][...]
        hn_ref[...] = hn.astype(BF16)

    args = [h, o, g_post] + ([g_next] if has_gain else [])
    in_specs = [_rows(tm, d), _rows(tm, d), _full((1, d))] + ([_full((1, d))] if has_gain else [])
    return pl.pallas_call(body, out_shape=(jax.ShapeDtypeStruct((t, d), F32), jax.ShapeDtypeStruct((t, d), BF16)),
                          grid=(t // tm,), in_specs=in_specs, out_specs=(_rows(tm, d), _rows(tm, d)),
                          compiler_params=_cp(("arbitrary",)), name=name)(*args)


def _ple_fwd(h2, pe, gl, g_ple, g_next, name, tm=512):
    t, d = h2.shape

    def body(h_ref, pe_ref, gl_ref, gp_ref, gn_ref, h3_ref, hn_ref):
        pv = pe_ref[...]
        e = pv * _rms_inv(pv) * gp_ref[...]
        h3 = h_ref[...] + jax.nn.sigmoid(gl_ref[...]) * e
        h3_ref[...] = h3
        hn_ref[...] = (h3 * _rms_inv(h3) * gn_ref[...]).astype(BF16)

    return pl.pallas_call(body, out_shape=(jax.ShapeDtypeStruct((t, d), F32), jax.ShapeDtypeStruct((t, d), BF16)),
                          grid=(t // tm,), in_specs=[_rows(tm, d)] * 3 + [_full((1, d))] * 2,
                          out_specs=(_rows(tm, d), _rows(tm, d)), compiler_params=_cp(("arbitrary",)),
                          name=name)(h2, pe, gl, g_ple, g_next)


def _loss_head(y, target, name, tm=512):
    t, d = y.shape

    def body(y_ref, t_ref, dy_ref, loss_ref):
        diff = y_ref[...] - t_ref[...]
        dy_ref[...] = diff * (1.0 / d)
        part = 0.5 * jnp.sum(jnp.mean(diff * diff, axis=-1, keepdims=True), axis=0, keepdims=True)
        part = jnp.broadcast_to(part, (8, LANES))

        @pl.when(pl.program_id(0) == 0)
        def _():
            loss_ref[...] = part

        @pl.when(pl.program_id(0) > 0)
        def _():
            loss_ref[...] += part

    return pl.pallas_call(body, out_shape=(jax.ShapeDtypeStruct((t, d), F32), jax.ShapeDtypeStruct((8, LANES), F32)),
                          grid=(t // tm,), in_specs=[_rows(tm, d)] * 2, out_specs=(_rows(tm, d), _full((8, LANES))),
                          compiler_params=_cp(("arbitrary",)), name=name)(y, target)


def _ple_bwd(dh3, pe, gl, g_ple, name, tm=512):
    t, d = dh3.shape

    def body(dh_ref, pe_ref, gl_ref, gp_ref, dgl_ref, dpe_ref, dg_ref):
        dh = dh_ref[...]
        pv = pe_ref[...]
        gp = gp_ref[...]
        gate = jax.nn.sigmoid(gl_ref[...])
        e = pv * _rms_inv(pv) * gp
        dgl_ref[...] = (dh * e * gate * (1.0 - gate)).astype(BF16)
        dpe, dg_rows = _rms_bwd(pv, gp, dh * gate)
        dpe_ref[...] = dpe.astype(BF16)
        _acc_rows(dg_ref, dg_rows)

    return pl.pallas_call(body, out_shape=(jax.ShapeDtypeStruct((t, d), BF16), jax.ShapeDtypeStruct((t, d), BF16),
                                           jax.ShapeDtypeStruct((1, d), F32)),
                          grid=(t // tm,), in_specs=[_rows(tm, d)] * 3 + [_full((1, d))],
                          out_specs=(_rows(tm, d), _rows(tm, d), _full((1, d))),
                          compiler_params=_cp(("arbitrary",)), name=name)(dh3, pe, gl, g_ple)


def _join(d_res, x_a, gain_a, d_a, x_b, gain_b, name, tm=512):
    t, d = d_res.shape
    has_ga = gain_a is not None
    has_b = x_b is not None

    def body(*refs):
        it = iter(refs)
        dres_ref, xa_ref = next(it), next(it)
        ga_ref = next(it) if has_ga else None
        da_ref = next(it)
        xb_ref, gb_ref = (next(it), next(it)) if has_b else (None, None)
        dout_ref = next(it)
        dga_ref = next(it) if has_ga else None
        db_ref, dgb_ref = (next(it), next(it)) if has_b else (None, None)
        dx, dg_rows = _rms_bwd(xa_ref[...], ga_ref[...] if has_ga else None, da_ref[...])
        dout = dres_ref[...] + dx
        dout_ref[...] = dout
        if has_ga:
            _acc_rows(dga_ref, dg_rows)
        if has_b:
            db, dgb_rows = _rms_bwd(xb_ref[...], gb_ref[...], dout)
            db_ref[...] = db.astype(BF16)
            _acc_rows(dgb_ref, dgb_rows)

    args, in_specs = [d_res, x_a], [_rows(tm, d), _rows(tm, d)]
    if has_ga:
        args.append(gain_a)
        in_specs.append(_full((1, d)))
    args.append(d_a)
    in_specs.append(_rows(tm, d))
    if has_b:
        args += [x_b, gain_b]
        in_specs += [_rows(tm, d), _full((1, d))]
    out_shape, out_specs = [jax.ShapeDtypeStruct((t, d), F32)], [_rows(tm, d)]
    if has_ga:
        out_shape.append(jax.ShapeDtypeStruct((1, d), F32))
        out_specs.append(_full((1, d)))
    if has_b:
        out_shape += [jax.ShapeDtypeStruct((t, d), BF16), jax.ShapeDtypeStruct((1, d), F32)]
        out_specs += [_rows(tm, d), _full((1, d))]
    return pl.pallas_call(body, out_shape=tuple(out_shape), grid=(t // tm,), in_specs=in_specs,
                          out_specs=tuple(out_specs), compiler_params=_cp(("arbitrary",)), name=name)(*args)


def _ffn_in(hn, w, name, tm=1024, tn=256):
    t, d = hn.shape
    f = w.shape[1] // 2
    tm = min(tm, t)
    nj = f // tn

    def body(h_ref, wa_ref, wb_ref, s_ref, ab_ref):
        h = h_ref[...]
        a = _dot(h, wa_ref[...], NN)
        b = _dot(h, wb_ref[...], NN)
        ab_ref[0] = a.astype(BF16)
        ab_ref[1] = b.astype(BF16)
        s_ref[...] = (a * jax.nn.sigmoid(a) * b).astype(BF16)

    return pl.pallas_call(
        body, out_shape=(jax.ShapeDtypeStruct((t, f), BF16), jax.ShapeDtypeStruct((2, t, f), BF16)), grid=(t // tm, nj),
        in_specs=[pl.BlockSpec((tm, d), lambda i, j: (i, 0)), pl.BlockSpec((d, tn), lambda i, j: (0, j)),
                  pl.BlockSpec((d, tn), lambda i, j: (0, nj + j))],
        out_specs=(pl.BlockSpec((tm, tn), lambda i, j: (i, j)), pl.BlockSpec((2, tm, tn), lambda i, j: (0, i, j))),
        compiler_params=_cp(("arbitrary",) * 2), name=name)(hn, w, w)


def _ffn_bwd(do, w_out, ab, name, tm=1024, tn=256):
    t, d = do.shape
    f = w_out.shape[0]
    tm = min(tm, t)

    def body(d_ref, w_ref, ab_ref, o_ref):
        ds = _dot(d_ref[...], w_ref[...], NT)
        a = ab_ref[0].astype(F32)
        sg = jax.nn.sigmoid(a)
        silu = a * sg
        o_ref[0] = (ds * ab_ref[1].astype(F32) * (sg + silu * (1.0 - sg))).astype(BF16)
        o_ref[1] = (ds * silu).astype(BF16)

    both = pl.BlockSpec((2, tm, tn), lambda i, j: (0, i, j))
    return pl.pallas_call(
        body, out_shape=jax.ShapeDtypeStruct((2, t, f), BF16), grid=(t // tm, f // tn),
        in_specs=[pl.BlockSpec((tm, d), lambda i, j: (i, 0)), pl.BlockSpec((tn, d), lambda i, j: (j, 0)), both],
        out_specs=both, compiler_params=_cp(("arbitrary",) * 2), name=name)(do, w_out, ab)


def _mm_halves_nt(x2, w, name, tm=1024, tn=512):
    _, t, f = x2.shape
    n = w.shape[0]
    tm = min(tm, t)

    def body(x_ref, w_ref, o_ref, acc):
        part = _dot(x_ref[...], w_ref[...], NT)

        @pl.when(pl.program_id(2) == 0)
        def _():
            acc[...] = part

        @pl.when(pl.program_id(2) == 1)
        def _():
            o_ref[...] = acc[...] + part

    return pl.pallas_call(
        body, out_shape=jax.ShapeDtypeStruct((t, n), F32), grid=(t // tm, n // tn, 2),
        in_specs=[pl.BlockSpec((None, tm, f), lambda i, j, k: (k, i, 0)), pl.BlockSpec((tn, f), lambda i, j, k: (j, k))],
        out_specs=pl.BlockSpec((tm, tn), lambda i, j, k: (i, j)), scratch_shapes=[pltpu.VMEM((tm, tn), F32)],
        compiler_params=_cp(("arbitrary",) * 3), name=name)(x2, w)


def _mm_halves_tn(h, x2, name, tm=256):
    t, d = h.shape
    f = x2.shape[2]
    tn = 2 * f // N_CHIPS

    def body(h_ref, x_ref, o_ref):
        o_ref[...] = _dot(h_ref[...], x_ref[...], TN)

    return pl.pallas_call(
        body, out_shape=jax.ShapeDtypeStruct((N_CHIPS, d, tn), F32), grid=(d // tm, N_CHIPS),
        in_specs=[pl.BlockSpec((t, tm), lambda i, j: (0, i)), pl.BlockSpec((None, t, tn), lambda i, j: (j // 2, 0, j % 2))],
        out_specs=pl.BlockSpec((None, tm, tn), lambda i, j: (j, i, 0)),
        compiler_params=_cp(("arbitrary",) * 2), name=name)(h, x2)


def _mixnorm_fwd(att, gm, g_out, name, tm=512):
    t, w = att.shape

    def body(a_ref, m_ref, g_ref, o_ref):
        a, m, g = a_ref[...], m_ref[...], g_ref[...]
        o_ref[:, :w] = (a * _rms_inv(a) * g[:, :w]).astype(BF16)
        o_ref[:, w:] = (m * _rms_inv(m) * g[:, w:]).astype(BF16)

    return pl.pallas_call(body, out_shape=jax.ShapeDtypeStruct((t, 2 * w), BF16), grid=(t // tm,),
                          in_specs=[_rows(tm, w), _rows(tm, w), _full((1, 2 * w))], out_specs=_rows(tm, 2 * w),
                          compiler_params=_cp(("arbitrary",)), name=name)(att, gm, g_out)


def _mixnorm_bwd(att, gm, g_out, dmixed, name, tm=512):
    t, w = att.shape

    def body(a_ref, m_ref, g_ref, d_ref, da_ref, dm_ref, dg_ref):
        g, d = g_ref[...], d_ref[...]
        da, dga = _rms_bwd(a_ref[...], g[:, :w], d[:, :w])
        dm, dgm = _rms_bwd(m_ref[...], g[:, w:], d[:, w:])
        da_ref[...] = da
        dm_ref[...] = dm
        _acc_rows(dg_ref, jnp.concatenate([dga, dgm], axis=1))

    return pl.pallas_call(body, out_shape=(jax.ShapeDtypeStruct((t, w), F32), jax.ShapeDtypeStruct((t, w), F32),
                                           jax.ShapeDtypeStruct((1, 2 * w), F32)),
                          grid=(t // tm,), in_specs=[_rows(tm, w), _rows(tm, w), _full((1, 2 * w)), _rows(tm, 2 * w)],
                          out_specs=(_rows(tm, w), _rows(tm, w), _full((1, 2 * w))),
                          compiler_params=_cp(("arbitrary",)), name=name)(att, gm, g_out, dmixed)


SCAN_BLOCK = 256


def _gate_fwd(zgf, b_pad, name):
    t = zgf.shape[0]
    fcol = zgf.shape[1] // LANES - 1
    nb = t // SCAN_BLOCK

    def body(f_ref, b_ref, c_ref):
        r = lax.broadcasted_iota(jnp.int32, (SCAN_BLOCK, SCAN_BLOCK), 0)
        s = lax.broadcasted_iota(jnp.int32, (SCAN_BLOCK, SCAN_BLOCK), 1)
        tril = (r >= s).astype(F32)
        head = lax.broadcasted_iota(jnp.int32, (SCAN_BLOCK, LANES), 1) < N_HEADS
        carry = jnp.zeros((1, LANES), F32)
        for blk in range(nb):
            rows = pl.ds(blk * SCAN_BLOCK, SCAN_BLOCK)
            x = f_ref[rows, :] + b_ref[...]
            lf = jnp.minimum(x, 0.0) - jnp.log1p(jnp.exp(-jnp.abs(x)))
            lf = jnp.where(head, lf, 0.0)
            cs = _dot(tril, lf, NN, HIGHEST) + carry
            c_ref[rows, :] = cs
            carry = carry + jnp.sum(lf, axis=0, keepdims=True)

    return pl.pallas_call(body, out_shape=jax.ShapeDtypeStruct((t, LANES), F32),
                          grid=(1,), in_specs=[pl.BlockSpec((t, LANES), lambda i: (0, fcol)), _full((1, LANES))],
                          out_specs=_full((t, LANES)), compiler_params=_cp(("arbitrary",)),
                          name=name)(zgf, b_pad)


def _gate_bwd(dc, zgf, b_pad, name):
    t = zgf.shape[0]
    fcol = zgf.shape[1] // LANES - 1
    nb = t // SCAN_BLOCK

    def body(dc_ref, f_ref, b_ref, dfl_ref, db_ref):
        r = lax.broadcasted_iota(jnp.int32, (SCAN_BLOCK, SCAN_BLOCK), 0)
        s = lax.broadcasted_iota(jnp.int32, (SCAN_BLOCK, SCAN_BLOCK), 1)
        triu = (s >= r).astype(F32)
        head = lax.broadcasted_iota(jnp.int32, (SCAN_BLOCK, LANES), 1) < N_HEADS
        carry = jnp.zeros((1, LANES), F32)
        db = jnp.zeros((1, LANES), F32)
        for blk in reversed(range(nb)):
            rows = pl.ds(blk * SCAN_BLOCK, SCAN_BLOCK)
            dc = dc_ref[rows, :]
            dlf = _dot(triu, dc, NN, HIGHEST) + carry
            x = f_ref[rows, :] + b_ref[...]
            dfl = jnp.where(head, dlf * jax.nn.sigmoid(-x), 0.0)
            dfl_ref[rows, :] = dfl.astype(BF16)
            db = db + jnp.sum(dfl, axis=0, keepdims=True)
            carry = carry + jnp.sum(dc, axis=0, keepdims=True)
        db_ref[...] = db

    return pl.pallas_call(body, out_shape=(jax.ShapeDtypeStruct((t, LANES), BF16), jax.ShapeDtypeStruct((1, LANES), F32)),
                          grid=(1,), in_specs=[_full((t, LANES)), pl.BlockSpec((t, LANES), lambda i: (0, fcol)),
                                               _full((1, LANES))],
                          out_specs=(_full((t, LANES)), _full((1, LANES))), compiler_params=_cp(("arbitrary",)),
                          name=name)(dc, zgf, b_pad)


ATT_BLOCK = 512
PAIRS = N_HEADS // 2
CQ_LANE = HEAD_DIM
CK_LANE = HEAD_DIM + 3
LSE_LANE = HEAD_DIM + 6


def _pick_col(x, idx):
    lane = lax.broadcasted_iota(jnp.int32, x.shape, 1)
    return jnp.sum(jnp.where(lane == idx, x, 0.0), axis=1, keepdims=True)


def _split3(x):
    hi = x.astype(BF16)
    r1 = x - hi.astype(F32)
    mid = r1.astype(BF16)
    lo = (r1 - mid.astype(F32)).astype(BF16)
    return hi, mid, lo


def _lanes_put(base, lane, start, vals):
    out = base
    for n, v in enumerate(vals):
        out = jnp.where(lane == start + n, v, out)
    return out


def _to_first_half(x, hh):
    return x if hh == 0 else pltpu.roll(x, HEAD_DIM, 1)


def _attn_prep(qkv, c, name, tm=512):
    t = qkv.shape[0]
    tm = min(tm, t)

    def body(q_ref, k_ref, v_ref, c_ref, qa_ref, ka_ref, va_ref):
        j = pl.program_id(1)
        lane = lax.broadcasted_iota(jnp.int32, (tm, LANES), 1)
        first = lane < HEAD_DIM
        q2, k2, v2 = q_ref[...].astype(F32), k_ref[...].astype(F32), v_ref[...].astype(F32)
        cc = c_ref[...]
        one = jnp.ones((tm, 1), F32)
        for hh in range(2):
            chi, cmid, clo = [v.astype(F32) for v in _split3(_pick_col(cc, 2 * j + hh))]
            qh = jnp.where(first, _to_first_half(q2, hh) * (HEAD_DIM ** -0.5), 0.0)
            kh = jnp.where(first, _to_first_half(k2, hh), 0.0)
            vh = jnp.where(first, _to_first_half(v2, hh), 0.0)
            qa = _lanes_put(qh, lane, CQ_LANE, [chi, cmid, clo, one, one, one])
            ka = _lanes_put(kh, lane, CQ_LANE, [one, one, one, -chi, -cmid, -clo, one, one, one])
            va = _lanes_put(vh, lane, CQ_LANE, [one, one, one])
            cols = slice(hh * LANES, (hh + 1) * LANES)
            qa_ref[:, cols] = qa.astype(BF16)
            ka_ref[:, cols] = ka.astype(BF16)
            va_ref[:, cols] = va.astype(BF16)

    blk = lambda off: pl.BlockSpec((tm, LANES), lambda i, j: (i, off + j))
    out = pl.BlockSpec((tm, 2 * LANES), lambda i, j: (i, j))
    shp = jax.ShapeDtypeStruct((t, N_HEADS * LANES), BF16)
    return pl.pallas_call(body, out_shape=(shp, shp, shp), grid=(t // tm, PAIRS),
                          in_specs=[blk(0), blk(PAIRS), blk(2 * PAIRS), pl.BlockSpec((tm, LANES), lambda i, j: (i, 0))],
                          out_specs=(out, out, out), compiler_params=_cp(("arbitrary",) * 2), name=name)(qkv, qkv, qkv, c)


def _causal_block(tb):
    return lax.broadcasted_iota(jnp.int32, (tb, tb), 0) >= lax.broadcasted_iota(jnp.int32, (tb, tb), 1)


def _causal_pairs(nb, key_major):
    pairs = [(q, k) for q in range(nb) for k in range(q + 1)]
    if key_major:
        pairs.sort(key=lambda qk: (qk[1], qk[0]))
    return (jnp.array([q for q, _ in pairs], jnp.int32), jnp.array([k for _, k in pairs], jnp.int32))


def _pair_grid_call(body, tables, arrays, in_specs, out_shapes, out_specs, scratch, side, name):
    n_steps = tables[0].shape[0]
    n_in, n_out = len(arrays), len(out_shapes)
    hosted = body
    if side is not None:
        s_in, s_out = len(side.arrays), len(side.out_shapes)

        def hosted(*refs):
            ins_end = 2 + n_in + s_in
            side_in, side_out = refs[2 + n_in:ins_end], refs[ins_end + n_out:ins_end + n_out + s_out]
            send, recv = refs[-2:]
            j, s = pl.program_id(0), pl.program_id(1)

            @pl.when((j == 0) & (s == 0))
            def _():
                side.start(side_in, side_out, send, recv)

            body(*refs[:2 + n_in], *refs[ins_end:ins_end + n_out], *refs[ins_end + n_out + s_out:-2])

            @pl.when((j == PAIRS - 1) & (s == n_steps - 1))
            def _():
                side.wait(side_in, side_out, send, recv)

        arrays = tuple(arrays) + tuple(side.arrays)
        in_specs = list(in_specs) + [HBM_SPEC] * s_in
        out_shapes = list(out_shapes) + side.out_shapes
        out_specs = list(out_specs) + [HBM_SPEC] * s_out
        scratch = list(scratch) + side.scratch
    grid_spec = pltpu.PrefetchScalarGridSpec(num_scalar_prefetch=2, grid=(PAIRS, n_steps), in_specs=in_specs,
                                             out_specs=tuple(out_specs), scratch_shapes=scratch)
    out = pl.pallas_call(hosted, out_shape=tuple(out_shapes), grid_spec=grid_spec,
                         compiler_params=_cp(("arbitrary",) * 2), name=name)(*tables, *arrays)
    return out[:n_out], out[n_out:]


def _attn_fwd(qa, ka, va, name, side=None):
    t = qa.shape[0]
    tb = min(ATT_BLOCK, t)
    q_tab, k_tab = _causal_pairs(t // tb, key_major=False)

    def body(q_tab_ref, k_tab_ref, q_ref, k_ref, v_ref, o_ref, lse_ref, m_s, acc_s):
        qi, kb = q_tab_ref[pl.program_id(1)], k_tab_ref[pl.program_id(1)]

        @pl.when(kb == 0)
        def _():
            m_s[...] = jnp.full(m_s.shape, NEG_INF, F32)
            acc_s[...] = jnp.zeros(acc_s.shape, F32)

        def step(diagonal):
            for hh in range(2):
                cols = slice(hh * LANES, (hh + 1) * LANES)
                sc = _dot(q_ref[:, cols], k_ref[:, cols], NT)
                if diagonal:
                    sc = jnp.where(_causal_block(tb), sc, NEG_INF)
                m_prev = m_s[hh]
                m_new = jnp.maximum(m_prev, jnp.max(sc, axis=1, keepdims=True))
                p = jnp.exp(sc - m_new)
                acc_s[hh] = jnp.exp(m_prev - m_new) * acc_s[hh] + _dot(p.astype(BF16), v_ref[:, cols], NN)
                m_s[hh] = m_new

        @pl.when(kb < qi)
        def _():
            step(False)

        @pl.when(kb == qi)
        def _():
            step(True)
            lane = lax.broadcasted_iota(jnp.int32, (tb, LANES), 1)
            outs, lses = [], []
            for hh in range(2):
                acc = acc_s[hh]
                l = _pick_col(acc, CQ_LANE)
                outs.append(acc / l)
                lses.append(m_s[hh] + jnp.log(l))
            o_ref[...] = jnp.where(lane < HEAD_DIM, outs[0], pltpu.roll(outs[1], HEAD_DIM, 1))
            lse_ref[...] = jnp.where(lane == 0, lses[0], jnp.where(lane == 1, lses[1], 0.0))

    qrow = lambda j, s, qt, kt: (qt[s], j)
    krow = lambda j, s, qt, kt: (kt[s], j)
    return _pair_grid_call(
        body, (q_tab, k_tab), (qa, ka, va),
        [pl.BlockSpec((tb, 2 * LANES), qrow), pl.BlockSpec((tb, 2 * LANES), krow), pl.BlockSpec((tb, 2 * LANES), krow)],
        [jax.ShapeDtypeStruct((t, D_ATT), F32), jax.ShapeDtypeStruct((t, PAIRS * LANES), F32)],
        [pl.BlockSpec((tb, LANES), qrow), pl.BlockSpec((tb, LANES), qrow)],
        [pltpu.VMEM((2, tb, 1), F32), pltpu.VMEM((2, tb, LANES), F32)], side, name)


def _attn_bwd_prep(qa, att, lse, datt, name, tm=512):
    t = qa.shape[0]
    tm = min(tm, t)

    def body(qa_ref, o_ref, lse_ref, do_ref, qb_ref, doa_ref):
        j = pl.program_id(1)
        lane = lax.broadcasted_iota(jnp.int32, (tm, LANES), 1)
        first = lane < HEAD_DIM
        do = do_ref[...]
        prod = do * o_ref[...]
        lse2 = lse_ref[...]
        for hh in range(2):
            cols = slice(hh * LANES, (hh + 1) * LANES)
            delta = jnp.sum(jnp.where(first if hh == 0 else ~first, prod, 0.0), axis=1, keepdims=True)
            doh = jnp.where(first, _to_first_half(do, hh), 0.0)
            doa_ref[:, cols] = _lanes_put(doh, lane, CQ_LANE, [v.astype(F32) for v in _split3(-delta)]).astype(BF16)
            nl = [v.astype(F32) for v in _split3(-_pick_col(lse2, hh))]
            qb_ref[:, cols] = _lanes_put(qa_ref[:, cols].astype(F32), lane, LSE_LANE, nl).astype(BF16)

    wide = pl.BlockSpec((tm, 2 * LANES), lambda i, j: (i, j))
    pair = pl.BlockSpec((tm, LANES), lambda i, j: (i, j))
    shp = jax.ShapeDtypeStruct((t, N_HEADS * LANES), BF16)
    return pl.pallas_call(body, out_shape=(shp, shp), grid=(t // tm, PAIRS), in_specs=[wide, pair, pair, pair],
                          out_specs=(wide, wide), compiler_params=_cp(("arbitrary",) * 2), name=name)(qa, att, lse, datt)


def _attn_bwd(qb, ka, va, doa, name, side=None):
    t = qb.shape[0]
    tb = min(ATT_BLOCK, t)
    nb = t // tb
    q_tab, k_tab = _causal_pairs(nb, key_major=True)

    def body(q_tab_ref, k_tab_ref, q_ref, k_ref, v_ref, do_ref, dq_ref, dk_ref, dv_ref, dk_s, dv_s):
        qi, kb = q_tab_ref[pl.program_id(1)], k_tab_ref[pl.program_id(1)]

        @pl.when(qi == kb)
        def _():
            dk_s[...] = jnp.zeros(dk_s.shape, F32)
            dv_s[...] = jnp.zeros(dv_s.shape, F32)

        def step(diagonal):
            rows = pl.ds(pl.multiple_of(qi * tb, tb), tb)
            for hh in range(2):
                cols = slice(hh * LANES, (hh + 1) * LANES)
                q, k, do = q_ref[:, cols], k_ref[:, cols], do_ref[:, cols]
                sc = _dot(q, k, NT)
                if diagonal:
                    sc = jnp.where(_causal_block(tb), sc, NEG_INF)
                p = jnp.exp(sc)
                ds = (p * _dot(do, v_ref[:, cols], NT)).astype(BF16)
                dv_s[:, cols] += _dot(p.astype(BF16), do, TN)
                dk_s[:, cols] += _dot(ds, q, TN)
                dq_new = _dot(ds, k, NN)

                @pl.when(kb == 0)
                def _():
                    dq_ref[rows, cols] = dq_new

                @pl.when(kb > 0)
                def _():
                    dq_ref[rows, cols] += dq_new

        @pl.when(qi == kb)
        def _():
            step(True)

        @pl.when(qi > kb)
        def _():
            step(False)

        @pl.when(qi == nb - 1)
        def _():
            dk_ref[...] = dk_s[...]
            dv_ref[...] = dv_s[...].astype(BF16)

    qrow = lambda j, s, qt, kt: (qt[s], j)
    krow = lambda j, s, qt, kt: (kt[s], j)
    blk = (tb, 2 * LANES)
    wide = (t, N_HEADS * LANES)
    return _pair_grid_call(
        body, (q_tab, k_tab), (qb, ka, va, doa),
        [pl.BlockSpec(blk, qrow), pl.BlockSpec(blk, krow), pl.BlockSpec(blk, krow), pl.BlockSpec(blk, qrow)],
        [jax.ShapeDtypeStruct(wide, F32), jax.ShapeDtypeStruct(wide, F32), jax.ShapeDtypeStruct(wide, BF16)],
        [pl.BlockSpec((t, 2 * LANES), lambda j, s, qt, kt: (0, j)), pl.BlockSpec(blk, krow), pl.BlockSpec(blk, krow)],
        [pltpu.VMEM(blk, F32), pltpu.VMEM(blk, F32)], side, name)


def _attn_bwd_post(dqa, dka, dva, name, tm=256):
    t = dqa.shape[0]
    tm = min(tm, t)

    def body(dq_ref, dk_ref, dv_ref, o_ref, dc_ref):
        lane = lax.broadcasted_iota(jnp.int32, (tm, LANES), 1)
        first = lane < HEAD_DIM
        dc = jnp.zeros((tm, LANES), F32)
        for j in range(PAIRS):
            packed = []
            for ref, gain in ((dq_ref, HEAD_DIM ** -0.5), (dk_ref, 1.0), (dv_ref, 1.0)):
                even = ref[:, 2 * j * LANES:(2 * j + 1) * LANES].astype(F32)
                odd = ref[:, (2 * j + 1) * LANES:(2 * j + 2) * LANES].astype(F32)
                packed.append((jnp.where(first, even, pltpu.roll(odd, HEAD_DIM, 1)) * gain).astype(BF16))
                if ref is dq_ref:
                    dc = dc + jnp.where(lane == 2 * j, _pick_col(even, CQ_LANE), 0.0)
                    dc = dc + jnp.where(lane == 2 * j + 1, _pick_col(odd, CQ_LANE), 0.0)
                if ref is dk_ref:
                    dc = dc - jnp.where(lane == 2 * j, _pick_col(even, CK_LANE), 0.0)
                    dc = dc - jnp.where(lane == 2 * j + 1, _pick_col(odd, CK_LANE), 0.0)
            for part, val in enumerate(packed):
                o_ref[:, (part * PAIRS + j) * LANES:(part * PAIRS + j + 1) * LANES] = val
        dc_ref[...] = dc

    wide = _rows(tm, N_HEADS * LANES)
    return pl.pallas_call(body, out_shape=(jax.ShapeDtypeStruct((t, 3 * D_ATT), BF16), jax.ShapeDtypeStruct((t, LANES), F32)),
                          grid=(t // tm,), in_specs=[wide, wide, wide], out_specs=(_rows(tm, 3 * D_ATT), _rows(tm, LANES)),
                          compiler_params=_cp(("arbitrary",)), name=name)(dqa, dka, dva)


GELU_K = 0.7978845608028654
GELU_A = 0.044715


def _gelu(x):
    th = jnp.tanh(GELU_K * (x + GELU_A * x * x * x))
    return 0.5 * x * (1.0 + th), th


def _group_mean_matrix():
    r = jnp.arange(D_GM)[:, None] // HEAD_DIM
    s = jnp.arange(D_GM)[None, :] // HEAD_DIM
    return jnp.where(r == s, 1.0 / HEAD_DIM, 0.0).astype(BF16)


def _group_mean(x, mean_mat):
    hi = x.astype(BF16)
    lo = (x - hi.astype(F32)).astype(BF16)
    return _dot(hi, mean_mat, NN) + _dot(lo, mean_mat, NN)


def _gm_forward_parts(g, w_ref, bias, gain, mean_mat):
    gel, _ = _gelu(g)
    u, vv = gel[:, :D_GM], gel[:, D_GM:]
    mu = _group_mean(vv, mean_mat)
    d = vv - mu
    rstd = lax.rsqrt(_group_mean(d * d, mean_mat) + EPS)
    xhat = d * rstd
    vn = (xhat * gain).astype(BF16)
    first = lax.broadcasted_iota(jnp.int32, (CHUNK, LANES), 1) < HEAD_DIM
    tri = lax.broadcasted_iota(jnp.int32, (CHUNK, CHUNK), 0) >= lax.broadcasted_iota(jnp.int32, (CHUNK, CHUNK), 1)
    parts = []
    for jp in range(D_GM // LANES):
        vp = vn[:, jp * LANES:(jp + 1) * LANES]
        wa = jnp.where(tri, w_ref[2 * jp], 0.0).astype(BF16)
        wb = jnp.where(tri, w_ref[2 * jp + 1], 0.0).astype(BF16)
        parts.append(jnp.where(first, _dot(wa, vp, NN), _dot(wb, vp, NN)))
    mixed = jnp.concatenate(parts, axis=1) + bias
    return u, xhat, rstd, vn, mixed


def _gmlp_fwd(zgf, w_s, bias_full, gain, mean_mat, name):
    t = zgf.shape[0]

    def body(g_ref, w_ref, b_ref, gain_ref, mm_ref, o_ref):
        u, _, _, _, mixed = _gm_forward_parts(g_ref[...], w_ref, b_ref[...], gain_ref[...], mm_ref[...])
        o_ref[...] = u * mixed

    return pl.pallas_call(body, out_shape=jax.ShapeDtypeStruct((t, D_GM), F32), grid=(t // CHUNK,),
                          in_specs=[_rows(CHUNK, 2 * D_GM), _full(w_s.shape), _full((CHUNK, D_GM)), _full((1, D_GM)),
                                    _full((D_GM, D_GM))],
                          out_specs=_rows(CHUNK, D_GM), compiler_params=_cp(("arbitrary",)),
                          name=name)(zgf, w_s, bias_full, gain, mean_mat)


def _gmlp_bwd(zgf, dgm, w_s, bias_full, gain, mean_mat, name):
    t = zgf.shape[0]

    def body(g_ref, d_ref, w_ref, b_ref, gain_ref, mm_ref, dg_ref, dw_ref, dmix_ref, dgain_ref):
        g, gain, mean_mat = g_ref[...], gain_ref[...], mm_ref[...]
        u, xhat, rstd, vn, mixed = _gm_forward_parts(g, w_ref, b_ref[...], gain, mean_mat)
        dgm_v = d_ref[...]
        du = dgm_v * mixed
        dmixed = dgm_v * u
        dm_b = dmixed.astype(BF16)
        first = lax.broadcasted_iota(jnp.int32, (CHUNK, LANES), 1) < HEAD_DIM
        tri = lax.broadcasted_iota(jnp.int32, (CHUNK, CHUNK), 0) >= lax.broadcasted_iota(jnp.int32, (CHUNK, CHUNK), 1)

        @pl.when(pl.program_id(0) == 0)
        def _():
            dw_ref[...] = jnp.zeros(dw_ref.shape, F32)
            dmix_ref[...] = jnp.zeros(dmix_ref.shape, F32)
            dgain_ref[...] = jnp.zeros(dgain_ref.shape, F32)

        dvn_parts = []
        for jp in range(D_GM // LANES):
            vp = vn[:, jp * LANES:(jp + 1) * LANES]
            dmp = dm_b[:, jp * LANES:(jp + 1) * LANES]
            halves = []
            for hh in range(2):
                sel = first if hh == 0 else ~first
                dw = jnp.where(tri, _dot(jnp.where(sel, dmp, jnp.zeros_like(dmp)), vp, NT), 0.0)
                grp = 2 * jp + hh
                dw_ref[grp] += dw
                wm = jnp.where(tri, w_ref[grp], 0.0).astype(BF16)
                halves.append(_dot(wm, dmp, TN))
            dvn_parts.append(jnp.where(first, halves[0], halves[1]))
        dvn = jnp.concatenate(dvn_parts, axis=1)
        dmix_ref[...] += dmixed
        dgain_ref[...] += jnp.sum(dvn * xhat, axis=0, keepdims=True)
        dxhat = dvn * gain
        m1 = _group_mean(dxhat, mean_mat)
        m2 = _group_mean(dxhat * xhat, mean_mat)
        dvv = rstd * (dxhat - m1 - xhat * m2)
        gel, th = _gelu(g)
        dgel = 0.5 * (1.0 + th) + 0.5 * g * (1.0 - th * th) * GELU_K * (1.0 + 3.0 * GELU_A * g * g)
        dg_ref[...] = (jnp.concatenate([du, dvv], axis=1) * dgel).astype(BF16)

    return pl.pallas_call(
        body, out_shape=(jax.ShapeDtypeStruct((t, 2 * D_GM), BF16), jax.ShapeDtypeStruct(w_s.shape, F32),
                         jax.ShapeDtypeStruct((CHUNK, D_GM), F32), jax.ShapeDtypeStruct((1, D_GM), F32)),
        grid=(t // CHUNK,),
        in_specs=[_rows(CHUNK, 2 * D_GM), _rows(CHUNK, D_GM), _full(w_s.shape), _full((CHUNK, D_GM)), _full((1, D_GM)),
                  _full((D_GM, D_GM))],
        out_specs=(_rows(CHUNK, 2 * D_GM), _full(w_s.shape), _full((CHUNK, D_GM)), _full((1, D_GM))),
        compiler_params=_cp(("arbitrary",)), name=name)(zgf, dgm, w_s, bias_full, gain, mean_mat)


def _row_tile(r, c, budget=1 << 19):
    best = None
    for tr in range(8, r + 1, 8):
        if r % tr == 0 and tr * c <= budget:
            best = tr
    return best if best is not None else r


def _adamw(w, g, m, v, name):
    nl, r, c = w.shape
    tr = _row_tile(r, c, 1 << 18)
    c1 = 1.0 - ADAM_B1 ** ADAM_STEP
    c2 = 1.0 - ADAM_B2 ** ADAM_STEP

    def body(w_ref, g_ref, m_ref, v_ref, d_ref, mo_ref, vo_ref):
        gv = g_ref[...]
        mn = ADAM_B1 * m_ref[...] + (1.0 - ADAM_B1) * gv
        vn = ADAM_B2 * v_ref[...] + (1.0 - ADAM_B2) * jnp.square(gv)
        mo_ref[...] = mn
        vo_ref[...] = vn
        d_ref[...] = -ADAM_LR * ((mn / c1) / (jnp.sqrt(vn / c2) + ADAM_EPS) + ADAM_WD * w_ref[...])

    spec = pl.BlockSpec((None, tr, c), lambda l, i: (l, i, 0))
    shp = jax.ShapeDtypeStruct(w.shape, F32)
    return pl.pallas_call(body, out_shape=(shp, shp, shp), grid=(nl, r // tr), in_specs=[spec] * 4,
                          out_specs=(spec, spec, spec), compiler_params=_cp(("arbitrary",) * 2), name=name)(w, g, m, v)


def _add_sibling(g, recv, c_idx, wire_dtype, name):
    nj, _, h, c = g.shape
    tr = _row_tile(h, c)

    def body(c_ref, g_ref, r_ref, o_ref):
        o_ref[...] = (g_ref[...] + r_ref[...]).astype(wire_dtype)

    grid_spec = pltpu.PrefetchScalarGridSpec(
        num_scalar_prefetch=1, grid=(nj, h // tr),
        in_specs=[pl.BlockSpec((None, None, tr, c), lambda j, i, c_ref: (j, c_ref[0], i, 0)),
                  pl.BlockSpec((None, tr, c), lambda j, i, c_ref: (j, i, 0))],
        out_specs=pl.BlockSpec((None, tr, c), lambda j, i, c_ref: (j, i, 0)))
    return pl.pallas_call(body, out_shape=jax.ShapeDtypeStruct((nj, h, c), wire_dtype), grid_spec=grid_spec,
                          compiler_params=_cp(("arbitrary",) * 2), name=name)(c_idx, g, recv)


def _add_chips(own, parts, place, name):
    _, h, c = own.shape
    tr = _row_tile(h, c)

    def body(p_ref, o_ref, a_ref, b_ref, c_ref, out_ref):
        out_ref[...] = ((o_ref[...].astype(F32) + a_ref[...].astype(F32)) + b_ref[...].astype(F32)) + c_ref[...].astype(F32)

    def other(k):
        return pl.BlockSpec((None, tr, c), lambda i, p_ref: (jnp.bitwise_xor(p_ref[0], k), i, 0))

    grid_spec = pltpu.PrefetchScalarGridSpec(
        num_scalar_prefetch=1, grid=(h // tr,),
        in_specs=[pl.BlockSpec((None, tr, c), lambda i, p_ref: (p_ref[0], i, 0)), other(1), other(2), other(3)],
        out_specs=pl.BlockSpec((None, tr, c), lambda i, p_ref: (p_ref[1], i, 0)))
    return pl.pallas_call(body, out_shape=jax.ShapeDtypeStruct((2, h, c), F32), grid_spec=grid_spec,
                          compiler_params=_cp(("arbitrary",)), name=name)(place, own, parts, parts, parts)


HBM_SPEC = pl.BlockSpec(memory_space=pltpu.HBM)


def _place():
    x, y, c = lax.axis_index("x"), lax.axis_index("y"), lax.axis_index("c")
    chips = [(1 - x, y), (x, 1 - y), (1 - x, 1 - y)]
    return x, y, c, 2 * x + y, chips, [2 * px + py for px, py in chips]


def _remote(src, dst, send_sem, recv_sem, dev):
    return pltpu.make_async_remote_copy(src_ref=src, dst_ref=dst, send_sem=send_sem, recv_sem=recv_sem,
                                        device_id=dev, device_id_type=MESH)


def _comm_call(body, arrays, out_shapes, sems, name, aliases=None):
    n = len(arrays)
    return pl.pallas_call(
        body, out_shape=tuple(out_shapes), in_specs=[HBM_SPEC] * n, out_specs=tuple([HBM_SPEC] * len(out_shapes)),
        scratch_shapes=[pltpu.SemaphoreType.DMA(s) for s in sems], input_output_aliases=aliases or {},
        compiler_params=pltpu.CompilerParams(has_side_effects=True), name=name)(*arrays)


class _SideCopies:
    def __init__(self, arrays, out_shapes, sem_shape, sends, recvs):
        self.arrays, self.out_shapes, self.sem_shape = list(arrays), list(out_shapes), sem_shape
        self.sends, self.recvs = sends, recvs

    @property
    def scratch(self):
        return [pltpu.SemaphoreType.DMA(self.sem_shape), pltpu.SemaphoreType.DMA(self.sem_shape)]

    def start(self, ins, outs, send, recv):
        for cp in self.sends(ins, outs, send, recv):
            cp.start()

    def wait(self, ins, outs, send, recv):
        for cp in self.recvs(ins, outs, send, recv):
            cp.wait_recv()
        for cp in self.sends(ins, outs, send, recv):
            cp.wait_send()

    def run(self, name):
        n = len(self.arrays)

        def body(*refs):
            ins, outs = refs[:n], refs[n:n + len(self.out_shapes)]
            send, recv = refs[-2:]
            self.start(ins, outs, send, recv)
            self.wait(ins, outs, send, recv)

        return _comm_call(body, self.arrays, self.out_shapes, [self.sem_shape] * 2, name)


def _gather_over_chips(shards):
    n = len(shards)

    def sends(ins, outs, send, recv):
        _, _, c, me, chips, _ = _place()
        return [_remote(ins[a].at[c], outs[a].at[me, c], send.at[a, k], recv.at[a, k], (px, py, c))
                for a in range(n) for k, (px, py) in enumerate(chips)]

    def recvs(ins, outs, send, recv):
        x, y, c, _, _, cidx = _place()
        return [_remote(outs[a].at[cidx[k], c], outs[a].at[cidx[k], c], send.at[a, k], recv.at[a, k], (x, y, 1 - c))
                for a in range(n) for k in range(3)]

    out_shapes = [jax.ShapeDtypeStruct((N_CHIPS,) + s.shape, s.dtype) for s in shards]
    return _SideCopies(shards, out_shapes, (n, 3), sends, recvs)


def _gather_finish(shards, partial, name):
    n = len(shards)

    def body(*refs):
        ins, part, outs = refs[:n], refs[n:2 * n], refs[2 * n:3 * n]
        send, recv = refs[3 * n:]
        x, y, c, me, _, cidx = _place()
        sib = (x, y, 1 - c)
        cps = []
        for a in range(n):
            for k in range(3):
                cps.append(_remote(part[a].at[cidx[k], c], outs[a].at[cidx[k], c], send.at[a, k], recv.at[a, k], sib))
            cps.append(_remote(ins[a], outs[a].at[me], send.at[a, 3], recv.at[a, 3], sib))
        for cp in cps:
            cp.start()
        for a in range(n):
            for k, blk in enumerate([outs[a].at[cidx[k], 1 - c] for k in range(3)] + [outs[a].at[me]]):
                _remote(blk, blk, send.at[a, k], recv.at[a, k], sib).wait_recv()
        for cp in cps:
            cp.wait_send()

    out_shapes = [jax.ShapeDtypeStruct(p.shape, p.dtype) for p in partial]
    return _comm_call(body, list(shards) + list(partial), out_shapes, [(n, 4), (n, 4)], name,
                      aliases={n + a: a for a in range(n)})


def _send_sibling_halves(grads, name):
    n = len(grads)

    def body(*refs):
        ins, outs = refs[:n], refs[n:2 * n]
        send, recv = refs[2 * n:]
        x, y, c, _, _, _ = _place()
        sib = (x, y, 1 - c)
        cps = []
        for a in range(n):
            for j in range(N_CHIPS):
                cp = _remote(ins[a].at[j, 1 - c], outs[a].at[j], send.at[a, j], recv.at[a, j], sib)
                cp.start()
                cps.append(cp)
        for cp in cps:
            cp.wait_recv()
        for cp in cps:
            cp.wait_send()

    out_shapes = [jax.ShapeDtypeStruct((g.shape[0],) + g.shape[2:], g.dtype) for g in grads]
    return _comm_call(body, grads, out_shapes, [(n, N_CHIPS), (n, N_CHIPS)], name)


def _scatter_over_chips(sums):
    n = len(sums)

    def sends(ins, outs, send, recv):
        _, _, c, me, chips, cidx = _place()
        return [_remote(ins[a].at[cidx[k]], outs[a].at[me], send.at[a, k], recv.at[a, k], (px, py, c))
                for a in range(n) for k, (px, py) in enumerate(chips)]

    def recvs(ins, outs, send, recv):
        x, y, c, _, _, cidx = _place()
        return [_remote(outs[a].at[cidx[k]], outs[a].at[cidx[k]], send.at[a, k], recv.at[a, k], (x, y, 1 - c))
                for a in range(n) for k in range(3)]

    return _SideCopies(sums, [jax.ShapeDtypeStruct(s.shape, s.dtype) for s in sums], (n, 3), sends, recvs)


def _exchange_halves(halves, name):
    n = len(halves)

    def body(*refs):
        ins, bufs = refs[:n], refs[n:2 * n]
        send, recv = refs[2 * n:]
        x, y, c, _, _, _ = _place()
        sib = (x, y, 1 - c)
        cps = []
        for a in range(n):
            cp = _remote(ins[a].at[c], bufs[a].at[c], send.at[a], recv.at[a], sib)
            cp.start()
            cps.append(cp)
        for a in range(n):
            blk = bufs[a].at[1 - c]
            _remote(blk, blk, send.at[a], recv.at[a], sib).wait_recv()
        for cp in cps:
            cp.wait_send()

    out_shapes = [jax.ShapeDtypeStruct(s.shape, s.dtype) for s in halves]
    return _comm_call(body, halves, out_shapes, [(n,), (n,)], name, aliases={a: a for a in range(n)})


def _gather_chips(slices, name):
    n = len(slices)

    def body(*refs):
        ins, outs = refs[:n], refs[n:2 * n]
        send, recv = refs[2 * n:]
        x, y, c, me, chips, cidx = _place()
        sib = (x, y, 1 - c)
        cps = []
        for a in range(n):
            for k, dev in enumerate([(px, py, c) for px, py in chips] + [sib]):
                cp = _remote(ins[a], outs[a].at[me], send.at[a, k], recv.at[a, k], dev)
                cp.start()
                cps.append(cp)
        for a in range(n):
            for k, slot in enumerate(cidx + [me]):
                blk = outs[a].at[slot]
                _remote(blk, blk, send.at[a, k], recv.at[a, k], sib).wait_recv()
        for cp in cps:
            cp.wait_send()

    out_shapes = [jax.ShapeDtypeStruct((N_CHIPS,) + s.shape, s.dtype) for s in slices]
    return _comm_call(body, slices, out_shapes, [(n, 4), (n, 4)], name)


def _rs_begin(grads, c_idx, wire_dtype, tag):
    split = [g.reshape(g.shape[0], 2, g.shape[1] // 2, g.shape[2]) for g in grads]
    recv = _send_sibling_halves(split, "rs_sibling_" + tag)
    sums = [_add_sibling(g, r, c_idx, wire_dtype, "rs_add_sibling_" + tag) for g, r in zip(split, recv)]
    return _scatter_over_chips(sums)


def _rs_end(scatter, parts, place, tag):
    halves = [_add_chips(s, p, place, "rs_add_chips_" + tag) for s, p in zip(scatter.arrays, parts)]
    both = _exchange_halves(halves, "rs_halves_" + tag)
    return [b.reshape(b.shape[0] * b.shape[1], b.shape[2]) for b in both]


SMALL_ORDER = ("gm_w_s", "mix_pre_norm", "mix_post_norm", "mix_out_norm", "ffn_pre_norm", "ffn_post_norm", "ple_norm",
               "gm_v_norm", "gm_b_s", "b_forget")
SMALL_ROWS_MULTIPLE = 64


def _pack_small(parts):
    flat = []
    for nme in SMALL_ORDER:
        v = parts[nme].reshape(-1)
        pad = (-v.shape[0]) % LANES
        flat.append(jnp.pad(v, (0, pad)) if pad else v)
    v = jnp.concatenate(flat)
    rows = v.shape[0] // LANES
    pad_rows = (-rows) % SMALL_ROWS_MULTIPLE
    return jnp.pad(v.reshape(rows, LANES), ((0, pad_rows), (0, 0)))


def _unpack_small(packed, shapes):
    flat = packed.reshape(-1)
    out, off = {}, 0
    for nme in SMALL_ORDER:
        size = 1
        for s in shapes[nme]:
            size *= s
        out[nme] = flat[off:off + size].reshape(shapes[nme])
        off += size + ((-size) % LANES)
    return out


def kernel(x, p, mix_pre_norm, mix_post_norm, w_in, b_forget, gm_v_norm, gm_w_s, gm_b_s, mix_out_norm, w_out, ffn_pre_norm, ffn_post_norm, w_ffn_in, w_ffn_out, w_ple, ple_norm, w_ple_gate, loss_target, m_mix_pre_norm, m_mix_post_norm, m_w_in, m_b_forget, m_gm_v_norm, m_gm_w_s, m_gm_b_s, m_mix_out_norm, m_w_out, m_ffn_pre_norm, m_ffn_post_norm, m_w_ffn_in, m_w_ffn_out, m_w_ple, m_ple_norm, m_w_ple_gate, v_mix_pre_norm, v_mix_post_norm, v_w_in, v_b_forget, v_gm_v_norm, v_gm_w_s, v_gm_b_s, v_mix_out_norm, v_w_out, v_ffn_pre_norm, v_ffn_post_norm, v_w_ffn_in, v_w_ffn_out, v_w_ple, v_ple_norm, v_w_ple_gate):
    weights = dict(mix_pre_norm=mix_pre_norm, mix_post_norm=mix_post_norm, w_in=w_in, b_forget=b_forget,
                   gm_v_norm=gm_v_norm, gm_w_s=gm_w_s, gm_b_s=gm_b_s, mix_out_norm=mix_out_norm, w_out=w_out,
                   ffn_pre_norm=ffn_pre_norm, ffn_post_norm=ffn_post_norm, w_ffn_in=w_ffn_in, w_ffn_out=w_ffn_out,
                   w_ple=w_ple, ple_norm=ple_norm, w_ple_gate=w_ple_gate)
    mom_m = dict(mix_pre_norm=m_mix_pre_norm, mix_post_norm=m_mix_post_norm, w_in=m_w_in, b_forget=m_b_forget,
                 gm_v_norm=m_gm_v_norm, gm_w_s=m_gm_w_s, gm_b_s=m_gm_b_s, mix_out_norm=m_mix_out_norm, w_out=m_w_out,
                 ffn_pre_norm=m_ffn_pre_norm, ffn_post_norm=m_ffn_post_norm, w_ffn_in=m_w_ffn_in,
                 w_ffn_out=m_w_ffn_out, w_ple=m_w_ple, ple_norm=m_ple_norm, w_ple_gate=m_w_ple_gate)
    mom_v = dict(mix_pre_norm=v_mix_pre_norm, mix_post_norm=v_mix_post_norm, w_in=v_w_in, b_forget=v_b_forget,
                 gm_v_norm=v_gm_v_norm, gm_w_s=v_gm_w_s, gm_b_s=v_gm_b_s, mix_out_norm=v_mix_out_norm, w_out=v_w_out,
                 ffn_pre_norm=v_ffn_pre_norm, ffn_post_norm=v_ffn_post_norm, w_ffn_in=v_w_ffn_in,
                 w_ffn_out=v_w_ffn_out, w_ple=v_w_ple, ple_norm=v_ple_norm, w_ple_gate=v_w_ple_gate)
    big = ("w_in", "w_out", "w_ffn_in", "w_ffn_out", "w_ple", "w_ple_gate")
    depth = w_in.shape[0]
    t, d = x.shape[1], x.shape[2]
    d_ff = w_ffn_out.shape[1] * N_CHIPS
    c_idx = lax.axis_index("c").astype(jnp.int32).reshape(1)
    place = jnp.stack([2 * lax.axis_index("x") + lax.axis_index("y"), lax.axis_index("c")]).astype(jnp.int32)
    h = x[0]
    target = loss_target[0]
    mean_mat = _group_mean_matrix()

    def row(a, i):
        return a[i].reshape(1, -1)

    def shards_of(i):
        shards = [weights[nme][i].astype(BF16) for nme in big]
        return [s.reshape(2, s.shape[0] // 2, s.shape[1]) for s in shards]

    saved = []
    hn = _norm_cast(h, row(mix_pre_norm, 0), "norm_first")
    gather = _gather_over_chips(shards_of(0))
    partial = gather.run("gather_first")
    for i in range(depth):
        gathered = _gather_finish(gather.arrays, partial, "gather_finish")
        gathered = [g.reshape(N_CHIPS, g.shape[2] * 2, g.shape[3]) for g in gathered]
        by_cols = lambda g: g.transpose(1, 0, 2).reshape(g.shape[1], N_CHIPS * g.shape[2])
        by_rows = lambda g: g.reshape(N_CHIPS * g.shape[1], g.shape[2])
        w_in_f = by_cols(gathered[0])
        w_qkv = w_in_f[:, :3 * D_ATT]
        w_gf = jnp.concatenate([w_in_f[:, 3 * D_ATT + N_HEADS:], w_in_f[:, 3 * D_ATT:3 * D_ATT + N_HEADS],
                                jnp.zeros((d, LANES - N_HEADS), BF16)], axis=1)
        w_out_f = by_rows(gathered[1])
        w_fi_f = by_cols(gathered[2])
        w_fo_f = by_rows(gathered[3])
        w_ple_f = by_cols(gathered[4])
        w_pg_f = by_rows(gathered[5])
        b_pad = jnp.pad(b_forget[i], (0, LANES - N_HEADS)).reshape(1, LANES)
        bias_full = jnp.repeat(gm_b_s[i].T, HEAD_DIM, axis=1)
        gain_v = row(gm_v_norm, i)

        qkv = _mm(hn, w_qkv, "nn", BF16, "mm_qkv")
        zgf = _mm(hn, w_gf, "nn", F32, "mm_gf", tn_cap=384)
        qa, ka, va = _attn_prep(qkv, _gate_fwd(zgf, b_pad, "gate_fwd"), "attn_prep")
        gather = _gather_over_chips(shards_of(i + 1)) if i + 1 < depth else None
        (att, lse), partial = _attn_fwd(qa, ka, va, "attn_fwd" if gather else "attn_fwd_last", gather)
        gm =_gmlp_fwd(zgf, gm_w_s[i], bias_full, gain_v, mean_mat, "gmlp_fwd")
        mixed = _mixnorm_fwd(att, gm, row(mix_out_norm, i), "mixnorm_fwd")
        o = _mm(mixed, w_out_f, "nn", F32, "mm_out")
        h1, hn2 = _resid_norm(h, o, row(mix_post_norm, i), row(ffn_pre_norm, i), "resid_mix")
        s, ab = _ffn_in(hn2, w_fi_f, "ffn_in")
        o2 = _mm(s, w_fo_f, "nn", F32, "mm_ffn_out")
        h2, hr = _resid_norm(h1, o2, row(ffn_post_norm, i), None, "resid_ffn")
        pe = _mm(p[i, 0], w_ple_f, "nn", F32, "mm_ple")
        gl = _mm(hr, w_pg_f, "nn", F32, "mm_ple_gate")
        g_next = row(mix_pre_norm, i + 1) if i + 1 < depth else row(mix_pre_norm, 0)
        h3, hn_next = _ple_fwd(h2, pe, gl, row(ple_norm, i), g_next, "ple_fwd")
        saved.append(dict(h=h, hn=hn, qa=qa, ka=ka, va=va, zgf=zgf, att=att, lse=lse, gm=gm, mixed=mixed,
                          o=o, h1=h1, hn2=hn2, ab=ab, s=s, o2=o2, h2=h2, hr=hr, pe=pe, gl=gl, w_qkv=w_qkv, w_gf=w_gf,
                          w_out=w_out_f, w_fi=w_fi_f, w_fo=w_fo_f, w_pg=w_pg_f, b_pad=b_pad, bias_full=bias_full,
                          gain_v=gain_v))
        h, hn = h3, hn_next

    dh, loss_blk = _loss_head(h, target, "loss_head")
    loss = lax.psum(loss_blk[0, 0], ("x", "y", "c"))

    small = {nme: [None] * depth for nme in SMALL_ORDER}
    big_grads = {nme: [None] * depth for nme in big}
    waiting = []
    for i in reversed(range(depth)):
        sv = saved[i]
        dgl, dpe, small["ple_norm"][i] = _ple_bwd(dh, sv["pe"], sv["gl"], row(ple_norm, i), "ple_bwd")
        dhr = _mm(dgl, sv["w_pg"], "nt", F32, "mm_d_hr")
        g_pg = _mm(sv["hr"], dgl, "tn", F32, "mm_dw_ple_gate").reshape(N_CHIPS, -1, d)
        g_ple = _mm(p[i, 0], dpe, "tn", F32, "mm_dw_ple", chip_split=True)
        dh2, do2, small["ffn_post_norm"][i] = _join(dh, sv["h2"], None, dhr, sv["o2"], row(ffn_post_norm, i), "join_ple")
        dab = _ffn_bwd(do2, sv["w_fo"], sv["ab"], "ffn_bwd")
        g_fo = _mm(sv["s"], do2, "tn", F32, "mm_dw_ffn_out", tm=256).reshape(N_CHIPS, -1, d)
        dhn2 = _mm_halves_nt(dab, sv["w_fi"], "mm_d_hn2")
        g_fi = _mm_halves_tn(sv["hn2"], dab, "mm_dw_ffn_in")
        dh1, small["ffn_pre_norm"][i], do, small["mix_post_norm"][i] = _join(
            dh2, sv["h1"], row(ffn_pre_norm, i), dhn2, sv["o"], row(mix_post_norm, i), "join_ffn")
        dmixed = _mm(do, sv["w_out"], "nt", F32, "mm_d_mixed")
        g_out = _mm(sv["mixed"], do, "tn", F32, "mm_dw_out").reshape(N_CHIPS, -1, d)
        datt, dgm, small["mix_out_norm"][i] = _mixnorm_bwd(sv["att"], sv["gm"], row(mix_out_norm, i), dmixed, "mixnorm_bwd")
        dg, small["gm_w_s"][i], dmix_sum, small["gm_v_norm"][i] = _gmlp_bwd(
            sv["zgf"], dgm, gm_w_s[i], sv["bias_full"], sv["gain_v"], mean_mat, "gmlp_bwd")
        small["gm_b_s"][i] = dmix_sum.reshape(CHUNK, N_HEADS, HEAD_DIM).sum(-1).T
        qb, doa = _attn_bwd_prep(sv["qa"], sv["att"], sv["lse"], datt, "attn_bwd_prep")
        batch = waiting + [("w_out", i, g_out), ("w_ffn_in", i, g_fi), ("w_ffn_out", i, g_fo), ("w_ple", i, g_ple),
                           ("w_ple_gate", i, g_pg)]
        tag = "layer" if waiting else "top"
        scatter = _rs_begin([g for _, _, g in batch], c_idx, BF16, tag)
        (dqa, dka, dva), parts = _attn_bwd(qb, sv["ka"], sv["va"], doa, "attn_bwd_" + tag, scatter)
        for (nme, layer, _), g in zip(batch, _rs_end(scatter, parts, place, tag)):
            big_grads[nme][layer] = g
        dqkv, dc = _attn_bwd_post(dqa, dka, dva, "attn_bwd_post")
        dfl, db = _gate_bwd(dc, sv["zgf"], sv["b_pad"], "gate_bwd")
        small["b_forget"][i] = db[0, :N_HEADS]
        dgf = jnp.concatenate([dg, dfl], axis=1)
        dhn = _mm(dqkv, sv["w_qkv"], "nt", F32, "mm_d_hn_qkv")
        dhn = _mm(dgf, sv["w_gf"], "nt", F32, "mm_d_hn_gf", add=dhn)
        g_qkv = _mm(sv["hn"], dqkv, "tn", F32, "mm_dw_qkv")
        g_gf = _mm(sv["hn"], dgf, "tn", F32, "mm_dw_gf", tn_cap=384)
        g_in = jnp.concatenate([g_qkv, g_gf[:, 2 * D_GM:2 * D_GM + N_HEADS], g_gf[:, :2 * D_GM]], axis=1)
        g_in = g_in.reshape(d, N_CHIPS, -1).transpose(1, 0, 2)
        dh, small["mix_pre_norm"][i] = _join(dh1, sv["h"], row(mix_pre_norm, i), dhn, None, None, "join_mix")
        waiting = [("w_in", i, g_in)]
    scatter = _rs_begin([waiting[0][2]], c_idx, BF16, "tail")
    big_grads["w_in"][0] = _rs_end(scatter, scatter.run("rs_chips_tail"), place, "tail")[0]
    grad_x = dh.reshape(1, t, d)

    small_shapes = {nme: weights[nme].shape for nme in SMALL_ORDER}
    small_part = _pack_small({nme: jnp.stack([g.reshape(small_shapes[nme][1:]) for g in small[nme]])
                              for nme in SMALL_ORDER})
    rows_small = small_part.shape[0]
    scatter = _rs_begin([small_part.reshape(N_CHIPS, rows_small // N_CHIPS, LANES)], c_idx, F32, "small")
    small_slice = _rs_end(scatter, scatter.run("rs_chips_small"), place, "small")[0]
    small_all = _gather_chips([small_slice], "gather_small")[0].reshape(1, rows_small, LANES)
    sd, sm, sv_ = _adamw(_pack_small({n_: weights[n_] for n_ in SMALL_ORDER})[None], small_all,
                         _pack_small({n_: mom_m[n_] for n_ in SMALL_ORDER})[None],
                         _pack_small({n_: mom_v[n_] for n_ in SMALL_ORDER})[None], "adamw_small")
    grads = _unpack_small(small_all[0], small_shapes)
    deltas = _unpack_small(sd[0], small_shapes)
    new_m = _unpack_small(sm[0], small_shapes)
    new_v = _unpack_small(sv_[0], small_shapes)

    for nme in big:
        g = jnp.stack(big_grads[nme]).reshape(weights[nme].shape)
        grads[nme] = g
        deltas[nme], new_m[nme], new_v[nme] = _adamw(weights[nme], g, mom_m[nme], mom_v[nme], "adamw_" + nme)

    order = ("mix_pre_norm", "mix_post_norm", "w_in", "b_forget", "gm_v_norm", "gm_w_s", "gm_b_s", "mix_out_norm",
             "w_out", "ffn_pre_norm", "ffn_post_norm", "w_ffn_in", "w_ffn_out", "w_ple", "ple_norm", "w_ple_gate")
    return (loss, grad_x, *[grads[n_] for n_ in order], *[deltas[n_] for n_ in order], *[new_m[n_] for n_ in order],
            *[new_v[n_] for n_ in order])
```

```python
import functools

import jax
import jax.numpy as jnp
from jax import lax
from jax.experimental import pallas as pl
from jax.experimental.pallas import tpu as pltpu

F32 = jnp.float32
BF16 = jnp.bfloat16
MESH = pl.DeviceIdType.MESH
HIGHEST = lax.Precision.HIGHEST

EPS = 1e-6
NEG_INF = -1e30
N_HEADS = 8
HEAD_DIM = 64
D_ATT = N_HEADS * HEAD_DIM
D_GM = 512
CHUNK = 128
LANES = 128
N_CHIPS = 4
ADAM_LR = 0.001
ADAM_B1 = 0.9
ADAM_B2 = 0.999
ADAM_EPS = 1e-08
ADAM_WD = 0.01
ADAM_STEP = 10
VMEM_LIMIT = 56 * 1024 * 1024


def _cp(sem=None):
    return pltpu.CompilerParams(dimension_semantics=sem, vmem_limit_bytes=VMEM_LIMIT)


def _full(shape):
    return pl.BlockSpec(shape, lambda *_: (0,) * len(shape))


def _rows(tm, width, col_block=0):
    return pl.BlockSpec((tm, width), lambda i: (i, col_block))


def _dot(a, b, dims, precision=None):
    return lax.dot_general(a, b, (dims, ((), ())), preferred_element_type=F32, precision=precision)


NN = ((1,), (0,))
NT = ((1,), (1,))
TN = ((0,), (0,))


def _pick(n, cap):
    best = None
    for t in range(LANES, min(n, cap) + 1, LANES):
        if n % t == 0:
            best = t
    assert best is not None, (n, cap)
    return best


def _mm(a, b, mode, out_dtype, name, tm=None, tn_cap=1024, add=None, chip_split=False):
    dims = {"nn": NN, "nt": NT, "tn": TN}[mode]
    if mode == "tn":
        k, m = a.shape
    else:
        m, k = a.shape
    n = b.shape[0] if mode == "nt" else b.shape[1]
    if tm is None:
        tm = 512 if mode == "tn" else 1024
    tm = min(tm, m)
    tn = n // N_CHIPS if chip_split else _pick(n, tn_cap)
    assert m % tm == 0 and n % tn == 0

    def body(*refs):
        a_ref, b_ref = refs[0], refs[1]
        o_ref = refs[-1]
        acc = _dot(a_ref[...].astype(BF16), b_ref[...].astype(BF16), dims)
        if add is not None:
            acc = acc + refs[2][...]
        o_ref[...] = acc.astype(out_dtype)

    a_spec = pl.BlockSpec((k, tm), lambda i, j: (0, i)) if mode == "tn" else pl.BlockSpec((tm, k), lambda i, j: (i, 0))
    b_spec = pl.BlockSpec((tn, k), lambda i, j: (j, 0)) if mode == "nt" else pl.BlockSpec((k, tn), lambda i, j: (0, j))
    in_specs = [a_spec, b_spec]
    args = [a, b]
    if add is not None:
        in_specs.append(pl.BlockSpec((tm, tn), lambda i, j: (i, j)))
        args.append(add)
    if chip_split:
        out_shape = jax.ShapeDtypeStruct((N_CHIPS, m, tn), out_dtype)
        out_spec = pl.BlockSpec((None, tm, tn), lambda i, j: (j, i, 0))
    else:
        out_shape = jax.ShapeDtypeStruct((m, n), out_dtype)
        out_spec = pl.BlockSpec((tm, tn), lambda i, j: (i, j))
    return pl.pallas_call(body, out_shape=out_shape, grid=(m // tm, n // tn), in_specs=in_specs, out_specs=out_spec,
                          compiler_params=_cp(("arbitrary", "arbitrary")), name=name)(*args)


def _rms_inv(x):
    return lax.rsqrt(jnp.mean(x * x, axis=-1, keepdims=True) + EPS)


def _rms_bwd(x, gain, dy):
    inv = _rms_inv(x)
    xhat = x * inv
    dxn = dy if gain is None else dy * gain
    dx = inv * (dxn - xhat * jnp.mean(dxn * xhat, axis=-1, keepdims=True))
    return dx, dy * xhat


def _acc_rows(ref, val):
    s = jnp.sum(val, axis=0, keepdims=True)

    @pl.when(pl.program_id(0) == 0)
    def _():
        ref[...] = s

    @pl.when(pl.program_id(0) > 0)
    def _():
        ref[...] += s


def _norm_cast(h, gain, name, tm=512):
    t, d = h.shape

    def body(h_ref, g_ref, o_ref):
        x = h_ref[...]
        o_ref[...] = (x * _rms_inv(x) * g_ref[...]).astype(BF16)

    return pl.pallas_call(body, out_shape=jax.ShapeDtypeStruct((t, d), BF16), grid=(t // tm,),
                          in_specs=[_rows(tm, d), _full((1, d))], out_specs=_rows(tm, d),
                          compiler_params=_cp(("arbitrary",)), name=name)(h, gain)


def _resid_norm(h, o, g_post, g_next, name, tm=512):
    t, d = h.shape
    has_gain = g_next is not None

    def body(*refs):
        h_ref, o_ref, gp_ref = refs[:3]
        h1_ref, hn_ref = refs[-2:]
        ov = o_ref[...]
        h1 = h_ref[...] + ov * _rms_inv(ov) * gp_ref[...]
        h1_ref[...] = h1
        hn = h1 * _rms_inv(h1)
        if has_gain:
            hn = hn * refs[3][...]
        hn_ref[...] = hn.astype(BF16)

    args = [h, o, g_post] + ([g_next] if has_gain else [])
    in_specs = [_rows(tm, d), _rows(tm, d), _full((1, d))] + ([_full((1, d))] if has_gain else [])
    return pl.pallas_call(body, out_shape=(jax.ShapeDtypeStruct((t, d), F32), jax.ShapeDtypeStruct((t, d), BF16)),
                          grid=(t // tm,), in_specs=in_specs, out_specs=(_rows(tm, d), _rows(tm, d)),
                          compiler_params=_cp(("arbitrary",)), name=name)(*args)


def _ple_fwd(h2, pe, gl, g_ple, g_next, name, tm=512):
    t, d = h2.shape

    def body(h_ref, pe_ref, gl_ref, gp_ref, gn_ref, h3_ref, hn_ref):
        pv = pe_ref[...]
        e = pv * _rms_inv(pv) * gp_ref[...]
        h3 = h_ref[...] + jax.nn.sigmoid(gl_ref[...]) * e
        h3_ref[...] = h3
        hn_ref[...] = (h3 * _rms_inv(h3) * gn_ref[...]).astype(BF16)

    return pl.pallas_call(body, out_shape=(jax.ShapeDtypeStruct((t, d), F32), jax.ShapeDtypeStruct((t, d), BF16)),
                          grid=(t // tm,), in_specs=[_rows(tm, d)] * 3 + [_full((1, d))] * 2,
                          out_specs=(_rows(tm, d), _rows(tm, d)), compiler_params=_cp(("arbitrary",)),
                          name=name)(h2, pe, gl, g_ple, g_next)


def _loss_head(y, target, name, tm=512):
    t, d = y.shape

    def body(y_ref, t_ref, dy_ref, loss_ref):
        diff = y_ref[...] - t_ref[...]
        dy_ref[...] = diff * (1.0 / d)
        part = 0.5 * jnp.sum(jnp.mean(diff * diff, axis=-1, keepdims=True), axis=0, keepdims=True)
        part = jnp.broadcast_to(part, (8, LANES))

        @pl.when(pl.program_id(0) == 0)
        def _():
            loss_ref[...] = part

        @pl.when(pl.program_id(0) > 0)
        def _():
            loss_ref[...] += part

    return pl.pallas_call(body, out_shape=(jax.ShapeDtypeStruct((t, d), F32), jax.ShapeDtypeStruct((8, LANES), F32)),
                          grid=(t // tm,), in_specs=[_rows(tm, d)] * 2, out_specs=(_rows(tm, d), _full((8, LANES))),
                          compiler_params=_cp(("arbitrary",)), name=name)(y, target)


def _ple_bwd(dh3, pe, gl, g_ple, name, tm=512):
    t, d = dh3.shape

    def body(dh_ref, pe_ref, gl_ref, gp_ref, dgl_ref, dpe_ref, dg_ref):
        dh = dh_ref[...]
        pv = pe_ref[...]
        gp = gp_ref[...]
        gate = jax.nn.sigmoid(gl_ref[...])
        e = pv * _rms_inv(pv) * gp
        dgl_ref[...] = (dh * e * gate * (1.0 - gate)).astype(BF16)
        dpe, dg_rows = _rms_bwd(pv, gp, dh * gate)
        dpe_ref[...] = dpe.astype(BF16)
        _acc_rows(dg_ref, dg_rows)

    return pl.pallas_call(body, out_shape=(jax.ShapeDtypeStruct((t, d), BF16), jax.ShapeDtypeStruct((t, d), BF16),
                                           jax.ShapeDtypeStruct((1, d), F32)),
                          grid=(t // tm,), in_specs=[_rows(tm, d)] * 3 + [_full((1, d))],
                          out_specs=(_rows(tm, d), _rows(tm, d), _full((1, d))),
                          compiler_params=_cp(("arbitrary",)), name=name)(dh3, pe, gl, g_ple)


def _join(d_res, x_a, gain_a, d_a, x_b, gain_b, name, tm=512):
    t, d = d_res.shape
    has_ga = gain_a is not None
    has_b = x_b is not None

    def body(*refs):
        it = iter(refs)
        dres_ref, xa_ref = next(it), next(it)
        ga_ref = next(it) if has_ga else None
        da_ref = next(it)
        xb_ref, gb_ref = (next(it), next(it)) if has_b else (None, None)
        dout_ref = next(it)
        dga_ref = next(it) if has_ga else None
        db_ref, dgb_ref = (next(it), next(it)) if has_b else (None, None)
        dx, dg_rows = _rms_bwd(xa_ref[...], ga_ref[...] if has_ga else None, da_ref[...])
        dout = dres_ref[...] + dx
        dout_ref[...] = dout
        if has_ga:
            _acc_rows(dga_ref, dg_rows)
        if has_b:
            db, dgb_rows = _rms_bwd(xb_ref[...], gb_ref[...], dout)
            db_ref[...] = db.astype(BF16)
            _acc_rows(dgb_ref, dgb_rows)

    args, in_specs = [d_res, x_a], [_rows(tm, d), _rows(tm, d)]
    if has_ga:
        args.append(gain_a)
        in_specs.append(_full((1, d)))
    args.append(d_a)
    in_specs.append(_rows(tm, d))
    if has_b:
        args += [x_b, gain_b]
        in_specs += [_rows(tm, d), _full((1, d))]
    out_shape, out_specs = [jax.ShapeDtypeStruct((t, d), F32)], [_rows(tm, d)]
    if has_ga:
        out_shape.append(jax.ShapeDtypeStruct((1, d), F32))
        out_specs.append(_full((1, d)))
    if has_b:
        out_shape += [jax.ShapeDtypeStruct((t, d), BF16), jax.ShapeDtypeStruct((1, d), F32)]
        out_specs += [_rows(tm, d), _full((1, d))]
    return pl.pallas_call(body, out_shape=tuple(out_shape), grid=(t // tm,), in_specs=in_specs,
                          out_specs=tuple(out_specs), compiler_params=_cp(("arbitrary",)), name=name)(*args)


def _ffn_in(hn, w, name, tm=2048, tn=256):
    t, d = hn.shape
    f = w.shape[1] // 2
    tm = min(tm, t)
    nj = f // tn

    def body(h_ref, wa_ref, wb_ref, s_ref, ab_ref):
        h = h_ref[...]
        a = _dot(h, wa_ref[...], NN)
        b = _dot(h, wb_ref[...], NN)
        ab_ref[0] = a.astype(BF16)
        ab_ref[1] = b.astype(BF16)
        s_ref[...] = (a * jax.nn.sigmoid(a) * b).astype(BF16)

    return pl.pallas_call(
        body, out_shape=(jax.ShapeDtypeStruct((t, f), BF16), jax.ShapeDtypeStruct((2, t, f), BF16)), grid=(t // tm, nj),
        in_specs=[pl.BlockSpec((tm, d), lambda i, j: (i, 0)), pl.BlockSpec((d, tn), lambda i, j: (0, j)),
                  pl.BlockSpec((d, tn), lambda i, j: (0, nj + j))],
        out_specs=(pl.BlockSpec((tm, tn), lambda i, j: (i, j)), pl.BlockSpec((2, tm, tn), lambda i, j: (0, i, j))),
        compiler_params=_cp(("arbitrary",) * 2), name=name)(hn, w, w)


def _ffn_bwd(do, w_out, ab, name, tm=2048, tn=256):
    t, d = do.shape
    f = w_out.shape[0]
    tm = min(tm, t)

    def body(d_ref, w_ref, ab_ref, o_ref):
        ds = _dot(d_ref[...], w_ref[...], NT)
        a = ab_ref[0].astype(F32)
        sg = jax.nn.sigmoid(a)
        silu = a * sg
        o_ref[0] = (ds * ab_ref[1].astype(F32) * (sg + silu * (1.0 - sg))).astype(BF16)
        o_ref[1] = (ds * silu).astype(BF16)

    both = pl.BlockSpec((2, tm, tn), lambda i, j: (0, i, j))
    return pl.pallas_call(
        body, out_shape=jax.ShapeDtypeStruct((2, t, f), BF16), grid=(t // tm, f // tn),
        in_specs=[pl.BlockSpec((tm, d), lambda i, j: (i, 0)), pl.BlockSpec((tn, d), lambda i, j: (j, 0)), both],
        out_specs=both, compiler_params=_cp(("arbitrary",) * 2), name=name)(do, w_out, ab)


def _mm_halves_nt(x2, w, name, tm=1024, tn=512):
    _, t, f = x2.shape
    n = w.shape[0]
    tm = min(tm, t)

    def body(x_ref, w_ref, o_ref, acc):
        part = _dot(x_ref[...], w_ref[...], NT)

        @pl.when(pl.program_id(2) == 0)
        def _():
            acc[...] = part

        @pl.when(pl.program_id(2) == 1)
        def _():
            o_ref[...] = acc[...] + part

    return pl.pallas_call(
        body, out_shape=jax.ShapeDtypeStruct((t, n), F32), grid=(t // tm, n // tn, 2),
        in_specs=[pl.BlockSpec((None, tm, f), lambda i, j, k: (k, i, 0)), pl.BlockSpec((tn, f), lambda i, j, k: (j, k))],
        out_specs=pl.BlockSpec((tm, tn), lambda i, j, k: (i, j)), scratch_shapes=[pltpu.VMEM((tm, tn), F32)],
        compiler_params=_cp(("arbitrary",) * 3), name=name)(x2, w)


def _mm_halves_tn(h, x2, name, tm=256):
    t, d = h.shape
    f = x2.shape[2]
    tn = 2 * f // N_CHIPS

    def body(h_ref, x_ref, o_ref):
        o_ref[...] = _dot(h_ref[...], x_ref[...], TN)

    return pl.pallas_call(
        body, out_shape=jax.ShapeDtypeStruct((N_CHIPS, d, tn), F32), grid=(d // tm, N_CHIPS),
        in_specs=[pl.BlockSpec((t, tm), lambda i, j: (0, i)), pl.BlockSpec((None, t, tn), lambda i, j: (j // 2, 0, j % 2))],
        out_specs=pl.BlockSpec((None, tm, tn), lambda i, j: (j, i, 0)),
        compiler_params=_cp(("arbitrary",) * 2), name=name)(h, x2)


def _mixnorm_fwd(att, gm, g_out, name, tm=512):
    t, w = att.shape

    def body(a_ref, m_ref, g_ref, o_ref):
        a, m, g = a_ref[...], m_ref[...], g_ref[...]
        o_ref[:, :w] = (a * _rms_inv(a) * g[:, :w]).astype(BF16)
        o_ref[:, w:] = (m * _rms_inv(m) * g[:, w:]).astype(BF16)

    return pl.pallas_call(body, out_shape=jax.ShapeDtypeStruct((t, 2 * w), BF16), grid=(t // tm,),
                          in_specs=[_rows(tm, w), _rows(tm, w), _full((1, 2 * w))], out_specs=_rows(tm, 2 * w),
                          compiler_params=_cp(("arbitrary",)), name=name)(att, gm, g_out)


def _mixnorm_bwd(att, gm, g_out, dmixed, name, tm=512):
    t, w = att.shape

    def body(a_ref, m_ref, g_ref, d_ref, da_ref, dm_ref, dg_ref):
        g, d = g_ref[...], d_ref[...]
        da, dga = _rms_bwd(a_ref[...], g[:, :w], d[:, :w])
        dm, dgm = _rms_bwd(m_ref[...], g[:, w:], d[:, w:])
        da_ref[...] = da
        dm_ref[...] = dm
        _acc_rows(dg_ref, jnp.concatenate([dga, dgm], axis=1))

    return pl.pallas_call(body, out_shape=(jax.ShapeDtypeStruct((t, w), F32), jax.ShapeDtypeStruct((t, w), F32),
                                           jax.ShapeDtypeStruct((1, 2 * w), F32)),
                          grid=(t // tm,), in_specs=[_rows(tm, w), _rows(tm, w), _full((1, 2 * w)), _rows(tm, 2 * w)],
                          out_specs=(_rows(tm, w), _rows(tm, w), _full((1, 2 * w))),
                          compiler_params=_cp(("arbitrary",)), name=name)(att, gm, g_out, dmixed)


SCAN_BLOCK = 256


def _gate_fwd(zgf, b_pad, name):
    t = zgf.shape[0]
    fcol = zgf.shape[1] // LANES - 1
    nb = t // SCAN_BLOCK

    def body(f_ref, b_ref, c_ref):
        r = lax.broadcasted_iota(jnp.int32, (SCAN_BLOCK, SCAN_BLOCK), 0)
        s = lax.broadcasted_iota(jnp.int32, (SCAN_BLOCK, SCAN_BLOCK), 1)
        tril = (r >= s).astype(F32)
        head = lax.broadcasted_iota(jnp.int32, (SCAN_BLOCK, LANES), 1) < N_HEADS
        carry = jnp.zeros((1, LANES), F32)
        for blk in range(nb):
            rows = pl.ds(blk * SCAN_BLOCK, SCAN_BLOCK)
            x = f_ref[rows, :] + b_ref[...]
            lf = jnp.minimum(x, 0.0) - jnp.log1p(jnp.exp(-jnp.abs(x)))
            lf = jnp.where(head, lf, 0.0)
            cs = _dot(tril, lf, NN, HIGHEST) + carry
            c_ref[rows, :] = cs
            carry = carry + jnp.sum(lf, axis=0, keepdims=True)

    return pl.pallas_call(body, out_shape=jax.ShapeDtypeStruct((t, LANES), F32),
                          grid=(1,), in_specs=[pl.BlockSpec((t, LANES), lambda i: (0, fcol)), _full((1, LANES))],
                          out_specs=_full((t, LANES)), compiler_params=_cp(("arbitrary",)),
                          name=name)(zgf, b_pad)


def _gate_bwd(dc, zgf, b_pad, name):
    t = zgf.shape[0]
    fcol = zgf.shape[1] // LANES - 1
    nb = t // SCAN_BLOCK

    def body(dc_ref, f_ref, b_ref, dfl_ref, db_ref):
        r = lax.broadcasted_iota(jnp.int32, (SCAN_BLOCK, SCAN_BLOCK), 0)
        s = lax.broadcasted_iota(jnp.int32, (SCAN_BLOCK, SCAN_BLOCK), 1)
        triu = (s >= r).astype(F32)
        head = lax.broadcasted_iota(jnp.int32, (SCAN_BLOCK, LANES), 1) < N_HEADS
        carry = jnp.zeros((1, LANES), F32)
        db = jnp.zeros((1, LANES), F32)
        for blk in reversed(range(nb)):
            rows = pl.ds(blk * SCAN_BLOCK, SCAN_BLOCK)
            dc = dc_ref[rows, :]
            dlf = _dot(triu, dc, NN, HIGHEST) + carry
            x = f_ref[rows, :] + b_ref[...]
            dfl = jnp.where(head, dlf * jax.nn.sigmoid(-x), 0.0)
            dfl_ref[rows, :] = dfl.astype(BF16)
            db = db + jnp.sum(dfl, axis=0, keepdims=True)
            carry = carry + jnp.sum(dc, axis=0, keepdims=True)
        db_ref[...] = db

    return pl.pallas_call(body, out_shape=(jax.ShapeDtypeStruct((t, LANES), BF16), jax.ShapeDtypeStruct((1, LANES), F32)),
                          grid=(1,), in_specs=[_full((t, LANES)), pl.BlockSpec((t, LANES), lambda i: (0, fcol)),
                                               _full((1, LANES))],
                          out_specs=(_full((t, LANES)), _full((1, LANES))), compiler_params=_cp(("arbitrary",)),
                          name=name)(dc, zgf, b_pad)


ATT_BLOCK = 512
PAIRS = N_HEADS // 2
CQ_LANE = HEAD_DIM
CK_LANE = HEAD_DIM + 3
LSE_LANE = HEAD_DIM + 6


def _pick_col(x, idx):
    lane = lax.broadcasted_iota(jnp.int32, x.shape, 1)
    return jnp.sum(jnp.where(lane == idx, x, 0.0), axis=1, keepdims=True)


def _split3(x):
    hi = x.astype(BF16)
    r1 = x - hi.astype(F32)
    mid = r1.astype(BF16)
    lo = (r1 - mid.astype(F32)).astype(BF16)
    return hi, mid, lo


def _lanes_put(base, lane, start, vals):
    out = base
    for n, v in enumerate(vals):
        out = jnp.where(lane == start + n, v, out)
    return out


def _to_first_half(x, hh):
    return x if hh == 0 else pltpu.roll(x, HEAD_DIM, 1)


def _attn_prep(qkv, c, name, tm=512):
    t = qkv.shape[0]
    tm = min(tm, t)

    def body(q_ref, k_ref, v_ref, c_ref, qa_ref, ka_ref, va_ref):
        j = pl.program_id(1)
        lane = lax.broadcasted_iota(jnp.int32, (tm, LANES), 1)
        first = lane < HEAD_DIM
        q2, k2, v2 = q_ref[...].astype(F32), k_ref[...].astype(F32), v_ref[...].astype(F32)
        cc = c_ref[...]
        one = jnp.ones((tm, 1), F32)
        for hh in range(2):
            chi, cmid, clo = [v.astype(F32) for v in _split3(_pick_col(cc, 2 * j + hh))]
            qh = jnp.where(first, _to_first_half(q2, hh) * (HEAD_DIM ** -0.5), 0.0)
            kh = jnp.where(first, _to_first_half(k2, hh), 0.0)
            vh = jnp.where(first, _to_first_half(v2, hh), 0.0)
            qa = _lanes_put(qh, lane, CQ_LANE, [chi, cmid, clo, one, one, one])
            ka = _lanes_put(kh, lane, CQ_LANE, [one, one, one, -chi, -cmid, -clo, one, one, one])
            va = _lanes_put(vh, lane, CQ_LANE, [one, one, one])
            cols = slice(hh * LANES, (hh + 1) * LANES)
            qa_ref[:, cols] = qa.astype(BF16)
            ka_ref[:, cols] = ka.astype(BF16)
            va_ref[:, cols] = va.astype(BF16)

    blk = lambda off: pl.BlockSpec((tm, LANES), lambda i, j: (i, off + j))
    out = pl.BlockSpec((tm, 2 * LANES), lambda i, j: (i, j))
    shp = jax.ShapeDtypeStruct((t, N_HEADS * LANES), BF16)
    return pl.pallas_call(body, out_shape=(shp, shp, shp), grid=(t // tm, PAIRS),
                          in_specs=[blk(0), blk(PAIRS), blk(2 * PAIRS), pl.BlockSpec((tm, LANES), lambda i, j: (i, 0))],
                          out_specs=(out, out, out), compiler_params=_cp(("arbitrary",) * 2), name=name)(qkv, qkv, qkv, c)


def _causal_block(tb):
    return lax.broadcasted_iota(jnp.int32, (tb, tb), 0) >= lax.broadcasted_iota(jnp.int32, (tb, tb), 1)


def _causal_pairs(nb, key_major):
    pairs = [(q, k) for q in range(nb) for k in range(q + 1)]
    if key_major:
        pairs.sort(key=lambda qk: (qk[1], qk[0]))
    return (jnp.array([q for q, _ in pairs], jnp.int32), jnp.array([k for _, k in pairs], jnp.int32))


def _pair_grid_call(body, tables, arrays, in_specs, out_shapes, out_specs, scratch, side, name):
    n_steps = tables[0].shape[0]
    n_in, n_out = len(arrays), len(out_shapes)
    hosted = body
    if side is not None:
        s_in, s_out = len(side.arrays), len(side.out_shapes)

        def hosted(*refs):
            ins_end = 2 + n_in + s_in
            side_in, side_out = refs[2 + n_in:ins_end], refs[ins_end + n_out:ins_end + n_out + s_out]
            send, recv = refs[-2:]
            j, s = pl.program_id(0), pl.program_id(1)

            @pl.when((j == 0) & (s == 0))
            def _():
                side.start(side_in, side_out, send, recv)

            body(*refs[:2 + n_in], *refs[ins_end:ins_end + n_out], *refs[ins_end + n_out + s_out:-2])

            @pl.when((j == PAIRS - 1) & (s == n_steps - 1))
            def _():
                side.wait(side_in, side_out, send, recv)

        arrays = tuple(arrays) + tuple(side.arrays)
        in_specs = list(in_specs) + [HBM_SPEC] * s_in
        out_shapes = list(out_shapes) + side.out_shapes
        out_specs = list(out_specs) + [HBM_SPEC] * s_out
        scratch = list(scratch) + side.scratch
    grid_spec = pltpu.PrefetchScalarGridSpec(num_scalar_prefetch=2, grid=(PAIRS, n_steps), in_specs=in_specs,
                                             out_specs=tuple(out_specs), scratch_shapes=scratch)
    out = pl.pallas_call(hosted, out_shape=tuple(out_shapes), grid_spec=grid_spec,
                         compiler_params=_cp(("arbitrary",) * 2), name=name)(*tables, *arrays)
    return out[:n_out], out[n_out:]


def _attn_fwd(qa, ka, va, name, side=None):
    t = qa.shape[0]
    tb = min(ATT_BLOCK, t)
    q_tab, k_tab = _causal_pairs(t // tb, key_major=False)

    def body(q_tab_ref, k_tab_ref, q_ref, k_ref, v_ref, o_ref, lse_ref, m_s, acc_s):
        qi, kb = q_tab_ref[pl.program_id(1)], k_tab_ref[pl.program_id(1)]

        @pl.when(kb == 0)
        def _():
            m_s[...] = jnp.full(m_s.shape, NEG_INF, F32)
            acc_s[...] = jnp.zeros(acc_s.shape, F32)

        def step(diagonal):
            for hh in range(2):
                cols = slice(hh * LANES, (hh + 1) * LANES)
                sc = _dot(q_ref[:, cols], k_ref[:, cols], NT)
                if diagonal:
                    sc = jnp.where(_causal_block(tb), sc, NEG_INF)
                m_prev = m_s[hh]
                m_new = jnp.maximum(m_prev, jnp.max(sc, axis=1, keepdims=True))
                p = jnp.exp(sc - m_new)
                acc_s[hh] = jnp.exp(m_prev - m_new) * acc_s[hh] + _dot(p.astype(BF16), v_ref[:, cols], NN)
                m_s[hh] = m_new

        @pl.when(kb < qi)
        def _():
            step(False)

        @pl.when(kb == qi)
        def _():
            step(True)
            lane = lax.broadcasted_iota(jnp.int32, (tb, LANES), 1)
            outs, lses = [], []
            for hh in range(2):
                acc = acc_s[hh]
                l = _pick_col(acc, CQ_LANE)
                outs.append(acc / l)
                lses.append(m_s[hh] + jnp.log(l))
            o_ref[...] = jnp.where(lane < HEAD_DIM, outs[0], pltpu.roll(outs[1], HEAD_DIM, 1))
            lse_ref[...] = jnp.where(lane == 0, lses[0], jnp.where(lane == 1, lses[1], 0.0))

    qrow = lambda j, s, qt, kt: (qt[s], j)
    krow = lambda j, s, qt, kt: (kt[s], j)
    return _pair_grid_call(
        body, (q_tab, k_tab), (qa, ka, va),
        [pl.BlockSpec((tb, 2 * LANES), qrow), pl.BlockSpec((tb, 2 * LANES), krow), pl.BlockSpec((tb, 2 * LANES), krow)],
        [jax.ShapeDtypeStruct((t, D_ATT), F32), jax.ShapeDtypeStruct((t, PAIRS * LANES), F32)],
        [pl.BlockSpec((tb, LANES), qrow), pl.BlockSpec((tb, LANES), qrow)],
        [pltpu.VMEM((2, tb, 1), F32), pltpu.VMEM((2, tb, LANES), F32)], side, name)


def _attn_bwd_prep(qa, att, lse, datt, name, tm=512):
    t = qa.shape[0]
    tm = min(tm, t)

    def body(qa_ref, o_ref, lse_ref, do_ref, qb_ref, doa_ref):
        j = pl.program_id(1)
        lane = lax.broadcasted_iota(jnp.int32, (tm, LANES), 1)
        first = lane < HEAD_DIM
        do = do_ref[...]
        prod = do * o_ref[...]
        lse2 = lse_ref[...]
        for hh in range(2):
            cols = slice(hh * LANES, (hh + 1) * LANES)
            delta = jnp.sum(jnp.where(first if hh == 0 else ~first, prod, 0.0), axis=1, keepdims=True)
            doh = jnp.where(first, _to_first_half(do, hh), 0.0)
            doa_ref[:, cols] = _lanes_put(doh, lane, CQ_LANE, [v.astype(F32) for v in _split3(-delta)]).astype(BF16)
            nl = [v.astype(F32) for v in _split3(-_pick_col(lse2, hh))]
            qb_ref[:, cols] = _lanes_put(qa_ref[:, cols].astype(F32), lane, LSE_LANE, nl).astype(BF16)

    wide = pl.BlockSpec((tm, 2 * LANES), lambda i, j: (i, j))
    pair = pl.BlockSpec((tm, LANES), lambda i, j: (i, j))
    shp = jax.ShapeDtypeStruct((t, N_HEADS * LANES), BF16)
    return pl.pallas_call(body, out_shape=(shp, shp), grid=(t // tm, PAIRS), in_specs=[wide, pair, pair, pair],
                          out_specs=(wide, wide), compiler_params=_cp(("arbitrary",) * 2), name=name)(qa, att, lse, datt)


def _attn_bwd(qb, ka, va, doa, name, side=None):
    t = qb.shape[0]
    tb = min(ATT_BLOCK, t)
    nb = t // tb
    q_tab, k_tab = _causal_pairs(nb, key_major=True)

    def body(q_tab_ref, k_tab_ref, q_ref, k_ref, v_ref, do_ref, dq_ref, dk_ref, dv_ref, dk_s, dv_s):
        qi, kb = q_tab_ref[pl.program_id(1)], k_tab_ref[pl.program_id(1)]

        @pl.when(qi == kb)
        def _():
            dk_s[...] = jnp.zeros(dk_s.shape, F32)
            dv_s[...] = jnp.zeros(dv_s.shape, F32)

        def step(diagonal):
            rows = pl.ds(pl.multiple_of(qi * tb, tb), tb)
            for hh in range(2):
                cols = slice(hh * LANES, (hh + 1) * LANES)
                q, k, do = q_ref[:, cols], k_ref[:, cols], do_ref[:, cols]
                sc = _dot(q, k, NT)
                if diagonal:
                    sc = jnp.where(_causal_block(tb), sc, NEG_INF)
                p = jnp.exp(sc)
                ds = (p * _dot(do, v_ref[:, cols], NT)).astype(BF16)
                dv_s[:, cols] += _dot(p.astype(BF16), do, TN)
                dk_s[:, cols] += _dot(ds, q, TN)
                dq_new = _dot(ds, k, NN)

                @pl.when(kb == 0)
                def _():
                    dq_ref[rows, cols] = dq_new

                @pl.when(kb > 0)
                def _():
                    dq_ref[rows, cols] += dq_new

        @pl.when(qi == kb)
        def _():
            step(True)

        @pl.when(qi > kb)
        def _():
            step(False)

        @pl.when(qi == nb - 1)
        def _():
            dk_ref[...] = dk_s[...]
            dv_ref[...] = dv_s[...].astype(BF16)

    qrow = lambda j, s, qt, kt: (qt[s], j)
    krow = lambda j, s, qt, kt: (kt[s], j)
    blk = (tb, 2 * LANES)
    wide = (t, N_HEADS * LANES)
    return _pair_grid_call(
        body, (q_tab, k_tab), (qb, ka, va, doa),
        [pl.BlockSpec(blk, qrow), pl.BlockSpec(blk, krow), pl.BlockSpec(blk, krow), pl.BlockSpec(blk, qrow)],
        [jax.ShapeDtypeStruct(wide, F32), jax.ShapeDtypeStruct(wide, F32), jax.ShapeDtypeStruct(wide, BF16)],
        [pl.BlockSpec((t, 2 * LANES), lambda j, s, qt, kt: (0, j)), pl.BlockSpec(blk, krow), pl.BlockSpec(blk, krow)],
        [pltpu.VMEM(blk, F32), pltpu.VMEM(blk, F32)], side, name)


def _attn_bwd_post(dqa, dka, dva, name, tm=256):
    t = dqa.shape[0]
    tm = min(tm, t)

    def body(dq_ref, dk_ref, dv_ref, o_ref, dc_ref):
        lane = lax.broadcasted_iota(jnp.int32, (tm, LANES), 1)
        first = lane < HEAD_DIM
        dc = jnp.zeros((tm, LANES), F32)
        for j in range(PAIRS):
            packed = []
            for ref, gain in ((dq_ref, HEAD_DIM ** -0.5), (dk_ref, 1.0), (dv_ref, 1.0)):
                even = ref[:, 2 * j * LANES:(2 * j + 1) * LANES].astype(F32)
                odd = ref[:, (2 * j + 1) * LANES:(2 * j + 2) * LANES].astype(F32)
                packed.append((jnp.where(first, even, pltpu.roll(odd, HEAD_DIM, 1)) * gain).astype(BF16))
                if ref is dq_ref:
                    dc = dc + jnp.where(lane == 2 * j, _pick_col(even, CQ_LANE), 0.0)
                    dc = dc + jnp.where(lane == 2 * j + 1, _pick_col(odd, CQ_LANE), 0.0)
                if ref is dk_ref:
                    dc = dc - jnp.where(lane == 2 * j, _pick_col(even, CK_LANE), 0.0)
                    dc = dc - jnp.where(lane == 2 * j + 1, _pick_col(odd, CK_LANE), 0.0)
            for part, val in enumerate(packed):
                o_ref[:, (part * PAIRS + j) * LANES:(part * PAIRS + j + 1) * LANES] = val
        dc_ref[...] = dc

    wide = _rows(tm, N_HEADS * LANES)
    return pl.pallas_call(body, out_shape=(jax.ShapeDtypeStruct((t, 3 * D_ATT), BF16), jax.ShapeDtypeStruct((t, LANES), F32)),
                          grid=(t // tm,), in_specs=[wide, wide, wide], out_specs=(_rows(tm, 3 * D_ATT), _rows(tm, LANES)),
                          compiler_params=_cp(("arbitrary",)), name=name)(dqa, dka, dva)


GELU_K = 0.7978845608028654
GELU_A = 0.044715


def _gelu(x):
    th = jnp.tanh(GELU_K * (x + GELU_A * x * x * x))
    return 0.5 * x * (1.0 + th), th


def _group_mean_matrix():
    r = jnp.arange(D_GM)[:, None] // HEAD_DIM
    s = jnp.arange(D_GM)[None, :] // HEAD_DIM
    return jnp.where(r == s, 1.0 / HEAD_DIM, 0.0).astype(BF16)


def _group_mean(x, mean_mat):
    hi = x.astype(BF16)
    lo = (x - hi.astype(F32)).astype(BF16)
    return _dot(hi, mean_mat, NN) + _dot(lo, mean_mat, NN)


def _gm_forward_parts(g, w_ref, bias, gain, mean_mat):
    gel, _ = _gelu(g)
    u, vv = gel[:, :D_GM], gel[:, D_GM:]
    mu = _group_mean(vv, mean_mat)
    d = vv - mu
    rstd = lax.rsqrt(_group_mean(d * d, mean_mat) + EPS)
    xhat = d * rstd
    vn = (xhat * gain).astype(BF16)
    first = lax.broadcasted_iota(jnp.int32, (CHUNK, LANES), 1) < HEAD_DIM
    tri = lax.broadcasted_iota(jnp.int32, (CHUNK, CHUNK), 0) >= lax.broadcasted_iota(jnp.int32, (CHUNK, CHUNK), 1)
    wm = [jnp.where(tri, w_ref[grp], 0.0).astype(BF16) for grp in range(w_ref.shape[0])]
    chunks = []
    for ck in range(g.shape[0] // CHUNK):
        parts = []
        for jp in range(D_GM // LANES):
            vp = vn[ck * CHUNK:(ck + 1) * CHUNK, jp * LANES:(jp + 1) * LANES]
            parts.append(jnp.where(first, _dot(wm[2 * jp], vp, NN), _dot(wm[2 * jp + 1], vp, NN)))
        chunks.append(jnp.concatenate(parts, axis=1) + bias)
    return u, xhat, rstd, vn, jnp.concatenate(chunks, axis=0), wm


GM_ROWS = 512


def _gmlp_fwd(zgf, w_s, bias_full, gain, mean_mat, name):
    t = zgf.shape[0]
    tm = min(GM_ROWS, t)

    def body(g_ref, w_ref, b_ref, gain_ref, mm_ref, o_ref):
        u, _, _, _, mixed, _ = _gm_forward_parts(g_ref[...], w_ref, b_ref[...], gain_ref[...], mm_ref[...])
        o_ref[...] = u * mixed

    return pl.pallas_call(body, out_shape=jax.ShapeDtypeStruct((t, D_GM), F32), grid=(t // tm,),
                          in_specs=[_rows(tm, 2 * D_GM), _full(w_s.shape), _full((CHUNK, D_GM)), _full((1, D_GM)),
                                    _full((D_GM, D_GM))],
                          out_specs=_rows(tm, D_GM), compiler_params=_cp(("arbitrary",)),
                          name=name)(zgf, w_s, bias_full, gain, mean_mat)


def _gmlp_bwd(zgf, dgm, w_s, bias_full, gain, mean_mat, name):
    t = zgf.shape[0]
    tm = min(GM_ROWS, t)

    def body(g_ref, d_ref, w_ref, b_ref, gain_ref, mm_ref, dg_ref, dw_ref, dmix_ref, dgain_ref):
        g, gain, mean_mat = g_ref[...], gain_ref[...], mm_ref[...]
        u, xhat, rstd, vn, mixed, wm = _gm_forward_parts(g, w_ref, b_ref[...], gain, mean_mat)
        dgm_v = d_ref[...]
        du = dgm_v * mixed
        dmixed = dgm_v * u
        dm_b = dmixed.astype(BF16)
        first = lax.broadcasted_iota(jnp.int32, (CHUNK, LANES), 1) < HEAD_DIM
        tri = lax.broadcasted_iota(jnp.int32, (CHUNK, CHUNK), 0) >= lax.broadcasted_iota(jnp.int32, (CHUNK, CHUNK), 1)

        @pl.when(pl.program_id(0) == 0)
        def _():
            dw_ref[...] = jnp.zeros(dw_ref.shape, F32)
            dmix_ref[...] = jnp.zeros(dmix_ref.shape, F32)
            dgain_ref[...] = jnp.zeros(dgain_ref.shape, F32)

        dw = [jnp.zeros((CHUNK, CHUNK), F32) for _ in wm]
        dmix = jnp.zeros((CHUNK, D_GM), F32)
        dvn_chunks = []
        for ck in range(tm // CHUNK):
            rows = slice(ck * CHUNK, (ck + 1) * CHUNK)
            dmix = dmix + dmixed[rows]
            dvn_parts = []
            for jp in range(D_GM // LANES):
                vp = vn[rows, jp * LANES:(jp + 1) * LANES]
                dmp = dm_b[rows, jp * LANES:(jp + 1) * LANES]
                halves = []
                for hh in range(2):
                    sel = first if hh == 0 else ~first
                    grp = 2 * jp + hh
                    dw[grp] = dw[grp] + _dot(jnp.where(sel, dmp, jnp.zeros_like(dmp)), vp, NT)
                    halves.append(_dot(wm[grp], dmp, TN))
                dvn_parts.append(jnp.where(first, halves[0], halves[1]))
            dvn_chunks.append(jnp.concatenate(dvn_parts, axis=1))
        dvn = jnp.concatenate(dvn_chunks, axis=0)
        for grp, dwg in enumerate(dw):
            dw_ref[grp] += jnp.where(tri, dwg, 0.0)
        dmix_ref[...] += dmix
        dgain_ref[...] += jnp.sum(dvn * xhat, axis=0, keepdims=True)
        dxhat = dvn * gain
        m1 = _group_mean(dxhat, mean_mat)
        m2 = _group_mean(dxhat * xhat, mean_mat)
        dvv = rstd * (dxhat - m1 - xhat * m2)
        gel, th = _gelu(g)
        dgel = 0.5 * (1.0 + th) + 0.5 * g * (1.0 - th * th) * GELU_K * (1.0 + 3.0 * GELU_A * g * g)
        dg_ref[...] = (jnp.concatenate([du, dvv], axis=1) * dgel).astype(BF16)

    return pl.pallas_call(
        body, out_shape=(jax.ShapeDtypeStruct((t, 2 * D_GM), BF16), jax.ShapeDtypeStruct(w_s.shape, F32),
                         jax.ShapeDtypeStruct((CHUNK, D_GM), F32), jax.ShapeDtypeStruct((1, D_GM), F32)),
        grid=(t // tm,),
        in_specs=[_rows(tm, 2 * D_GM), _rows(tm, D_GM), _full(w_s.shape), _full((CHUNK, D_GM)), _full((1, D_GM)),
                  _full((D_GM, D_GM))],
        out_specs=(_rows(tm, 2 * D_GM), _full(w_s.shape), _full((CHUNK, D_GM)), _full((1, D_GM))),
        compiler_params=_cp(("arbitrary",)), name=name)(zgf, dgm, w_s, bias_full, gain, mean_mat)


def _row_tile(r, c, budget=1 << 19):
    best = None
    for tr in range(8, r + 1, 8):
        if r % tr == 0 and tr * c <= budget:
            best = tr
    return best if best is not None else r


def _adamw(w, g, m, v, name):
    nl, r, c = w.shape
    tr = _row_tile(r, c, 1 << 18)
    c1 = 1.0 - ADAM_B1 ** ADAM_STEP
    c2 = 1.0 - ADAM_B2 ** ADAM_STEP

    def body(w_ref, g_ref, m_ref, v_ref, d_ref, mo_ref, vo_ref):
        gv = g_ref[...]
        mn = ADAM_B1 * m_ref[...] + (1.0 - ADAM_B1) * gv
        vn = ADAM_B2 * v_ref[...] + (1.0 - ADAM_B2) * jnp.square(gv)
        mo_ref[...] = mn
        vo_ref[...] = vn
        d_ref[...] = -ADAM_LR * ((mn / c1) / (jnp.sqrt(vn / c2) + ADAM_EPS) + ADAM_WD * w_ref[...])

    spec = pl.BlockSpec((None, tr, c), lambda l, i: (l, i, 0))
    shp = jax.ShapeDtypeStruct(w.shape, F32)
    return pl.pallas_call(body, out_shape=(shp, shp, shp), grid=(nl, r // tr), in_specs=[spec] * 4,
                          out_specs=(spec, spec, spec), compiler_params=_cp(("arbitrary",) * 2), name=name)(w, g, m, v)


def _add_sibling(g, recv, c_idx, wire_dtype, name):
    nj, _, h, c = g.shape
    tr = _row_tile(h, c)

    def body(c_ref, g_ref, r_ref, o_ref):
        o_ref[...] = (g_ref[...] + r_ref[...]).astype(wire_dtype)

    grid_spec = pltpu.PrefetchScalarGridSpec(
        num_scalar_prefetch=1, grid=(nj, h // tr),
        in_specs=[pl.BlockSpec((None, None, tr, c), lambda j, i, c_ref: (j, c_ref[0], i, 0)),
                  pl.BlockSpec((None, tr, c), lambda j, i, c_ref: (j, i, 0))],
        out_specs=pl.BlockSpec((None, tr, c), lambda j, i, c_ref: (j, i, 0)))
    return pl.pallas_call(body, out_shape=jax.ShapeDtypeStruct((nj, h, c), wire_dtype), grid_spec=grid_spec,
                          compiler_params=_cp(("arbitrary",) * 2), name=name)(c_idx, g, recv)


def _add_chips(own, parts, place, name):
    _, h, c = own.shape
    tr = _row_tile(h, c)

    def body(p_ref, o_ref, a_ref, b_ref, c_ref, out_ref):
        out_ref[...] = ((o_ref[...].astype(F32) + a_ref[...].astype(F32)) + b_ref[...].astype(F32)) + c_ref[...].astype(F32)

    def other(k):
        return pl.BlockSpec((None, tr, c), lambda i, p_ref: (jnp.bitwise_xor(p_ref[0], k), i, 0))

    grid_spec = pltpu.PrefetchScalarGridSpec(
        num_scalar_prefetch=1, grid=(h // tr,),
        in_specs=[pl.BlockSpec((None, tr, c), lambda i, p_ref: (p_ref[0], i, 0)), other(1), other(2), other(3)],
        out_specs=pl.BlockSpec((None, tr, c), lambda i, p_ref: (p_ref[1], i, 0)))
    return pl.pallas_call(body, out_shape=jax.ShapeDtypeStruct((2, h, c), F32), grid_spec=grid_spec,
                          compiler_params=_cp(("arbitrary",)), name=name)(place, own, parts, parts, parts)


HBM_SPEC = pl.BlockSpec(memory_space=pltpu.HBM)


def _place():
    x, y, c = lax.axis_index("x"), lax.axis_index("y"), lax.axis_index("c")
    chips = [(1 - x, y), (x, 1 - y), (1 - x, 1 - y)]
    return x, y, c, 2 * x + y, chips, [2 * px + py for px, py in chips]


def _remote(src, dst, send_sem, recv_sem, dev):
    return pltpu.make_async_remote_copy(src_ref=src, dst_ref=dst, send_sem=send_sem, recv_sem=recv_sem,
                                        device_id=dev, device_id_type=MESH)


def _comm_call(body, arrays, out_shapes, sems, name, aliases=None):
    n = len(arrays)
    return pl.pallas_call(
        body, out_shape=tuple(out_shapes), in_specs=[HBM_SPEC] * n, out_specs=tuple([HBM_SPEC] * len(out_shapes)),
        scratch_shapes=[pltpu.SemaphoreType.DMA(s) for s in sems], input_output_aliases=aliases or {},
        compiler_params=pltpu.CompilerParams(has_side_effects=True), name=name)(*arrays)


class _SideCopies:
    def __init__(self, arrays, out_shapes, sem_shape, sends, recvs):
        self.arrays, self.out_shapes, self.sem_shape = list(arrays), list(out_shapes), sem_shape
        self.sends, self.recvs = sends, recvs

    @property
    def scratch(self):
        return [pltpu.SemaphoreType.DMA(self.sem_shape), pltpu.SemaphoreType.DMA(self.sem_shape)]

    def start(self, ins, outs, send, recv):
        for cp in self.sends(ins, outs, send, recv):
            cp.start()

    def wait(self, ins, outs, send, recv):
        for cp in self.recvs(ins, outs, send, recv):
            cp.wait_recv()
        for cp in self.sends(ins, outs, send, recv):
            cp.wait_send()

    def run(self, name):
        n = len(self.arrays)

        def body(*refs):
            ins, outs = refs[:n], refs[n:n + len(self.out_shapes)]
            send, recv = refs[-2:]
            self.start(ins, outs, send, recv)
            self.wait(ins, outs, send, recv)

        return _comm_call(body, self.arrays, self.out_shapes, [self.sem_shape] * 2, name)


def _gather_over_chips(shards):
    n = len(shards)

    def sends(ins, outs, send, recv):
        _, _, c, me, chips, _ = _place()
        return [_remote(ins[a].at[c], outs[a].at[me, c], send.at[a, k], recv.at[a, k], (px, py, c))
                for a in range(n) for k, (px, py) in enumerate(chips)]

    def recvs(ins, outs, send, recv):
        x, y, c, _, _, cidx = _place()
        return [_remote(outs[a].at[cidx[k], c], outs[a].at[cidx[k], c], send.at[a, k], recv.at[a, k], (x, y, 1 - c))
                for a in range(n) for k in range(3)]

    out_shapes = [jax.ShapeDtypeStruct((N_CHIPS,) + s.shape, s.dtype) for s in shards]
    return _SideCopies(shards, out_shapes, (n, 3), sends, recvs)


def _gather_finish(shards, partial, name):
    n = len(shards)

    def body(*refs):
        ins, part, outs = refs[:n], refs[n:2 * n], refs[2 * n:3 * n]
        send, recv = refs[3 * n:]
        x, y, c, me, _, cidx = _place()
        sib = (x, y, 1 - c)
        cps = []
        for a in range(n):
            for k in range(3):
                cps.append(_remote(part[a].at[cidx[k], c], outs[a].at[cidx[k], c], send.at[a, k], recv.at[a, k], sib))
            cps.append(_remote(ins[a], outs[a].at[me], send.at[a, 3], recv.at[a, 3], sib))
        for cp in cps:
            cp.start()
        for a in range(n):
            for k, blk in enumerate([outs[a].at[cidx[k], 1 - c] for k in range(3)] + [outs[a].at[me]]):
                _remote(blk, blk, send.at[a, k], recv.at[a, k], sib).wait_recv()
        for cp in cps:
            cp.wait_send()

    out_shapes = [jax.ShapeDtypeStruct(p.shape, p.dtype) for p in partial]
    return _comm_call(body, list(shards) + list(partial), out_shapes, [(n, 4), (n, 4)], name,
                      aliases={n + a: a for a in range(n)})


def _send_sibling_halves(grads, name):
    n = len(grads)

    def body(*refs):
        ins, outs = refs[:n], refs[n:2 * n]
        send, recv = refs[2 * n:]
        x, y, c, _, _, _ = _place()
        sib = (x, y, 1 - c)
        cps = []
        for a in range(n):
            for j in range(N_CHIPS):
                cp = _remote(ins[a].at[j, 1 - c], outs[a].at[j], send.at[a, j], recv.at[a, j], sib)
                cp.start()
                cps.append(cp)
        for cp in cps:
            cp.wait_recv()
        for cp in cps:
            cp.wait_send()

    out_shapes = [jax.ShapeDtypeStruct((g.shape[0],) + g.shape[2:], g.dtype) for g in grads]
    return _comm_call(body, grads, out_shapes, [(n, N_CHIPS), (n, N_CHIPS)], name)


def _scatter_over_chips(sums):
    n = len(sums)

    def sends(ins, outs, send, recv):
        _, _, c, me, chips, cidx = _place()
        return [_remote(ins[a].at[cidx[k]], outs[a].at[me], send.at[a, k], recv.at[a, k], (px, py, c))
                for a in range(n) for k, (px, py) in enumerate(chips)]

    def recvs(ins, outs, send, recv):
        x, y, c, _, _, cidx = _place()
        return [_remote(outs[a].at[cidx[k]], outs[a].at[cidx[k]], send.at[a, k], recv.at[a, k], (x, y, 1 - c))
                for a in range(n) for k in range(3)]

    return _SideCopies(sums, [jax.ShapeDtypeStruct(s.shape, s.dtype) for s in sums], (n, 3), sends, recvs)


def _exchange_halves(halves, name):
    n = len(halves)

    def body(*refs):
        ins, bufs = refs[:n], refs[n:2 * n]
        send, recv = refs[2 * n:]
        x, y, c, _, _, _ = _place()
        sib = (x, y, 1 - c)
        cps = []
        for a in range(n):
            cp = _remote(ins[a].at[c], bufs[a].at[c], send.at[a], recv.at[a], sib)
            cp.start()
            cps.append(cp)
        for a in range(n):
            blk = bufs[a].at[1 - c]
            _remote(blk, blk, send.at[a], recv.at[a], sib).wait_recv()
        for cp in cps:
            cp.wait_send()

    out_shapes = [jax.ShapeDtypeStruct(s.shape, s.dtype) for s in halves]
    return _comm_call(body, halves, out_shapes, [(n,), (n,)], name, aliases={a: a for a in range(n)})


def _gather_chips(slices, name):
    n = len(slices)

    def body(*refs):
        ins, outs = refs[:n], refs[n:2 * n]
        send, recv = refs[2 * n:]
        x, y, c, me, chips, cidx = _place()
        sib = (x, y, 1 - c)
        cps = []
        for a in range(n):
            for k, dev in enumerate([(px, py, c) for px, py in chips] + [sib]):
                cp = _remote(ins[a], outs[a].at[me], send.at[a, k], recv.at[a, k], dev)
                cp.start()
                cps.append(cp)
        for a in range(n):
            for k, slot in enumerate(cidx + [me]):
                blk = outs[a].at[slot]
                _remote(blk, blk, send.at[a, k], recv.at[a, k], sib).wait_recv()
        for cp in cps:
            cp.wait_send()

    out_shapes = [jax.ShapeDtypeStruct((N_CHIPS,) + s.shape, s.dtype) for s in slices]
    return _comm_call(body, slices, out_shapes, [(n, 4), (n, 4)], name)


def _rs_begin(grads, c_idx, wire_dtype, tag):
    split = [g.reshape(g.shape[0], 2, g.shape[1] // 2, g.shape[2]) for g in grads]
    recv = _send_sibling_halves(split, "rs_sibling_" + tag)
    sums = [_add_sibling(g, r, c_idx, wire_dtype, "rs_add_sibling_" + tag) for g, r in zip(split, recv)]
    return _scatter_over_chips(sums)


def _rs_end(scatter, parts, place, tag):
    halves = [_add_chips(s, p, place, "rs_add_chips_" + tag) for s, p in zip(scatter.arrays, parts)]
    both = _exchange_halves(halves, "rs_halves_" + tag)
    return [b.reshape(b.shape[0] * b.shape[1], b.shape[2]) for b in both]


SMALL_ORDER = ("gm_w_s", "mix_pre_norm", "mix_post_norm", "mix_out_norm", "ffn_pre_norm", "ffn_post_norm", "ple_norm",
               "gm_v_norm", "gm_b_s", "b_forget")
SMALL_ROWS_MULTIPLE = 64


def _pack_small(parts):
    flat = []
    for nme in SMALL_ORDER:
        v = parts[nme].reshape(-1)
        pad = (-v.shape[0]) % LANES
        flat.append(jnp.pad(v, (0, pad)) if pad else v)
    v = jnp.concatenate(flat)
    rows = v.shape[0] // LANES
    pad_rows = (-rows) % SMALL_ROWS_MULTIPLE
    return jnp.pad(v.reshape(rows, LANES), ((0, pad_rows), (0, 0)))


def _unpack_small(packed, shapes):
    flat = packed.reshape(-1)
    out, off = {}, 0
    for nme in SMALL_ORDER:
        size = 1
        for s in shapes[nme]:
            size *= s
        out[nme] = flat[off:off + size].reshape(shapes[nme])
        off += size + ((-size) % LANES)
    return out


def kernel(x, p, mix_pre_norm, mix_post_norm, w_in, b_forget, gm_v_norm, gm_w_s, gm_b_s, mix_out_norm, w_out, ffn_pre_norm, ffn_post_norm, w_ffn_in, w_ffn_out, w_ple, ple_norm, w_ple_gate, loss_target, m_mix_pre_norm, m_mix_post_norm, m_w_in, m_b_forget, m_gm_v_norm, m_gm_w_s, m_gm_b_s, m_mix_out_norm, m_w_out, m_ffn_pre_norm, m_ffn_post_norm, m_w_ffn_in, m_w_ffn_out, m_w_ple, m_ple_norm, m_w_ple_gate, v_mix_pre_norm, v_mix_post_norm, v_w_in, v_b_forget, v_gm_v_norm, v_gm_w_s, v_gm_b_s, v_mix_out_norm, v_w_out, v_ffn_pre_norm, v_ffn_post_norm, v_w_ffn_in, v_w_ffn_out, v_w_ple, v_ple_norm, v_w_ple_gate):
    weights = dict(mix_pre_norm=mix_pre_norm, mix_post_norm=mix_post_norm, w_in=w_in, b_forget=b_forget,
                   gm_v_norm=gm_v_norm, gm_w_s=gm_w_s, gm_b_s=gm_b_s, mix_out_norm=mix_out_norm, w_out=w_out,
                   ffn_pre_norm=ffn_pre_norm, ffn_post_norm=ffn_post_norm, w_ffn_in=w_ffn_in, w_ffn_out=w_ffn_out,
                   w_ple=w_ple, ple_norm=ple_norm, w_ple_gate=w_ple_gate)
    mom_m = dict(mix_pre_norm=m_mix_pre_norm, mix_post_norm=m_mix_post_norm, w_in=m_w_in, b_forget=m_b_forget,
                 gm_v_norm=m_gm_v_norm, gm_w_s=m_gm_w_s, gm_b_s=m_gm_b_s, mix_out_norm=m_mix_out_norm, w_out=m_w_out,
                 ffn_pre_norm=m_ffn_pre_norm, ffn_post_norm=m_ffn_post_norm, w_ffn_in=m_w_ffn_in,
                 w_ffn_out=m_w_ffn_out, w_ple=m_w_ple, ple_norm=m_ple_norm, w_ple_gate=m_w_ple_gate)
    mom_v = dict(mix_pre_norm=v_mix_pre_norm, mix_post_norm=v_mix_post_norm, w_in=v_w_in, b_forget=v_b_forget,
                 gm_v_norm=v_gm_v_norm, gm_w_s=v_gm_w_s, gm_b_s=v_gm_b_s, mix_out_norm=v_mix_out_norm, w_out=v_w_out,
                 ffn_pre_norm=v_ffn_pre_norm, ffn_post_norm=v_ffn_post_norm, w_ffn_in=v_w_ffn_in,
                 w_ffn_out=v_w_ffn_out, w_ple=v_w_ple, ple_norm=v_ple_norm, w_ple_gate=v_w_ple_gate)
    big = ("w_in", "w_out", "w_ffn_in", "w_ffn_out", "w_ple", "w_ple_gate")
    depth = w_in.shape[0]
    t, d = x.shape[1], x.shape[2]
    d_ff = w_ffn_out.shape[1] * N_CHIPS
    c_idx = lax.axis_index("c").astype(jnp.int32).reshape(1)
    place = jnp.stack([2 * lax.axis_index("x") + lax.axis_index("y"), lax.axis_index("c")]).astype(jnp.int32)
    h = x[0]
    target = loss_target[0]
    mean_mat = _group_mean_matrix()

    def row(a, i):
        return a[i].reshape(1, -1)

    def shards_of(i):
        shards = [weights[nme][i].astype(BF16) for nme in big]
        return [s.reshape(2, s.shape[0] // 2, s.shape[1]) for s in shards]

    saved = []
    hn = _norm_cast(h, row(mix_pre_norm, 0), "norm_first")
    by_cols = lambda g: g.transpose(1, 0, 2).reshape(g.shape[1], N_CHIPS * g.shape[2])
    by_rows = lambda g: g.reshape(N_CHIPS * g.shape[1], g.shape[2])
    whole = lambda g: g.reshape(N_CHIPS, g.shape[2] * 2, g.shape[3])
    gather = _gather_over_chips(shards_of(0)[:1])
    w_in_g = _gather_finish(gather.arrays, gather.run("gather_first"), "gather_finish_first")[0]
    for i in range(depth):
        w_in_f = by_cols(whole(w_in_g))
        w_qkv = w_in_f[:, :3 * D_ATT]
        w_gf = jnp.concatenate([w_in_f[:, 3 * D_ATT + N_HEADS:], w_in_f[:, 3 * D_ATT:3 * D_ATT + N_HEADS],
                                jnp.zeros((d, LANES - N_HEADS), BF16)], axis=1)
        b_pad = jnp.pad(b_forget[i], (0, LANES - N_HEADS)).reshape(1, LANES)
        bias_full = jnp.repeat(gm_b_s[i].T, HEAD_DIM, axis=1)
        gain_v = row(gm_v_norm, i)

        qkv = _mm(hn, w_qkv, "nn", BF16, "mm_qkv")
        zgf = _mm(hn, w_gf, "nn", F32, "mm_gf", tn_cap=384)
        qa, ka, va = _attn_prep(qkv, _gate_fwd(zgf, b_pad, "gate_fwd"), "attn_prep")
        last = i + 1 == depth
        gather = _gather_over_chips(shards_of(i)[1:] + ([] if last else shards_of(i + 1)[:1]))
        (att, lse), partial = _attn_fwd(qa, ka, va, "attn_fwd_last" if last else "attn_fwd", gather)
        gathered = _gather_finish(gather.arrays, partial, "gather_finish_last" if last else "gather_finish")
        w_out_f, w_fi_f, w_fo_f = by_rows(whole(gathered[0])), by_cols(whole(gathered[1])), by_rows(whole(gathered[2]))
        w_ple_f, w_pg_f = by_cols(whole(gathered[3])), by_rows(whole(gathered[4]))
        w_in_g = None if last else gathered[5]
        gm = _gmlp_fwd(zgf, gm_w_s[i], bias_full, gain_v, mean_mat, "gmlp_fwd")
        mixed = _mixnorm_fwd(att, gm, row(mix_out_norm, i), "mixnorm_fwd")
        o = _mm(mixed, w_out_f, "nn", F32, "mm_out")
        h1, hn2 = _resid_norm(h, o, row(mix_post_norm, i), row(ffn_pre_norm, i), "resid_mix")
        s, ab = _ffn_in(hn2, w_fi_f, "ffn_in")
        o2 = _mm(s, w_fo_f, "nn", F32, "mm_ffn_out")
        h2, hr = _resid_norm(h1, o2, row(ffn_post_norm, i), None, "resid_ffn")
        pe = _mm(p[i, 0], w_ple_f, "nn", F32, "mm_ple")
        gl = _mm(hr, w_pg_f, "nn", F32, "mm_ple_gate")
        g_next = row(mix_pre_norm, i + 1) if i + 1 < depth else row(mix_pre_norm, 0)
        h3, hn_next = _ple_fwd(h2, pe, gl, row(ple_norm, i), g_next, "ple_fwd")
        saved.append(dict(h=h, hn=hn, qa=qa, ka=ka, va=va, zgf=zgf, att=att, lse=lse, gm=gm, mixed=mixed,
                          o=o, h1=h1, hn2=hn2, ab=ab, s=s, o2=o2, h2=h2, hr=hr, pe=pe, gl=gl, w_qkv=w_qkv, w_gf=w_gf,
                          w_out=w_out_f, w_fi=w_fi_f, w_fo=w_fo_f, w_pg=w_pg_f, b_pad=b_pad, bias_full=bias_full,
                          gain_v=gain_v))
        h, hn = h3, hn_next

    dh, loss_blk = _loss_head(h, target, "loss_head")
    loss = lax.psum(loss_blk[0, 0], ("x", "y", "c"))

    small = {nme: [None] * depth for nme in SMALL_ORDER}
    big_grads = {nme: [None] * depth for nme in big}
    waiting = []
    for i in reversed(range(depth)):
        sv = saved[i]
        dgl, dpe, small["ple_norm"][i] = _ple_bwd(dh, sv["pe"], sv["gl"], row(ple_norm, i), "ple_bwd")
        dhr = _mm(dgl, sv["w_pg"], "nt", F32, "mm_d_hr")
        g_pg = _mm(sv["hr"], dgl, "tn", F32, "mm_dw_ple_gate").reshape(N_CHIPS, -1, d)
        g_ple = _mm(p[i, 0], dpe, "tn", F32, "mm_dw_ple", chip_split=True)
        dh2, do2, small["ffn_post_norm"][i] = _join(dh, sv["h2"], None, dhr, sv["o2"], row(ffn_post_norm, i), "join_ple")
        dab = _ffn_bwd(do2, sv["w_fo"], sv["ab"], "ffn_bwd")
        g_fo = _mm(sv["s"], do2, "tn", F32, "mm_dw_ffn_out", tm=256).reshape(N_CHIPS, -1, d)
        dhn2 = _mm_halves_nt(dab, sv["w_fi"], "mm_d_hn2")
        g_fi = _mm_halves_tn(sv["hn2"], dab, "mm_dw_ffn_in")
        dh1, small["ffn_pre_norm"][i], do, small["mix_post_norm"][i] = _join(
            dh2, sv["h1"], row(ffn_pre_norm, i), dhn2, sv["o"], row(mix_post_norm, i), "join_ffn")
        dmixed = _mm(do, sv["w_out"], "nt", F32, "mm_d_mixed")
        g_out = _mm(sv["mixed"], do, "tn", F32, "mm_dw_out").reshape(N_CHIPS, -1, d)
        datt, dgm, small["mix_out_norm"][i] = _mixnorm_bwd(sv["att"], sv["gm"], row(mix_out_norm, i), dmixed, "mixnorm_bwd")
        dg, small["gm_w_s"][i], dmix_sum, small["gm_v_norm"][i] = _gmlp_bwd(
            sv["zgf"], dgm, gm_w_s[i], sv["bias_full"], sv["gain_v"], mean_mat, "gmlp_bwd")
        small["gm_b_s"][i] = dmix_sum.reshape(CHUNK, N_HEADS, HEAD_DIM).sum(-1).T
        qb, doa = _attn_bwd_prep(sv["qa"], sv["att"], sv["lse"], datt, "attn_bwd_prep")
        batch = waiting + [("w_out", i, g_out), ("w_ffn_in", i, g_fi), ("w_ffn_out", i, g_fo), ("w_ple", i, g_ple),
                           ("w_ple_gate", i, g_pg)]
        tag = "layer" if waiting else "top"
        scatter = _rs_begin([g for _, _, g in batch], c_idx, BF16, tag)
        (dqa, dka, dva), parts = _attn_bwd(qb, sv["ka"], sv["va"], doa, "attn_bwd_" + tag, scatter)
        for (nme, layer, _), g in zip(batch, _rs_end(scatter, parts, place, tag)):
            big_grads[nme][layer] = g
        dqkv, dc = _attn_bwd_post(dqa, dka, dva, "attn_bwd_post")
        dfl, db = _gate_bwd(dc, sv["zgf"], sv["b_pad"], "gate_bwd")
        small["b_forget"][i] = db[0, :N_HEADS]
        dgf = jnp.concatenate([dg, dfl], axis=1)
        dhn = _mm(dqkv, sv["w_qkv"], "nt", F32, "mm_d_hn_qkv")
        dhn = _mm(dgf, sv["w_gf"], "nt", F32, "mm_d_hn_gf", add=dhn)
        g_qkv = _mm(sv["hn"], dqkv, "tn", F32, "mm_dw_qkv")
        g_gf = _mm(sv["hn"], dgf, "tn", F32, "mm_dw_gf", tn_cap=384)
        g_in = jnp.concatenate([g_qkv, g_gf[:, 2 * D_GM:2 * D_GM + N_HEADS], g_gf[:, :2 * D_GM]], axis=1)
        g_in = g_in.reshape(d, N_CHIPS, -1).transpose(1, 0, 2)
        dh, small["mix_pre_norm"][i] = _join(dh1, sv["h"], row(mix_pre_norm, i), dhn, None, None, "join_mix")
        waiting = [("w_in", i, g_in)]
    scatter = _rs_begin([waiting[0][2]], c_idx, BF16, "tail")
    big_grads["w_in"][0] = _rs_end(scatter, scatter.run("rs_chips_tail"), place, "tail")[0]
    grad_x = dh.reshape(1, t, d)

    small_shapes = {nme: weights[nme].shape for nme in SMALL_ORDER}
    small_part = _pack_small({nme: jnp.stack([g.reshape(small_shapes[nme][1:]) for g in small[nme]])
                              for nme in SMALL_ORDER})
    rows_small = small_part.shape[0]
    scatter = _rs_begin([small_part.reshape(N_CHIPS, rows_small // N_CHIPS, LANES)], c_idx, F32, "small")
    small_slice = _rs_end(scatter, scatter.run("rs_chips_small"), place, "small")[0]
    small_all = _gather_chips([small_slice], "gather_small")[0].reshape(1, rows_small, LANES)
    sd, sm, sv_ = _adamw(_pack_small({n_: weights[n_] for n_ in SMALL_ORDER})[None], small_all,
                         _pack_small({n_: mom_m[n_] for n_ in SMALL_ORDER})[None],
                         _pack_small({n_: mom_v[n_] for n_ in SMALL_ORDER})[None], "adamw_small")
    grads = _unpack_small(small_all[0], small_shapes)
    deltas = _unpack_small(sd[0], small_shapes)
    new_m = _unpack_small(sm[0], small_shapes)
    new_v = _unpack_small(sv_[0], small_shapes)

    for nme in big:
        g = jnp.stack(big_grads[nme]).reshape(weights[nme].shape)
        grads[nme] = g
        deltas[nme], new_m[nme], new_v[nme] = _adamw(weights[nme], g, mom_m[nme], mom_v[nme], "adamw_" + nme)

    order = ("mix_pre_norm", "mix_post_norm", "w_in", "b_forget", "gm_v_norm", "gm_w_s", "gm_b_s", "mix_out_norm",
             "w_out", "ffn_pre_norm", "ffn_post_norm", "w_ffn_in", "w_ffn_out", "w_ple", "ple_norm", "w_ple_gate")
    return (loss, grad_x, *[grads[n_] for n_ in order], *[deltas[n_] for n_ in order], *[new_m[n_] for n_ in order],
            *[new_v[n_] for n_ in order])
```

```python
import functools

import jax
import jax.numpy as jnp
from jax import lax
from jax.experimental import pallas as pl
from jax.experimental.pallas import tpu as pltpu

F32 = jnp.float32
BF16 = jnp.bfloat16
MESH = pl.DeviceIdType.MESH
HIGHEST = lax.Precision.HIGHEST

EPS = 1e-6
NEG_INF = -1e30
N_HEADS = 8
HEAD_DIM = 64
D_ATT = N_HEADS * HEAD_DIM
D_GM = 512
CHUNK = 128
LANES = 128
N_CHIPS = 4
ADAM_LR = 0.001
ADAM_B1 = 0.9
ADAM_B2 = 0.999
ADAM_EPS = 1e-08
ADAM_WD = 0.01
ADAM_STEP = 10
VMEM_LIMIT = 56 * 1024 * 1024


def _cp(sem=None):
    return pltpu.CompilerParams(dimension_semantics=sem, vmem_limit_bytes=VMEM_LIMIT)


def _full(shape):
    return pl.BlockSpec(shape, lambda *_: (0,) * len(shape))


def _rows(tm, width, col_block=0):
    return pl.BlockSpec((tm, width), lambda i: (i, col_block))


def _dot(a, b, dims, precision=None):
    return lax.dot_general(a, b, (dims, ((), ())), preferred_element_type=F32, precision=precision)


NN = ((1,), (0,))
NT = ((1,), (1,))
TN = ((0,), (0,))


def _pick(n, cap):
    best = None
    for t in range(LANES, min(n, cap) + 1, LANES):
        if n % t == 0:
            best = t
    assert best is not None, (n, cap)
    return best


def _mm(a, b, mode, out_dtype, name, tm=None, tn_cap=1024, add=None, chip_split=False):
    dims = {"nn": NN, "nt": NT, "tn": TN}[mode]
    if mode == "tn":
        k, m = a.shape
    else:
        m, k = a.shape
    n = b.shape[0] if mode == "nt" else b.shape[1]
    if tm is None:
        tm = 512 if mode == "tn" else 1024
    tm = min(tm, m)
    tn = n // N_CHIPS if chip_split else _pick(n, tn_cap)
    assert m % tm == 0 and n % tn == 0

    def body(*refs):
        a_ref, b_ref = refs[0], refs[1]
        o_ref = refs[-1]
        acc = _dot(a_ref[...].astype(BF16), b_ref[...].astype(BF16), dims)
        if add is not None:
            acc = acc + refs[2][...]
        o_ref[...] = acc.astype(out_dtype)

    a_spec = pl.BlockSpec((k, tm), lambda i, j: (0, i)) if mode == "tn" else pl.BlockSpec((tm, k), lambda i, j: (i, 0))
    b_spec = pl.BlockSpec((tn, k), lambda i, j: (j, 0)) if mode == "nt" else pl.BlockSpec((k, tn), lambda i, j: (0, j))
    in_specs = [a_spec, b_spec]
    args = [a, b]
    if add is not None:
        in_specs.append(pl.BlockSpec((tm, tn), lambda i, j: (i, j)))
        args.append(add)
    if chip_split:
        out_shape = jax.ShapeDtypeStruct((N_CHIPS, m, tn), out_dtype)
        out_spec = pl.BlockSpec((None, tm, tn), lambda i, j: (j, i, 0))
    else:
        out_shape = jax.ShapeDtypeStruct((m, n), out_dtype)
        out_spec = pl.BlockSpec((tm, tn), lambda i, j: (i, j))
    return pl.pallas_call(body, out_shape=out_shape, grid=(m // tm, n // tn), in_specs=in_specs, out_specs=out_spec,
                          compiler_params=_cp(("arbitrary", "arbitrary")), name=name)(*args)


def _rms_inv(x):
    return lax.rsqrt(jnp.mean(x * x, axis=-1, keepdims=True) + EPS)


def _rms_bwd(x, gain, dy):
    inv = _rms_inv(x)
    xhat = x * inv
    dxn = dy if gain is None else dy * gain
    dx = inv * (dxn - xhat * jnp.mean(dxn * xhat, axis=-1, keepdims=True))
    return dx, dy * xhat


def _acc_rows(ref, val):
    s = jnp.sum(val, axis=0, keepdims=True)

    @pl.when(pl.program_id(0) == 0)
    def _():
        ref[...] = s

    @pl.when(pl.program_id(0) > 0)
    def _():
        ref[...] += s


def _norm_cast(h, gain, name, tm=512):
    t, d = h.shape

    def body(h_ref, g_ref, o_ref):
        x = h_ref[...]
        o_ref[...] = (x * _rms_inv(x) * g_ref[...]).astype(BF16)

    return pl.pallas_call(body, out_shape=jax.ShapeDtypeStruct((t, d), BF16), grid=(t // tm,),
                          in_specs=[_rows(tm, d), _full((1, d))], out_specs=_rows(tm, d),
                          compiler_params=_cp(("arbitrary",)), name=name)(h, gain)


def _resid_norm(h, o, g_post, g_next, name, tm=512):
    t, d = h.shape
    has_gain = g_next is not None

    def body(*refs):
        h_ref, o_ref, gp_ref = refs[:3]
        h1_ref, hn_ref = refs[-2:]
        ov = o_ref[...]
        h1 = h_ref[...] + ov * _rms_inv(ov) * gp_ref[...]
        h1_ref[...] = h1
        hn = h1 * _rms_inv(h1)
        if has_gain:
            hn = hn * refs[3][...]
        hn_ref[...] = hn.astype(BF16)

    args = [h, o, g_post] + ([g_next] if has_gain else [])
    in_specs = [_rows(tm, d), _rows(tm, d), _full((1, d))] + ([_full((1, d))] if has_gain else [])
    return pl.pallas_call(body, out_shape=(jax.ShapeDtypeStruct((t, d), F32), jax.ShapeDtypeStruct((t, d), BF16)),
                          grid=(t // tm,), in_specs=in_specs, out_specs=(_rows(tm, d), _rows(tm, d)),
                          compiler_params=_cp(("arbitrary",)), name=name)(*args)


def _ple_fwd(h2, pe, gl, g_ple, g_next, name, tm=512):
    t, d = h2.shape

    def body(h_ref, pe_ref, gl_ref, gp_ref, gn_ref, h3_ref, hn_ref):
        pv = pe_ref[...]
        e = pv * _rms_inv(pv) * gp_ref[...]
        h3 = h_ref[...] + jax.nn.sigmoid(gl_ref[...]) * e
        h3_ref[...] = h3
        hn_ref[...] = (h3 * _rms_inv(h3) * gn_ref[...]).astype(BF16)

    return pl.pallas_call(body, out_shape=(jax.ShapeDtypeStruct((t, d), F32), jax.ShapeDtypeStruct((t, d), BF16)),
                          grid=(t // tm,), in_specs=[_rows(tm, d)] * 3 + [_full((1, d))] * 2,
                          out_specs=(_rows(tm, d), _rows(tm, d)), compiler_params=_cp(("arbitrary",)),
                          name=name)(h2, pe, gl, g_ple, g_next)


def _loss_head(y, target, name, tm=512):
    t, d = y.shape

    def body(y_ref, t_ref, dy_ref, loss_ref):
        diff = y_ref[...] - t_ref[...]
        dy_ref[...] = diff * (1.0 / d)
        part = 0.5 * jnp.sum(jnp.mean(diff * diff, axis=-1, keepdims=True), axis=0, keepdims=True)
        part = jnp.broadcast_to(part, (8, LANES))

        @pl.when(pl.program_id(0) == 0)
        def _():
            loss_ref[...] = part

        @pl.when(pl.program_id(0) > 0)
        def _():
            loss_ref[...] += part

    return pl.pallas_call(body, out_shape=(jax.ShapeDtypeStruct((t, d), F32), jax.ShapeDtypeStruct((8, LANES), F32)),
                          grid=(t // tm,), in_specs=[_rows(tm, d)] * 2, out_specs=(_rows(tm, d), _full((8, LANES))),
                          compiler_params=_cp(("arbitrary",)), name=name)(y, target)


def _ple_bwd(dh3, pe, gl, g_ple, name, tm=512):
    t, d = dh3.shape

    def body(dh_ref, pe_ref, gl_ref, gp_ref, dgl_ref, dpe_ref, dg_ref):
        dh = dh_ref[...]
        pv = pe_ref[...]
        gp = gp_ref[...]
        gate = jax.nn.sigmoid(gl_ref[...])
        e = pv * _rms_inv(pv) * gp
        dgl_ref[...] = (dh * e * gate * (1.0 - gate)).astype(BF16)
        dpe, dg_rows = _rms_bwd(pv, gp, dh * gate)
        dpe_ref[...] = dpe.astype(BF16)
        _acc_rows(dg_ref, dg_rows)

    return pl.pallas_call(body, out_shape=(jax.ShapeDtypeStruct((t, d), BF16), jax.ShapeDtypeStruct((t, d), BF16),
                                           jax.ShapeDtypeStruct((1, d), F32)),
                          grid=(t // tm,), in_specs=[_rows(tm, d)] * 3 + [_full((1, d))],
                          out_specs=(_rows(tm, d), _rows(tm, d), _full((1, d))),
                          compiler_params=_cp(("arbitrary",)), name=name)(dh3, pe, gl, g_ple)


def _join(d_res, x_a, gain_a, d_a, x_b, gain_b, name, tm=512):
    t, d = d_res.shape
    has_ga = gain_a is not None
    has_b = x_b is not None

    def body(*refs):
        it = iter(refs)
        dres_ref, xa_ref = next(it), next(it)
        ga_ref = next(it) if has_ga else None
        da_ref = next(it)
        xb_ref, gb_ref = (next(it), next(it)) if has_b else (None, None)
        dout_ref = next(it)
        dga_ref = next(it) if has_ga else None
        db_ref, dgb_ref = (next(it), next(it)) if has_b else (None, None)
        dx, dg_rows = _rms_bwd(xa_ref[...], ga_ref[...] if has_ga else None, da_ref[...])
        dout = dres_ref[...] + dx
        dout_ref[...] = dout
        if has_ga:
            _acc_rows(dga_ref, dg_rows)
        if has_b:
            db, dgb_rows = _rms_bwd(xb_ref[...], gb_ref[...], dout)
            db_ref[...] = db.astype(BF16)
            _acc_rows(dgb_ref, dgb_rows)

    args, in_specs = [d_res, x_a], [_rows(tm, d), _rows(tm, d)]
    if has_ga:
        args.append(gain_a)
        in_specs.append(_full((1, d)))
    args.append(d_a)
    in_specs.append(_rows(tm, d))
    if has_b:
        args += [x_b, gain_b]
        in_specs += [_rows(tm, d), _full((1, d))]
    out_shape, out_specs = [jax.ShapeDtypeStruct((t, d), F32)], [_rows(tm, d)]
    if has_ga:
        out_shape.append(jax.ShapeDtypeStruct((1, d), F32))
        out_specs.append(_full((1, d)))
    if has_b:
        out_shape += [jax.ShapeDtypeStruct((t, d), BF16), jax.ShapeDtypeStruct((1, d), F32)]
        out_specs += [_rows(tm, d), _full((1, d))]
    return pl.pallas_call(body, out_shape=tuple(out_shape), grid=(t // tm,), in_specs=in_specs,
                          out_specs=tuple(out_specs), compiler_params=_cp(("arbitrary",)), name=name)(*args)


def _ffn_in(hn, w, name, tm=2048, tn=256):
    t, d = hn.shape
    f = w.shape[1] // 2
    tm = min(tm, t)
    nj = f // tn

    def body(h_ref, wa_ref, wb_ref, s_ref, ab_ref):
        h = h_ref[...]
        a = _dot(h, wa_ref[...], NN)
        b = _dot(h, wb_ref[...], NN)
        ab_ref[0] = a.astype(BF16)
        ab_ref[1] = b.astype(BF16)
        s_ref[...] = (a * jax.nn.sigmoid(a) * b).astype(BF16)

    return pl.pallas_call(
        body, out_shape=(jax.ShapeDtypeStruct((t, f), BF16), jax.ShapeDtypeStruct((2, t, f), BF16)), grid=(t // tm, nj),
        in_specs=[pl.BlockSpec((tm, d), lambda i, j: (i, 0)), pl.BlockSpec((d, tn), lambda i, j: (0, j)),
                  pl.BlockSpec((d, tn), lambda i, j: (0, nj + j))],
        out_specs=(pl.BlockSpec((tm, tn), lambda i, j: (i, j)), pl.BlockSpec((2, tm, tn), lambda i, j: (0, i, j))),
        compiler_params=_cp(("arbitrary",) * 2), name=name)(hn, w, w)


def _ffn_bwd(do, w_out, ab, name, tm=2048, tn=256):
    t, d = do.shape
    f = w_out.shape[0]
    tm = min(tm, t)

    def body(d_ref, w_ref, ab_ref, o_ref):
        ds = _dot(d_ref[...], w_ref[...], NT)
        a = ab_ref[0].astype(F32)
        sg = jax.nn.sigmoid(a)
        silu = a * sg
        o_ref[0] = (ds * ab_ref[1].astype(F32) * (sg + silu * (1.0 - sg))).astype(BF16)
        o_ref[1] = (ds * silu).astype(BF16)

    both = pl.BlockSpec((2, tm, tn), lambda i, j: (0, i, j))
    return pl.pallas_call(
        body, out_shape=jax.ShapeDtypeStruct((2, t, f), BF16), grid=(t // tm, f // tn),
        in_specs=[pl.BlockSpec((tm, d), lambda i, j: (i, 0)), pl.BlockSpec((tn, d), lambda i, j: (j, 0)), both],
        out_specs=both, compiler_params=_cp(("arbitrary",) * 2), name=name)(do, w_out, ab)


def _mm_halves_nt(x2, w, name, tm=1024, tn=512):
    _, t, f = x2.shape
    n = w.shape[0]
    tm = min(tm, t)

    def body(x_ref, w_ref, o_ref, acc):
        part = _dot(x_ref[...], w_ref[...], NT)

        @pl.when(pl.program_id(2) == 0)
        def _():
            acc[...] = part

        @pl.when(pl.program_id(2) == 1)
        def _():
            o_ref[...] = acc[...] + part

    return pl.pallas_call(
        body, out_shape=jax.ShapeDtypeStruct((t, n), F32), grid=(t // tm, n // tn, 2),
        in_specs=[pl.BlockSpec((None, tm, f), lambda i, j, k: (k, i, 0)), pl.BlockSpec((tn, f), lambda i, j, k: (j, k))],
        out_specs=pl.BlockSpec((tm, tn), lambda i, j, k: (i, j)), scratch_shapes=[pltpu.VMEM((tm, tn), F32)],
        compiler_params=_cp(("arbitrary",) * 3), name=name)(x2, w)


def _mm_halves_tn(h, x2, name, tm=256):
    t, d = h.shape
    f = x2.shape[2]
    tn = 2 * f // N_CHIPS

    def body(h_ref, x_ref, o_ref):
        o_ref[...] = _dot(h_ref[...], x_ref[...], TN)

    return pl.pallas_call(
        body, out_shape=jax.ShapeDtypeStruct((N_CHIPS, d, tn), F32), grid=(d // tm, N_CHIPS),
        in_specs=[pl.BlockSpec((t, tm), lambda i, j: (0, i)), pl.BlockSpec((None, t, tn), lambda i, j: (j // 2, 0, j % 2))],
        out_specs=pl.BlockSpec((None, tm, tn), lambda i, j: (j, i, 0)),
        compiler_params=_cp(("arbitrary",) * 2), name=name)(h, x2)


def _mixnorm_fwd(att, gm, g_out, name, tm=512):
    t, w = att.shape

    def body(a_ref, m_ref, g_ref, o_ref):
        a, m, g = a_ref[...], m_ref[...], g_ref[...]
        o_ref[:, :w] = (a * _rms_inv(a) * g[:, :w]).astype(BF16)
        o_ref[:, w:] = (m * _rms_inv(m) * g[:, w:]).astype(BF16)

    return pl.pallas_call(body, out_shape=jax.ShapeDtypeStruct((t, 2 * w), BF16), grid=(t // tm,),
                          in_specs=[_rows(tm, w), _rows(tm, w), _full((1, 2 * w))], out_specs=_rows(tm, 2 * w),
                          compiler_params=_cp(("arbitrary",)), name=name)(att, gm, g_out)


def _mixnorm_bwd(att, gm, g_out, dmixed, name, tm=512):
    t, w = att.shape

    def body(a_ref, m_ref, g_ref, d_ref, da_ref, dm_ref, dg_ref):
        g, d = g_ref[...], d_ref[...]
        da, dga = _rms_bwd(a_ref[...], g[:, :w], d[:, :w])
        dm, dgm = _rms_bwd(m_ref[...], g[:, w:], d[:, w:])
        da_ref[...] = da
        dm_ref[...] = dm
        _acc_rows(dg_ref, jnp.concatenate([dga, dgm], axis=1))

    return pl.pallas_call(body, out_shape=(jax.ShapeDtypeStruct((t, w), F32), jax.ShapeDtypeStruct((t, w), F32),
                                           jax.ShapeDtypeStruct((1, 2 * w), F32)),
                          grid=(t // tm,), in_specs=[_rows(tm, w), _rows(tm, w), _full((1, 2 * w)), _rows(tm, 2 * w)],
                          out_specs=(_rows(tm, w), _rows(tm, w), _full((1, 2 * w))),
                          compiler_params=_cp(("arbitrary",)), name=name)(att, gm, g_out, dmixed)


SCAN_BLOCK = 256


def _gate_fwd(zgf, b_pad, name):
    t = zgf.shape[0]
    fcol = zgf.shape[1] // LANES - 1
    nb = t // SCAN_BLOCK

    def body(f_ref, b_ref, c_ref):
        r = lax.broadcasted_iota(jnp.int32, (SCAN_BLOCK, SCAN_BLOCK), 0)
        s = lax.broadcasted_iota(jnp.int32, (SCAN_BLOCK, SCAN_BLOCK), 1)
        tril = (r >= s).astype(F32)
        head = lax.broadcasted_iota(jnp.int32, (SCAN_BLOCK, LANES), 1) < N_HEADS
        carry = jnp.zeros((1, LANES), F32)
        for blk in range(nb):
            rows = pl.ds(blk * SCAN_BLOCK, SCAN_BLOCK)
            x = f_ref[rows, :] + b_ref[...]
            lf = jnp.minimum(x, 0.0) - jnp.log1p(jnp.exp(-jnp.abs(x)))
            lf = jnp.where(head, lf, 0.0)
            cs = _dot(tril, lf, NN, HIGHEST) + carry
            c_ref[rows, :] = cs
            carry = carry + jnp.sum(lf, axis=0, keepdims=True)

    return pl.pallas_call(body, out_shape=jax.ShapeDtypeStruct((t, LANES), F32),
                          grid=(1,), in_specs=[pl.BlockSpec((t, LANES), lambda i: (0, fcol)), _full((1, LANES))],
                          out_specs=_full((t, LANES)), compiler_params=_cp(("arbitrary",)),
                          name=name)(zgf, b_pad)


def _gate_bwd(dc, zgf, b_pad, name):
    t = zgf.shape[0]
    fcol = zgf.shape[1] // LANES - 1
    nb = t // SCAN_BLOCK

    def body(dc_ref, f_ref, b_ref, dfl_ref, db_ref):
        r = lax.broadcasted_iota(jnp.int32, (SCAN_BLOCK, SCAN_BLOCK), 0)
        s = lax.broadcasted_iota(jnp.int32, (SCAN_BLOCK, SCAN_BLOCK), 1)
        triu = (s >= r).astype(F32)
        head = lax.broadcasted_iota(jnp.int32, (SCAN_BLOCK, LANES), 1) < N_HEADS
        carry = jnp.zeros((1, LANES), F32)
        db = jnp.zeros((1, LANES), F32)
        for blk in reversed(range(nb)):
            rows = pl.ds(blk * SCAN_BLOCK, SCAN_BLOCK)
            dc = dc_ref[rows, :]
            dlf = _dot(triu, dc, NN, HIGHEST) + carry
            x = f_ref[rows, :] + b_ref[...]
            dfl = jnp.where(head, dlf * jax.nn.sigmoid(-x), 0.0)
            dfl_ref[rows, :] = dfl.astype(BF16)
            db = db + jnp.sum(dfl, axis=0, keepdims=True)
            carry = carry + jnp.sum(dc, axis=0, keepdims=True)
        db_ref[...] = db

    return pl.pallas_call(body, out_shape=(jax.ShapeDtypeStruct((t, LANES), BF16), jax.ShapeDtypeStruct((1, LANES), F32)),
                          grid=(1,), in_specs=[_full((t, LANES)), pl.BlockSpec((t, LANES), lambda i: (0, fcol)),
                                               _full((1, LANES))],
                          out_specs=(_full((t, LANES)), _full((1, LANES))), compiler_params=_cp(("arbitrary",)),
                          name=name)(dc, zgf, b_pad)


ATT_BLOCK = 512
PAIRS = N_HEADS // 2
CQ_LANE = HEAD_DIM
CK_LANE = HEAD_DIM + 3
LSE_LANE = HEAD_DIM + 6


def _pick_col(x, idx):
    lane = lax.broadcasted_iota(jnp.int32, x.shape, 1)
    return jnp.sum(jnp.where(lane == idx, x, 0.0), axis=1, keepdims=True)


def _split3(x):
    hi = x.astype(BF16)
    r1 = x - hi.astype(F32)
    mid = r1.astype(BF16)
    lo = (r1 - mid.astype(F32)).astype(BF16)
    return hi, mid, lo


def _lanes_put(base, lane, start, vals):
    out = base
    for n, v in enumerate(vals):
        out = jnp.where(lane == start + n, v, out)
    return out


def _to_first_half(x, hh):
    return x if hh == 0 else pltpu.roll(x, HEAD_DIM, 1)


def _attn_prep(qkv, c, name, tm=512):
    t = qkv.shape[0]
    tm = min(tm, t)

    def body(q_ref, k_ref, v_ref, c_ref, qa_ref, ka_ref, va_ref):
        j = pl.program_id(1)
        lane = lax.broadcasted_iota(jnp.int32, (tm, LANES), 1)
        first = lane < HEAD_DIM
        q2, k2, v2 = q_ref[...].astype(F32), k_ref[...].astype(F32), v_ref[...].astype(F32)
        cc = c_ref[...]
        one = jnp.ones((tm, 1), F32)
        for hh in range(2):
            chi, cmid, clo = [v.astype(F32) for v in _split3(_pick_col(cc, 2 * j + hh))]
            qh = jnp.where(first, _to_first_half(q2, hh) * (HEAD_DIM ** -0.5), 0.0)
            kh = jnp.where(first, _to_first_half(k2, hh), 0.0)
            vh = jnp.where(first, _to_first_half(v2, hh), 0.0)
            qa = _lanes_put(qh, lane, CQ_LANE, [chi, cmid, clo, one, one, one])
            ka = _lanes_put(kh, lane, CQ_LANE, [one, one, one, -chi, -cmid, -clo, one, one, one])
            va = _lanes_put(vh, lane, CQ_LANE, [one, one, one])
            cols = slice(hh * LANES, (hh + 1) * LANES)
            qa_ref[:, cols] = qa.astype(BF16)
            ka_ref[:, cols] = ka.astype(BF16)
            va_ref[:, cols] = va.astype(BF16)

    blk = lambda off: pl.BlockSpec((tm, LANES), lambda i, j: (i, off + j))
    out = pl.BlockSpec((tm, 2 * LANES), lambda i, j: (i, j))
    shp = jax.ShapeDtypeStruct((t, N_HEADS * LANES), BF16)
    return pl.pallas_call(body, out_shape=(shp, shp, shp), grid=(t // tm, PAIRS),
                          in_specs=[blk(0), blk(PAIRS), blk(2 * PAIRS), pl.BlockSpec((tm, LANES), lambda i, j: (i, 0))],
                          out_specs=(out, out, out), compiler_params=_cp(("arbitrary",) * 2), name=name)(qkv, qkv, qkv, c)


def _causal_block(tb):
    return lax.broadcasted_iota(jnp.int32, (tb, tb), 0) >= lax.broadcasted_iota(jnp.int32, (tb, tb), 1)


def _causal_pairs(nb, key_major):
    pairs = [(q, k) for q in range(nb) for k in range(q + 1)]
    if key_major:
        pairs.sort(key=lambda qk: (qk[1], qk[0]))
    return (jnp.array([q for q, _ in pairs], jnp.int32), jnp.array([k for _, k in pairs], jnp.int32))


def _host(side, body, n_lead, arrays, in_specs, out_shapes, out_specs, scratch, grid):
    if side is None:
        return body, tuple(arrays), list(in_specs), list(out_shapes), list(out_specs), list(scratch)
    n_in, n_out, s_in, s_out = len(arrays), len(out_shapes), len(side.arrays), len(side.out_shapes)

    def hosted(*refs):
        ins_end = n_lead + n_in + s_in
        side_in, side_out = refs[n_lead + n_in:ins_end], refs[ins_end + n_out:ins_end + n_out + s_out]
        send, recv = refs[-2:]
        ids = [pl.program_id(ax) for ax in range(len(grid))]
        first, last = ids[0] == 0, ids[0] == grid[0] - 1
        for ax in range(1, len(grid)):
            first, last = first & (ids[ax] == 0), last & (ids[ax] == grid[ax] - 1)

        @pl.when(first)
        def _():
            side.start(side_in, side_out, send, recv)

        body(*refs[:n_lead + n_in], *refs[ins_end:ins_end + n_out], *refs[ins_end + n_out + s_out:-2])

        @pl.when(last)
        def _():
            side.wait(side_in, side_out, send, recv)

    return (hosted, tuple(arrays) + tuple(side.arrays), list(in_specs) + [HBM_SPEC] * s_in,
            list(out_shapes) + side.out_shapes, list(out_specs) + [HBM_SPEC] * s_out, list(scratch) + side.scratch)


def _pair_grid_call(body, tables, arrays, in_specs, out_shapes, out_specs, scratch, side, name):
    grid = (PAIRS, tables[0].shape[0])
    n_out = len(out_shapes)
    body, arrays, in_specs, out_shapes, out_specs, scratch = _host(side, body, 2, arrays, in_specs, out_shapes,
                                                                    out_specs, scratch, grid)
    grid_spec = pltpu.PrefetchScalarGridSpec(num_scalar_prefetch=2, grid=grid, in_specs=in_specs,
                                             out_specs=tuple(out_specs), scratch_shapes=scratch)
    out = pl.pallas_call(body, out_shape=tuple(out_shapes), grid_spec=grid_spec,
                         compiler_params=_cp(("arbitrary",) * 2), name=name)(*tables, *arrays)
    return out[:n_out], out[n_out:]


def _attn_fwd(qa, ka, va, name, side=None):
    t = qa.shape[0]
    tb = min(ATT_BLOCK, t)
    q_tab, k_tab = _causal_pairs(t // tb, key_major=False)

    def body(q_tab_ref, k_tab_ref, q_ref, k_ref, v_ref, o_ref, lse_ref, m0, m1, acc0, acc1):
        qi, kb = q_tab_ref[pl.program_id(1)], k_tab_ref[pl.program_id(1)]
        m_s, acc_s = (m0, m1), (acc0, acc1)

        @pl.when(kb == 0)
        def _():
            for hh in range(2):
                m_s[hh][...] = jnp.full(m_s[hh].shape, NEG_INF, F32)
                acc_s[hh][...] = jnp.zeros(acc_s[hh].shape, F32)

        def step(diagonal):
            for hh in range(2):
                cols = slice(hh * LANES, (hh + 1) * LANES)
                sc = _dot(q_ref[:, cols], k_ref[:, cols], NT)
                if diagonal:
                    sc = jnp.where(_causal_block(tb), sc, NEG_INF)
                m_prev = m_s[hh][...]
                m_new = jnp.maximum(m_prev, jnp.max(sc, axis=1, keepdims=True))
                p = jnp.exp(sc - m_new)
                acc_s[hh][...] = jnp.exp(m_prev - m_new) * acc_s[hh][...] + _dot(p.astype(BF16), v_ref[:, cols], NN)
                m_s[hh][...] = m_new

        @pl.when(kb < qi)
        def _():
            step(False)

        @pl.when(kb == qi)
        def _():
            step(True)
            lane = lax.broadcasted_iota(jnp.int32, (tb, LANES), 1)
            outs, lses = [], []
            for hh in range(2):
                acc = acc_s[hh][...]
                l = _pick_col(acc, CQ_LANE)
                outs.append(acc / l)
                lses.append(m_s[hh][...] + jnp.log(l))
            o_ref[...] = jnp.where(lane < HEAD_DIM, outs[0], pltpu.roll(outs[1], HEAD_DIM, 1))
            lse_ref[...] = jnp.where(lane == 0, lses[0], jnp.where(lane == 1, lses[1], 0.0))

    qrow = lambda j, s, qt, kt: (qt[s], j)
    krow = lambda j, s, qt, kt: (kt[s], j)
    return _pair_grid_call(
        body, (q_tab, k_tab), (qa, ka, va),
        [pl.BlockSpec((tb, 2 * LANES), qrow), pl.BlockSpec((tb, 2 * LANES), krow), pl.BlockSpec((tb, 2 * LANES), krow)],
        [jax.ShapeDtypeStruct((t, D_ATT), F32), jax.ShapeDtypeStruct((t, PAIRS * LANES), F32)],
        [pl.BlockSpec((tb, LANES), qrow), pl.BlockSpec((tb, LANES), qrow)],
        [pltpu.VMEM((tb, 1), F32)] * 2 + [pltpu.VMEM((tb, LANES), F32)] * 2, side, name)


def _attn_bwd_prep(qa, att, lse, datt, name, tm=512):
    t = qa.shape[0]
    tm = min(tm, t)

    def body(qa_ref, o_ref, lse_ref, do_ref, qb_ref, doa_ref):
        j = pl.program_id(1)
        lane = lax.broadcasted_iota(jnp.int32, (tm, LANES), 1)
        first = lane < HEAD_DIM
        do = do_ref[...]
        prod = do * o_ref[...]
        lse2 = lse_ref[...]
        for hh in range(2):
            cols = slice(hh * LANES, (hh + 1) * LANES)
            delta = jnp.sum(jnp.where(first if hh == 0 else ~first, prod, 0.0), axis=1, keepdims=True)
            doh = jnp.where(first, _to_first_half(do, hh), 0.0)
            doa_ref[:, cols] = _lanes_put(doh, lane, CQ_LANE, [v.astype(F32) for v in _split3(-delta)]).astype(BF16)
            nl = [v.astype(F32) for v in _split3(-_pick_col(lse2, hh))]
            qb_ref[:, cols] = _lanes_put(qa_ref[:, cols].astype(F32), lane, LSE_LANE, nl).astype(BF16)

    wide = pl.BlockSpec((tm, 2 * LANES), lambda i, j: (i, j))
    pair = pl.BlockSpec((tm, LANES), lambda i, j: (i, j))
    shp = jax.ShapeDtypeStruct((t, N_HEADS * LANES), BF16)
    return pl.pallas_call(body, out_shape=(shp, shp), grid=(t // tm, PAIRS), in_specs=[wide, pair, pair, pair],
                          out_specs=(wide, wide), compiler_params=_cp(("arbitrary",) * 2), name=name)(qa, att, lse, datt)


def _attn_bwd(qb, ka, va, doa, name, side=None):
    t = qb.shape[0]
    tb = min(ATT_BLOCK, t)
    nb = t // tb
    q_tab, k_tab = _causal_pairs(nb, key_major=True)

    def body(q_tab_ref, k_tab_ref, q_ref, k_ref, v_ref, do_ref, dq_ref, dk_ref, dv_ref, dk0, dk1, dv0, dv1):
        qi, kb = q_tab_ref[pl.program_id(1)], k_tab_ref[pl.program_id(1)]
        dk_s, dv_s = (dk0, dk1), (dv0, dv1)

        @pl.when(qi == kb)
        def _():
            for ref in dk_s + dv_s:
                ref[...] = jnp.zeros(ref.shape, F32)

        def step(diagonal):
            rows = pl.ds(pl.multiple_of(qi * tb, tb), tb)
            for hh in range(2):
                cols = slice(hh * LANES, (hh + 1) * LANES)
                q, k, do = q_ref[:, cols], k_ref[:, cols], do_ref[:, cols]
                sc = _dot(q, k, NT)
                if diagonal:
                    sc = jnp.where(_causal_block(tb), sc, NEG_INF)
                p = jnp.exp(sc)
                ds = (p * _dot(do, v_ref[:, cols], NT)).astype(BF16)
                dv_s[hh][...] += _dot(p.astype(BF16), do, TN)
                dk_s[hh][...] += _dot(ds, q, TN)
                dq_new = _dot(ds, k, NN)

                @pl.when(kb == 0)
                def _():
                    dq_ref[rows, cols] = dq_new

                @pl.when(kb > 0)
                def _():
                    dq_ref[rows, cols] += dq_new

        @pl.when(qi == kb)
        def _():
            step(True)

        @pl.when(qi > kb)
        def _():
            step(False)

        @pl.when(qi == nb - 1)
        def _():
            for hh in range(2):
                cols = slice(hh * LANES, (hh + 1) * LANES)
                dk_ref[:, cols] = dk_s[hh][...]
                dv_ref[:, cols] = dv_s[hh][...].astype(BF16)

    qrow = lambda j, s, qt, kt: (qt[s], j)
    krow = lambda j, s, qt, kt: (kt[s], j)
    blk = (tb, 2 * LANES)
    wide = (t, N_HEADS * LANES)
    return _pair_grid_call(
        body, (q_tab, k_tab), (qb, ka, va, doa),
        [pl.BlockSpec(blk, qrow), pl.BlockSpec(blk, krow), pl.BlockSpec(blk, krow), pl.BlockSpec(blk, qrow)],
        [jax.ShapeDtypeStruct(wide, F32), jax.ShapeDtypeStruct(wide, F32), jax.ShapeDtypeStruct(wide, BF16)],
        [pl.BlockSpec((t, 2 * LANES), lambda j, s, qt, kt: (0, j)), pl.BlockSpec(blk, krow), pl.BlockSpec(blk, krow)],
        [pltpu.VMEM((tb, LANES), F32)] * 4, side, name)


def _attn_bwd_post(dqa, dka, dva, name, tm=256):
    t = dqa.shape[0]
    tm = min(tm, t)

    def body(dq_ref, dk_ref, dv_ref, o_ref, dc_ref):
        lane = lax.broadcasted_iota(jnp.int32, (tm, LANES), 1)
        first = lane < HEAD_DIM
        dc = jnp.zeros((tm, LANES), F32)
        for j in range(PAIRS):
            packed = []
            for ref, gain in ((dq_ref, HEAD_DIM ** -0.5), (dk_ref, 1.0), (dv_ref, 1.0)):
                even = ref[:, 2 * j * LANES:(2 * j + 1) * LANES].astype(F32)
                odd = ref[:, (2 * j + 1) * LANES:(2 * j + 2) * LANES].astype(F32)
                packed.append((jnp.where(first, even, pltpu.roll(odd, HEAD_DIM, 1)) * gain).astype(BF16))
                if ref is dq_ref:
                    dc = dc + jnp.where(lane == 2 * j, _pick_col(even, CQ_LANE), 0.0)
                    dc = dc + jnp.where(lane == 2 * j + 1, _pick_col(odd, CQ_LANE), 0.0)
                if ref is dk_ref:
                    dc = dc - jnp.where(lane == 2 * j, _pick_col(even, CK_LANE), 0.0)
                    dc = dc - jnp.where(lane == 2 * j + 1, _pick_col(odd, CK_LANE), 0.0)
            for part, val in enumerate(packed):
                o_ref[:, (part * PAIRS + j) * LANES:(part * PAIRS + j + 1) * LANES] = val
        dc_ref[...] = dc

    wide = _rows(tm, N_HEADS * LANES)
    return pl.pallas_call(body, out_shape=(jax.ShapeDtypeStruct((t, 3 * D_ATT), BF16), jax.ShapeDtypeStruct((t, LANES), F32)),
                          grid=(t // tm,), in_specs=[wide, wide, wide], out_specs=(_rows(tm, 3 * D_ATT), _rows(tm, LANES)),
                          compiler_params=_cp(("arbitrary",)), name=name)(dqa, dka, dva)


GELU_K = 0.7978845608028654
GELU_A = 0.044715


def _gelu(x):
    th = jnp.tanh(GELU_K * (x + GELU_A * x * x * x))
    return 0.5 * x * (1.0 + th), th


def _group_mean_matrix():
    r = jnp.arange(D_GM)[:, None] // HEAD_DIM
    s = jnp.arange(D_GM)[None, :] // HEAD_DIM
    return jnp.where(r == s, 1.0 / HEAD_DIM, 0.0).astype(BF16)


def _group_mean(x, mean_mat):
    hi = x.astype(BF16)
    lo = (x - hi.astype(F32)).astype(BF16)
    return _dot(hi, mean_mat, NN) + _dot(lo, mean_mat, NN)


def _gm_forward_parts(g, w_ref, bias, gain, mean_mat):
    gel, _ = _gelu(g)
    u, vv = gel[:, :D_GM], gel[:, D_GM:]
    mu = _group_mean(vv, mean_mat)
    d = vv - mu
    rstd = lax.rsqrt(_group_mean(d * d, mean_mat) + EPS)
    xhat = d * rstd
    vn = (xhat * gain).astype(BF16)
    first = lax.broadcasted_iota(jnp.int32, (CHUNK, LANES), 1) < HEAD_DIM
    tri = lax.broadcasted_iota(jnp.int32, (CHUNK, CHUNK), 0) >= lax.broadcasted_iota(jnp.int32, (CHUNK, CHUNK), 1)
    wm = [jnp.where(tri, w_ref[grp], 0.0).astype(BF16) for grp in range(w_ref.shape[0])]
    chunks = []
    for ck in range(g.shape[0] // CHUNK):
        parts = []
        for jp in range(D_GM // LANES):
            vp = vn[ck * CHUNK:(ck + 1) * CHUNK, jp * LANES:(jp + 1) * LANES]
            parts.append(jnp.where(first, _dot(wm[2 * jp], vp, NN), _dot(wm[2 * jp + 1], vp, NN)))
        chunks.append(jnp.concatenate(parts, axis=1) + bias)
    return u, xhat, rstd, vn, jnp.concatenate(chunks, axis=0), wm


GM_ROWS = 512


def _gmlp_fwd(zgf, w_s, bias_full, gain, mean_mat, name):
    t = zgf.shape[0]
    tm = min(GM_ROWS, t)

    def body(g_ref, w_ref, b_ref, gain_ref, mm_ref, o_ref):
        u, _, _, _, mixed, _ = _gm_forward_parts(g_ref[...], w_ref, b_ref[...], gain_ref[...], mm_ref[...])
        o_ref[...] = u * mixed

    return pl.pallas_call(body, out_shape=jax.ShapeDtypeStruct((t, D_GM), F32), grid=(t // tm,),
                          in_specs=[_rows(tm, 2 * D_GM), _full(w_s.shape), _full((CHUNK, D_GM)), _full((1, D_GM)),
                                    _full((D_GM, D_GM))],
                          out_specs=_rows(tm, D_GM), compiler_params=_cp(("arbitrary",)),
                          name=name)(zgf, w_s, bias_full, gain, mean_mat)


def _gmlp_bwd(zgf, dgm, w_s, bias_full, gain, mean_mat, name, side=None):
    t = zgf.shape[0]
    tm = min(GM_ROWS, t)

    def body(g_ref, d_ref, w_ref, b_ref, gain_ref, mm_ref, dg_ref, dw_ref, dmix_ref, dgain_ref):
        g, gain, mean_mat = g_ref[...], gain_ref[...], mm_ref[...]
        u, xhat, rstd, vn, mixed, wm = _gm_forward_parts(g, w_ref, b_ref[...], gain, mean_mat)
        dgm_v = d_ref[...]
        du = dgm_v * mixed
        dmixed = dgm_v * u
        dm_b = dmixed.astype(BF16)
        first = lax.broadcasted_iota(jnp.int32, (CHUNK, LANES), 1) < HEAD_DIM
        tri = lax.broadcasted_iota(jnp.int32, (CHUNK, CHUNK), 0) >= lax.broadcasted_iota(jnp.int32, (CHUNK, CHUNK), 1)

        @pl.when(pl.program_id(0) == 0)
        def _():
            dw_ref[...] = jnp.zeros(dw_ref.shape, F32)
            dmix_ref[...] = jnp.zeros(dmix_ref.shape, F32)
            dgain_ref[...] = jnp.zeros(dgain_ref.shape, F32)

        dw = [jnp.zeros((CHUNK, CHUNK), F32) for _ in wm]
        dmix = jnp.zeros((CHUNK, D_GM), F32)
        dvn_chunks = []
        for ck in range(tm // CHUNK):
            rows = slice(ck * CHUNK, (ck + 1) * CHUNK)
            dmix = dmix + dmixed[rows]
            dvn_parts = []
            for jp in range(D_GM // LANES):
                vp = vn[rows, jp * LANES:(jp + 1) * LANES]
                dmp = dm_b[rows, jp * LANES:(jp + 1) * LANES]
                halves = []
                for hh in range(2):
                    sel = first if hh == 0 else ~first
                    grp = 2 * jp + hh
                    dw[grp] = dw[grp] + _dot(jnp.where(sel, dmp, jnp.zeros_like(dmp)), vp, NT)
                    halves.append(_dot(wm[grp], dmp, TN))
                dvn_parts.append(jnp.where(first, halves[0], halves[1]))
            dvn_chunks.append(jnp.concatenate(dvn_parts, axis=1))
        dvn = jnp.concatenate(dvn_chunks, axis=0)
        for grp, dwg in enumerate(dw):
            dw_ref[grp] += jnp.where(tri, dwg, 0.0)
        dmix_ref[...] += dmix
        dgain_ref[...] += jnp.sum(dvn * xhat, axis=0, keepdims=True)
        dxhat = dvn * gain
        m1 = _group_mean(dxhat, mean_mat)
        m2 = _group_mean(dxhat * xhat, mean_mat)
        dvv = rstd * (dxhat - m1 - xhat * m2)
        gel, th = _gelu(g)
        dgel = 0.5 * (1.0 + th) + 0.5 * g * (1.0 - th * th) * GELU_K * (1.0 + 3.0 * GELU_A * g * g)
        dg_ref[...] = (jnp.concatenate([du, dvv], axis=1) * dgel).astype(BF16)

    grid = (t // tm,)
    body, arrays, in_specs, out_shapes, out_specs, scratch = _host(
        side, body, 0, (zgf, dgm, w_s, bias_full, gain, mean_mat),
        [_rows(tm, 2 * D_GM), _rows(tm, D_GM), _full(w_s.shape), _full((CHUNK, D_GM)), _full((1, D_GM)),
         _full((D_GM, D_GM))],
        [jax.ShapeDtypeStruct((t, 2 * D_GM), BF16), jax.ShapeDtypeStruct(w_s.shape, F32),
         jax.ShapeDtypeStruct((CHUNK, D_GM), F32), jax.ShapeDtypeStruct((1, D_GM), F32)],
        [_rows(tm, 2 * D_GM), _full(w_s.shape), _full((CHUNK, D_GM)), _full((1, D_GM))], [], grid)
    out = pl.pallas_call(body, out_shape=tuple(out_shapes), grid=grid, in_specs=in_specs, out_specs=tuple(out_specs),
                         scratch_shapes=scratch, compiler_params=_cp(("arbitrary",)), name=name)(*arrays)
    return out[:4], out[4:]


def _row_tile(r, c, budget=1 << 19):
    best = None
    for tr in range(8, r + 1, 8):
        if r % tr == 0 and tr * c <= budget:
            best = tr
    return best if best is not None else r


def _adamw(w, g, m, v, name):
    nl, r, c = w.shape
    tr = _row_tile(r, c, 1 << 18)
    c1 = 1.0 - ADAM_B1 ** ADAM_STEP
    c2 = 1.0 - ADAM_B2 ** ADAM_STEP

    def body(w_ref, g_ref, m_ref, v_ref, d_ref, mo_ref, vo_ref):
        gv = g_ref[...]
        mn = ADAM_B1 * m_ref[...] + (1.0 - ADAM_B1) * gv
        vn = ADAM_B2 * v_ref[...] + (1.0 - ADAM_B2) * jnp.square(gv)
        mo_ref[...] = mn
        vo_ref[...] = vn
        d_ref[...] = -ADAM_LR * ((mn / c1) / (jnp.sqrt(vn / c2) + ADAM_EPS) + ADAM_WD * w_ref[...])

    spec = pl.BlockSpec((None, tr, c), lambda l, i: (l, i, 0))
    shp = jax.ShapeDtypeStruct(w.shape, F32)
    return pl.pallas_call(body, out_shape=(shp, shp, shp), grid=(nl, r // tr), in_specs=[spec] * 4,
                          out_specs=(spec, spec, spec), compiler_params=_cp(("arbitrary",) * 2), name=name)(w, g, m, v)


def _add_sibling(g, recv, c_idx, wire_dtype, name):
    nj, _, h, c = g.shape
    tr = _row_tile(h, c)

    def body(c_ref, g_ref, r_ref, o_ref):
        o_ref[...] = (g_ref[...] + r_ref[...]).astype(wire_dtype)

    grid_spec = pltpu.PrefetchScalarGridSpec(
        num_scalar_prefetch=1, grid=(nj, h // tr),
        in_specs=[pl.BlockSpec((None, None, tr, c), lambda j, i, c_ref: (j, c_ref[0], i, 0)),
                  pl.BlockSpec((None, tr, c), lambda j, i, c_ref: (j, i, 0))],
        out_specs=pl.BlockSpec((None, tr, c), lambda j, i, c_ref: (j, i, 0)))
    return pl.pallas_call(body, out_shape=jax.ShapeDtypeStruct((nj, h, c), wire_dtype), grid_spec=grid_spec,
                          compiler_params=_cp(("arbitrary",) * 2), name=name)(c_idx, g, recv)


def _add_chips(own, parts, place, name):
    _, h, c = own.shape
    tr = _row_tile(h, c)

    def body(p_ref, o_ref, a_ref, b_ref, c_ref, out_ref):
        out_ref[...] = ((o_ref[...].astype(F32) + a_ref[...].astype(F32)) + b_ref[...].astype(F32)) + c_ref[...].astype(F32)

    def other(k):
        return pl.BlockSpec((None, tr, c), lambda i, p_ref: (jnp.bitwise_xor(p_ref[0], k), i, 0))

    grid_spec = pltpu.PrefetchScalarGridSpec(
        num_scalar_prefetch=1, grid=(h // tr,),
        in_specs=[pl.BlockSpec((None, tr, c), lambda i, p_ref: (p_ref[0], i, 0)), other(1), other(2), other(3)],
        out_specs=pl.BlockSpec((None, tr, c), lambda i, p_ref: (p_ref[1], i, 0)))
    return pl.pallas_call(body, out_shape=jax.ShapeDtypeStruct((2, h, c), F32), grid_spec=grid_spec,
                          compiler_params=_cp(("arbitrary",)), name=name)(place, own, parts, parts, parts)


HBM_SPEC = pl.BlockSpec(memory_space=pltpu.HBM)


def _place():
    x, y, c = lax.axis_index("x"), lax.axis_index("y"), lax.axis_index("c")
    chips = [(1 - x, y), (x, 1 - y), (1 - x, 1 - y)]
    return x, y, c, 2 * x + y, chips, [2 * px + py for px, py in chips]


def _remote(src, dst, send_sem, recv_sem, dev):
    return pltpu.make_async_remote_copy(src_ref=src, dst_ref=dst, send_sem=send_sem, recv_sem=recv_sem,
                                        device_id=dev, device_id_type=MESH)


def _comm_call(body, arrays, out_shapes, sems, name, aliases=None):
    n = len(arrays)
    return pl.pallas_call(
        body, out_shape=tuple(out_shapes), in_specs=[HBM_SPEC] * n, out_specs=tuple([HBM_SPEC] * len(out_shapes)),
        scratch_shapes=[pltpu.SemaphoreType.DMA(s) for s in sems], input_output_aliases=aliases or {},
        compiler_params=pltpu.CompilerParams(has_side_effects=True), name=name)(*arrays)


class _SideCopies:
    def __init__(self, arrays, out_shapes, sem_shape, sends, recvs):
        self.arrays, self.out_shapes, self.sem_shape = list(arrays), list(out_shapes), sem_shape
        self.sends, self.recvs = sends, recvs

    @property
    def scratch(self):
        return [pltpu.SemaphoreType.DMA(self.sem_shape), pltpu.SemaphoreType.DMA(self.sem_shape)]

    def start(self, ins, outs, send, recv):
        for cp in self.sends(ins, outs, send, recv):
            cp.start()

    def wait(self, ins, outs, send, recv):
        for cp in self.recvs(ins, outs, send, recv):
            cp.wait_recv()
        for cp in self.sends(ins, outs, send, recv):
            cp.wait_send()

    def run(self, name):
        n = len(self.arrays)

        def body(*refs):
            ins, outs = refs[:n], refs[n:n + len(self.out_shapes)]
            send, recv = refs[-2:]
            self.start(ins, outs, send, recv)
            self.wait(ins, outs, send, recv)

        return _comm_call(body, self.arrays, self.out_shapes, [self.sem_shape] * 2, name)


def _gather_over_chips(shards):
    n = len(shards)

    def sends(ins, outs, send, recv):
        _, _, c, me, chips, _ = _place()
        return [_remote(ins[a].at[c], outs[a].at[me, c], send.at[a, k], recv.at[a, k], (px, py, c))
                for a in range(n) for k, (px, py) in enumerate(chips)]

    def recvs(ins, outs, send, recv):
        x, y, c, _, _, cidx = _place()
        return [_remote(outs[a].at[cidx[k], c], outs[a].at[cidx[k], c], send.at[a, k], recv.at[a, k], (x, y, 1 - c))
                for a in range(n) for k in range(3)]

    out_shapes = [jax.ShapeDtypeStruct((N_CHIPS,) + s.shape, s.dtype) for s in shards]
    return _SideCopies(shards, out_shapes, (n, 3), sends, recvs)


def _gather_finish(shards, partial, name):
    n = len(shards)

    def body(*refs):
        ins, part, outs = refs[:n], refs[n:2 * n], refs[2 * n:3 * n]
        send, recv = refs[3 * n:]
        x, y, c, me, _, cidx = _place()
        sib = (x, y, 1 - c)
        cps = []
        for a in range(n):
            for k in range(3):
                cps.append(_remote(part[a].at[cidx[k], c], outs[a].at[cidx[k], c], send.at[a, k], recv.at[a, k], sib))
            cps.append(_remote(ins[a], outs[a].at[me], send.at[a, 3], recv.at[a, 3], sib))
        for cp in cps:
            cp.start()
        for a in range(n):
            for k, blk in enumerate([outs[a].at[cidx[k], 1 - c] for k in range(3)] + [outs[a].at[me]]):
                _remote(blk, blk, send.at[a, k], recv.at[a, k], sib).wait_recv()
        for cp in cps:
            cp.wait_send()

    out_shapes = [jax.ShapeDtypeStruct(p.shape, p.dtype) for p in partial]
    return _comm_call(body, list(shards) + list(partial), out_shapes, [(n, 4), (n, 4)], name,
                      aliases={n + a: a for a in range(n)})


def _sibling_halves(grads):
    n = len(grads)

    def copies(ins, outs, send, recv):
        x, y, c, _, _, _ = _place()
        return [_remote(ins[a].at[j, 1 - c], outs[a].at[j], send.at[a, j], recv.at[a, j], (x, y, 1 - c))
                for a in range(n) for j in range(N_CHIPS)]

    out_shapes = [jax.ShapeDtypeStruct((g.shape[0],) + g.shape[2:], g.dtype) for g in grads]
    return _SideCopies(grads, out_shapes, (n, N_CHIPS), copies, copies)


def _scatter_over_chips(sums):
    n = len(sums)

    def sends(ins, outs, send, recv):
        _, _, c, me, chips, cidx = _place()
        return [_remote(ins[a].at[cidx[k]], outs[a].at[me], send.at[a, k], recv.at[a, k], (px, py, c))
                for a in range(n) for k, (px, py) in enumerate(chips)]

    def recvs(ins, outs, send, recv):
        x, y, c, _, _, cidx = _place()
        return [_remote(outs[a].at[cidx[k]], outs[a].at[cidx[k]], send.at[a, k], recv.at[a, k], (x, y, 1 - c))
                for a in range(n) for k in range(3)]

    return _SideCopies(sums, [jax.ShapeDtypeStruct(s.shape, s.dtype) for s in sums], (n, 3), sends, recvs)


def _exchange_halves(halves, name):
    n = len(halves)

    def body(*refs):
        ins, bufs = refs[:n], refs[n:2 * n]
        send, recv = refs[2 * n:]
        x, y, c, _, _, _ = _place()
        sib = (x, y, 1 - c)
        cps = []
        for a in range(n):
            cp = _remote(ins[a].at[c], bufs[a].at[c], send.at[a], recv.at[a], sib)
            cp.start()
            cps.append(cp)
        for a in range(n):
            blk = bufs[a].at[1 - c]
            _remote(blk, blk, send.at[a], recv.at[a], sib).wait_recv()
        for cp in cps:
            cp.wait_send()

    out_shapes = [jax.ShapeDtypeStruct(s.shape, s.dtype) for s in halves]
    return _comm_call(body, halves, out_shapes, [(n,), (n,)], name, aliases={a: a for a in range(n)})


def _gather_chips(slices, name):
    n = len(slices)

    def body(*refs):
        ins, outs = refs[:n], refs[n:2 * n]
        send, recv = refs[2 * n:]
        x, y, c, me, chips, cidx = _place()
        sib = (x, y, 1 - c)
        cps = []
        for a in range(n):
            for k, dev in enumerate([(px, py, c) for px, py in chips] + [sib]):
                cp = _remote(ins[a], outs[a].at[me], send.at[a, k], recv.at[a, k], dev)
                cp.start()
                cps.append(cp)
        for a in range(n):
            for k, slot in enumerate(cidx + [me]):
                blk = outs[a].at[slot]
                _remote(blk, blk, send.at[a, k], recv.at[a, k], sib).wait_recv()
        for cp in cps:
            cp.wait_send()

    out_shapes = [jax.ShapeDtypeStruct((N_CHIPS,) + s.shape, s.dtype) for s in slices]
    return _comm_call(body, slices, out_shapes, [(n, 4), (n, 4)], name)


def _rs_siblings(grads):
    return _sibling_halves([g.reshape(g.shape[0], 2, g.shape[1] // 2, g.shape[2]) for g in grads])


def _rs_chips(siblings, recv, c_idx, wire_dtype, tag):
    return _scatter_over_chips([_add_sibling(g, r, c_idx, wire_dtype, "rs_add_sibling_" + tag)
                                for g, r in zip(siblings.arrays, recv)])


def _rs_end(scatter, parts, place, tag):
    halves = [_add_chips(s, p, place, "rs_add_chips_" + tag) for s, p in zip(scatter.arrays, parts)]
    both = _exchange_halves(halves, "rs_halves_" + tag)
    return [b.reshape(b.shape[0] * b.shape[1], b.shape[2]) for b in both]


SMALL_ORDER = ("gm_w_s", "mix_pre_norm", "mix_post_norm", "mix_out_norm", "ffn_pre_norm", "ffn_post_norm", "ple_norm",
               "gm_v_norm", "gm_b_s", "b_forget")
SMALL_ROWS_MULTIPLE = 64
SMALL_BLOCK = 8 * LANES


def _chip_columns(pieces, width):
    out = []
    for j in range(N_CHIPS):
        lo, hi, parts, off = j * width, (j + 1) * width, [], 0
        for piece in pieces:
            a, b = max(lo, off), min(hi, off + piece.shape[1])
            if a < b:
                parts.append(piece[:, a - off:b - off])
            off += piece.shape[1]
        out.append(parts[0] if len(parts) == 1 else jnp.concatenate(parts, axis=1))
    return jnp.stack(out)


def _columns(sliced, lo, hi):
    width, parts = sliced.shape[2], []
    for j in range(N_CHIPS):
        a, b = max(lo, j * width), min(hi, (j + 1) * width)
        if a < b:
            parts.append(sliced[j][:, a - j * width:b - j * width])
    return parts[0] if len(parts) == 1 else jnp.concatenate(parts, axis=1)


def _pack_small(parts):
    blocks = []
    for nme in SMALL_ORDER:
        v = parts[nme].reshape(-1)
        pad = (-v.shape[0]) % SMALL_BLOCK
        blocks.append((jnp.pad(v, (0, pad)) if pad else v).reshape(-1, LANES))
    rows = sum(b.shape[0] for b in blocks)
    if rows % SMALL_ROWS_MULTIPLE:
        blocks.append(jnp.zeros((SMALL_ROWS_MULTIPLE - rows % SMALL_ROWS_MULTIPLE, LANES), F32))
    return jnp.concatenate(blocks, axis=0)


def _unpack_small(packed, shapes):
    out, row = {}, 0
    for nme in SMALL_ORDER:
        size = 1
        for s in shapes[nme]:
            size *= s
        rows = -(-size // SMALL_BLOCK) * (SMALL_BLOCK // LANES)
        block = packed[row:row + rows]
        out[nme] = (block if size % SMALL_BLOCK == 0 else block.reshape(-1)[:size]).reshape(shapes[nme])
        row += rows
    return out


def kernel(x, p, mix_pre_norm, mix_post_norm, w_in, b_forget, gm_v_norm, gm_w_s, gm_b_s, mix_out_norm, w_out, ffn_pre_norm, ffn_post_norm, w_ffn_in, w_ffn_out, w_ple, ple_norm, w_ple_gate, loss_target, m_mix_pre_norm, m_mix_post_norm, m_w_in, m_b_forget, m_gm_v_norm, m_gm_w_s, m_gm_b_s, m_mix_out_norm, m_w_out, m_ffn_pre_norm, m_ffn_post_norm, m_w_ffn_in, m_w_ffn_out, m_w_ple, m_ple_norm, m_w_ple_gate, v_mix_pre_norm, v_mix_post_norm, v_w_in, v_b_forget, v_gm_v_norm, v_gm_w_s, v_gm_b_s, v_mix_out_norm, v_w_out, v_ffn_pre_norm, v_ffn_post_norm, v_w_ffn_in, v_w_ffn_out, v_w_ple, v_ple_norm, v_w_ple_gate):
    weights = dict(mix_pre_norm=mix_pre_norm, mix_post_norm=mix_post_norm, w_in=w_in, b_forget=b_forget,
                   gm_v_norm=gm_v_norm, gm_w_s=gm_w_s, gm_b_s=gm_b_s, mix_out_norm=mix_out_norm, w_out=w_out,
                   ffn_pre_norm=ffn_pre_norm, ffn_post_norm=ffn_post_norm, w_ffn_in=w_ffn_in, w_ffn_out=w_ffn_out,
                   w_ple=w_ple, ple_norm=ple_norm, w_ple_gate=w_ple_gate)
    mom_m = dict(mix_pre_norm=m_mix_pre_norm, mix_post_norm=m_mix_post_norm, w_in=m_w_in, b_forget=m_b_forget,
                 gm_v_norm=m_gm_v_norm, gm_w_s=m_gm_w_s, gm_b_s=m_gm_b_s, mix_out_norm=m_mix_out_norm, w_out=m_w_out,
                 ffn_pre_norm=m_ffn_pre_norm, ffn_post_norm=m_ffn_post_norm, w_ffn_in=m_w_ffn_in,
                 w_ffn_out=m_w_ffn_out, w_ple=m_w_ple, ple_norm=m_ple_norm, w_ple_gate=m_w_ple_gate)
    mom_v = dict(mix_pre_norm=v_mix_pre_norm, mix_post_norm=v_mix_post_norm, w_in=v_w_in, b_forget=v_b_forget,
                 gm_v_norm=v_gm_v_norm, gm_w_s=v_gm_w_s, gm_b_s=v_gm_b_s, mix_out_norm=v_mix_out_norm, w_out=v_w_out,
                 ffn_pre_norm=v_ffn_pre_norm, ffn_post_norm=v_ffn_post_norm, w_ffn_in=v_w_ffn_in,
                 w_ffn_out=v_w_ffn_out, w_ple=v_w_ple, ple_norm=v_ple_norm, w_ple_gate=v_w_ple_gate)
    big = ("w_in", "w_out", "w_ffn_in", "w_ffn_out", "w_ple", "w_ple_gate")
    depth = w_in.shape[0]
    t, d = x.shape[1], x.shape[2]
    d_ff = w_ffn_out.shape[1] * N_CHIPS
    c_idx = lax.axis_index("c").astype(jnp.int32).reshape(1)
    place = jnp.stack([2 * lax.axis_index("x") + lax.axis_index("y"), lax.axis_index("c")]).astype(jnp.int32)
    h = x[0]
    target = loss_target[0]
    mean_mat = _group_mean_matrix()

    def row(a, i):
        return a[i].reshape(1, -1)

    def shards_of(i):
        shards = [weights[nme][i].astype(BF16) for nme in big]
        return [s.reshape(2, s.shape[0] // 2, s.shape[1]) for s in shards]

    saved = []
    hn = _norm_cast(h, row(mix_pre_norm, 0), "norm_first")
    by_cols = lambda g: g.transpose(1, 0, 2).reshape(g.shape[1], N_CHIPS * g.shape[2])
    by_rows = lambda g: g.reshape(N_CHIPS * g.shape[1], g.shape[2])
    whole = lambda g: g.reshape(N_CHIPS, g.shape[2] * 2, g.shape[3])
    gather = _gather_over_chips(shards_of(0)[:1])
    w_in_g = _gather_finish(gather.arrays, gather.run("gather_first"), "gather_finish_first")[0]
    for i in range(depth):
        w_in_c = whole(w_in_g)
        w_qkv = _columns(w_in_c, 0, 3 * D_ATT)
        w_gf = jnp.concatenate([_columns(w_in_c, 3 * D_ATT + N_HEADS, N_CHIPS * w_in_c.shape[2]),
                                _columns(w_in_c, 3 * D_ATT, 3 * D_ATT + N_HEADS),
                                jnp.zeros((d, LANES - N_HEADS), BF16)], axis=1)
        b_pad = jnp.pad(b_forget[i], (0, LANES - N_HEADS)).reshape(1, LANES)
        bias_full = jnp.repeat(gm_b_s[i].T, HEAD_DIM, axis=1)
        gain_v = row(gm_v_norm, i)

        qkv = _mm(hn, w_qkv, "nn", BF16, "mm_qkv")
        zgf = _mm(hn, w_gf, "nn", F32, "mm_gf", tn_cap=384)
        qa, ka, va = _attn_prep(qkv, _gate_fwd(zgf, b_pad, "gate_fwd"), "attn_prep")
        last = i + 1 == depth
        gather = _gather_over_chips(shards_of(i)[1:] + ([] if last else shards_of(i + 1)[:1]))
        (att, lse), partial = _attn_fwd(qa, ka, va, "attn_fwd_last" if last else "attn_fwd", gather)
        gathered = _gather_finish(gather.arrays, partial, "gather_finish_last" if last else "gather_finish")
        w_out_f, w_fi_f, w_fo_f = by_rows(whole(gathered[0])), by_cols(whole(gathered[1])), by_rows(whole(gathered[2]))
        w_ple_f, w_pg_f = by_cols(whole(gathered[3])), by_rows(whole(gathered[4]))
        w_in_g = None if last else gathered[5]
        gm = _gmlp_fwd(zgf, gm_w_s[i], bias_full, gain_v, mean_mat, "gmlp_fwd")
        mixed = _mixnorm_fwd(att, gm, row(mix_out_norm, i), "mixnorm_fwd")
        o = _mm(mixed, w_out_f, "nn", F32, "mm_out")
        h1, hn2 = _resid_norm(h, o, row(mix_post_norm, i), row(ffn_pre_norm, i), "resid_mix")
        s, ab = _ffn_in(hn2, w_fi_f, "ffn_in")
        o2 = _mm(s, w_fo_f, "nn", F32, "mm_ffn_out")
        h2, hr = _resid_norm(h1, o2, row(ffn_post_norm, i), None, "resid_ffn")
        pe = _mm(p[i, 0], w_ple_f, "nn", F32, "mm_ple")
        gl = _mm(hr, w_pg_f, "nn", F32, "mm_ple_gate")
        g_next = row(mix_pre_norm, i + 1) if i + 1 < depth else row(mix_pre_norm, 0)
        h3, hn_next = _ple_fwd(h2, pe, gl, row(ple_norm, i), g_next, "ple_fwd")
        saved.append(dict(h=h, hn=hn, qa=qa, ka=ka, va=va, zgf=zgf, att=att, lse=lse, gm=gm, mixed=mixed,
                          o=o, h1=h1, hn2=hn2, ab=ab, s=s, o2=o2, h2=h2, hr=hr, pe=pe, gl=gl, w_qkv=w_qkv, w_gf=w_gf,
                          w_out=w_out_f, w_fi=w_fi_f, w_fo=w_fo_f, w_pg=w_pg_f, b_pad=b_pad, bias_full=bias_full,
                          gain_v=gain_v))
        h, hn = h3, hn_next

    dh, loss_blk = _loss_head(h, target, "loss_head")
    loss = lax.psum(loss_blk[0, 0], ("x", "y", "c"))

    small = {nme: [None] * depth for nme in SMALL_ORDER}
    big_grads = {nme: [None] * depth for nme in big}
    waiting = []
    for i in reversed(range(depth)):
        sv = saved[i]
        dgl, dpe, small["ple_norm"][i] = _ple_bwd(dh, sv["pe"], sv["gl"], row(ple_norm, i), "ple_bwd")
        dhr = _mm(dgl, sv["w_pg"], "nt", F32, "mm_d_hr")
        g_pg = _mm(sv["hr"], dgl, "tn", F32, "mm_dw_ple_gate").reshape(N_CHIPS, -1, d)
        g_ple = _mm(p[i, 0], dpe, "tn", F32, "mm_dw_ple", chip_split=True)
        dh2, do2, small["ffn_post_norm"][i] = _join(dh, sv["h2"], None, dhr, sv["o2"], row(ffn_post_norm, i), "join_ple")
        dab = _ffn_bwd(do2, sv["w_fo"], sv["ab"], "ffn_bwd")
        g_fo = _mm(sv["s"], do2, "tn", F32, "mm_dw_ffn_out", tm=256).reshape(N_CHIPS, -1, d)
        dhn2 = _mm_halves_nt(dab, sv["w_fi"], "mm_d_hn2")
        g_fi = _mm_halves_tn(sv["hn2"], dab, "mm_dw_ffn_in")
        dh1, small["ffn_pre_norm"][i], do, small["mix_post_norm"][i] = _join(
            dh2, sv["h1"], row(ffn_pre_norm, i), dhn2, sv["o"], row(mix_post_norm, i), "join_ffn")
        dmixed = _mm(do, sv["w_out"], "nt", F32, "mm_d_mixed")
        g_out = _mm(sv["mixed"], do, "tn", F32, "mm_dw_out").reshape(N_CHIPS, -1, d)
        datt, dgm, small["mix_out_norm"][i] = _mixnorm_bwd(sv["att"], sv["gm"], row(mix_out_norm, i), dmixed, "mixnorm_bwd")
        batch = waiting + [("w_out", i, g_out), ("w_ffn_in", i, g_fi), ("w_ffn_out", i, g_fo), ("w_ple", i, g_ple),
                           ("w_ple_gate", i, g_pg)]
        tag = "layer" if waiting else "top"
        siblings = _rs_siblings([g for _, _, g in batch])
        (dg, small["gm_w_s"][i], dmix_sum, small["gm_v_norm"][i]), recv = _gmlp_bwd(
            sv["zgf"], dgm, gm_w_s[i], sv["bias_full"], sv["gain_v"], mean_mat, "gmlp_bwd_" + tag, siblings)
        small["gm_b_s"][i] = dmix_sum.reshape(CHUNK, N_HEADS, HEAD_DIM).sum(-1).T
        qb, doa = _attn_bwd_prep(sv["qa"], sv["att"], sv["lse"], datt, "attn_bwd_prep")
        scatter = _rs_chips(siblings, recv, c_idx, BF16, tag)
        (dqa, dka, dva), parts = _attn_bwd(qb, sv["ka"], sv["va"], doa, "attn_bwd_" + tag, scatter)
        for (nme, layer, _), g in zip(batch, _rs_end(scatter, parts, place, tag)):
            big_grads[nme][layer] = g
        dqkv, dc = _attn_bwd_post(dqa, dka, dva, "attn_bwd_post")
        dfl, db = _gate_bwd(dc, sv["zgf"], sv["b_pad"], "gate_bwd")
        small["b_forget"][i] = db[0, :N_HEADS]
        dgf = jnp.concatenate([dg, dfl], axis=1)
        dhn = _mm(dqkv, sv["w_qkv"], "nt", F32, "mm_d_hn_qkv")
        dhn = _mm(dgf, sv["w_gf"], "nt", F32, "mm_d_hn_gf", add=dhn)
        g_qkv = _mm(sv["hn"], dqkv, "tn", F32, "mm_dw_qkv")
        g_gf = _mm(sv["hn"], dgf, "tn", F32, "mm_dw_gf", tn_cap=384)
        g_in = _chip_columns([g_qkv, g_gf[:, 2 * D_GM:2 * D_GM + N_HEADS], g_gf[:, :2 * D_GM]], w_in.shape[2])
        dh, small["mix_pre_norm"][i] = _join(dh1, sv["h"], row(mix_pre_norm, i), dhn, None, None, "join_mix")
        waiting = [("w_in", i, g_in)]
    siblings = _rs_siblings([waiting[0][2]])
    scatter = _rs_chips(siblings, siblings.run("rs_sibling_tail"), c_idx, BF16, "tail")
    big_grads["w_in"][0] = _rs_end(scatter, scatter.run("rs_chips_tail"), place, "tail")[0]
    grad_x = dh.reshape(1, t, d)

    small_shapes = {nme: weights[nme].shape for nme in SMALL_ORDER}
    small_part = _pack_small({nme: jnp.stack([g.reshape(small_shapes[nme][1:]) for g in small[nme]])
                              for nme in SMALL_ORDER})
    rows_small = small_part.shape[0]
    siblings = _rs_siblings([small_part.reshape(N_CHIPS, rows_small // N_CHIPS, LANES)])
    scatter = _rs_chips(siblings, siblings.run("rs_sibling_small"), c_idx, F32, "small")
    small_slice = _rs_end(scatter, scatter.run("rs_chips_small"), place, "small")[0]
    small_all = _gather_chips([small_slice], "gather_small")[0].reshape(1, rows_small, LANES)
    sd, sm, sv_ = _adamw(_pack_small({n_: weights[n_] for n_ in SMALL_ORDER})[None], small_all,
                         _pack_small({n_: mom_m[n_] for n_ in SMALL_ORDER})[None],
                         _pack_small({n_: mom_v[n_] for n_ in SMALL_ORDER})[None], "adamw_small")
    grads = _unpack_small(small_all[0], small_shapes)
    deltas = _unpack_small(sd[0], small_shapes)
    new_m = _unpack_small(sm[0], small_shapes)
    new_v = _unpack_small(sv_[0], small_shapes)

    for nme in big:
        g = jnp.stack(big_grads[nme]).reshape(weights[nme].shape)
        grads[nme] = g
        deltas[nme], new_m[nme], new_v[nme] = _adamw(weights[nme], g, mom_m[nme], mom_v[nme], "adamw_" + nme)

    order = ("mix_pre_norm", "mix_post_norm", "w_in", "b_forget", "gm_v_norm", "gm_w_s", "gm_b_s", "mix_out_norm",
             "w_out", "ffn_pre_norm", "ffn_post_norm", "w_ffn_in", "w_ffn_out", "w_ple", "ple_norm", "w_ple_gate")
    return (loss, grad_x, *[grads[n_] for n_ in order], *[deltas[n_] for n_ in order], *[new_m[n_] for n_ in order],
            *[new_v[n_] for n_ in order])
```

```python
import functools

import jax
import jax.numpy as jnp
from jax import lax
from jax.experimental import pallas as pl
from jax.experimental.pallas import tpu as pltpu

F32 = jnp.float32
BF16 = jnp.bfloat16
MESH = pl.DeviceIdType.MESH
HIGHEST = lax.Precision.HIGHEST

EPS = 1e-6
NEG_INF = -1e30
N_HEADS = 8
HEAD_DIM = 64
D_ATT = N_HEADS * HEAD_DIM
D_GM = 512
CHUNK = 128
LANES = 128
N_CHIPS = 4
ADAM_LR = 0.001
ADAM_B1 = 0.9
ADAM_B2 = 0.999
ADAM_EPS = 1e-08
ADAM_WD = 0.01
ADAM_STEP = 10
VMEM_LIMIT = 56 * 1024 * 1024


def _cp(sem=None):
    return pltpu.CompilerParams(dimension_semantics=sem, vmem_limit_bytes=VMEM_LIMIT)


def _full(shape):
    return pl.BlockSpec(shape, lambda *_: (0,) * len(shape))


def _rows(tm, width, col_block=0):
    return pl.BlockSpec((tm, width), lambda i: (i, col_block))


def _dot(a, b, dims, precision=None):
    return lax.dot_general(a, b, (dims, ((), ())), preferred_element_type=F32, precision=precision)


NN = ((1,), (0,))
NT = ((1,), (1,))
TN = ((0,), (0,))


def _pick(n, cap):
    best = None
    for t in range(LANES, min(n, cap) + 1, LANES):
        if n % t == 0:
            best = t
    assert best is not None, (n, cap)
    return best


def _mm(a, b, mode, out_dtype, name, tm=None, tn_cap=1024, chip_split=False):
    dims = {"nn": NN, "nt": NT, "tn": TN}[mode]
    if mode == "tn":
        k, m = a.shape
    else:
        m, k = a.shape
    n = b.shape[0] if mode == "nt" else b.shape[1]
    if tm is None:
        tm = 512 if mode == "tn" else 1024
    tm = min(tm, m)
    tn = n // N_CHIPS if chip_split else _pick(n, tn_cap)
    assert m % tm == 0 and n % tn == 0

    def body(a_ref, b_ref, o_ref):
        o_ref[...] = _dot(a_ref[...].astype(BF16), b_ref[...].astype(BF16), dims).astype(out_dtype)

    a_spec = pl.BlockSpec((k, tm), lambda i, j: (0, i)) if mode == "tn" else pl.BlockSpec((tm, k), lambda i, j: (i, 0))
    b_spec = pl.BlockSpec((tn, k), lambda i, j: (j, 0)) if mode == "nt" else pl.BlockSpec((k, tn), lambda i, j: (0, j))
    if chip_split:
        out_shape = jax.ShapeDtypeStruct((N_CHIPS, m, tn), out_dtype)
        out_spec = pl.BlockSpec((None, tm, tn), lambda i, j: (j, i, 0))
    else:
        out_shape = jax.ShapeDtypeStruct((m, n), out_dtype)
        out_spec = pl.BlockSpec((tm, tn), lambda i, j: (i, j))
    return pl.pallas_call(body, out_shape=out_shape, grid=(m // tm, n // tn), in_specs=[a_spec, b_spec],
                          out_specs=out_spec, compiler_params=_cp(("arbitrary", "arbitrary")), name=name)(a, b)


def _rms_inv(x):
    return lax.rsqrt(jnp.mean(x * x, axis=-1, keepdims=True) + EPS)


def _rms_bwd(x, gain, dy):
    inv = _rms_inv(x)
    xhat = x * inv
    dxn = dy if gain is None else dy * gain
    dx = inv * (dxn - xhat * jnp.mean(dxn * xhat, axis=-1, keepdims=True))
    return dx, dy * xhat


def _acc_rows(ref, val):
    s = jnp.sum(val, axis=0, keepdims=True)

    @pl.when(pl.program_id(0) == 0)
    def _():
        ref[...] = s

    @pl.when(pl.program_id(0) > 0)
    def _():
        ref[...] += s


class _Prod:
    def __init__(self, *terms):
        self.terms = terms


def _rowwise(fn, name, rows, consts, outs, accs=(), tm=512):
    arrays, specs, loaders = [], [], []
    t = next(x for x in rows if not isinstance(x, _Prod)).shape[0]
    tm = min(tm, t)
    for x in rows:
        pos = len(arrays)
        if isinstance(x, _Prod):
            dims = []
            for term in x.terms:
                a, b, mode = term[:3]
                if len(term) == 4:
                    f = a.shape[2]
                    specs += [pl.BlockSpec((None, tm, f), lambda i, half=term[3]: (half, i, 0)),
                              pl.BlockSpec((b.shape[0], f), lambda i, half=term[3]: (0, half))]
                else:
                    specs += [_rows(tm, a.shape[1]), _full(b.shape)]
                arrays += [a, b]
                dims.append(NT if mode == "nt" else NN)

            def load(refs, pos=pos, dims=dims):
                total = None
                for k, dm in enumerate(dims):
                    part = _dot(refs[pos + 2 * k][...].astype(BF16), refs[pos + 2 * k + 1][...], dm)
                    total = part if total is None else total + part
                return total
        else:
            arrays.append(x)
            specs.append(_rows(tm, x.shape[1]))

            def load(refs, pos=pos):
                return refs[pos][...]
        loaders.append(load)
    for cst in consts:
        loaders.append(lambda refs, pos=len(arrays): refs[pos][...])
        arrays.append(cst)
        specs.append(_full(cst.shape))
    n_in = len(arrays)

    def body(*refs):
        res = fn(*[ld(refs) for ld in loaders])
        out_refs = refs[n_in:]
        for k, (_, dtype) in enumerate(outs):
            out_refs[k][...] = res[k].astype(dtype)
        for k in range(len(accs)):
            _acc_rows(out_refs[len(outs) + k], res[len(outs) + k])

    out_shape = [jax.ShapeDtypeStruct((t, c), dtype) for c, dtype in outs] + [jax.ShapeDtypeStruct((1, c), F32) for c in accs]
    out_specs = [_rows(tm, c) for c, _ in outs] + [_full((1, c)) for c in accs]
    return pl.pallas_call(body, out_shape=tuple(out_shape), grid=(t // tm,), in_specs=specs, out_specs=tuple(out_specs),
                          compiler_params=_cp(("arbitrary",)), name=name)(*arrays)


def _norm_cast(h, gain, name):
    return _rowwise(lambda x, g: (x * _rms_inv(x) * g,), name, [h], [gain], [(h.shape[1], BF16)])[0]


def _resid_norm(h, o, g_post, g_next, name):
    d = h.shape[1]

    def fn(hv, ov, gp, *gn):
        h1 = hv + ov * _rms_inv(ov) * gp
        hn = h1 * _rms_inv(h1)
        return ov, h1, hn * gn[0] if gn else hn

    return _rowwise(fn, name, [h, o], [g_post] + ([] if g_next is None else [g_next]), [(d, F32), (d, F32), (d, BF16)])


def _ple_fwd(h2, pe, gl, g_ple, g_next, name):
    d = h2.shape[1]

    def fn(hv, pv, gv, gp, gn):
        h3 = hv + jax.nn.sigmoid(gv) * (pv * _rms_inv(pv) * gp)
        return pv, gv, h3, h3 * _rms_inv(h3) * gn

    return _rowwise(fn, name, [h2, pe, gl], [g_ple, g_next], [(d, F32)] * 3 + [(d, BF16)])


def _loss_head(y, target, name, tm=512):
    t, d = y.shape

    def body(y_ref, t_ref, dy_ref, loss_ref):
        diff = y_ref[...] - t_ref[...]
        dy_ref[...] = diff * (1.0 / d)
        part = 0.5 * jnp.sum(jnp.mean(diff * diff, axis=-1, keepdims=True), axis=0, keepdims=True)
        part = jnp.broadcast_to(part, (8, LANES))

        @pl.when(pl.program_id(0) == 0)
        def _():
            loss_ref[...] = part

        @pl.when(pl.program_id(0) > 0)
        def _():
            loss_ref[...] += part

    return pl.pallas_call(body, out_shape=(jax.ShapeDtypeStruct((t, d), F32), jax.ShapeDtypeStruct((8, LANES), F32)),
                          grid=(t // tm,), in_specs=[_rows(tm, d)] * 2, out_specs=(_rows(tm, d), _full((8, LANES))),
                          compiler_params=_cp(("arbitrary",)), name=name)(y, target)


def _ple_bwd(dh3, pe, gl, g_ple, name):
    d = dh3.shape[1]

    def fn(dh, pv, gv, gp):
        gate = jax.nn.sigmoid(gv)
        dpe, dg_rows = _rms_bwd(pv, gp, dh * gate)
        return dh * (pv * _rms_inv(pv) * gp) * gate * (1.0 - gate), dpe, dg_rows

    return _rowwise(fn, name, [dh3, pe, gl], [g_ple], [(d, BF16), (d, BF16)], [d])


def _join(d_res, x_a, gain_a, d_a, x_b, gain_b, name, tm=512):
    d = d_res.shape[1]
    has_ga = gain_a is not None
    has_b = x_b is not None

    def fn(*vals):
        it = iter(vals)
        dres, xa, da = next(it), next(it), next(it)
        xb = next(it) if has_b else None
        ga = next(it) if has_ga else None
        dx, dga_rows = _rms_bwd(xa, ga, da)
        dout = dres + dx
        if not has_b:
            return (dout, dga_rows) if has_ga else (dout,)
        db, dgb_rows = _rms_bwd(xb, next(it), dout)
        return (dout, db, dga_rows, dgb_rows) if has_ga else (dout, db, dgb_rows)

    rows = [d_res, x_a, d_a] + ([x_b] if has_b else [])
    consts = ([gain_a] if has_ga else []) + ([gain_b] if has_b else [])
    return _rowwise(fn, name, rows, consts, [(d, F32)] + ([(d, BF16)] if has_b else []), [d] * (has_ga + has_b), tm)


def _ffn_in(hn, w, name, tm=2048, tn=256):
    t, d = hn.shape
    f = w.shape[1] // 2
    tm = min(tm, t)
    nj = f // tn

    def body(h_ref, wa_ref, wb_ref, s_ref, ab_ref):
        h = h_ref[...]
        a = _dot(h, wa_ref[...], NN)
        b = _dot(h, wb_ref[...], NN)
        ab_ref[0] = a.astype(BF16)
        ab_ref[1] = b.astype(BF16)
        s_ref[...] = (a * jax.nn.sigmoid(a) * b).astype(BF16)

    return pl.pallas_call(
        body, out_shape=(jax.ShapeDtypeStruct((t, f), BF16), jax.ShapeDtypeStruct((2, t, f), BF16)), grid=(t // tm, nj),
        in_specs=[pl.BlockSpec((tm, d), lambda i, j: (i, 0)), pl.BlockSpec((d, tn), lambda i, j: (0, j)),
                  pl.BlockSpec((d, tn), lambda i, j: (0, nj + j))],
        out_specs=(pl.BlockSpec((tm, tn), lambda i, j: (i, j)), pl.BlockSpec((2, tm, tn), lambda i, j: (0, i, j))),
        compiler_params=_cp(("arbitrary",) * 2), name=name)(hn, w, w)


def _ffn_bwd(do, w_out, ab, name, tm=2048, tn=256):
    t, d = do.shape
    f = w_out.shape[0]
    tm = min(tm, t)

    def body(d_ref, w_ref, ab_ref, o_ref):
        ds = _dot(d_ref[...], w_ref[...], NT)
        a = ab_ref[0].astype(F32)
        sg = jax.nn.sigmoid(a)
        silu = a * sg
        o_ref[0] = (ds * ab_ref[1].astype(F32) * (sg + silu * (1.0 - sg))).astype(BF16)
        o_ref[1] = (ds * silu).astype(BF16)

    both = pl.BlockSpec((2, tm, tn), lambda i, j: (0, i, j))
    return pl.pallas_call(
        body, out_shape=jax.ShapeDtypeStruct((2, t, f), BF16), grid=(t // tm, f // tn),
        in_specs=[pl.BlockSpec((tm, d), lambda i, j: (i, 0)), pl.BlockSpec((tn, d), lambda i, j: (j, 0)), both],
        out_specs=both, compiler_params=_cp(("arbitrary",) * 2), name=name)(do, w_out, ab)


def _mm_halves_tn(h, x2, name, tm=256):
    t, d = h.shape
    f = x2.shape[2]
    tn = 2 * f // N_CHIPS

    def body(h_ref, x_ref, o_ref):
        o_ref[...] = _dot(h_ref[...], x_ref[...], TN)

    return pl.pallas_call(
        body, out_shape=jax.ShapeDtypeStruct((N_CHIPS, d, tn), F32), grid=(d // tm, N_CHIPS),
        in_specs=[pl.BlockSpec((t, tm), lambda i, j: (0, i)), pl.BlockSpec((None, t, tn), lambda i, j: (j // 2, 0, j % 2))],
        out_specs=pl.BlockSpec((None, tm, tn), lambda i, j: (j, i, 0)),
        compiler_params=_cp(("arbitrary",) * 2), name=name)(h, x2)


def _mixnorm_fwd(att, gm, g_out, name, tm=512):
    t, w = att.shape

    def body(a_ref, m_ref, g_ref, o_ref):
        a, m, g = a_ref[...], m_ref[...], g_ref[...]
        o_ref[:, :w] = (a * _rms_inv(a) * g[:, :w]).astype(BF16)
        o_ref[:, w:] = (m * _rms_inv(m) * g[:, w:]).astype(BF16)

    return pl.pallas_call(body, out_shape=jax.ShapeDtypeStruct((t, 2 * w), BF16), grid=(t // tm,),
                          in_specs=[_rows(tm, w), _rows(tm, w), _full((1, 2 * w))], out_specs=_rows(tm, 2 * w),
                          compiler_params=_cp(("arbitrary",)), name=name)(att, gm, g_out)


def _mixnorm_bwd(att, gm, g_out, dmixed, name):
    w = att.shape[1]

    def fn(a, m, d, g):
        da, dga = _rms_bwd(a, g[:, :w], d[:, :w])
        dm, dgm = _rms_bwd(m, g[:, w:], d[:, w:])
        return da, dm, jnp.concatenate([dga, dgm], axis=1)

    return _rowwise(fn, name, [att, gm, dmixed], [g_out], [(w, F32), (w, F32)], [2 * w])


SCAN_BLOCK = 256


def _gate_fwd(zgf, b_pad, name):
    t = zgf.shape[0]
    fcol = zgf.shape[1] // LANES - 1
    nb = t // SCAN_BLOCK

    def body(f_ref, b_ref, c_ref):
        r = lax.broadcasted_iota(jnp.int32, (SCAN_BLOCK, SCAN_BLOCK), 0)
        s = lax.broadcasted_iota(jnp.int32, (SCAN_BLOCK, SCAN_BLOCK), 1)
        tril = (r >= s).astype(F32)
        head = lax.broadcasted_iota(jnp.int32, (SCAN_BLOCK, LANES), 1) < N_HEADS
        carry = jnp.zeros((1, LANES), F32)
        for blk in range(nb):
            rows = pl.ds(blk * SCAN_BLOCK, SCAN_BLOCK)
            x = f_ref[rows, :] + b_ref[...]
            lf = jnp.minimum(x, 0.0) - jnp.log1p(jnp.exp(-jnp.abs(x)))
            lf = jnp.where(head, lf, 0.0)
            cs = _dot(tril, lf, NN, HIGHEST) + carry
            c_ref[rows, :] = cs
            carry = carry + jnp.sum(lf, axis=0, keepdims=True)

    return pl.pallas_call(body, out_shape=jax.ShapeDtypeStruct((t, LANES), F32),
                          grid=(1,), in_specs=[pl.BlockSpec((t, LANES), lambda i: (0, fcol)), _full((1, LANES))],
                          out_specs=_full((t, LANES)), compiler_params=_cp(("arbitrary",)),
                          name=name)(zgf, b_pad)


def _gate_bwd(dc, zgf, b_pad, name):
    t = zgf.shape[0]
    fcol = zgf.shape[1] // LANES - 1
    nb = t // SCAN_BLOCK

    def body(dc_ref, f_ref, b_ref, dfl_ref, db_ref):
        r = lax.broadcasted_iota(jnp.int32, (SCAN_BLOCK, SCAN_BLOCK), 0)
        s = lax.broadcasted_iota(jnp.int32, (SCAN_BLOCK, SCAN_BLOCK), 1)
        triu = (s >= r).astype(F32)
        head = lax.broadcasted_iota(jnp.int32, (SCAN_BLOCK, LANES), 1) < N_HEADS
        carry = jnp.zeros((1, LANES), F32)
        db = jnp.zeros((1, LANES), F32)
        for blk in reversed(range(nb)):
            rows = pl.ds(blk * SCAN_BLOCK, SCAN_BLOCK)
            dc = dc_ref[rows, :]
            dlf = _dot(triu, dc, NN, HIGHEST) + carry
            x = f_ref[rows, :] + b_ref[...]
            dfl = jnp.where(head, dlf * jax.nn.sigmoid(-x), 0.0)
            dfl_ref[rows, :] = dfl.astype(BF16)
            db = db + jnp.sum(dfl, axis=0, keepdims=True)
            carry = carry + jnp.sum(dc, axis=0, keepdims=True)
        db_ref[...] = db

    return pl.pallas_call(body, out_shape=(jax.ShapeDtypeStruct((t, LANES), BF16), jax.ShapeDtypeStruct((1, LANES), F32)),
                          grid=(1,), in_specs=[_full((t, LANES)), pl.BlockSpec((t, LANES), lambda i: (0, fcol)),
                                               _full((1, LANES))],
                          out_specs=(_full((t, LANES)), _full((1, LANES))), compiler_params=_cp(("arbitrary",)),
                          name=name)(dc, zgf, b_pad)


ATT_BLOCK = 512
PAIRS = N_HEADS // 2
CQ_LANE = HEAD_DIM
CK_LANE = HEAD_DIM + 3
LSE_LANE = HEAD_DIM + 6


def _pick_col(x, idx):
    lane = lax.broadcasted_iota(jnp.int32, x.shape, 1)
    return jnp.sum(jnp.where(lane == idx, x, 0.0), axis=1, keepdims=True)


def _split3(x):
    hi = x.astype(BF16)
    r1 = x - hi.astype(F32)
    mid = r1.astype(BF16)
    lo = (r1 - mid.astype(F32)).astype(BF16)
    return hi, mid, lo


def _lanes_put(base, lane, start, vals):
    out = base
    for n, v in enumerate(vals):
        out = jnp.where(lane == start + n, v, out)
    return out


def _to_first_half(x, hh):
    return x if hh == 0 else pltpu.roll(x, HEAD_DIM, 1)


def _attn_prep(qkv, c, name, tm=512):
    t = qkv.shape[0]
    tm = min(tm, t)

    def body(q_ref, k_ref, v_ref, c_ref, qa_ref, ka_ref, va_ref):
        j = pl.program_id(1)
        lane = lax.broadcasted_iota(jnp.int32, (tm, LANES), 1)
        first = lane < HEAD_DIM
        q2, k2, v2 = q_ref[...].astype(F32), k_ref[...].astype(F32), v_ref[...].astype(F32)
        cc = c_ref[...]
        one = jnp.ones((tm, 1), F32)
        for hh in range(2):
            chi, cmid, clo = [v.astype(F32) for v in _split3(_pick_col(cc, 2 * j + hh))]
            qh = jnp.where(first, _to_first_half(q2, hh) * (HEAD_DIM ** -0.5), 0.0)
            kh = jnp.where(first, _to_first_half(k2, hh), 0.0)
            vh = jnp.where(first, _to_first_half(v2, hh), 0.0)
            qa = _lanes_put(qh, lane, CQ_LANE, [chi, cmid, clo, one, one, one])
            ka = _lanes_put(kh, lane, CQ_LANE, [one, one, one, -chi, -cmid, -clo, one, one, one])
            va = _lanes_put(vh, lane, CQ_LANE, [one, one, one])
            cols = slice(hh * LANES, (hh + 1) * LANES)
            qa_ref[:, cols] = qa.astype(BF16)
            ka_ref[:, cols] = ka.astype(BF16)
            va_ref[:, cols] = va.astype(BF16)

    blk = lambda off: pl.BlockSpec((tm, LANES), lambda i, j: (i, off + j))
    out = pl.BlockSpec((tm, 2 * LANES), lambda i, j: (i, j))
    shp = jax.ShapeDtypeStruct((t, N_HEADS * LANES), BF16)
    return pl.pallas_call(body, out_shape=(shp, shp, shp), grid=(t // tm, PAIRS),
                          in_specs=[blk(0), blk(PAIRS), blk(2 * PAIRS), pl.BlockSpec((tm, LANES), lambda i, j: (i, 0))],
                          out_specs=(out, out, out), compiler_params=_cp(("arbitrary",) * 2), name=name)(qkv, qkv, qkv, c)


def _causal_block(tb):
    return lax.broadcasted_iota(jnp.int32, (tb, tb), 0) >= lax.broadcasted_iota(jnp.int32, (tb, tb), 1)


def _causal_pairs(nb, key_major):
    pairs = [(q, k) for q in range(nb) for k in range(q + 1)]
    if key_major:
        pairs.sort(key=lambda qk: (qk[1], qk[0]))
    return (jnp.array([q for q, _ in pairs], jnp.int32), jnp.array([k for _, k in pairs], jnp.int32))


def _host(side, body, n_lead, arrays, in_specs, out_shapes, out_specs, scratch, grid):
    if side is None:
        return body, tuple(arrays), list(in_specs), list(out_shapes), list(out_specs), list(scratch)
    n_in, n_out, s_in, s_out = len(arrays), len(out_shapes), len(side.arrays), len(side.out_shapes)

    def hosted(*refs):
        ins_end = n_lead + n_in + s_in
        side_in, side_out = refs[n_lead + n_in:ins_end], refs[ins_end + n_out:ins_end + n_out + s_out]
        send, recv = refs[-2:]
        ids = [pl.program_id(ax) for ax in range(len(grid))]
        first, last = ids[0] == 0, ids[0] == grid[0] - 1
        for ax in range(1, len(grid)):
            first, last = first & (ids[ax] == 0), last & (ids[ax] == grid[ax] - 1)

        @pl.when(first)
        def _():
            side.start(side_in, side_out, send, recv)

        body(*refs[:n_lead + n_in], *refs[ins_end:ins_end + n_out], *refs[ins_end + n_out + s_out:-2])

        @pl.when(last)
        def _():
            side.wait(side_in, side_out, send, recv)

    return (hosted, tuple(arrays) + tuple(side.arrays), list(in_specs) + [HBM_SPEC] * s_in,
            list(out_shapes) + side.out_shapes, list(out_specs) + [HBM_SPEC] * s_out, list(scratch) + side.scratch)


def _pair_grid_call(body, tables, arrays, in_specs, out_shapes, out_specs, scratch, side, name):
    grid = (PAIRS, tables[0].shape[0])
    n_out = len(out_shapes)
    body, arrays, in_specs, out_shapes, out_specs, scratch = _host(side, body, 2, arrays, in_specs, out_shapes,
                                                                    out_specs, scratch, grid)
    grid_spec = pltpu.PrefetchScalarGridSpec(num_scalar_prefetch=2, grid=grid, in_specs=in_specs,
                                             out_specs=tuple(out_specs), scratch_shapes=scratch)
    out = pl.pallas_call(body, out_shape=tuple(out_shapes), grid_spec=grid_spec,
                         compiler_params=_cp(("arbitrary",) * 2), name=name)(*tables, *arrays)
    return out[:n_out], out[n_out:]


def _attn_fwd(qa, ka, va, name, side=None):
    t = qa.shape[0]
    tb = min(ATT_BLOCK, t)
    q_tab, k_tab = _causal_pairs(t // tb, key_major=False)

    def body(q_tab_ref, k_tab_ref, q_ref, k_ref, v_ref, o_ref, lse_ref, m0, m1, acc0, acc1):
        qi, kb = q_tab_ref[pl.program_id(1)], k_tab_ref[pl.program_id(1)]
        m_s, acc_s = (m0, m1), (acc0, acc1)

        @pl.when(kb == 0)
        def _():
            for hh in range(2):
                m_s[hh][...] = jnp.full(m_s[hh].shape, NEG_INF, F32)
                acc_s[hh][...] = jnp.zeros(acc_s[hh].shape, F32)

        def step(diagonal):
            for hh in range(2):
                cols = slice(hh * LANES, (hh + 1) * LANES)
                sc = _dot(q_ref[:, cols], k_ref[:, cols], NT)
                if diagonal:
                    sc = jnp.where(_causal_block(tb), sc, NEG_INF)
                m_prev = m_s[hh][...]
                m_new = jnp.maximum(m_prev, jnp.max(sc, axis=1, keepdims=True))
                p = jnp.exp(sc - m_new)
                acc_s[hh][...] = jnp.exp(m_prev - m_new) * acc_s[hh][...] + _dot(p.astype(BF16), v_ref[:, cols], NN)
                m_s[hh][...] = m_new

        @pl.when(kb < qi)
        def _():
            step(False)

        @pl.when(kb == qi)
        def _():
            step(True)
            lane = lax.broadcasted_iota(jnp.int32, (tb, LANES), 1)
            outs, lses = [], []
            for hh in range(2):
                acc = acc_s[hh][...]
                l = _pick_col(acc, CQ_LANE)
                outs.append(acc / l)
                lses.append(m_s[hh][...] + jnp.log(l))
            o_ref[...] = jnp.where(lane < HEAD_DIM, outs[0], pltpu.roll(outs[1], HEAD_DIM, 1))
            lse_ref[...] = jnp.where(lane == 0, lses[0], jnp.where(lane == 1, lses[1], 0.0))

    qrow = lambda j, s, qt, kt: (qt[s], j)
    krow = lambda j, s, qt, kt: (kt[s], j)
    return _pair_grid_call(
        body, (q_tab, k_tab), (qa, ka, va),
        [pl.BlockSpec((tb, 2 * LANES), qrow), pl.BlockSpec((tb, 2 * LANES), krow), pl.BlockSpec((tb, 2 * LANES), krow)],
        [jax.ShapeDtypeStruct((t, D_ATT), F32), jax.ShapeDtypeStruct((t, PAIRS * LANES), F32)],
        [pl.BlockSpec((tb, LANES), qrow), pl.BlockSpec((tb, LANES), qrow)],
        [pltpu.VMEM((tb, 1), F32)] * 2 + [pltpu.VMEM((tb, LANES), F32)] * 2, side, name)


def _attn_bwd_prep(qa, att, lse, datt, name, tm=512):
    t = qa.shape[0]
    tm = min(tm, t)

    def body(qa_ref, o_ref, lse_ref, do_ref, qb_ref, doa_ref):
        j = pl.program_id(1)
        lane = lax.broadcasted_iota(jnp.int32, (tm, LANES), 1)
        first = lane < HEAD_DIM
        do = do_ref[...]
        prod = do * o_ref[...]
        lse2 = lse_ref[...]
        for hh in range(2):
            cols = slice(hh * LANES, (hh + 1) * LANES)
            delta = jnp.sum(jnp.where(first if hh == 0 else ~first, prod, 0.0), axis=1, keepdims=True)
            doh = jnp.where(first, _to_first_half(do, hh), 0.0)
            doa_ref[:, cols] = _lanes_put(doh, lane, CQ_LANE, [v.astype(F32) for v in _split3(-delta)]).astype(BF16)
            nl = [v.astype(F32) for v in _split3(-_pick_col(lse2, hh))]
            qb_ref[:, cols] = _lanes_put(qa_ref[:, cols].astype(F32), lane, LSE_LANE, nl).astype(BF16)

    wide = pl.BlockSpec((tm, 2 * LANES), lambda i, j: (i, j))
    pair = pl.BlockSpec((tm, LANES), lambda i, j: (i, j))
    shp = jax.ShapeDtypeStruct((t, N_HEADS * LANES), BF16)
    return pl.pallas_call(body, out_shape=(shp, shp), grid=(t // tm, PAIRS), in_specs=[wide, pair, pair, pair],
                          out_specs=(wide, wide), compiler_params=_cp(("arbitrary",) * 2), name=name)(qa, att, lse, datt)


def _attn_bwd(qb, ka, va, doa, name, side=None):
    t = qb.shape[0]
    tb = min(ATT_BLOCK, t)
    nb = t // tb
    q_tab, k_tab = _causal_pairs(nb, key_major=True)

    def body(q_tab_ref, k_tab_ref, q_ref, k_ref, v_ref, do_ref, dq_ref, dk_ref, dv_ref, dk0, dk1, dv0, dv1):
        qi, kb = q_tab_ref[pl.program_id(1)], k_tab_ref[pl.program_id(1)]
        dk_s, dv_s = (dk0, dk1), (dv0, dv1)

        @pl.when(qi == kb)
        def _():
            for ref in dk_s + dv_s:
                ref[...] = jnp.zeros(ref.shape, F32)

        def step(diagonal):
            rows = pl.ds(pl.multiple_of(qi * tb, tb), tb)
            for hh in range(2):
                cols = slice(hh * LANES, (hh + 1) * LANES)
                q, k, do = q_ref[:, cols], k_ref[:, cols], do_ref[:, cols]
                sc = _dot(q, k, NT)
                if diagonal:
                    sc = jnp.where(_causal_block(tb), sc, NEG_INF)
                p = jnp.exp(sc)
                ds = (p * _dot(do, v_ref[:, cols], NT)).astype(BF16)
                dv_s[hh][...] += _dot(p.astype(BF16), do, TN)
                dk_s[hh][...] += _dot(ds, q, TN)
                dq_new = _dot(ds, k, NN)

                @pl.when(kb == 0)
                def _():
                    dq_ref[rows, cols] = dq_new

                @pl.when(kb > 0)
                def _():
                    dq_ref[rows, cols] += dq_new

        @pl.when(qi == kb)
        def _():
            step(True)

        @pl.when(qi > kb)
        def _():
            step(False)

        @pl.when(qi == nb - 1)
        def _():
            for hh in range(2):
                cols = slice(hh * LANES, (hh + 1) * LANES)
                dk_ref[:, cols] = dk_s[hh][...]
                dv_ref[:, cols] = dv_s[hh][...].astype(BF16)

    qrow = lambda j, s, qt, kt: (qt[s], j)
    krow = lambda j, s, qt, kt: (kt[s], j)
    blk = (tb, 2 * LANES)
    wide = (t, N_HEADS * LANES)
    return _pair_grid_call(
        body, (q_tab, k_tab), (qb, ka, va, doa),
        [pl.BlockSpec(blk, qrow), pl.BlockSpec(blk, krow), pl.BlockSpec(blk, krow), pl.BlockSpec(blk, qrow)],
        [jax.ShapeDtypeStruct(wide, F32), jax.ShapeDtypeStruct(wide, F32), jax.ShapeDtypeStruct(wide, BF16)],
        [pl.BlockSpec((t, 2 * LANES), lambda j, s, qt, kt: (0, j)), pl.BlockSpec(blk, krow), pl.BlockSpec(blk, krow)],
        [pltpu.VMEM((tb, LANES), F32)] * 4, side, name)


def _attn_bwd_post(dqa, dka, dva, name, tm=256):
    t = dqa.shape[0]
    tm = min(tm, t)

    def body(dq_ref, dk_ref, dv_ref, o_ref, dc_ref):
        lane = lax.broadcasted_iota(jnp.int32, (tm, LANES), 1)
        first = lane < HEAD_DIM
        dc = jnp.zeros((tm, LANES), F32)
        for j in range(PAIRS):
            packed = []
            for ref, gain in ((dq_ref, HEAD_DIM ** -0.5), (dk_ref, 1.0), (dv_ref, 1.0)):
                even = ref[:, 2 * j * LANES:(2 * j + 1) * LANES].astype(F32)
                odd = ref[:, (2 * j + 1) * LANES:(2 * j + 2) * LANES].astype(F32)
                packed.append((jnp.where(first, even, pltpu.roll(odd, HEAD_DIM, 1)) * gain).astype(BF16))
                if ref is dq_ref:
                    dc = dc + jnp.where(lane == 2 * j, _pick_col(even, CQ_LANE), 0.0)
                    dc = dc + jnp.where(lane == 2 * j + 1, _pick_col(odd, CQ_LANE), 0.0)
                if ref is dk_ref:
                    dc = dc - jnp.where(lane == 2 * j, _pick_col(even, CK_LANE), 0.0)
                    dc = dc - jnp.where(lane == 2 * j + 1, _pick_col(odd, CK_LANE), 0.0)
            for part, val in enumerate(packed):
                o_ref[:, (part * PAIRS + j) * LANES:(part * PAIRS + j + 1) * LANES] = val
        dc_ref[...] = dc

    wide = _rows(tm, N_HEADS * LANES)
    return pl.pallas_call(body, out_shape=(jax.ShapeDtypeStruct((t, 3 * D_ATT), BF16), jax.ShapeDtypeStruct((t, LANES), F32)),
                          grid=(t // tm,), in_specs=[wide, wide, wide], out_specs=(_rows(tm, 3 * D_ATT), _rows(tm, LANES)),
                          compiler_params=_cp(("arbitrary",)), name=name)(dqa, dka, dva)


GELU_K = 0.7978845608028654
GELU_A = 0.044715


def _gelu(x):
    th = jnp.tanh(GELU_K * (x + GELU_A * x * x * x))
    return 0.5 * x * (1.0 + th), th


def _group_mean_matrix():
    r = jnp.arange(D_GM)[:, None] // HEAD_DIM
    s = jnp.arange(D_GM)[None, :] // HEAD_DIM
    return jnp.where(r == s, 1.0 / HEAD_DIM, 0.0).astype(BF16)


def _group_mean(x, mean_mat):
    hi = x.astype(BF16)
    lo = (x - hi.astype(F32)).astype(BF16)
    return _dot(hi, mean_mat, NN) + _dot(lo, mean_mat, NN)


def _gm_forward_parts(g, w_ref, bias, gain, mean_mat):
    gel, _ = _gelu(g)
    u, vv = gel[:, :D_GM], gel[:, D_GM:]
    mu = _group_mean(vv, mean_mat)
    d = vv - mu
    rstd = lax.rsqrt(_group_mean(d * d, mean_mat) + EPS)
    xhat = d * rstd
    vn = (xhat * gain).astype(BF16)
    first = lax.broadcasted_iota(jnp.int32, (CHUNK, LANES), 1) < HEAD_DIM
    tri = lax.broadcasted_iota(jnp.int32, (CHUNK, CHUNK), 0) >= lax.broadcasted_iota(jnp.int32, (CHUNK, CHUNK), 1)
    wm = [jnp.where(tri, w_ref[grp], 0.0).astype(BF16) for grp in range(w_ref.shape[0])]
    chunks = []
    for ck in range(g.shape[0] // CHUNK):
        parts = []
        for jp in range(D_GM // LANES):
            vp = vn[ck * CHUNK:(ck + 1) * CHUNK, jp * LANES:(jp + 1) * LANES]
            parts.append(jnp.where(first, _dot(wm[2 * jp], vp, NN), _dot(wm[2 * jp + 1], vp, NN)))
        chunks.append(jnp.concatenate(parts, axis=1) + bias)
    return u, xhat, rstd, vn, jnp.concatenate(chunks, axis=0), wm


GM_ROWS = 512


def _gmlp_fwd(zgf, w_s, bias_full, gain, mean_mat, name):
    t = zgf.shape[0]
    tm = min(GM_ROWS, t)

    def body(g_ref, w_ref, b_ref, gain_ref, mm_ref, o_ref):
        u, _, _, _, mixed, _ = _gm_forward_parts(g_ref[...], w_ref, b_ref[...], gain_ref[...], mm_ref[...])
        o_ref[...] = u * mixed

    return pl.pallas_call(body, out_shape=jax.ShapeDtypeStruct((t, D_GM), F32), grid=(t // tm,),
                          in_specs=[_rows(tm, 2 * D_GM), _full(w_s.shape), _full((CHUNK, D_GM)), _full((1, D_GM)),
                                    _full((D_GM, D_GM))],
                          out_specs=_rows(tm, D_GM), compiler_params=_cp(("arbitrary",)),
                          name=name)(zgf, w_s, bias_full, gain, mean_mat)


def _gmlp_bwd(zgf, dgm, w_s, bias_full, gain, mean_mat, name, side=None):
    t = zgf.shape[0]
    tm = min(GM_ROWS, t)

    def body(g_ref, d_ref, w_ref, b_ref, gain_ref, mm_ref, dg_ref, dw_ref, dmix_ref, dgain_ref):
        g, gain, mean_mat = g_ref[...], gain_ref[...], mm_ref[...]
        u, xhat, rstd, vn, mixed, wm = _gm_forward_parts(g, w_ref, b_ref[...], gain, mean_mat)
        dgm_v = d_ref[...]
        du = dgm_v * mixed
        dmixed = dgm_v * u
        dm_b = dmixed.astype(BF16)
        first = lax.broadcasted_iota(jnp.int32, (CHUNK, LANES), 1) < HEAD_DIM
        tri = lax.broadcasted_iota(jnp.int32, (CHUNK, CHUNK), 0) >= lax.broadcasted_iota(jnp.int32, (CHUNK, CHUNK), 1)

        @pl.when(pl.program_id(0) == 0)
        def _():
            dw_ref[...] = jnp.zeros(dw_ref.shape, F32)
            dmix_ref[...] = jnp.zeros(dmix_ref.shape, F32)
            dgain_ref[...] = jnp.zeros(dgain_ref.shape, F32)

        dw = [jnp.zeros((CHUNK, CHUNK), F32) for _ in wm]
        dmix = jnp.zeros((CHUNK, D_GM), F32)
        dvn_chunks = []
        for ck in range(tm // CHUNK):
            rows = slice(ck * CHUNK, (ck + 1) * CHUNK)
            dmix = dmix + dmixed[rows]
            dvn_parts = []
            for jp in range(D_GM // LANES):
                vp = vn[rows, jp * LANES:(jp + 1) * LANES]
                dmp = dm_b[rows, jp * LANES:(jp + 1) * LANES]
                halves = []
                for hh in range(2):
                    sel = first if hh == 0 else ~first
                    grp = 2 * jp + hh
                    dw[grp] = dw[grp] + _dot(jnp.where(sel, dmp, jnp.zeros_like(dmp)), vp, NT)
                    halves.append(_dot(wm[grp], dmp, TN))
                dvn_parts.append(jnp.where(first, halves[0], halves[1]))
            dvn_chunks.append(jnp.concatenate(dvn_parts, axis=1))
        dvn = jnp.concatenate(dvn_chunks, axis=0)
        for grp, dwg in enumerate(dw):
            dw_ref[grp] += jnp.where(tri, dwg, 0.0)
        dmix_ref[...] += dmix
        dgain_ref[...] += jnp.sum(dvn * xhat, axis=0, keepdims=True)
        dxhat = dvn * gain
        m1 = _group_mean(dxhat, mean_mat)
        m2 = _group_mean(dxhat * xhat, mean_mat)
        dvv = rstd * (dxhat - m1 - xhat * m2)
        gel, th = _gelu(g)
        dgel = 0.5 * (1.0 + th) + 0.5 * g * (1.0 - th * th) * GELU_K * (1.0 + 3.0 * GELU_A * g * g)
        dg_ref[...] = (jnp.concatenate([du, dvv], axis=1) * dgel).astype(BF16)

    grid = (t // tm,)
    body, arrays, in_specs, out_shapes, out_specs, scratch = _host(
        side, body, 0, (zgf, dgm, w_s, bias_full, gain, mean_mat),
        [_rows(tm, 2 * D_GM), _rows(tm, D_GM), _full(w_s.shape), _full((CHUNK, D_GM)), _full((1, D_GM)),
         _full((D_GM, D_GM))],
        [jax.ShapeDtypeStruct((t, 2 * D_GM), BF16), jax.ShapeDtypeStruct(w_s.shape, F32),
         jax.ShapeDtypeStruct((CHUNK, D_GM), F32), jax.ShapeDtypeStruct((1, D_GM), F32)],
        [_rows(tm, 2 * D_GM), _full(w_s.shape), _full((CHUNK, D_GM)), _full((1, D_GM))], [], grid)
    out = pl.pallas_call(body, out_shape=tuple(out_shapes), grid=grid, in_specs=in_specs, out_specs=tuple(out_specs),
                         scratch_shapes=scratch, compiler_params=_cp(("arbitrary",)), name=name)(*arrays)
    return out[:4], out[4:]


def _row_tile(r, c, budget=1 << 19):
    best = None
    for tr in range(8, r + 1, 8):
        if r % tr == 0 and tr * c <= budget:
            best = tr
    return best if best is not None else r


def _adamw(w, g, m, v, name):
    nl, r, c = w.shape
    tr = _row_tile(r, c, 1 << 18)
    c1 = 1.0 - ADAM_B1 ** ADAM_STEP
    c2 = 1.0 - ADAM_B2 ** ADAM_STEP

    def body(w_ref, g_ref, m_ref, v_ref, d_ref, mo_ref, vo_ref):
        gv = g_ref[...]
        mn = ADAM_B1 * m_ref[...] + (1.0 - ADAM_B1) * gv
        vn = ADAM_B2 * v_ref[...] + (1.0 - ADAM_B2) * jnp.square(gv)
        mo_ref[...] = mn
        vo_ref[...] = vn
        d_ref[...] = -ADAM_LR * ((mn / c1) / (jnp.sqrt(vn / c2) + ADAM_EPS) + ADAM_WD * w_ref[...])

    spec = pl.BlockSpec((None, tr, c), lambda l, i: (l, i, 0))
    shp = jax.ShapeDtypeStruct(w.shape, F32)
    return pl.pallas_call(body, out_shape=(shp, shp, shp), grid=(nl, r // tr), in_specs=[spec] * 4,
                          out_specs=(spec, spec, spec), compiler_params=_cp(("arbitrary",) * 2), name=name)(w, g, m, v)


def _add_sibling(g, recv, c_idx, wire_dtype, name):
    nj, _, h, c = g.shape
    tr = _row_tile(h, c)

    def body(c_ref, g_ref, r_ref, o_ref):
        o_ref[...] = (g_ref[...] + r_ref[...]).astype(wire_dtype)

    grid_spec = pltpu.PrefetchScalarGridSpec(
        num_scalar_prefetch=1, grid=(nj, h // tr),
        in_specs=[pl.BlockSpec((None, None, tr, c), lambda j, i, c_ref: (j, c_ref[0], i, 0)),
                  pl.BlockSpec((None, tr, c), lambda j, i, c_ref: (j, i, 0))],
        out_specs=pl.BlockSpec((None, tr, c), lambda j, i, c_ref: (j, i, 0)))
    return pl.pallas_call(body, out_shape=jax.ShapeDtypeStruct((nj, h, c), wire_dtype), grid_spec=grid_spec,
                          compiler_params=_cp(("arbitrary",) * 2), name=name)(c_idx, g, recv)


def _add_chips(own, parts, place, name):
    _, h, c = own.shape
    tr = _row_tile(h, c)

    def body(p_ref, o_ref, a_ref, b_ref, c_ref, out_ref):
        out_ref[...] = ((o_ref[...].astype(F32) + a_ref[...].astype(F32)) + b_ref[...].astype(F32)) + c_ref[...].astype(F32)

    def other(k):
        return pl.BlockSpec((None, tr, c), lambda i, p_ref: (jnp.bitwise_xor(p_ref[0], k), i, 0))

    grid_spec = pltpu.PrefetchScalarGridSpec(
        num_scalar_prefetch=1, grid=(h // tr,),
        in_specs=[pl.BlockSpec((None, tr, c), lambda i, p_ref: (p_ref[0], i, 0)), other(1), other(2), other(3)],
        out_specs=pl.BlockSpec((None, tr, c), lambda i, p_ref: (p_ref[1], i, 0)))
    return pl.pallas_call(body, out_shape=jax.ShapeDtypeStruct((2, h, c), F32), grid_spec=grid_spec,
                          compiler_params=_cp(("arbitrary",)), name=name)(place, own, parts, parts, parts)


HBM_SPEC = pl.BlockSpec(memory_space=pltpu.HBM)


def _place():
    x, y, c = lax.axis_index("x"), lax.axis_index("y"), lax.axis_index("c")
    chips = [(1 - x, y), (x, 1 - y), (1 - x, 1 - y)]
    return x, y, c, 2 * x + y, chips, [2 * px + py for px, py in chips]


def _remote(src, dst, send_sem, recv_sem, dev):
    return pltpu.make_async_remote_copy(src_ref=src, dst_ref=dst, send_sem=send_sem, recv_sem=recv_sem,
                                        device_id=dev, device_id_type=MESH)


def _comm_call(body, arrays, out_shapes, sems, name, aliases=None):
    n = len(arrays)
    return pl.pallas_call(
        body, out_shape=tuple(out_shapes), in_specs=[HBM_SPEC] * n, out_specs=tuple([HBM_SPEC] * len(out_shapes)),
        scratch_shapes=[pltpu.SemaphoreType.DMA(s) for s in sems], input_output_aliases=aliases or {},
        compiler_params=pltpu.CompilerParams(has_side_effects=True), name=name)(*arrays)


class _SideCopies:
    def __init__(self, arrays, out_shapes, sem_shape, sends, recvs):
        self.arrays, self.out_shapes, self.sem_shape = list(arrays), list(out_shapes), sem_shape
        self.sends, self.recvs = sends, recvs

    @property
    def scratch(self):
        return [pltpu.SemaphoreType.DMA(self.sem_shape), pltpu.SemaphoreType.DMA(self.sem_shape)]

    def start(self, ins, outs, send, recv):
        for cp in self.sends(ins, outs, send, recv):
            cp.start()

    def wait(self, ins, outs, send, recv):
        for cp in self.recvs(ins, outs, send, recv):
            cp.wait_recv()
        for cp in self.sends(ins, outs, send, recv):
            cp.wait_send()

    def run(self, name):
        n = len(self.arrays)

        def body(*refs):
            ins, outs = refs[:n], refs[n:n + len(self.out_shapes)]
            send, recv = refs[-2:]
            self.start(ins, outs, send, recv)
            self.wait(ins, outs, send, recv)

        return _comm_call(body, self.arrays, self.out_shapes, [self.sem_shape] * 2, name)


def _gather_over_chips(shards):
    n = len(shards)

    def sends(ins, outs, send, recv):
        _, _, c, me, chips, _ = _place()
        return [_remote(ins[a].at[c], outs[a].at[me, c], send.at[a, k], recv.at[a, k], (px, py, c))
                for a in range(n) for k, (px, py) in enumerate(chips)]

    def recvs(ins, outs, send, recv):
        x, y, c, _, _, cidx = _place()
        return [_remote(outs[a].at[cidx[k], c], outs[a].at[cidx[k], c], send.at[a, k], recv.at[a, k], (x, y, 1 - c))
                for a in range(n) for k in range(3)]

    out_shapes = [jax.ShapeDtypeStruct((N_CHIPS,) + s.shape, s.dtype) for s in shards]
    return _SideCopies(shards, out_shapes, (n, 3), sends, recvs)


def _gather_finish(shards, partial, name):
    n = len(shards)

    def body(*refs):
        ins, part, outs = refs[:n], refs[n:2 * n], refs[2 * n:3 * n]
        send, recv = refs[3 * n:]
        x, y, c, me, _, cidx = _place()
        sib = (x, y, 1 - c)
        cps = []
        for a in range(n):
            for k in range(3):
                cps.append(_remote(part[a].at[cidx[k], c], outs[a].at[cidx[k], c], send.at[a, k], recv.at[a, k], sib))
            cps.append(_remote(ins[a], outs[a].at[me], send.at[a, 3], recv.at[a, 3], sib))
        for cp in cps:
            cp.start()
        for a in range(n):
            for k, blk in enumerate([outs[a].at[cidx[k], 1 - c] for k in range(3)] + [outs[a].at[me]]):
                _remote(blk, blk, send.at[a, k], recv.at[a, k], sib).wait_recv()
        for cp in cps:
            cp.wait_send()

    out_shapes = [jax.ShapeDtypeStruct(p.shape, p.dtype) for p in partial]
    return _comm_call(body, list(shards) + list(partial), out_shapes, [(n, 4), (n, 4)], name,
                      aliases={n + a: a for a in range(n)})


def _sibling_halves(grads):
    n = len(grads)

    def copies(ins, outs, send, recv):
        x, y, c, _, _, _ = _place()
        return [_remote(ins[a].at[j, 1 - c], outs[a].at[j], send.at[a, j], recv.at[a, j], (x, y, 1 - c))
                for a in range(n) for j in range(N_CHIPS)]

    out_shapes = [jax.ShapeDtypeStruct((g.shape[0],) + g.shape[2:], g.dtype) for g in grads]
    return _SideCopies(grads, out_shapes, (n, N_CHIPS), copies, copies)


def _scatter_over_chips(sums):
    n = len(sums)

    def sends(ins, outs, send, recv):
        _, _, c, me, chips, cidx = _place()
        return [_remote(ins[a].at[cidx[k]], outs[a].at[me], send.at[a, k], recv.at[a, k], (px, py, c))
                for a in range(n) for k, (px, py) in enumerate(chips)]

    def recvs(ins, outs, send, recv):
        x, y, c, _, _, cidx = _place()
        return [_remote(outs[a].at[cidx[k]], outs[a].at[cidx[k]], send.at[a, k], recv.at[a, k], (x, y, 1 - c))
                for a in range(n) for k in range(3)]

    return _SideCopies(sums, [jax.ShapeDtypeStruct(s.shape, s.dtype) for s in sums], (n, 3), sends, recvs)


def _exchange_halves(halves, name):
    n = len(halves)

    def body(*refs):
        ins, bufs = refs[:n], refs[n:2 * n]
        send, recv = refs[2 * n:]
        x, y, c, _, _, _ = _place()
        sib = (x, y, 1 - c)
        cps = []
        for a in range(n):
            cp = _remote(ins[a].at[c], bufs[a].at[c], send.at[a], recv.at[a], sib)
            cp.start()
            cps.append(cp)
        for a in range(n):
            blk = bufs[a].at[1 - c]
            _remote(blk, blk, send.at[a], recv.at[a], sib).wait_recv()
        for cp in cps:
            cp.wait_send()

    out_shapes = [jax.ShapeDtypeStruct(s.shape, s.dtype) for s in halves]
    return _comm_call(body, halves, out_shapes, [(n,), (n,)], name, aliases={a: a for a in range(n)})


def _gather_chips(slices, name):
    n = len(slices)

    def body(*refs):
        ins, outs = refs[:n], refs[n:2 * n]
        send, recv = refs[2 * n:]
        x, y, c, me, chips, cidx = _place()
        sib = (x, y, 1 - c)
        cps = []
        for a in range(n):
            for k, dev in enumerate([(px, py, c) for px, py in chips] + [sib]):
                cp = _remote(ins[a], outs[a].at[me], send.at[a, k], recv.at[a, k], dev)
                cp.start()
                cps.append(cp)
        for a in range(n):
            for k, slot in enumerate(cidx + [me]):
                blk = outs[a].at[slot]
                _remote(blk, blk, send.at[a, k], recv.at[a, k], sib).wait_recv()
        for cp in cps:
            cp.wait_send()

    out_shapes = [jax.ShapeDtypeStruct((N_CHIPS,) + s.shape, s.dtype) for s in slices]
    return _comm_call(body, slices, out_shapes, [(n, 4), (n, 4)], name)


def _rs_siblings(grads):
    return _sibling_halves([g.reshape(g.shape[0], 2, g.shape[1] // 2, g.shape[2]) for g in grads])


def _rs_chips(siblings, recv, c_idx, wire_dtype, tag):
    return _scatter_over_chips([_add_sibling(g, r, c_idx, wire_dtype, "rs_add_sibling_" + tag)
                                for g, r in zip(siblings.arrays, recv)])


def _rs_end(scatter, parts, place, tag):
    halves = [_add_chips(s, p, place, "rs_add_chips_" + tag) for s, p in zip(scatter.arrays, parts)]
    both = _exchange_halves(halves, "rs_halves_" + tag)
    return [b.reshape(b.shape[0] * b.shape[1], b.shape[2]) for b in both]


SMALL_ORDER = ("gm_w_s", "mix_pre_norm", "mix_post_norm", "mix_out_norm", "ffn_pre_norm", "ffn_post_norm", "ple_norm",
               "gm_v_norm", "gm_b_s", "b_forget")
SMALL_ROWS_MULTIPLE = 64
SMALL_BLOCK = 8 * LANES


def _chip_columns(pieces, width):
    out = []
    for j in range(N_CHIPS):
        lo, hi, parts, off = j * width, (j + 1) * width, [], 0
        for piece in pieces:
            a, b = max(lo, off), min(hi, off + piece.shape[1])
            if a < b:
                parts.append(piece[:, a - off:b - off])
            off += piece.shape[1]
        out.append(parts[0] if len(parts) == 1 else jnp.concatenate(parts, axis=1))
    return jnp.stack(out)


def _columns(sliced, lo, hi):
    width, parts = sliced.shape[2], []
    for j in range(N_CHIPS):
        a, b = max(lo, j * width), min(hi, (j + 1) * width)
        if a < b:
            parts.append(sliced[j][:, a - j * width:b - j * width])
    return parts[0] if len(parts) == 1 else jnp.concatenate(parts, axis=1)


def _pack_small(parts):
    blocks = []
    for nme in SMALL_ORDER:
        v = parts[nme].reshape(-1)
        pad = (-v.shape[0]) % SMALL_BLOCK
        blocks.append((jnp.pad(v, (0, pad)) if pad else v).reshape(-1, LANES))
    rows = sum(b.shape[0] for b in blocks)
    if rows % SMALL_ROWS_MULTIPLE:
        blocks.append(jnp.zeros((SMALL_ROWS_MULTIPLE - rows % SMALL_ROWS_MULTIPLE, LANES), F32))
    return jnp.concatenate(blocks, axis=0)


def _unpack_small(packed, shapes):
    out, row = {}, 0
    for nme in SMALL_ORDER:
        size = 1
        for s in shapes[nme]:
            size *= s
        rows = -(-size // SMALL_BLOCK) * (SMALL_BLOCK // LANES)
        block = packed[row:row + rows]
        out[nme] = (block if size % SMALL_BLOCK == 0 else block.reshape(-1)[:size]).reshape(shapes[nme])
        row += rows
    return out


def kernel(x, p, mix_pre_norm, mix_post_norm, w_in, b_forget, gm_v_norm, gm_w_s, gm_b_s, mix_out_norm, w_out, ffn_pre_norm, ffn_post_norm, w_ffn_in, w_ffn_out, w_ple, ple_norm, w_ple_gate, loss_target, m_mix_pre_norm, m_mix_post_norm, m_w_in, m_b_forget, m_gm_v_norm, m_gm_w_s, m_gm_b_s, m_mix_out_norm, m_w_out, m_ffn_pre_norm, m_ffn_post_norm, m_w_ffn_in, m_w_ffn_out, m_w_ple, m_ple_norm, m_w_ple_gate, v_mix_pre_norm, v_mix_post_norm, v_w_in, v_b_forget, v_gm_v_norm, v_gm_w_s, v_gm_b_s, v_mix_out_norm, v_w_out, v_ffn_pre_norm, v_ffn_post_norm, v_w_ffn_in, v_w_ffn_out, v_w_ple, v_ple_norm, v_w_ple_gate):
    weights = dict(mix_pre_norm=mix_pre_norm, mix_post_norm=mix_post_norm, w_in=w_in, b_forget=b_forget,
                   gm_v_norm=gm_v_norm, gm_w_s=gm_w_s, gm_b_s=gm_b_s, mix_out_norm=mix_out_norm, w_out=w_out,
                   ffn_pre_norm=ffn_pre_norm, ffn_post_norm=ffn_post_norm, w_ffn_in=w_ffn_in, w_ffn_out=w_ffn_out,
                   w_ple=w_ple, ple_norm=ple_norm, w_ple_gate=w_ple_gate)
    mom_m = dict(mix_pre_norm=m_mix_pre_norm, mix_post_norm=m_mix_post_norm, w_in=m_w_in, b_forget=m_b_forget,
                 gm_v_norm=m_gm_v_norm, gm_w_s=m_gm_w_s, gm_b_s=m_gm_b_s, mix_out_norm=m_mix_out_norm, w_out=m_w_out,
                 ffn_pre_norm=m_ffn_pre_norm, ffn_post_norm=m_ffn_post_norm, w_ffn_in=m_w_ffn_in,
                 w_ffn_out=m_w_ffn_out, w_ple=m_w_ple, ple_norm=m_ple_norm, w_ple_gate=m_w_ple_gate)
    mom_v = dict(mix_pre_norm=v_mix_pre_norm, mix_post_norm=v_mix_post_norm, w_in=v_w_in, b_forget=v_b_forget,
                 gm_v_norm=v_gm_v_norm, gm_w_s=v_gm_w_s, gm_b_s=v_gm_b_s, mix_out_norm=v_mix_out_norm, w_out=v_w_out,
                 ffn_pre_norm=v_ffn_pre_norm, ffn_post_norm=v_ffn_post_norm, w_ffn_in=v_w_ffn_in,
                 w_ffn_out=v_w_ffn_out, w_ple=v_w_ple, ple_norm=v_ple_norm, w_ple_gate=v_w_ple_gate)
    big = ("w_in", "w_out", "w_ffn_in", "w_ffn_out", "w_ple", "w_ple_gate")
    depth = w_in.shape[0]
    t, d = x.shape[1], x.shape[2]
    d_ff = w_ffn_out.shape[1] * N_CHIPS
    c_idx = lax.axis_index("c").astype(jnp.int32).reshape(1)
    place = jnp.stack([2 * lax.axis_index("x") + lax.axis_index("y"), lax.axis_index("c")]).astype(jnp.int32)
    h = x[0]
    target = loss_target[0]
    mean_mat = _group_mean_matrix()

    def row(a, i):
        return a[i].reshape(1, -1)

    def shards_of(i):
        shards = [weights[nme][i].astype(BF16) for nme in big]
        return [s.reshape(2, s.shape[0] // 2, s.shape[1]) for s in shards]

    saved = []
    hn = _norm_cast(h, row(mix_pre_norm, 0), "norm_first")
    by_cols = lambda g: g.transpose(1, 0, 2).reshape(g.shape[1], N_CHIPS * g.shape[2])
    by_rows = lambda g: g.reshape(N_CHIPS * g.shape[1], g.shape[2])
    whole = lambda g: g.reshape(N_CHIPS, g.shape[2] * 2, g.shape[3])
    gather = _gather_over_chips(shards_of(0)[:1])
    w_in_g = _gather_finish(gather.arrays, gather.run("gather_first"), "gather_finish_first")[0]
    for i in range(depth):
        w_in_c = whole(w_in_g)
        w_qkv = _columns(w_in_c, 0, 3 * D_ATT)
        w_gf = jnp.concatenate([_columns(w_in_c, 3 * D_ATT + N_HEADS, N_CHIPS * w_in_c.shape[2]),
                                _columns(w_in_c, 3 * D_ATT, 3 * D_ATT + N_HEADS),
                                jnp.zeros((d, LANES - N_HEADS), BF16)], axis=1)
        b_pad = jnp.pad(b_forget[i], (0, LANES - N_HEADS)).reshape(1, LANES)
        bias_full = jnp.repeat(gm_b_s[i].T, HEAD_DIM, axis=1)
        gain_v = row(gm_v_norm, i)

        qkv = _mm(hn, w_qkv, "nn", BF16, "mm_qkv")
        zgf = _mm(hn, w_gf, "nn", F32, "mm_gf", tn_cap=384)
        qa, ka, va = _attn_prep(qkv, _gate_fwd(zgf, b_pad, "gate_fwd"), "attn_prep")
        last = i + 1 == depth
        gather = _gather_over_chips(shards_of(i)[1:] + ([] if last else shards_of(i + 1)[:1]))
        (att, lse), partial = _attn_fwd(qa, ka, va, "attn_fwd_last" if last else "attn_fwd", gather)
        gathered = _gather_finish(gather.arrays, partial, "gather_finish_last" if last else "gather_finish")
        w_out_f, w_fi_f, w_fo_f = by_rows(whole(gathered[0])), by_cols(whole(gathered[1])), by_rows(whole(gathered[2]))
        w_ple_f, w_pg_f = by_cols(whole(gathered[3])), by_rows(whole(gathered[4]))
        w_in_g = None if last else gathered[5]
        gm = _gmlp_fwd(zgf, gm_w_s[i], bias_full, gain_v, mean_mat, "gmlp_fwd")
        mixed = _mixnorm_fwd(att, gm, row(mix_out_norm, i), "mixnorm_fwd")
        o, h1, hn2 = _resid_norm(h, _Prod((mixed, w_out_f, "nn")), row(mix_post_norm, i), row(ffn_pre_norm, i),
                                 "out_resid")
        s, ab = _ffn_in(hn2, w_fi_f, "ffn_in")
        o2, h2, hr = _resid_norm(h1, _Prod((s, w_fo_f, "nn")), row(ffn_post_norm, i), None, "ffn_out_resid")
        g_next = row(mix_pre_norm, i + 1) if i + 1 < depth else row(mix_pre_norm, 0)
        pe, gl, h3, hn_next = _ple_fwd(h2, _Prod((p[i, 0], w_ple_f, "nn")), _Prod((hr, w_pg_f, "nn")),
                                       row(ple_norm, i), g_next, "ple_fwd")
        saved.append(dict(h=h, hn=hn, qa=qa, ka=ka, va=va, zgf=zgf, att=att, lse=lse, gm=gm, mixed=mixed,
                          o=o, h1=h1, hn2=hn2, ab=ab, s=s, o2=o2, h2=h2, hr=hr, pe=pe, gl=gl, w_qkv=w_qkv, w_gf=w_gf,
                          w_out=w_out_f, w_fi=w_fi_f, w_fo=w_fo_f, w_pg=w_pg_f, b_pad=b_pad, bias_full=bias_full,
                          gain_v=gain_v))
        h, hn = h3, hn_next

    dh, loss_blk = _loss_head(h, target, "loss_head")
    loss = lax.psum(loss_blk[0, 0], ("x", "y", "c"))

    small = {nme: [None] * depth for nme in SMALL_ORDER}
    big_grads = {nme: [None] * depth for nme in big}
    waiting = []
    for i in reversed(range(depth)):
        sv = saved[i]
        dgl, dpe, small["ple_norm"][i] = _ple_bwd(dh, sv["pe"], sv["gl"], row(ple_norm, i), "ple_bwd")
        g_pg = _mm(sv["hr"], dgl, "tn", F32, "mm_dw_ple_gate").reshape(N_CHIPS, -1, d)
        g_ple = _mm(p[i, 0], dpe, "tn", F32, "mm_dw_ple", chip_split=True)
        dh2, do2, small["ffn_post_norm"][i] = _join(dh, sv["h2"], None, _Prod((dgl, sv["w_pg"], "nt")), sv["o2"],
                                                    row(ffn_post_norm, i), "join_ple")
        dab = _ffn_bwd(do2, sv["w_fo"], sv["ab"], "ffn_bwd")
        g_fo = _mm(sv["s"], do2, "tn", F32, "mm_dw_ffn_out", tm=256).reshape(N_CHIPS, -1, d)
        g_fi = _mm_halves_tn(sv["hn2"], dab, "mm_dw_ffn_in")
        dh1, do, small["ffn_pre_norm"][i], small["mix_post_norm"][i] = _join(
            dh2, sv["h1"], row(ffn_pre_norm, i), _Prod((dab, sv["w_fi"], "nt", 0), (dab, sv["w_fi"], "nt", 1)), sv["o"],
            row(mix_post_norm, i), "join_ffn", tm=256)
        g_out = _mm(sv["mixed"], do, "tn", F32, "mm_dw_out").reshape(N_CHIPS, -1, d)
        datt, dgm, small["mix_out_norm"][i] = _mixnorm_bwd(sv["att"], sv["gm"], row(mix_out_norm, i),
                                                          _Prod((do, sv["w_out"], "nt")), "mixnorm_bwd")
        batch = waiting + [("w_out", i, g_out), ("w_ffn_in", i, g_fi), ("w_ffn_out", i, g_fo), ("w_ple", i, g_ple),
                           ("w_ple_gate", i, g_pg)]
        tag = "layer" if waiting else "top"
        siblings = _rs_siblings([g for _, _, g in batch])
        (dg, small["gm_w_s"][i], dmix_sum, small["gm_v_norm"][i]), recv = _gmlp_bwd(
            sv["zgf"], dgm, gm_w_s[i], sv["bias_full"], sv["gain_v"], mean_mat, "gmlp_bwd_" + tag, siblings)
        small["gm_b_s"][i] = dmix_sum.reshape(CHUNK, N_HEADS, HEAD_DIM).sum(-1).T
        qb, doa = _attn_bwd_prep(sv["qa"], sv["att"], sv["lse"], datt, "attn_bwd_prep")
        scatter = _rs_chips(siblings, recv, c_idx, BF16, tag)
        (dqa, dka, dva), parts = _attn_bwd(qb, sv["ka"], sv["va"], doa, "attn_bwd_" + tag, scatter)
        for (nme, layer, _), g in zip(batch, _rs_end(scatter, parts, place, tag)):
            big_grads[nme][layer] = g
        dqkv, dc = _attn_bwd_post(dqa, dka, dva, "attn_bwd_post")
        dfl, db = _gate_bwd(dc, sv["zgf"], sv["b_pad"], "gate_bwd")
        small["b_forget"][i] = db[0, :N_HEADS]
        dgf = jnp.concatenate([dg, dfl], axis=1)
        g_qkv = _mm(sv["hn"], dqkv, "tn", F32, "mm_dw_qkv")
        g_gf = _mm(sv["hn"], dgf, "tn", F32, "mm_dw_gf", tn_cap=384)
        g_in = _chip_columns([g_qkv, g_gf[:, 2 * D_GM:2 * D_GM + N_HEADS], g_gf[:, :2 * D_GM]], w_in.shape[2])
        dh, small["mix_pre_norm"][i] = _join(dh1, sv["h"], row(mix_pre_norm, i),
                                             _Prod((dqkv, sv["w_qkv"], "nt"), (dgf, sv["w_gf"], "nt")), None, None, "join_mix")
        waiting = [("w_in", i, g_in)]
    siblings = _rs_siblings([waiting[0][2]])
    scatter = _rs_chips(siblings, siblings.run("rs_sibling_tail"), c_idx, BF16, "tail")
    big_grads["w_in"][0] = _rs_end(scatter, scatter.run("rs_chips_tail"), place, "tail")[0]
    grad_x = dh.reshape(1, t, d)

    small_shapes = {nme: weights[nme].shape for nme in SMALL_ORDER}
    small_part = _pack_small({nme: jnp.stack([g.reshape(small_shapes[nme][1:]) for g in small[nme]])
                              for nme in SMALL_ORDER})
    rows_small = small_part.shape[0]
    siblings = _rs_siblings([small_part.reshape(N_CHIPS, rows_small // N_CHIPS, LANES)])
    scatter = _rs_chips(siblings, siblings.run("rs_sibling_small"), c_idx, F32, "small")
    small_slice = _rs_end(scatter, scatter.run("rs_chips_small"), place, "small")[0]
    small_all = _gather_chips([small_slice], "gather_small")[0].reshape(1, rows_small, LANES)
    sd, sm, sv_ = _adamw(_pack_small({n_: weights[n_] for n_ in SMALL_ORDER})[None], small_all,
                         _pack_small({n_: mom_m[n_] for n_ in SMALL_ORDER})[None],
                         _pack_small({n_: mom_v[n_] for n_ in SMALL_ORDER})[None], "adamw_small")
    grads = _unpack_small(small_all[0], small_shapes)
    deltas = _unpack_small(sd[0], small_shapes)
    new_m = _unpack_small(sm[0], small_shapes)
    new_v = _unpack_small(sv_[0], small_shapes)

    for nme in big:
        g = jnp.stack(big_grads[nme]).reshape(weights[nme].shape)
        grads[nme] = g
        deltas[nme], new_m[nme], new_v[nme] = _adamw(weights[nme], g, mom_m[nme], mom_v[nme], "adamw_" + nme)

    order = ("mix_pre_norm", "mix_post_norm", "w_in", "b_forget", "gm_v_norm", "gm_w_s", "gm_b_s", "mix_out_norm",
             "w_out", "ffn_pre_norm", "ffn_post_norm", "w_ffn_in", "w_ffn_out", "w_ple", "ple_norm", "w_ple_gate")
    return (loss, grad_x, *[grads[n_] for n_ in order], *[deltas[n_] for n_ in order], *[new_m[n_] for n_ in order],
            *[new_v[n_] for n_ in order])
```

```python
import functools

import jax
import jax.numpy as jnp
from jax import lax
from jax.experimental import pallas as pl
from jax.experimental.pallas import tpu as pltpu

F32 = jnp.float32
BF16 = jnp.bfloat16
MESH = pl.DeviceIdType.MESH
HIGHEST = lax.Precision.HIGHEST

EPS = 1e-6
NEG_INF = -1e30
N_HEADS = 8
HEAD_DIM = 64
D_ATT = N_HEADS * HEAD_DIM
D_GM = 512
CHUNK = 128
LANES = 128
N_CHIPS = 4
ADAM_LR = 0.001
ADAM_B1 = 0.9
ADAM_B2 = 0.999
ADAM_EPS = 1e-08
ADAM_WD = 0.01
ADAM_STEP = 10
VMEM_LIMIT = 56 * 1024 * 1024


def _cp(sem=None):
    return pltpu.CompilerParams(dimension_semantics=sem, vmem_limit_bytes=VMEM_LIMIT)


def _full(shape):
    return pl.BlockSpec(shape, lambda *_: (0,) * len(shape))


def _rows(tm, width, col_block=0):
    return pl.BlockSpec((tm, width), lambda i: (i, col_block))


def _dot(a, b, dims, precision=None):
    return lax.dot_general(a, b, (dims, ((), ())), preferred_element_type=F32, precision=precision)


NN = ((1,), (0,))
NT = ((1,), (1,))
TN = ((0,), (0,))


def _pick(n, cap):
    best = None
    for t in range(LANES, min(n, cap) + 1, LANES):
        if n % t == 0:
            best = t
    assert best is not None, (n, cap)
    return best


def _mm(a, b, mode, out_dtype, name, tm=None, tn_cap=1024, chip_split=False):
    dims = {"nn": NN, "nt": NT, "tn": TN}[mode]
    if mode == "tn":
        k, m = a.shape
    else:
        m, k = a.shape
    n = b.shape[0] if mode == "nt" else b.shape[1]
    if tm is None:
        tm = 512 if mode == "tn" else 1024
    tm = min(tm, m)
    tn = n // N_CHIPS if chip_split else _pick(n, tn_cap)
    assert m % tm == 0 and n % tn == 0

    def body(a_ref, b_ref, o_ref):
        o_ref[...] = _dot(a_ref[...].astype(BF16), b_ref[...].astype(BF16), dims).astype(out_dtype)

    a_spec = pl.BlockSpec((k, tm), lambda i, j: (0, i)) if mode == "tn" else pl.BlockSpec((tm, k), lambda i, j: (i, 0))
    b_spec = pl.BlockSpec((tn, k), lambda i, j: (j, 0)) if mode == "nt" else pl.BlockSpec((k, tn), lambda i, j: (0, j))
    if chip_split:
        out_shape = jax.ShapeDtypeStruct((N_CHIPS, m, tn), out_dtype)
        out_spec = pl.BlockSpec((None, tm, tn), lambda i, j: (j, i, 0))
    else:
        out_shape = jax.ShapeDtypeStruct((m, n), out_dtype)
        out_spec = pl.BlockSpec((tm, tn), lambda i, j: (i, j))
    return pl.pallas_call(body, out_shape=out_shape, grid=(m // tm, n // tn), in_specs=[a_spec, b_spec],
                          out_specs=out_spec, compiler_params=_cp(("arbitrary", "arbitrary")), name=name)(a, b)


def _rms_inv(x):
    return lax.rsqrt(jnp.mean(x * x, axis=-1, keepdims=True) + EPS)


def _rms_bwd(x, gain, dy):
    inv = _rms_inv(x)
    xhat = x * inv
    dxn = dy if gain is None else dy * gain
    dx = inv * (dxn - xhat * jnp.mean(dxn * xhat, axis=-1, keepdims=True))
    return dx, dy * xhat


def _acc_rows(ref, val):
    s = jnp.sum(val, axis=0, keepdims=True)

    @pl.when(pl.program_id(0) == 0)
    def _():
        ref[...] = s

    @pl.when(pl.program_id(0) > 0)
    def _():
        ref[...] += s


class _Prod:
    def __init__(self, *terms):
        self.terms = terms


def _rowwise(fn, name, rows, consts, outs, accs=(), tm=512):
    arrays, specs, loaders = [], [], []
    t = next(x for x in rows if not isinstance(x, _Prod)).shape[0]
    tm = min(tm, t)
    for x in rows:
        pos = len(arrays)
        if isinstance(x, _Prod):
            dims = []
            for term in x.terms:
                a, b, mode = term[:3]
                if len(term) == 4:
                    f = a.shape[2]
                    specs += [pl.BlockSpec((None, tm, f), lambda i, half=term[3]: (half, i, 0)),
                              pl.BlockSpec((b.shape[0], f), lambda i, half=term[3]: (0, half))]
                else:
                    specs += [_rows(tm, a.shape[1]), _full(b.shape)]
                arrays += [a, b]
                dims.append(NT if mode == "nt" else NN)

            def load(refs, pos=pos, dims=dims):
                total = None
                for k, dm in enumerate(dims):
                    part = _dot(refs[pos + 2 * k][...].astype(BF16), refs[pos + 2 * k + 1][...], dm)
                    total = part if total is None else total + part
                return total
        else:
            arrays.append(x)
            specs.append(_rows(tm, x.shape[1]))

            def load(refs, pos=pos):
                return refs[pos][...]
        loaders.append(load)
    for cst in consts:
        loaders.append(lambda refs, pos=len(arrays): refs[pos][...])
        arrays.append(cst)
        specs.append(_full(cst.shape))
    n_in = len(arrays)

    def body(*refs):
        res = fn(*[ld(refs) for ld in loaders])
        out_refs = refs[n_in:]
        for k, (_, dtype) in enumerate(outs):
            out_refs[k][...] = res[k].astype(dtype)
        for k in range(len(accs)):
            _acc_rows(out_refs[len(outs) + k], res[len(outs) + k])

    out_shape = [jax.ShapeDtypeStruct((t, c), dtype) for c, dtype in outs] + [jax.ShapeDtypeStruct((1, c), F32) for c in accs]
    out_specs = [_rows(tm, c) for c, _ in outs] + [_full((1, c)) for c in accs]
    return pl.pallas_call(body, out_shape=tuple(out_shape), grid=(t // tm,), in_specs=specs, out_specs=tuple(out_specs),
                          compiler_params=_cp(("arbitrary",)), name=name)(*arrays)


def _norm_cast(h, gain, name):
    return _rowwise(lambda x, g: (x * _rms_inv(x) * g,), name, [h], [gain], [(h.shape[1], BF16)])[0]


def _resid_norm(h, o, g_post, g_next, name):
    d = h.shape[1]

    def fn(hv, ov, gp, *gn):
        h1 = hv + ov * _rms_inv(ov) * gp
        hn = h1 * _rms_inv(h1)
        return ov, h1, hn * gn[0] if gn else hn

    return _rowwise(fn, name, [h, o], [g_post] + ([] if g_next is None else [g_next]), [(d, F32), (d, F32), (d, BF16)])


def _ple_fwd(h2, pe, gl, g_ple, g_next, name):
    d = h2.shape[1]

    def fn(hv, pv, gv, gp, gn):
        h3 = hv + jax.nn.sigmoid(gv) * (pv * _rms_inv(pv) * gp)
        return pv, gv, h3, h3 * _rms_inv(h3) * gn

    return _rowwise(fn, name, [h2, pe, gl], [g_ple, g_next], [(d, F32)] * 3 + [(d, BF16)])


def _loss_head(y, target, name, tm=512):
    t, d = y.shape

    def body(y_ref, t_ref, dy_ref, loss_ref):
        diff = y_ref[...] - t_ref[...]
        dy_ref[...] = diff * (1.0 / d)
        part = 0.5 * jnp.sum(jnp.mean(diff * diff, axis=-1, keepdims=True), axis=0, keepdims=True)
        part = jnp.broadcast_to(part, (8, LANES))

        @pl.when(pl.program_id(0) == 0)
        def _():
            loss_ref[...] = part

        @pl.when(pl.program_id(0) > 0)
        def _():
            loss_ref[...] += part

    return pl.pallas_call(body, out_shape=(jax.ShapeDtypeStruct((t, d), F32), jax.ShapeDtypeStruct((8, LANES), F32)),
                          grid=(t // tm,), in_specs=[_rows(tm, d)] * 2, out_specs=(_rows(tm, d), _full((8, LANES))),
                          compiler_params=_cp(("arbitrary",)), name=name)(y, target)


def _ple_bwd(dh3, pe, gl, g_ple, name):
    d = dh3.shape[1]

    def fn(dh, pv, gv, gp):
        gate = jax.nn.sigmoid(gv)
        dpe, dg_rows = _rms_bwd(pv, gp, dh * gate)
        return dh * (pv * _rms_inv(pv) * gp) * gate * (1.0 - gate), dpe, dg_rows

    return _rowwise(fn, name, [dh3, pe, gl], [g_ple], [(d, BF16), (d, BF16)], [d])


def _join(d_res, x_a, gain_a, d_a, x_b, gain_b, name, tm=512):
    d = d_res.shape[1]
    has_ga = gain_a is not None
    has_b = x_b is not None

    def fn(*vals):
        it = iter(vals)
        dres, xa, da = next(it), next(it), next(it)
        xb = next(it) if has_b else None
        ga = next(it) if has_ga else None
        dx, dga_rows = _rms_bwd(xa, ga, da)
        dout = dres + dx
        if not has_b:
            return (dout, dga_rows) if has_ga else (dout,)
        db, dgb_rows = _rms_bwd(xb, next(it), dout)
        return (dout, db, dga_rows, dgb_rows) if has_ga else (dout, db, dgb_rows)

    rows = [d_res, x_a, d_a] + ([x_b] if has_b else [])
    consts = ([gain_a] if has_ga else []) + ([gain_b] if has_b else [])
    return _rowwise(fn, name, rows, consts, [(d, F32)] + ([(d, BF16)] if has_b else []), [d] * (has_ga + has_b), tm)


def _ffn_in(hn, w, name, tm=2048, tn=256):
    t, d = hn.shape
    f = w.shape[1] // 2
    tm = min(tm, t)
    nj = f // tn

    def body(h_ref, wa_ref, wb_ref, s_ref, ab_ref):
        h = h_ref[...]
        a = _dot(h, wa_ref[...], NN)
        b = _dot(h, wb_ref[...], NN)
        ab_ref[0] = a.astype(BF16)
        ab_ref[1] = b.astype(BF16)
        s_ref[...] = (a * jax.nn.sigmoid(a) * b).astype(BF16)

    return pl.pallas_call(
        body, out_shape=(jax.ShapeDtypeStruct((t, f), BF16), jax.ShapeDtypeStruct((2, t, f), BF16)), grid=(t // tm, nj),
        in_specs=[pl.BlockSpec((tm, d), lambda i, j: (i, 0)), pl.BlockSpec((d, tn), lambda i, j: (0, j)),
                  pl.BlockSpec((d, tn), lambda i, j: (0, nj + j))],
        out_specs=(pl.BlockSpec((tm, tn), lambda i, j: (i, j)), pl.BlockSpec((2, tm, tn), lambda i, j: (0, i, j))),
        compiler_params=_cp(("arbitrary",) * 2), name=name)(hn, w, w)


def _ffn_bwd(do, w_out, ab, name, tm=2048, tn=256):
    t, d = do.shape
    f = w_out.shape[0]
    tm = min(tm, t)

    def body(d_ref, w_ref, ab_ref, o_ref):
        ds = _dot(d_ref[...], w_ref[...], NT)
        a = ab_ref[0].astype(F32)
        sg = jax.nn.sigmoid(a)
        silu = a * sg
        o_ref[0] = (ds * ab_ref[1].astype(F32) * (sg + silu * (1.0 - sg))).astype(BF16)
        o_ref[1] = (ds * silu).astype(BF16)

    both = pl.BlockSpec((2, tm, tn), lambda i, j: (0, i, j))
    return pl.pallas_call(
        body, out_shape=jax.ShapeDtypeStruct((2, t, f), BF16), grid=(t // tm, f // tn),
        in_specs=[pl.BlockSpec((tm, d), lambda i, j: (i, 0)), pl.BlockSpec((tn, d), lambda i, j: (j, 0)), both],
        out_specs=both, compiler_params=_cp(("arbitrary",) * 2), name=name)(do, w_out, ab)


def _mm_halves_tn(h, x2, name, tm=512):
    t, d = h.shape
    f = x2.shape[2]
    tn = 2 * f // N_CHIPS

    def body(h_ref, x_ref, o_ref):
        o_ref[...] = _dot(h_ref[...], x_ref[...], TN)

    return pl.pallas_call(
        body, out_shape=jax.ShapeDtypeStruct((N_CHIPS, d, tn), F32), grid=(N_CHIPS, d // tm),
        in_specs=[pl.BlockSpec((t, tm), lambda j, i: (0, i)), pl.BlockSpec((None, t, tn), lambda j, i: (j // 2, 0, j % 2))],
        out_specs=pl.BlockSpec((None, tm, tn), lambda j, i: (j, i, 0)),
        compiler_params=_cp(("arbitrary",) * 2), name=name)(h, x2)


def _mixnorm_fwd(att, gm, g_out, name, tm=512):
    t, w = att.shape

    def body(a_ref, m_ref, g_ref, o_ref):
        a, m, g = a_ref[...], m_ref[...], g_ref[...]
        o_ref[:, :w] = (a * _rms_inv(a) * g[:, :w]).astype(BF16)
        o_ref[:, w:] = (m * _rms_inv(m) * g[:, w:]).astype(BF16)

    return pl.pallas_call(body, out_shape=jax.ShapeDtypeStruct((t, 2 * w), BF16), grid=(t // tm,),
                          in_specs=[_rows(tm, w), _rows(tm, w), _full((1, 2 * w))], out_specs=_rows(tm, 2 * w),
                          compiler_params=_cp(("arbitrary",)), name=name)(att, gm, g_out)


def _mixnorm_bwd(att, gm, g_out, dmixed, name):
    w = att.shape[1]

    def fn(a, m, d, g):
        da, dga = _rms_bwd(a, g[:, :w], d[:, :w])
        dm, dgm = _rms_bwd(m, g[:, w:], d[:, w:])
        return da, dm, jnp.concatenate([dga, dgm], axis=1)

    return _rowwise(fn, name, [att, gm, dmixed], [g_out], [(w, F32), (w, F32)], [2 * w])


SCAN_BLOCK = 256


def _gate_fwd(zgf, b_pad, name):
    t = zgf.shape[0]
    fcol = zgf.shape[1] // LANES - 1
    nb = t // SCAN_BLOCK

    def body(f_ref, b_ref, c_ref):
        r = lax.broadcasted_iota(jnp.int32, (SCAN_BLOCK, SCAN_BLOCK), 0)
        s = lax.broadcasted_iota(jnp.int32, (SCAN_BLOCK, SCAN_BLOCK), 1)
        tril = (r >= s).astype(F32)
        head = lax.broadcasted_iota(jnp.int32, (SCAN_BLOCK, LANES), 1) < N_HEADS
        carry = jnp.zeros((1, LANES), F32)
        for blk in range(nb):
            rows = pl.ds(blk * SCAN_BLOCK, SCAN_BLOCK)
            x = f_ref[rows, :] + b_ref[...]
            lf = jnp.minimum(x, 0.0) - jnp.log1p(jnp.exp(-jnp.abs(x)))
            lf = jnp.where(head, lf, 0.0)
            cs = _dot(tril, lf, NN, HIGHEST) + carry
            c_ref[rows, :] = cs
            carry = carry + jnp.sum(lf, axis=0, keepdims=True)

    return pl.pallas_call(body, out_shape=jax.ShapeDtypeStruct((t, LANES), F32),
                          grid=(1,), in_specs=[pl.BlockSpec((t, LANES), lambda i: (0, fcol)), _full((1, LANES))],
                          out_specs=_full((t, LANES)), compiler_params=_cp(("arbitrary",)),
                          name=name)(zgf, b_pad)


def _gate_bwd(dc, zgf, b_pad, name):
    t = zgf.shape[0]
    fcol = zgf.shape[1] // LANES - 1
    nb = t // SCAN_BLOCK

    def body(dc_ref, f_ref, b_ref, dfl_ref, db_ref):
        r = lax.broadcasted_iota(jnp.int32, (SCAN_BLOCK, SCAN_BLOCK), 0)
        s = lax.broadcasted_iota(jnp.int32, (SCAN_BLOCK, SCAN_BLOCK), 1)
        triu = (s >= r).astype(F32)
        head = lax.broadcasted_iota(jnp.int32, (SCAN_BLOCK, LANES), 1) < N_HEADS
        carry = jnp.zeros((1, LANES), F32)
        db = jnp.zeros((1, LANES), F32)
        for blk in reversed(range(nb)):
            rows = pl.ds(blk * SCAN_BLOCK, SCAN_BLOCK)
            dc = dc_ref[rows, :]
            dlf = _dot(triu, dc, NN, HIGHEST) + carry
            x = f_ref[rows, :] + b_ref[...]
            dfl = jnp.where(head, dlf * jax.nn.sigmoid(-x), 0.0)
            dfl_ref[rows, :] = dfl.astype(BF16)
            db = db + jnp.sum(dfl, axis=0, keepdims=True)
            carry = carry + jnp.sum(dc, axis=0, keepdims=True)
        db_ref[...] = db

    return pl.pallas_call(body, out_shape=(jax.ShapeDtypeStruct((t, LANES), BF16), jax.ShapeDtypeStruct((1, LANES), F32)),
                          grid=(1,), in_specs=[_full((t, LANES)), pl.BlockSpec((t, LANES), lambda i: (0, fcol)),
                                               _full((1, LANES))],
                          out_specs=(_full((t, LANES)), _full((1, LANES))), compiler_params=_cp(("arbitrary",)),
                          name=name)(dc, zgf, b_pad)


ATT_BLOCK = 512
PAIRS = N_HEADS // 2
CQ_LANE = HEAD_DIM
CK_LANE = HEAD_DIM + 3
LSE_LANE = HEAD_DIM + 6


def _pick_col(x, idx):
    lane = lax.broadcasted_iota(jnp.int32, x.shape, 1)
    return jnp.sum(jnp.where(lane == idx, x, 0.0), axis=1, keepdims=True)


def _split3(x):
    hi = x.astype(BF16)
    r1 = x - hi.astype(F32)
    mid = r1.astype(BF16)
    lo = (r1 - mid.astype(F32)).astype(BF16)
    return hi, mid, lo


def _lanes_put(base, lane, start, vals):
    out = base
    for n, v in enumerate(vals):
        out = jnp.where(lane == start + n, v, out)
    return out


def _to_first_half(x, hh):
    return x if hh == 0 else pltpu.roll(x, HEAD_DIM, 1)


def _attn_prep(qkv, c, name, tm=512):
    t = qkv.shape[0]
    tm = min(tm, t)

    def body(q_ref, k_ref, v_ref, c_ref, qa_ref, ka_ref, va_ref):
        j = pl.program_id(1)
        lane = lax.broadcasted_iota(jnp.int32, (tm, LANES), 1)
        first = lane < HEAD_DIM
        q2, k2, v2 = q_ref[...].astype(F32), k_ref[...].astype(F32), v_ref[...].astype(F32)
        cc = c_ref[...]
        one = jnp.ones((tm, 1), F32)
        for hh in range(2):
            chi, cmid, clo = [v.astype(F32) for v in _split3(_pick_col(cc, 2 * j + hh))]
            qh = jnp.where(first, _to_first_half(q2, hh) * (HEAD_DIM ** -0.5), 0.0)
            kh = jnp.where(first, _to_first_half(k2, hh), 0.0)
            vh = jnp.where(first, _to_first_half(v2, hh), 0.0)
            qa = _lanes_put(qh, lane, CQ_LANE, [chi, cmid, clo, one, one, one])
            ka = _lanes_put(kh, lane, CQ_LANE, [one, one, one, -chi, -cmid, -clo, one, one, one])
            va = _lanes_put(vh, lane, CQ_LANE, [one, one, one])
            cols = slice(hh * LANES, (hh + 1) * LANES)
            qa_ref[:, cols] = qa.astype(BF16)
            ka_ref[:, cols] = ka.astype(BF16)
            va_ref[:, cols] = va.astype(BF16)

    blk = lambda off: pl.BlockSpec((tm, LANES), lambda i, j: (i, off + j))
    out = pl.BlockSpec((tm, 2 * LANES), lambda i, j: (i, j))
    shp = jax.ShapeDtypeStruct((t, N_HEADS * LANES), BF16)
    return pl.pallas_call(body, out_shape=(shp, shp, shp), grid=(t // tm, PAIRS),
                          in_specs=[blk(0), blk(PAIRS), blk(2 * PAIRS), pl.BlockSpec((tm, LANES), lambda i, j: (i, 0))],
                          out_specs=(out, out, out), compiler_params=_cp(("arbitrary",) * 2), name=name)(qkv, qkv, qkv, c)


def _causal_block(tb):
    return lax.broadcasted_iota(jnp.int32, (tb, tb), 0) >= lax.broadcasted_iota(jnp.int32, (tb, tb), 1)


def _causal_pairs(nb, key_major):
    pairs = [(q, k) for q in range(nb) for k in range(q + 1)]
    if key_major:
        pairs.sort(key=lambda qk: (qk[1], qk[0]))
    return (jnp.array([q for q, _ in pairs], jnp.int32), jnp.array([k for _, k in pairs], jnp.int32))


def _host(side, body, n_lead, arrays, in_specs, out_shapes, out_specs, scratch, grid):
    if side is None:
        return body, tuple(arrays), list(in_specs), list(out_shapes), list(out_specs), list(scratch)
    n_in, n_out, s_in, s_out = len(arrays), len(out_shapes), len(side.arrays), len(side.out_shapes)

    def hosted(*refs):
        ins_end = n_lead + n_in + s_in
        side_in, side_out = refs[n_lead + n_in:ins_end], refs[ins_end + n_out:ins_end + n_out + s_out]
        send, recv = refs[-2:]
        ids = [pl.program_id(ax) for ax in range(len(grid))]
        first, last = ids[0] == 0, ids[0] == grid[0] - 1
        for ax in range(1, len(grid)):
            first, last = first & (ids[ax] == 0), last & (ids[ax] == grid[ax] - 1)

        @pl.when(first)
        def _():
            side.start(side_in, side_out, send, recv)

        body(*refs[:n_lead + n_in], *refs[ins_end:ins_end + n_out], *refs[ins_end + n_out + s_out:-2])

        @pl.when(last)
        def _():
            side.wait(side_in, side_out, send, recv)

    return (hosted, tuple(arrays) + tuple(side.arrays), list(in_specs) + [HBM_SPEC] * s_in,
            list(out_shapes) + side.out_shapes, list(out_specs) + [HBM_SPEC] * s_out, list(scratch) + side.scratch)


def _pair_grid_call(body, tables, arrays, in_specs, out_shapes, out_specs, scratch, side, name):
    grid = (PAIRS, tables[0].shape[0])
    n_out = len(out_shapes)
    body, arrays, in_specs, out_shapes, out_specs, scratch = _host(side, body, 2, arrays, in_specs, out_shapes,
                                                                    out_specs, scratch, grid)
    grid_spec = pltpu.PrefetchScalarGridSpec(num_scalar_prefetch=2, grid=grid, in_specs=in_specs,
                                             out_specs=tuple(out_specs), scratch_shapes=scratch)
    out = pl.pallas_call(body, out_shape=tuple(out_shapes), grid_spec=grid_spec,
                         compiler_params=_cp(("arbitrary",) * 2), name=name)(*tables, *arrays)
    return out[:n_out], out[n_out:]


def _attn_fwd(qa, ka, va, name, side=None):
    t = qa.shape[0]
    tb = min(ATT_BLOCK, t)
    q_tab, k_tab = _causal_pairs(t // tb, key_major=False)

    def body(q_tab_ref, k_tab_ref, q_ref, k_ref, v_ref, o_ref, lse_ref, m0, m1, acc0, acc1):
        qi, kb = q_tab_ref[pl.program_id(1)], k_tab_ref[pl.program_id(1)]
        m_s, acc_s = (m0, m1), (acc0, acc1)

        @pl.when(kb == 0)
        def _():
            for hh in range(2):
                m_s[hh][...] = jnp.full(m_s[hh].shape, NEG_INF, F32)
                acc_s[hh][...] = jnp.zeros(acc_s[hh].shape, F32)

        def step(diagonal):
            for hh in range(2):
                cols = slice(hh * LANES, (hh + 1) * LANES)
                sc = _dot(q_ref[:, cols], k_ref[:, cols], NT)
                if diagonal:
                    sc = jnp.where(_causal_block(tb), sc, NEG_INF)
                m_prev = m_s[hh][...]
                m_new = jnp.maximum(m_prev, jnp.max(sc, axis=1, keepdims=True))
                p = jnp.exp(sc - m_new)
                acc_s[hh][...] = jnp.exp(m_prev - m_new) * acc_s[hh][...] + _dot(p.astype(BF16), v_ref[:, cols], NN)
                m_s[hh][...] = m_new

        @pl.when(kb < qi)
        def _():
            step(False)

        @pl.when(kb == qi)
        def _():
            step(True)
            lane = lax.broadcasted_iota(jnp.int32, (tb, LANES), 1)
            outs, lses = [], []
            for hh in range(2):
                acc = acc_s[hh][...]
                l = _pick_col(acc, CQ_LANE)
                outs.append(acc / l)
                lses.append(m_s[hh][...] + jnp.log(l))
            o_ref[...] = jnp.where(lane < HEAD_DIM, outs[0], pltpu.roll(outs[1], HEAD_DIM, 1))
            lse_ref[...] = jnp.where(lane == 0, lses[0], jnp.where(lane == 1, lses[1], 0.0))

    qrow = lambda j, s, qt, kt: (qt[s], j)
    krow = lambda j, s, qt, kt: (kt[s], j)
    return _pair_grid_call(
        body, (q_tab, k_tab), (qa, ka, va),
        [pl.BlockSpec((tb, 2 * LANES), qrow), pl.BlockSpec((tb, 2 * LANES), krow), pl.BlockSpec((tb, 2 * LANES), krow)],
        [jax.ShapeDtypeStruct((t, D_ATT), F32), jax.ShapeDtypeStruct((t, PAIRS * LANES), F32)],
        [pl.BlockSpec((tb, LANES), qrow), pl.BlockSpec((tb, LANES), qrow)],
        [pltpu.VMEM((tb, 1), F32)] * 2 + [pltpu.VMEM((tb, LANES), F32)] * 2, side, name)


def _attn_bwd_prep(qa, att, lse, datt, name, tm=512):
    t = qa.shape[0]
    tm = min(tm, t)

    def body(qa_ref, o_ref, lse_ref, do_ref, qb_ref, doa_ref):
        j = pl.program_id(1)
        lane = lax.broadcasted_iota(jnp.int32, (tm, LANES), 1)
        first = lane < HEAD_DIM
        do = do_ref[...]
        prod = do * o_ref[...]
        lse2 = lse_ref[...]
        for hh in range(2):
            cols = slice(hh * LANES, (hh + 1) * LANES)
            delta = jnp.sum(jnp.where(first if hh == 0 else ~first, prod, 0.0), axis=1, keepdims=True)
            doh = jnp.where(first, _to_first_half(do, hh), 0.0)
            doa_ref[:, cols] = _lanes_put(doh, lane, CQ_LANE, [v.astype(F32) for v in _split3(-delta)]).astype(BF16)
            nl = [v.astype(F32) for v in _split3(-_pick_col(lse2, hh))]
            qb_ref[:, cols] = _lanes_put(qa_ref[:, cols].astype(F32), lane, LSE_LANE, nl).astype(BF16)

    wide = pl.BlockSpec((tm, 2 * LANES), lambda i, j: (i, j))
    pair = pl.BlockSpec((tm, LANES), lambda i, j: (i, j))
    shp = jax.ShapeDtypeStruct((t, N_HEADS * LANES), BF16)
    return pl.pallas_call(body, out_shape=(shp, shp), grid=(t // tm, PAIRS), in_specs=[wide, pair, pair, pair],
                          out_specs=(wide, wide), compiler_params=_cp(("arbitrary",) * 2), name=name)(qa, att, lse, datt)


def _attn_bwd(qb, ka, va, doa, name, side=None):
    t = qb.shape[0]
    tb = min(ATT_BLOCK, t)
    nb = t // tb
    q_tab, k_tab = _causal_pairs(nb, key_major=True)

    def body(q_tab_ref, k_tab_ref, q_ref, k_ref, v_ref, do_ref, dq_ref, dk_ref, dv_ref, dk0, dk1, dv0, dv1):
        qi, kb = q_tab_ref[pl.program_id(1)], k_tab_ref[pl.program_id(1)]
        dk_s, dv_s = (dk0, dk1), (dv0, dv1)

        @pl.when(qi == kb)
        def _():
            for ref in dk_s + dv_s:
                ref[...] = jnp.zeros(ref.shape, F32)

        def step(diagonal):
            rows = pl.ds(pl.multiple_of(qi * tb, tb), tb)
            for hh in range(2):
                cols = slice(hh * LANES, (hh + 1) * LANES)
                q, k, do = q_ref[:, cols], k_ref[:, cols], do_ref[:, cols]
                sc = _dot(q, k, NT)
                if diagonal:
                    sc = jnp.where(_causal_block(tb), sc, NEG_INF)
                p = jnp.exp(sc)
                ds = (p * _dot(do, v_ref[:, cols], NT)).astype(BF16)
                dv_s[hh][...] += _dot(p.astype(BF16), do, TN)
                dk_s[hh][...] += _dot(ds, q, TN)
                dq_new = _dot(ds, k, NN)

                @pl.when(kb == 0)
                def _():
                    dq_ref[rows, cols] = dq_new

                @pl.when(kb > 0)
                def _():
                    dq_ref[rows, cols] += dq_new

        @pl.when(qi == kb)
        def _():
            step(True)

        @pl.when(qi > kb)
        def _():
            step(False)

        @pl.when(qi == nb - 1)
        def _():
            for hh in range(2):
                cols = slice(hh * LANES, (hh + 1) * LANES)
                dk_ref[:, cols] = dk_s[hh][...]
                dv_ref[:, cols] = dv_s[hh][...].astype(BF16)

    qrow = lambda j, s, qt, kt: (qt[s], j)
    krow = lambda j, s, qt, kt: (kt[s], j)
    blk = (tb, 2 * LANES)
    wide = (t, N_HEADS * LANES)
    return _pair_grid_call(
        body, (q_tab, k_tab), (qb, ka, va, doa),
        [pl.BlockSpec(blk, qrow), pl.BlockSpec(blk, krow), pl.BlockSpec(blk, krow), pl.BlockSpec(blk, qrow)],
        [jax.ShapeDtypeStruct(wide, F32), jax.ShapeDtypeStruct(wide, F32), jax.ShapeDtypeStruct(wide, BF16)],
        [pl.BlockSpec((t, 2 * LANES), lambda j, s, qt, kt: (0, j)), pl.BlockSpec(blk, krow), pl.BlockSpec(blk, krow)],
        [pltpu.VMEM((tb, LANES), F32)] * 4, side, name)


def _attn_bwd_post(dqa, dka, dva, name, tm=256):
    t = dqa.shape[0]
    tm = min(tm, t)

    def body(dq_ref, dk_ref, dv_ref, o_ref, dc_ref):
        lane = lax.broadcasted_iota(jnp.int32, (tm, LANES), 1)
        first = lane < HEAD_DIM
        dc = jnp.zeros((tm, LANES), F32)
        for j in range(PAIRS):
            packed = []
            for ref, gain in ((dq_ref, HEAD_DIM ** -0.5), (dk_ref, 1.0), (dv_ref, 1.0)):
                even = ref[:, 2 * j * LANES:(2 * j + 1) * LANES].astype(F32)
                odd = ref[:, (2 * j + 1) * LANES:(2 * j + 2) * LANES].astype(F32)
                packed.append((jnp.where(first, even, pltpu.roll(odd, HEAD_DIM, 1)) * gain).astype(BF16))
                if ref is dq_ref:
                    dc = dc + jnp.where(lane == 2 * j, _pick_col(even, CQ_LANE), 0.0)
                    dc = dc + jnp.where(lane == 2 * j + 1, _pick_col(odd, CQ_LANE), 0.0)
                if ref is dk_ref:
                    dc = dc - jnp.where(lane == 2 * j, _pick_col(even, CK_LANE), 0.0)
                    dc = dc - jnp.where(lane == 2 * j + 1, _pick_col(odd, CK_LANE), 0.0)
            for part, val in enumerate(packed):
                o_ref[:, (part * PAIRS + j) * LANES:(part * PAIRS + j + 1) * LANES] = val
        dc_ref[...] = dc

    wide = _rows(tm, N_HEADS * LANES)
    return pl.pallas_call(body, out_shape=(jax.ShapeDtypeStruct((t, 3 * D_ATT), BF16), jax.ShapeDtypeStruct((t, LANES), F32)),
                          grid=(t // tm,), in_specs=[wide, wide, wide], out_specs=(_rows(tm, 3 * D_ATT), _rows(tm, LANES)),
                          compiler_params=_cp(("arbitrary",)), name=name)(dqa, dka, dva)


GELU_K = 0.7978845608028654
GELU_A = 0.044715


def _gelu(x):
    th = jnp.tanh(GELU_K * (x + GELU_A * x * x * x))
    return 0.5 * x * (1.0 + th), th


def _group_mean_matrix():
    r = jnp.arange(D_GM)[:, None] // HEAD_DIM
    s = jnp.arange(D_GM)[None, :] // HEAD_DIM
    return jnp.where(r == s, 1.0 / HEAD_DIM, 0.0).astype(BF16)


def _group_mean(x, mean_mat):
    hi = x.astype(BF16)
    lo = (x - hi.astype(F32)).astype(BF16)
    return _dot(hi, mean_mat, NN) + _dot(lo, mean_mat, NN)


def _gm_forward_parts(g, w_ref, bias, gain, mean_mat):
    gel, _ = _gelu(g)
    u, vv = gel[:, :D_GM], gel[:, D_GM:]
    mu = _group_mean(vv, mean_mat)
    d = vv - mu
    rstd = lax.rsqrt(_group_mean(d * d, mean_mat) + EPS)
    xhat = d * rstd
    vn = (xhat * gain).astype(BF16)
    first = lax.broadcasted_iota(jnp.int32, (CHUNK, LANES), 1) < HEAD_DIM
    tri = lax.broadcasted_iota(jnp.int32, (CHUNK, CHUNK), 0) >= lax.broadcasted_iota(jnp.int32, (CHUNK, CHUNK), 1)
    wm = [jnp.where(tri, w_ref[grp], 0.0).astype(BF16) for grp in range(w_ref.shape[0])]
    chunks = []
    for ck in range(g.shape[0] // CHUNK):
        parts = []
        for jp in range(D_GM // LANES):
            vp = vn[ck * CHUNK:(ck + 1) * CHUNK, jp * LANES:(jp + 1) * LANES]
            parts.append(jnp.where(first, _dot(wm[2 * jp], vp, NN), _dot(wm[2 * jp + 1], vp, NN)))
        chunks.append(jnp.concatenate(parts, axis=1) + bias)
    return u, xhat, rstd, vn, jnp.concatenate(chunks, axis=0), wm


GM_ROWS = 512


def _gmlp_fwd(zgf, w_s, bias_full, gain, mean_mat, name):
    t = zgf.shape[0]
    tm = min(GM_ROWS, t)

    def body(g_ref, w_ref, b_ref, gain_ref, mm_ref, o_ref):
        u, _, _, _, mixed, _ = _gm_forward_parts(g_ref[...], w_ref, b_ref[...], gain_ref[...], mm_ref[...])
        o_ref[...] = u * mixed

    return pl.pallas_call(body, out_shape=jax.ShapeDtypeStruct((t, D_GM), F32), grid=(t // tm,),
                          in_specs=[_rows(tm, 2 * D_GM), _full(w_s.shape), _full((CHUNK, D_GM)), _full((1, D_GM)),
                                    _full((D_GM, D_GM))],
                          out_specs=_rows(tm, D_GM), compiler_params=_cp(("arbitrary",)),
                          name=name)(zgf, w_s, bias_full, gain, mean_mat)


def _gmlp_bwd(zgf, dgm, w_s, bias_full, gain, mean_mat, name, side=None):
    t = zgf.shape[0]
    tm = min(GM_ROWS, t)

    def body(g_ref, d_ref, w_ref, b_ref, gain_ref, mm_ref, dg_ref, dw_ref, dmix_ref, dgain_ref):
        g, gain, mean_mat = g_ref[...], gain_ref[...], mm_ref[...]
        u, xhat, rstd, vn, mixed, wm = _gm_forward_parts(g, w_ref, b_ref[...], gain, mean_mat)
        dgm_v = d_ref[...]
        du = dgm_v * mixed
        dmixed = dgm_v * u
        dm_b = dmixed.astype(BF16)
        first = lax.broadcasted_iota(jnp.int32, (CHUNK, LANES), 1) < HEAD_DIM
        tri = lax.broadcasted_iota(jnp.int32, (CHUNK, CHUNK), 0) >= lax.broadcasted_iota(jnp.int32, (CHUNK, CHUNK), 1)

        @pl.when(pl.program_id(0) == 0)
        def _():
            dw_ref[...] = jnp.zeros(dw_ref.shape, F32)
            dmix_ref[...] = jnp.zeros(dmix_ref.shape, F32)
            dgain_ref[...] = jnp.zeros(dgain_ref.shape, F32)

        dw = [jnp.zeros((CHUNK, CHUNK), F32) for _ in wm]
        dmix = jnp.zeros((CHUNK, D_GM), F32)
        dvn_chunks = []
        for ck in range(tm // CHUNK):
            rows = slice(ck * CHUNK, (ck + 1) * CHUNK)
            dmix = dmix + dmixed[rows]
            dvn_parts = []
            for jp in range(D_GM // LANES):
                vp = vn[rows, jp * LANES:(jp + 1) * LANES]
                dmp = dm_b[rows, jp * LANES:(jp + 1) * LANES]
                halves = []
                for hh in range(2):
                    sel = first if hh == 0 else ~first
                    grp = 2 * jp + hh
                    dw[grp] = dw[grp] + _dot(jnp.where(sel, dmp, jnp.zeros_like(dmp)), vp, NT)
                    halves.append(_dot(wm[grp], dmp, TN))
                dvn_parts.append(jnp.where(first, halves[0], halves[1]))
            dvn_chunks.append(jnp.concatenate(dvn_parts, axis=1))
        dvn = jnp.concatenate(dvn_chunks, axis=0)
        for grp, dwg in enumerate(dw):
            dw_ref[grp] += jnp.where(tri, dwg, 0.0)
        dmix_ref[...] += dmix
        dgain_ref[...] += jnp.sum(dvn * xhat, axis=0, keepdims=True)
        dxhat = dvn * gain
        m1 = _group_mean(dxhat, mean_mat)
        m2 = _group_mean(dxhat * xhat, mean_mat)
        dvv = rstd * (dxhat - m1 - xhat * m2)
        gel, th = _gelu(g)
        dgel = 0.5 * (1.0 + th) + 0.5 * g * (1.0 - th * th) * GELU_K * (1.0 + 3.0 * GELU_A * g * g)
        dg_ref[...] = (jnp.concatenate([du, dvv], axis=1) * dgel).astype(BF16)

    grid = (t // tm,)
    body, arrays, in_specs, out_shapes, out_specs, scratch = _host(
        side, body, 0, (zgf, dgm, w_s, bias_full, gain, mean_mat),
        [_rows(tm, 2 * D_GM), _rows(tm, D_GM), _full(w_s.shape), _full((CHUNK, D_GM)), _full((1, D_GM)),
         _full((D_GM, D_GM))],
        [jax.ShapeDtypeStruct((t, 2 * D_GM), BF16), jax.ShapeDtypeStruct(w_s.shape, F32),
         jax.ShapeDtypeStruct((CHUNK, D_GM), F32), jax.ShapeDtypeStruct((1, D_GM), F32)],
        [_rows(tm, 2 * D_GM), _full(w_s.shape), _full((CHUNK, D_GM)), _full((1, D_GM))], [], grid)
    out = pl.pallas_call(body, out_shape=tuple(out_shapes), grid=grid, in_specs=in_specs, out_specs=tuple(out_specs),
                         scratch_shapes=scratch, compiler_params=_cp(("arbitrary",)), name=name)(*arrays)
    return out[:4], out[4:]


def _row_tile(r, c, budget=1 << 19):
    best = None
    for tr in range(8, r + 1, 8):
        if r % tr == 0 and tr * c <= budget:
            best = tr
    return best if best is not None else r


def _adamw(w, g, m, v, name):
    nl, r, c = w.shape
    tr = _row_tile(r, c, 1 << 18)
    c1 = 1.0 - ADAM_B1 ** ADAM_STEP
    c2 = 1.0 - ADAM_B2 ** ADAM_STEP

    def body(w_ref, g_ref, m_ref, v_ref, d_ref, mo_ref, vo_ref):
        gv = g_ref[...]
        mn = ADAM_B1 * m_ref[...] + (1.0 - ADAM_B1) * gv
        vn = ADAM_B2 * v_ref[...] + (1.0 - ADAM_B2) * jnp.square(gv)
        mo_ref[...] = mn
        vo_ref[...] = vn
        d_ref[...] = -ADAM_LR * ((mn / c1) / (jnp.sqrt(vn / c2) + ADAM_EPS) + ADAM_WD * w_ref[...])

    spec = pl.BlockSpec((None, tr, c), lambda l, i: (l, i, 0))
    shp = jax.ShapeDtypeStruct(w.shape, F32)
    return pl.pallas_call(body, out_shape=(shp, shp, shp), grid=(nl, r // tr), in_specs=[spec] * 4,
                          out_specs=(spec, spec, spec), compiler_params=_cp(("arbitrary",) * 2), name=name)(w, g, m, v)


def _add_sibling(g, recv, c_idx, wire_dtype, name):
    nj, _, h, c = g.shape
    tr = _row_tile(h, c)

    def body(c_ref, g_ref, r_ref, o_ref):
        o_ref[...] = (g_ref[...] + r_ref[...]).astype(wire_dtype)

    grid_spec = pltpu.PrefetchScalarGridSpec(
        num_scalar_prefetch=1, grid=(nj, h // tr),
        in_specs=[pl.BlockSpec((None, None, tr, c), lambda j, i, c_ref: (j, c_ref[0], i, 0)),
                  pl.BlockSpec((None, tr, c), lambda j, i, c_ref: (j, i, 0))],
        out_specs=pl.BlockSpec((None, tr, c), lambda j, i, c_ref: (j, i, 0)))
    return pl.pallas_call(body, out_shape=jax.ShapeDtypeStruct((nj, h, c), wire_dtype), grid_spec=grid_spec,
                          compiler_params=_cp(("arbitrary",) * 2), name=name)(c_idx, g, recv)


def _add_chips(own, parts, place, name):
    _, h, c = own.shape
    tr = _row_tile(h, c)

    def body(p_ref, o_ref, a_ref, b_ref, c_ref, out_ref):
        out_ref[...] = ((o_ref[...].astype(F32) + a_ref[...].astype(F32)) + b_ref[...].astype(F32)) + c_ref[...].astype(F32)

    def other(k):
        return pl.BlockSpec((None, tr, c), lambda i, p_ref: (jnp.bitwise_xor(p_ref[0], k), i, 0))

    grid_spec = pltpu.PrefetchScalarGridSpec(
        num_scalar_prefetch=1, grid=(h // tr,),
        in_specs=[pl.BlockSpec((None, tr, c), lambda i, p_ref: (p_ref[0], i, 0)), other(1), other(2), other(3)],
        out_specs=pl.BlockSpec((None, tr, c), lambda i, p_ref: (p_ref[1], i, 0)))
    return pl.pallas_call(body, out_shape=jax.ShapeDtypeStruct((2, h, c), F32), grid_spec=grid_spec,
                          compiler_params=_cp(("arbitrary",)), name=name)(place, own, parts, parts, parts)


HBM_SPEC = pl.BlockSpec(memory_space=pltpu.HBM)


def _place():
    x, y, c = lax.axis_index("x"), lax.axis_index("y"), lax.axis_index("c")
    chips = [(1 - x, y), (x, 1 - y), (1 - x, 1 - y)]
    return x, y, c, 2 * x + y, chips, [2 * px + py for px, py in chips]


def _remote(src, dst, send_sem, recv_sem, dev):
    return pltpu.make_async_remote_copy(src_ref=src, dst_ref=dst, send_sem=send_sem, recv_sem=recv_sem,
                                        device_id=dev, device_id_type=MESH)


def _comm_call(body, arrays, out_shapes, sems, name, aliases=None):
    n = len(arrays)
    return pl.pallas_call(
        body, out_shape=tuple(out_shapes), in_specs=[HBM_SPEC] * n, out_specs=tuple([HBM_SPEC] * len(out_shapes)),
        scratch_shapes=[pltpu.SemaphoreType.DMA(s) for s in sems], input_output_aliases=aliases or {},
        compiler_params=pltpu.CompilerParams(has_side_effects=True), name=name)(*arrays)


class _SideCopies:
    def __init__(self, arrays, out_shapes, sem_shape, sends, recvs):
        self.arrays, self.out_shapes, self.sem_shape = list(arrays), list(out_shapes), sem_shape
        self.sends, self.recvs = sends, recvs

    @property
    def scratch(self):
        return [pltpu.SemaphoreType.DMA(self.sem_shape), pltpu.SemaphoreType.DMA(self.sem_shape)]

    def start(self, ins, outs, send, recv):
        for cp in self.sends(ins, outs, send, recv):
            cp.start()

    def wait(self, ins, outs, send, recv):
        for cp in self.recvs(ins, outs, send, recv):
            cp.wait_recv()
        for cp in self.sends(ins, outs, send, recv):
            cp.wait_send()

    def run(self, name):
        n = len(self.arrays)

        def body(*refs):
            ins, outs = refs[:n], refs[n:n + len(self.out_shapes)]
            send, recv = refs[-2:]
            self.start(ins, outs, send, recv)
            self.wait(ins, outs, send, recv)

        return _comm_call(body, self.arrays, self.out_shapes, [self.sem_shape] * 2, name)


def _gather_over_chips(shards):
    n = len(shards)

    def sends(ins, outs, send, recv):
        x, y, c, me, chips, _ = _place()
        over_ici = [_remote(ins[a].at[c], outs[a].at[me, c], send.at[a, k], recv.at[a, k], (px, py, c))
                    for a in range(n) for k, (px, py) in enumerate(chips)]
        return over_ici + [_remote(ins[a], outs[a].at[me], send.at[a, 3], recv.at[a, 3], (x, y, 1 - c)) for a in range(n)]

    def recvs(ins, outs, send, recv):
        x, y, c, me, _, cidx = _place()
        slots = [[outs[a].at[cidx[k], c] for k in range(3)] + [outs[a].at[me]] for a in range(n)]
        return [_remote(blk, blk, send.at[a, k], recv.at[a, k], (x, y, 1 - c))
                for a in range(n) for k, blk in enumerate(slots[a])]

    out_shapes = [jax.ShapeDtypeStruct((N_CHIPS,) + s.shape, s.dtype) for s in shards]
    return _SideCopies(shards, out_shapes, (n, 4), sends, recvs)


def _gather_finish(partial, name):
    n = len(partial)

    def body(*refs):
        part, outs = refs[:n], refs[n:2 * n]
        send, recv = refs[2 * n:]
        x, y, c, _, _, cidx = _place()
        sib = (x, y, 1 - c)
        cps = [_remote(part[a].at[cidx[k], c], outs[a].at[cidx[k], c], send.at[a, k], recv.at[a, k], sib)
               for a in range(n) for k in range(3)]
        for cp in cps:
            cp.start()
        for a in range(n):
            for k in range(3):
                blk = outs[a].at[cidx[k], 1 - c]
                _remote(blk, blk, send.at[a, k], recv.at[a, k], sib).wait_recv()
        for cp in cps:
            cp.wait_send()

    out_shapes = [jax.ShapeDtypeStruct(p.shape, p.dtype) for p in partial]
    return _comm_call(body, list(partial), out_shapes, [(n, 3), (n, 3)], name, aliases={a: a for a in range(n)})


def _sibling_halves(grads):
    n = len(grads)

    def copies(ins, outs, send, recv):
        x, y, c, _, _, _ = _place()
        return [_remote(ins[a].at[j, 1 - c], outs[a].at[j], send.at[a, j], recv.at[a, j], (x, y, 1 - c))
                for a in range(n) for j in range(N_CHIPS)]

    out_shapes = [jax.ShapeDtypeStruct((g.shape[0],) + g.shape[2:], g.dtype) for g in grads]
    return _SideCopies(grads, out_shapes, (n, N_CHIPS), copies, copies)


def _scatter_over_chips(sums):
    n = len(sums)

    def sends(ins, outs, send, recv):
        _, _, c, me, chips, cidx = _place()
        return [_remote(ins[a].at[cidx[k]], outs[a].at[me], send.at[a, k], recv.at[a, k], (px, py, c))
                for a in range(n) for k, (px, py) in enumerate(chips)]

    def recvs(ins, outs, send, recv):
        x, y, c, _, _, cidx = _place()
        return [_remote(outs[a].at[cidx[k]], outs[a].at[cidx[k]], send.at[a, k], recv.at[a, k], (x, y, 1 - c))
                for a in range(n) for k in range(3)]

    return _SideCopies(sums, [jax.ShapeDtypeStruct(s.shape, s.dtype) for s in sums], (n, 3), sends, recvs)


def _exchange_halves(halves, name):
    n = len(halves)

    def body(*refs):
        ins, bufs = refs[:n], refs[n:2 * n]
        send, recv = refs[2 * n:]
        x, y, c, _, _, _ = _place()
        sib = (x, y, 1 - c)
        cps = []
        for a in range(n):
            cp = _remote(ins[a].at[c], bufs[a].at[c], send.at[a], recv.at[a], sib)
            cp.start()
            cps.append(cp)
        for a in range(n):
            blk = bufs[a].at[1 - c]
            _remote(blk, blk, send.at[a], recv.at[a], sib).wait_recv()
        for cp in cps:
            cp.wait_send()

    out_shapes = [jax.ShapeDtypeStruct(s.shape, s.dtype) for s in halves]
    return _comm_call(body, halves, out_shapes, [(n,), (n,)], name, aliases={a: a for a in range(n)})


def _gather_chips(slices, name):
    n = len(slices)

    def body(*refs):
        ins, outs = refs[:n], refs[n:2 * n]
        send, recv = refs[2 * n:]
        x, y, c, me, chips, cidx = _place()
        sib = (x, y, 1 - c)
        cps = []
        for a in range(n):
            for k, dev in enumerate([(px, py, c) for px, py in chips] + [sib]):
                cp = _remote(ins[a], outs[a].at[me], send.at[a, k], recv.at[a, k], dev)
                cp.start()
                cps.append(cp)
        for a in range(n):
            for k, slot in enumerate(cidx + [me]):
                blk = outs[a].at[slot]
                _remote(blk, blk, send.at[a, k], recv.at[a, k], sib).wait_recv()
        for cp in cps:
            cp.wait_send()

    out_shapes = [jax.ShapeDtypeStruct((N_CHIPS,) + s.shape, s.dtype) for s in slices]
    return _comm_call(body, slices, out_shapes, [(n, 4), (n, 4)], name)


def _rs_siblings(grads):
    return _sibling_halves([g.reshape(g.shape[0], 2, g.shape[1] // 2, g.shape[2]) for g in grads])


def _rs_chips(siblings, recv, c_idx, wire_dtype, tag):
    wire = wire_dtype if isinstance(wire_dtype, list) else [wire_dtype] * len(recv)
    return _scatter_over_chips([_add_sibling(g, r, c_idx, w, "rs_add_sibling_" + tag)
                                for g, r, w in zip(siblings.arrays, recv, wire)])


def _rs_end(scatter, parts, place, tag):
    halves = [_add_chips(s, p, place, "rs_add_chips_" + tag) for s, p in zip(scatter.arrays, parts)]
    both = _exchange_halves(halves, "rs_halves_" + tag)
    return [b.reshape(b.shape[0] * b.shape[1], b.shape[2]) for b in both]


SMALL_ORDER = ("gm_w_s", "mix_pre_norm", "mix_post_norm", "mix_out_norm", "ffn_pre_norm", "ffn_post_norm", "ple_norm",
               "gm_v_norm", "gm_b_s", "b_forget")
SMALL_ROWS_MULTIPLE = 64
SMALL_BLOCK = 8 * LANES


def _chip_columns(pieces, width):
    out = []
    for j in range(N_CHIPS):
        lo, hi, parts, off = j * width, (j + 1) * width, [], 0
        for piece in pieces:
            a, b = max(lo, off), min(hi, off + piece.shape[1])
            if a < b:
                parts.append(piece[:, a - off:b - off])
            off += piece.shape[1]
        out.append(parts[0] if len(parts) == 1 else jnp.concatenate(parts, axis=1))
    return jnp.stack(out)


def _columns(sliced, lo, hi):
    width, parts = sliced.shape[2], []
    for j in range(N_CHIPS):
        a, b = max(lo, j * width), min(hi, (j + 1) * width)
        if a < b:
            parts.append(sliced[j][:, a - j * width:b - j * width])
    return parts[0] if len(parts) == 1 else jnp.concatenate(parts, axis=1)


def _pack_small(parts):
    blocks = []
    for nme in SMALL_ORDER:
        v = parts[nme].reshape(-1)
        pad = (-v.shape[0]) % SMALL_BLOCK
        blocks.append((jnp.pad(v, (0, pad)) if pad else v).reshape(-1, LANES))
    rows = sum(b.shape[0] for b in blocks)
    if rows % SMALL_ROWS_MULTIPLE:
        blocks.append(jnp.zeros((SMALL_ROWS_MULTIPLE - rows % SMALL_ROWS_MULTIPLE, LANES), F32))
    return jnp.concatenate(blocks, axis=0)


def _unpack_small(packed, shapes):
    out, row = {}, 0
    for nme in SMALL_ORDER:
        size = 1
        for s in shapes[nme]:
            size *= s
        rows = -(-size // SMALL_BLOCK) * (SMALL_BLOCK // LANES)
        block = packed[row:row + rows]
        out[nme] = (block if size % SMALL_BLOCK == 0 else block.reshape(-1)[:size]).reshape(shapes[nme])
        row += rows
    return out


def kernel(x, p, mix_pre_norm, mix_post_norm, w_in, b_forget, gm_v_norm, gm_w_s, gm_b_s, mix_out_norm, w_out, ffn_pre_norm, ffn_post_norm, w_ffn_in, w_ffn_out, w_ple, ple_norm, w_ple_gate, loss_target, m_mix_pre_norm, m_mix_post_norm, m_w_in, m_b_forget, m_gm_v_norm, m_gm_w_s, m_gm_b_s, m_mix_out_norm, m_w_out, m_ffn_pre_norm, m_ffn_post_norm, m_w_ffn_in, m_w_ffn_out, m_w_ple, m_ple_norm, m_w_ple_gate, v_mix_pre_norm, v_mix_post_norm, v_w_in, v_b_forget, v_gm_v_norm, v_gm_w_s, v_gm_b_s, v_mix_out_norm, v_w_out, v_ffn_pre_norm, v_ffn_post_norm, v_w_ffn_in, v_w_ffn_out, v_w_ple, v_ple_norm, v_w_ple_gate):
    weights = dict(mix_pre_norm=mix_pre_norm, mix_post_norm=mix_post_norm, w_in=w_in, b_forget=b_forget,
                   gm_v_norm=gm_v_norm, gm_w_s=gm_w_s, gm_b_s=gm_b_s, mix_out_norm=mix_out_norm, w_out=w_out,
                   ffn_pre_norm=ffn_pre_norm, ffn_post_norm=ffn_post_norm, w_ffn_in=w_ffn_in, w_ffn_out=w_ffn_out,
                   w_ple=w_ple, ple_norm=ple_norm, w_ple_gate=w_ple_gate)
    mom_m = dict(mix_pre_norm=m_mix_pre_norm, mix_post_norm=m_mix_post_norm, w_in=m_w_in, b_forget=m_b_forget,
                 gm_v_norm=m_gm_v_norm, gm_w_s=m_gm_w_s, gm_b_s=m_gm_b_s, mix_out_norm=m_mix_out_norm, w_out=m_w_out,
                 ffn_pre_norm=m_ffn_pre_norm, ffn_post_norm=m_ffn_post_norm, w_ffn_in=m_w_ffn_in,
                 w_ffn_out=m_w_ffn_out, w_ple=m_w_ple, ple_norm=m_ple_norm, w_ple_gate=m_w_ple_gate)
    mom_v = dict(mix_pre_norm=v_mix_pre_norm, mix_post_norm=v_mix_post_norm, w_in=v_w_in, b_forget=v_b_forget,
                 gm_v_norm=v_gm_v_norm, gm_w_s=v_gm_w_s, gm_b_s=v_gm_b_s, mix_out_norm=v_mix_out_norm, w_out=v_w_out,
                 ffn_pre_norm=v_ffn_pre_norm, ffn_post_norm=v_ffn_post_norm, w_ffn_in=v_w_ffn_in,
                 w_ffn_out=v_w_ffn_out, w_ple=v_w_ple, ple_norm=v_ple_norm, w_ple_gate=v_w_ple_gate)
    big = ("w_in", "w_out", "w_ffn_in", "w_ffn_out", "w_ple", "w_ple_gate")
    depth = w_in.shape[0]
    t, d = x.shape[1], x.shape[2]
    d_ff = w_ffn_out.shape[1] * N_CHIPS
    c_idx = lax.axis_index("c").astype(jnp.int32).reshape(1)
    place = jnp.stack([2 * lax.axis_index("x") + lax.axis_index("y"), lax.axis_index("c")]).astype(jnp.int32)
    h = x[0]
    target = loss_target[0]
    mean_mat = _group_mean_matrix()

    def row(a, i):
        return a[i].reshape(1, -1)

    def shards_of(i):
        shards = [weights[nme][i].astype(BF16) for nme in big]
        return [s.reshape(2, s.shape[0] // 2, s.shape[1]) for s in shards]

    saved = []
    hn = _norm_cast(h, row(mix_pre_norm, 0), "norm_first")
    by_cols = lambda g: g.transpose(1, 0, 2).reshape(g.shape[1], N_CHIPS * g.shape[2])
    by_rows = lambda g: g.reshape(N_CHIPS * g.shape[1], g.shape[2])
    whole = lambda g: g.reshape(N_CHIPS, g.shape[2] * 2, g.shape[3])
    gather = _gather_over_chips(shards_of(0)[:1])
    w_in_g = _gather_finish(gather.run("gather_first"), "gather_finish_first")[0]
    for i in range(depth):
        w_in_c = whole(w_in_g)
        w_qkv = _columns(w_in_c, 0, 3 * D_ATT)
        w_gf = jnp.concatenate([_columns(w_in_c, 3 * D_ATT + N_HEADS, N_CHIPS * w_in_c.shape[2]),
                                _columns(w_in_c, 3 * D_ATT, 3 * D_ATT + N_HEADS),
                                jnp.zeros((d, LANES - N_HEADS), BF16)], axis=1)
        b_pad = jnp.pad(b_forget[i], (0, LANES - N_HEADS)).reshape(1, LANES)
        bias_full = jnp.repeat(gm_b_s[i].T, HEAD_DIM, axis=1)
        gain_v = row(gm_v_norm, i)

        qkv = _mm(hn, w_qkv, "nn", BF16, "mm_qkv")
        zgf = _mm(hn, w_gf, "nn", F32, "mm_gf", tn_cap=384)
        qa, ka, va = _attn_prep(qkv, _gate_fwd(zgf, b_pad, "gate_fwd"), "attn_prep")
        last = i + 1 == depth
        gather = _gather_over_chips(shards_of(i)[1:] + ([] if last else shards_of(i + 1)[:1]))
        (att, lse), partial = _attn_fwd(qa, ka, va, "attn_fwd_last" if last else "attn_fwd", gather)
        gathered = _gather_finish(partial, "gather_finish_last" if last else "gather_finish")
        w_out_f, w_fi_f, w_fo_f = by_rows(whole(gathered[0])), by_cols(whole(gathered[1])), by_rows(whole(gathered[2]))
        w_ple_f, w_pg_f = by_cols(whole(gathered[3])), by_rows(whole(gathered[4]))
        w_in_g = None if last else gathered[5]
        gm = _gmlp_fwd(zgf, gm_w_s[i], bias_full, gain_v, mean_mat, "gmlp_fwd")
        mixed = _mixnorm_fwd(att, gm, row(mix_out_norm, i), "mixnorm_fwd")
        o, h1, hn2 = _resid_norm(h, _Prod((mixed, w_out_f, "nn")), row(mix_post_norm, i), row(ffn_pre_norm, i),
                                 "out_resid")
        s, ab = _ffn_in(hn2, w_fi_f, "ffn_in")
        o2, h2, hr = _resid_norm(h1, _Prod((s, w_fo_f, "nn")), row(ffn_post_norm, i), None, "ffn_out_resid")
        g_next = row(mix_pre_norm, i + 1) if i + 1 < depth else row(mix_pre_norm, 0)
        pe, gl, h3, hn_next = _ple_fwd(h2, _Prod((p[i, 0], w_ple_f, "nn")), _Prod((hr, w_pg_f, "nn")),
                                       row(ple_norm, i), g_next, "ple_fwd")
        saved.append(dict(h=h, hn=hn, qa=qa, ka=ka, va=va, zgf=zgf, att=att, lse=lse, gm=gm, mixed=mixed,
                          o=o, h1=h1, hn2=hn2, ab=ab, s=s, o2=o2, h2=h2, hr=hr, pe=pe, gl=gl, w_qkv=w_qkv, w_gf=w_gf,
                          w_out=w_out_f, w_fi=w_fi_f, w_fo=w_fo_f, w_pg=w_pg_f, b_pad=b_pad, bias_full=bias_full,
                          gain_v=gain_v))
        h, hn = h3, hn_next

    dh, loss_blk = _loss_head(h, target, "loss_head")
    loss = lax.psum(loss_blk[0, 0], ("x", "y", "c"))

    small = {nme: [None] * depth for nme in SMALL_ORDER}
    big_grads = {nme: [None] * depth for nme in big}
    waiting = []
    for i in reversed(range(depth)):
        sv = saved[i]
        dgl, dpe, small["ple_norm"][i] = _ple_bwd(dh, sv["pe"], sv["gl"], row(ple_norm, i), "ple_bwd")
        g_pg = _mm(sv["hr"], dgl, "tn", F32, "mm_dw_ple_gate").reshape(N_CHIPS, -1, d)
        g_ple = _mm(p[i, 0], dpe, "tn", F32, "mm_dw_ple", chip_split=True)
        dh2, do2, small["ffn_post_norm"][i] = _join(dh, sv["h2"], None, _Prod((dgl, sv["w_pg"], "nt")), sv["o2"],
                                                    row(ffn_post_norm, i), "join_ple")
        dab = _ffn_bwd(do2, sv["w_fo"], sv["ab"], "ffn_bwd")
        g_fo = _mm(sv["s"], do2, "tn", F32, "mm_dw_ffn_out", tm=256).reshape(N_CHIPS, -1, d)
        g_fi = _mm_halves_tn(sv["hn2"], dab, "mm_dw_ffn_in")
        dh1, do, small["ffn_pre_norm"][i], small["mix_post_norm"][i] = _join(
            dh2, sv["h1"], row(ffn_pre_norm, i), _Prod((dab, sv["w_fi"], "nt", 0), (dab, sv["w_fi"], "nt", 1)), sv["o"],
            row(mix_post_norm, i), "join_ffn", tm=256)
        g_out = _mm(sv["mixed"], do, "tn", F32, "mm_dw_out").reshape(N_CHIPS, -1, d)
        datt, dgm, small["mix_out_norm"][i] = _mixnorm_bwd(sv["att"], sv["gm"], row(mix_out_norm, i),
                                                          _Prod((do, sv["w_out"], "nt")), "mixnorm_bwd")
        batch = waiting + [("w_out", i, g_out), ("w_ffn_in", i, g_fi), ("w_ffn_out", i, g_fo), ("w_ple", i, g_ple),
                           ("w_ple_gate", i, g_pg)]
        tag = "layer" if waiting else "top"
        siblings = _rs_siblings([g for _, _, g in batch])
        (dg, small["gm_w_s"][i], dmix_sum, small["gm_v_norm"][i]), recv = _gmlp_bwd(
            sv["zgf"], dgm, gm_w_s[i], sv["bias_full"], sv["gain_v"], mean_mat, "gmlp_bwd_" + tag, siblings)
        small["gm_b_s"][i] = dmix_sum.reshape(CHUNK, N_HEADS, HEAD_DIM).sum(-1).T
        qb, doa = _attn_bwd_prep(sv["qa"], sv["att"], sv["lse"], datt, "attn_bwd_prep")
        scatter = _rs_chips(siblings, recv, c_idx, BF16, tag)
        (dqa, dka, dva), parts = _attn_bwd(qb, sv["ka"], sv["va"], doa, "attn_bwd_" + tag, scatter)
        for (nme, layer, _), g in zip(batch, _rs_end(scatter, parts, place, tag)):
            big_grads[nme][layer] = g
        dqkv, dc = _attn_bwd_post(dqa, dka, dva, "attn_bwd_post")
        dfl, db = _gate_bwd(dc, sv["zgf"], sv["b_pad"], "gate_bwd")
        small["b_forget"][i] = db[0, :N_HEADS]
        dgf = jnp.concatenate([dg, dfl], axis=1)
        g_qkv = _mm(sv["hn"], dqkv, "tn", F32, "mm_dw_qkv")
        g_gf = _mm(sv["hn"], dgf, "tn", F32, "mm_dw_gf", tn_cap=384)
        g_in = _chip_columns([g_qkv, g_gf[:, 2 * D_GM:2 * D_GM + N_HEADS], g_gf[:, :2 * D_GM]], w_in.shape[2])
        dh, small["mix_pre_norm"][i] = _join(dh1, sv["h"], row(mix_pre_norm, i),
                                             _Prod((dqkv, sv["w_qkv"], "nt"), (dgf, sv["w_gf"], "nt")), None, None, "join_mix")
        waiting = [("w_in", i, g_in)]
    grad_x = dh.reshape(1, t, d)

    small_shapes = {nme: weights[nme].shape for nme in SMALL_ORDER}
    small_part = _pack_small({nme: jnp.stack([g.reshape(small_shapes[nme][1:]) for g in small[nme]])
                              for nme in SMALL_ORDER})
    rows_small = small_part.shape[0]
    siblings = _rs_siblings([waiting[0][2], small_part.reshape(N_CHIPS, rows_small // N_CHIPS, LANES)])
    scatter = _rs_chips(siblings, siblings.run("rs_sibling_tail"), c_idx, [BF16, F32], "tail")
    big_grads["w_in"][0], small_slice = _rs_end(scatter, scatter.run("rs_chips_tail"), place, "tail")
    small_all = _gather_chips([small_slice], "gather_small")[0].reshape(1, rows_small, LANES)
    sd, sm, sv_ = _adamw(_pack_small({n_: weights[n_] for n_ in SMALL_ORDER})[None], small_all,
                         _pack_small({n_: mom_m[n_] for n_ in SMALL_ORDER})[None],
                         _pack_small({n_: mom_v[n_] for n_ in SMALL_ORDER})[None], "adamw_small")
    grads = _unpack_small(small_all[0], small_shapes)
    deltas = _unpack_small(sd[0], small_shapes)
    new_m = _unpack_small(sm[0], small_shapes)
    new_v = _unpack_small(sv_[0], small_shapes)

    for nme in big:
        g = jnp.stack(big_grads[nme]).reshape(weights[nme].shape)
        grads[nme] = g
        deltas[nme], new_m[nme], new_v[nme] = _adamw(weights[nme], g, mom_m[nme], mom_v[nme], "adamw_" + nme)

    order = ("mix_pre_norm", "mix_post_norm", "w_in", "b_forget", "gm_v_norm", "gm_w_s", "gm_b_s", "mix_out_norm",
             "w_out", "ffn_pre_norm", "ffn_post_norm", "w_ffn_in", "w_ffn_out", "w_ple", "ple_norm", "w_ple_gate")
    return (loss, grad_x, *[grads[n_] for n_ in order], *[deltas[n_] for n_ in order], *[new_m[n_] for n_ in order],
            *[new_v[n_] for n_ in order])
```

```python
import functools

import jax
import jax.numpy as jnp
from jax import lax
from jax.experimental import pallas as pl
from jax.experimental.pallas import tpu as pltpu

F32 = jnp.float32
BF16 = jnp.bfloat16
MESH = pl.DeviceIdType.MESH
HIGHEST = lax.Precision.HIGHEST

EPS = 1e-6
NEG_INF = -1e30
N_HEADS = 8
HEAD_DIM = 64
D_ATT = N_HEADS * HEAD_DIM
D_GM = 512
CHUNK = 128
LANES = 128
N_CHIPS = 4
ADAM_LR = 0.001
ADAM_B1 = 0.9
ADAM_B2 = 0.999
ADAM_EPS = 1e-08
ADAM_WD = 0.01
ADAM_STEP = 10
VMEM_LIMIT = 56 * 1024 * 1024


def _cp(sem=None):
    return pltpu.CompilerParams(dimension_semantics=sem, vmem_limit_bytes=VMEM_LIMIT)


def _full(shape):
    return pl.BlockSpec(shape, lambda *_: (0,) * len(shape))


def _rows(tm, width, col_block=0):
    return pl.BlockSpec((tm, width), lambda i: (i, col_block))


def _dot(a, b, dims, precision=None):
    return lax.dot_general(a, b, (dims, ((), ())), preferred_element_type=F32, precision=precision)


NN = ((1,), (0,))
NT = ((1,), (1,))
TN = ((0,), (0,))


def _pick(n, cap):
    best = None
    for t in range(LANES, min(n, cap) + 1, LANES):
        if n % t == 0:
            best = t
    assert best is not None, (n, cap)
    return best


def _mm(a, b, mode, out_dtype, name, tm=None, tn_cap=1024, chip_split=False):
    dims = {"nn": NN, "nt": NT, "tn": TN}[mode]
    if mode == "tn":
        k, m = a.shape
    else:
        m, k = a.shape
    n = b.shape[0] if mode == "nt" else b.shape[1]
    if tm is None:
        tm = 512 if mode == "tn" else 1024
    tm = min(tm, m)
    tn = n // N_CHIPS if chip_split else _pick(n, tn_cap)
    assert m % tm == 0 and n % tn == 0

    def body(a_ref, b_ref, o_ref):
        o_ref[...] = _dot(a_ref[...].astype(BF16), b_ref[...].astype(BF16), dims).astype(out_dtype)

    a_spec = pl.BlockSpec((k, tm), lambda i, j: (0, i)) if mode == "tn" else pl.BlockSpec((tm, k), lambda i, j: (i, 0))
    b_spec = pl.BlockSpec((tn, k), lambda i, j: (j, 0)) if mode == "nt" else pl.BlockSpec((k, tn), lambda i, j: (0, j))
    if chip_split:
        out_shape = jax.ShapeDtypeStruct((N_CHIPS, m, tn), out_dtype)
        out_spec = pl.BlockSpec((None, tm, tn), lambda i, j: (j, i, 0))
    else:
        out_shape = jax.ShapeDtypeStruct((m, n), out_dtype)
        out_spec = pl.BlockSpec((tm, tn), lambda i, j: (i, j))
    return pl.pallas_call(body, out_shape=out_shape, grid=(m // tm, n // tn), in_specs=[a_spec, b_spec],
                          out_specs=out_spec, compiler_params=_cp(("arbitrary", "arbitrary")), name=name)(a, b)


def _rms_inv(x):
    return lax.rsqrt(jnp.mean(x * x, axis=-1, keepdims=True) + EPS)


def _rms_bwd(x, gain, dy):
    inv = _rms_inv(x)
    xhat = x * inv
    dxn = dy if gain is None else dy * gain
    dx = inv * (dxn - xhat * jnp.mean(dxn * xhat, axis=-1, keepdims=True))
    return dx, dy * xhat


def _acc_rows(ref, val):
    s = jnp.sum(val, axis=0, keepdims=True)

    @pl.when(pl.program_id(0) == 0)
    def _():
        ref[...] = s

    @pl.when(pl.program_id(0) > 0)
    def _():
        ref[...] += s


class _Prod:
    def __init__(self, *terms):
        self.terms = terms


def _rowwise(fn, name, rows, consts, outs, accs=(), tm=512):
    arrays, specs, loaders = [], [], []
    t = next(x for x in rows if not isinstance(x, _Prod)).shape[0]
    tm = min(tm, t)
    for x in rows:
        pos = len(arrays)
        if isinstance(x, _Prod):
            dims = []
            for term in x.terms:
                a, b, mode = term[:3]
                if len(term) == 4:
                    f = a.shape[2]
                    specs += [pl.BlockSpec((None, tm, f), lambda i, half=term[3]: (half, i, 0)),
                              pl.BlockSpec((b.shape[0], f), lambda i, half=term[3]: (0, half))]
                else:
                    specs += [_rows(tm, a.shape[1]), _full(b.shape)]
                arrays += [a, b]
                dims.append(NT if mode == "nt" else NN)

            def load(refs, pos=pos, dims=dims):
                total = None
                for k, dm in enumerate(dims):
                    part = _dot(refs[pos + 2 * k][...].astype(BF16), refs[pos + 2 * k + 1][...], dm)
                    total = part if total is None else total + part
                return total
        else:
            arrays.append(x)
            specs.append(_rows(tm, x.shape[1]))

            def load(refs, pos=pos):
                return refs[pos][...]
        loaders.append(load)
    for cst in consts:
        loaders.append(lambda refs, pos=len(arrays): refs[pos][...])
        arrays.append(cst)
        specs.append(_full(cst.shape))
    n_in = len(arrays)

    def body(*refs):
        res = fn(*[ld(refs) for ld in loaders])
        out_refs = refs[n_in:]
        for k, (_, dtype) in enumerate(outs):
            out_refs[k][...] = res[k].astype(dtype)
        for k in range(len(accs)):
            _acc_rows(out_refs[len(outs) + k], res[len(outs) + k])

    out_shape = [jax.ShapeDtypeStruct((t, c), dtype) for c, dtype in outs] + [jax.ShapeDtypeStruct((1, c), F32) for c in accs]
    out_specs = [_rows(tm, c) for c, _ in outs] + [_full((1, c)) for c in accs]
    return pl.pallas_call(body, out_shape=tuple(out_shape), grid=(t // tm,), in_specs=specs, out_specs=tuple(out_specs),
                          compiler_params=_cp(("arbitrary",)), name=name)(*arrays)


def _norm_cast(h, gain, name):
    return _rowwise(lambda x, g: (x * _rms_inv(x) * g,), name, [h], [gain], [(h.shape[1], BF16)])[0]


def _resid_norm(h, o, g_post, g_next, name):
    d = h.shape[1]

    def fn(hv, ov, gp, *gn):
        h1 = hv + ov * _rms_inv(ov) * gp
        hn = h1 * _rms_inv(h1)
        return ov, h1, hn * gn[0] if gn else hn

    return _rowwise(fn, name, [h, o], [g_post] + ([] if g_next is None else [g_next]), [(d, F32), (d, F32), (d, BF16)])


def _ple_fwd(h2, pe, gl, g_ple, g_next, name):
    d = h2.shape[1]

    def fn(hv, pv, gv, gp, gn):
        h3 = hv + jax.nn.sigmoid(gv) * (pv * _rms_inv(pv) * gp)
        return pv, gv, h3, h3 * _rms_inv(h3) * gn

    return _rowwise(fn, name, [h2, pe, gl], [g_ple, g_next], [(d, F32)] * 3 + [(d, BF16)])


def _loss_head(y, target, name, tm=512):
    t, d = y.shape

    def body(y_ref, t_ref, dy_ref, loss_ref):
        diff = y_ref[...] - t_ref[...]
        dy_ref[...] = diff * (1.0 / d)
        part = 0.5 * jnp.sum(jnp.mean(diff * diff, axis=-1, keepdims=True), axis=0, keepdims=True)
        part = jnp.broadcast_to(part, (8, LANES))

        @pl.when(pl.program_id(0) == 0)
        def _():
            loss_ref[...] = part

        @pl.when(pl.program_id(0) > 0)
        def _():
            loss_ref[...] += part

    return pl.pallas_call(body, out_shape=(jax.ShapeDtypeStruct((t, d), F32), jax.ShapeDtypeStruct((8, LANES), F32)),
                          grid=(t // tm,), in_specs=[_rows(tm, d)] * 2, out_specs=(_rows(tm, d), _full((8, LANES))),
                          compiler_params=_cp(("arbitrary",)), name=name)(y, target)


def _ple_bwd(dh3, pe, gl, g_ple, name):
    d = dh3.shape[1]

    def fn(dh, pv, gv, gp):
        gate = jax.nn.sigmoid(gv)
        dpe, dg_rows = _rms_bwd(pv, gp, dh * gate)
        return dh * (pv * _rms_inv(pv) * gp) * gate * (1.0 - gate), dpe, dg_rows

    return _rowwise(fn, name, [dh3, pe, gl], [g_ple], [(d, BF16), (d, BF16)], [d])


def _join(d_res, x_a, gain_a, d_a, x_b, gain_b, name, tm=512):
    d = d_res.shape[1]
    has_ga = gain_a is not None
    has_b = x_b is not None

    def fn(*vals):
        it = iter(vals)
        dres, xa, da = next(it), next(it), next(it)
        xb = next(it) if has_b else None
        ga = next(it) if has_ga else None
        dx, dga_rows = _rms_bwd(xa, ga, da)
        dout = dres + dx
        if not has_b:
            return (dout, dga_rows) if has_ga else (dout,)
        db, dgb_rows = _rms_bwd(xb, next(it), dout)
        return (dout, db, dga_rows, dgb_rows) if has_ga else (dout, db, dgb_rows)

    rows = [d_res, x_a, d_a] + ([x_b] if has_b else [])
    consts = ([gain_a] if has_ga else []) + ([gain_b] if has_b else [])
    return _rowwise(fn, name, rows, consts, [(d, F32)] + ([(d, BF16)] if has_b else []), [d] * (has_ga + has_b), tm)


def _ffn_in(hn, w, name, tm=2048, tn=256):
    t, d = hn.shape
    f = w.shape[1] // 2
    tm = min(tm, t)
    nj = f // tn

    def body(h_ref, wa_ref, wb_ref, s_ref, ab_ref):
        h = h_ref[...]
        a = _dot(h, wa_ref[...], NN)
        b = _dot(h, wb_ref[...], NN)
        ab_ref[0] = a.astype(BF16)
        ab_ref[1] = b.astype(BF16)
        s_ref[...] = (a * jax.nn.sigmoid(a) * b).astype(BF16)

    return pl.pallas_call(
        body, out_shape=(jax.ShapeDtypeStruct((t, f), BF16), jax.ShapeDtypeStruct((2, t, f), BF16)), grid=(t // tm, nj),
        in_specs=[pl.BlockSpec((tm, d), lambda i, j: (i, 0)), pl.BlockSpec((d, tn), lambda i, j: (0, j)),
                  pl.BlockSpec((d, tn), lambda i, j: (0, nj + j))],
        out_specs=(pl.BlockSpec((tm, tn), lambda i, j: (i, j)), pl.BlockSpec((2, tm, tn), lambda i, j: (0, i, j))),
        compiler_params=_cp(("arbitrary",) * 2), name=name)(hn, w, w)


def _ffn_bwd(do, w_out, ab, name, tm=2048, tn=256):
    t, d = do.shape
    f = w_out.shape[0]
    tm = min(tm, t)

    def body(d_ref, w_ref, ab_ref, o_ref):
        ds = _dot(d_ref[...], w_ref[...], NT)
        a = ab_ref[0].astype(F32)
        sg = jax.nn.sigmoid(a)
        silu = a * sg
        o_ref[0] = (ds * ab_ref[1].astype(F32) * (sg + silu * (1.0 - sg))).astype(BF16)
        o_ref[1] = (ds * silu).astype(BF16)

    both = pl.BlockSpec((2, tm, tn), lambda i, j: (0, i, j))
    return pl.pallas_call(
        body, out_shape=jax.ShapeDtypeStruct((2, t, f), BF16), grid=(t // tm, f // tn),
        in_specs=[pl.BlockSpec((tm, d), lambda i, j: (i, 0)), pl.BlockSpec((tn, d), lambda i, j: (j, 0)), both],
        out_specs=both, compiler_params=_cp(("arbitrary",) * 2), name=name)(do, w_out, ab)


def _mm_halves_tn(h, x2, name, tm=512):
    t, d = h.shape
    f = x2.shape[2]
    tn = 2 * f // N_CHIPS

    def body(h_ref, x_ref, o_ref):
        o_ref[...] = _dot(h_ref[...], x_ref[...], TN)

    return pl.pallas_call(
        body, out_shape=jax.ShapeDtypeStruct((N_CHIPS, d, tn), F32), grid=(N_CHIPS, d // tm),
        in_specs=[pl.BlockSpec((t, tm), lambda j, i: (0, i)), pl.BlockSpec((None, t, tn), lambda j, i: (j // 2, 0, j % 2))],
        out_specs=pl.BlockSpec((None, tm, tn), lambda j, i: (j, i, 0)),
        compiler_params=_cp(("arbitrary",) * 2), name=name)(h, x2)


def _mixnorm_fwd(att, gm, g_out, name, tm=512):
    t, w = att.shape

    def body(a_ref, m_ref, g_ref, o_ref):
        a, m, g = a_ref[...], m_ref[...], g_ref[...]
        o_ref[:, :w] = (a * _rms_inv(a) * g[:, :w]).astype(BF16)
        o_ref[:, w:] = (m * _rms_inv(m) * g[:, w:]).astype(BF16)

    return pl.pallas_call(body, out_shape=jax.ShapeDtypeStruct((t, 2 * w), BF16), grid=(t // tm,),
                          in_specs=[_rows(tm, w), _rows(tm, w), _full((1, 2 * w))], out_specs=_rows(tm, 2 * w),
                          compiler_params=_cp(("arbitrary",)), name=name)(att, gm, g_out)


def _mixnorm_bwd(att, gm, g_out, dmixed, name):
    w = att.shape[1]

    def fn(a, m, d, g):
        da, dga = _rms_bwd(a, g[:, :w], d[:, :w])
        dm, dgm = _rms_bwd(m, g[:, w:], d[:, w:])
        return da, dm, jnp.concatenate([dga, dgm], axis=1)

    return _rowwise(fn, name, [att, gm, dmixed], [g_out], [(w, F32), (w, F32)], [2 * w])


SCAN_BLOCK = 256


def _gate_fwd(zgf, b_pad, name):
    t = zgf.shape[0]
    fcol = zgf.shape[1] // LANES - 1
    nb = t // SCAN_BLOCK

    def body(f_ref, b_ref, c_ref):
        r = lax.broadcasted_iota(jnp.int32, (SCAN_BLOCK, SCAN_BLOCK), 0)
        s = lax.broadcasted_iota(jnp.int32, (SCAN_BLOCK, SCAN_BLOCK), 1)
        tril = (r >= s).astype(F32)
        head = lax.broadcasted_iota(jnp.int32, (SCAN_BLOCK, LANES), 1) < N_HEADS
        carry = jnp.zeros((1, LANES), F32)
        for blk in range(nb):
            rows = pl.ds(blk * SCAN_BLOCK, SCAN_BLOCK)
            x = f_ref[rows, :] + b_ref[...]
            lf = jnp.minimum(x, 0.0) - jnp.log1p(jnp.exp(-jnp.abs(x)))
            lf = jnp.where(head, lf, 0.0)
            cs = _dot(tril, lf, NN, HIGHEST) + carry
            c_ref[rows, :] = cs
            carry = carry + jnp.sum(lf, axis=0, keepdims=True)

    return pl.pallas_call(body, out_shape=jax.ShapeDtypeStruct((t, LANES), F32),
                          grid=(1,), in_specs=[pl.BlockSpec((t, LANES), lambda i: (0, fcol)), _full((1, LANES))],
                          out_specs=_full((t, LANES)), compiler_params=_cp(("arbitrary",)),
                          name=name)(zgf, b_pad)


def _gate_bwd(dc, zgf, b_pad, name):
    t = zgf.shape[0]
    fcol = zgf.shape[1] // LANES - 1
    nb = t // SCAN_BLOCK

    def body(dc_ref, f_ref, b_ref, dfl_ref, db_ref):
        r = lax.broadcasted_iota(jnp.int32, (SCAN_BLOCK, SCAN_BLOCK), 0)
        s = lax.broadcasted_iota(jnp.int32, (SCAN_BLOCK, SCAN_BLOCK), 1)
        triu = (s >= r).astype(F32)
        head = lax.broadcasted_iota(jnp.int32, (SCAN_BLOCK, LANES), 1) < N_HEADS
        carry = jnp.zeros((1, LANES), F32)
        db = jnp.zeros((1, LANES), F32)
        for blk in reversed(range(nb)):
            rows = pl.ds(blk * SCAN_BLOCK, SCAN_BLOCK)
            dc = dc_ref[rows, :]
            dlf = _dot(triu, dc, NN, HIGHEST) + carry
            x = f_ref[rows, :] + b_ref[...]
            dfl = jnp.where(head, dlf * jax.nn.sigmoid(-x), 0.0)
            dfl_ref[rows, :] = dfl.astype(BF16)
            db = db + jnp.sum(dfl, axis=0, keepdims=True)
            carry = carry + jnp.sum(dc, axis=0, keepdims=True)
        db_ref[...] = db

    return pl.pallas_call(body, out_shape=(jax.ShapeDtypeStruct((t, LANES), BF16), jax.ShapeDtypeStruct((1, LANES), F32)),
                          grid=(1,), in_specs=[_full((t, LANES)), pl.BlockSpec((t, LANES), lambda i: (0, fcol)),
                                               _full((1, LANES))],
                          out_specs=(_full((t, LANES)), _full((1, LANES))), compiler_params=_cp(("arbitrary",)),
                          name=name)(dc, zgf, b_pad)


ATT_BLOCK = 512
PAIRS = N_HEADS // 2
CQ_LANE = HEAD_DIM
CK_LANE = HEAD_DIM + 3
LSE_LANE = HEAD_DIM + 6


def _pick_col(x, idx):
    lane = lax.broadcasted_iota(jnp.int32, x.shape, 1)
    return jnp.sum(jnp.where(lane == idx, x, 0.0), axis=1, keepdims=True)


def _split3(x):
    hi = x.astype(BF16)
    r1 = x - hi.astype(F32)
    mid = r1.astype(BF16)
    lo = (r1 - mid.astype(F32)).astype(BF16)
    return hi, mid, lo


def _lanes_put(base, lane, start, vals):
    out = base
    for n, v in enumerate(vals):
        out = jnp.where(lane == start + n, v, out)
    return out


def _to_first_half(x, hh):
    return x if hh == 0 else pltpu.roll(x, HEAD_DIM, 1)


def _attn_prep(qkv, c, name, tm=512):
    t = qkv.shape[0]
    tm = min(tm, t)

    def body(q_ref, k_ref, v_ref, c_ref, qa_ref, ka_ref, va_ref):
        j = pl.program_id(1)
        lane = lax.broadcasted_iota(jnp.int32, (tm, LANES), 1)
        first = lane < HEAD_DIM
        q2, k2, v2 = q_ref[...].astype(F32), k_ref[...].astype(F32), v_ref[...].astype(F32)
        cc = c_ref[...]
        one = jnp.ones((tm, 1), F32)
        for hh in range(2):
            chi, cmid, clo = [v.astype(F32) for v in _split3(_pick_col(cc, 2 * j + hh))]
            qh = jnp.where(first, _to_first_half(q2, hh) * (HEAD_DIM ** -0.5), 0.0)
            kh = jnp.where(first, _to_first_half(k2, hh), 0.0)
            vh = jnp.where(first, _to_first_half(v2, hh), 0.0)
            qa = _lanes_put(qh, lane, CQ_LANE, [chi, cmid, clo, one, one, one])
            ka = _lanes_put(kh, lane, CQ_LANE, [one, one, one, -chi, -cmid, -clo, one, one, one])
            va = _lanes_put(vh, lane, CQ_LANE, [one, one, one])
            cols = slice(hh * LANES, (hh + 1) * LANES)
            qa_ref[:, cols] = qa.astype(BF16)
            ka_ref[:, cols] = ka.astype(BF16)
            va_ref[:, cols] = va.astype(BF16)

    blk = lambda off: pl.BlockSpec((tm, LANES), lambda i, j: (i, off + j))
    out = pl.BlockSpec((tm, 2 * LANES), lambda i, j: (i, j))
    shp = jax.ShapeDtypeStruct((t, N_HEADS * LANES), BF16)
    return pl.pallas_call(body, out_shape=(shp, shp, shp), grid=(t // tm, PAIRS),
                          in_specs=[blk(0), blk(PAIRS), blk(2 * PAIRS), pl.BlockSpec((tm, LANES), lambda i, j: (i, 0))],
                          out_specs=(out, out, out), compiler_params=_cp(("arbitrary",) * 2), name=name)(qkv, qkv, qkv, c)


def _causal_bias(tb):
    keep = jnp.arange(tb)[:, None] >= jnp.arange(tb)[None, :]
    return jnp.where(keep, 0.0, NEG_INF).astype(F32)


def _causal_pairs(nb, key_major):
    pairs = [(q, k) for q in range(nb) for k in range(q + 1)]
    if key_major:
        pairs.sort(key=lambda qk: (qk[1], qk[0]))
    return (jnp.array([q for q, _ in pairs], jnp.int32), jnp.array([k for _, k in pairs], jnp.int32))


def _host(side, body, n_lead, arrays, in_specs, out_shapes, out_specs, scratch, grid):
    if side is None:
        return body, tuple(arrays), list(in_specs), list(out_shapes), list(out_specs), list(scratch)
    n_in, n_out, s_in, s_out = len(arrays), len(out_shapes), len(side.arrays), len(side.out_shapes)

    def hosted(*refs):
        ins_end = n_lead + n_in + s_in
        side_in, side_out = refs[n_lead + n_in:ins_end], refs[ins_end + n_out:ins_end + n_out + s_out]
        send, recv = refs[-2:]
        ids = [pl.program_id(ax) for ax in range(len(grid))]
        first, last = ids[0] == 0, ids[0] == grid[0] - 1
        for ax in range(1, len(grid)):
            first, last = first & (ids[ax] == 0), last & (ids[ax] == grid[ax] - 1)

        @pl.when(first)
        def _():
            side.start(side_in, side_out, send, recv)

        body(*refs[:n_lead + n_in], *refs[ins_end:ins_end + n_out], *refs[ins_end + n_out + s_out:-2])

        @pl.when(last)
        def _():
            side.wait(side_in, side_out, send, recv)

    return (hosted, tuple(arrays) + tuple(side.arrays), list(in_specs) + [HBM_SPEC] * s_in,
            list(out_shapes) + side.out_shapes, list(out_specs) + [HBM_SPEC] * s_out, list(scratch) + side.scratch)


def _pair_grid_call(body, tables, arrays, in_specs, out_shapes, out_specs, scratch, side, name):
    grid = (PAIRS, tables[0].shape[0])
    n_out = len(out_shapes)
    body, arrays, in_specs, out_shapes, out_specs, scratch = _host(side, body, 2, arrays, in_specs, out_shapes,
                                                                    out_specs, scratch, grid)
    grid_spec = pltpu.PrefetchScalarGridSpec(num_scalar_prefetch=2, grid=grid, in_specs=in_specs,
                                             out_specs=tuple(out_specs), scratch_shapes=scratch)
    out = pl.pallas_call(body, out_shape=tuple(out_shapes), grid_spec=grid_spec,
                         compiler_params=_cp(("arbitrary",) * 2), name=name)(*tables, *arrays)
    return out[:n_out], out[n_out:]


def _attn_fwd(qa, ka, va, name, side=None):
    t = qa.shape[0]
    tb = min(ATT_BLOCK, t)
    q_tab, k_tab = _causal_pairs(t // tb, key_major=False)

    def body(q_tab_ref, k_tab_ref, q_ref, k_ref, v_ref, bias_ref, o_ref, lse_ref, m0, m1, acc0, acc1):
        qi, kb = q_tab_ref[pl.program_id(1)], k_tab_ref[pl.program_id(1)]
        m_s, acc_s = (m0, m1), (acc0, acc1)

        @pl.when(kb == 0)
        def _():
            for hh in range(2):
                m_s[hh][...] = jnp.full(m_s[hh].shape, NEG_INF, F32)
                acc_s[hh][...] = jnp.zeros(acc_s[hh].shape, F32)

        def step(diagonal):
            for hh in range(2):
                cols = slice(hh * LANES, (hh + 1) * LANES)
                sc = _dot(q_ref[:, cols], k_ref[:, cols], NT)
                if diagonal:
                    sc = sc + bias_ref[...]
                m_prev = m_s[hh][...]
                m_new = jnp.maximum(m_prev, jnp.max(sc, axis=1, keepdims=True))
                p = jnp.exp(sc - m_new)
                acc_s[hh][...] = jnp.exp(m_prev - m_new) * acc_s[hh][...] + _dot(p.astype(BF16), v_ref[:, cols], NN)
                m_s[hh][...] = m_new

        @pl.when(kb < qi)
        def _():
            step(False)

        @pl.when(kb == qi)
        def _():
            step(True)
            lane = lax.broadcasted_iota(jnp.int32, (tb, LANES), 1)
            outs, lses = [], []
            for hh in range(2):
                acc = acc_s[hh][...]
                l = _pick_col(acc, CQ_LANE)
                outs.append(acc / l)
                lses.append(m_s[hh][...] + jnp.log(l))
            o_ref[...] = jnp.where(lane < HEAD_DIM, outs[0], pltpu.roll(outs[1], HEAD_DIM, 1))
            lse_ref[...] = jnp.where(lane == 0, lses[0], jnp.where(lane == 1, lses[1], 0.0))

    qrow = lambda j, s, qt, kt: (qt[s], j)
    krow = lambda j, s, qt, kt: (kt[s], j)
    return _pair_grid_call(
        body, (q_tab, k_tab), (qa, ka, va, _causal_bias(tb)),
        [pl.BlockSpec((tb, 2 * LANES), qrow), pl.BlockSpec((tb, 2 * LANES), krow), pl.BlockSpec((tb, 2 * LANES), krow),
         pl.BlockSpec((tb, tb), lambda j, s, qt, kt: (0, 0))],
        [jax.ShapeDtypeStruct((t, D_ATT), F32), jax.ShapeDtypeStruct((t, PAIRS * LANES), F32)],
        [pl.BlockSpec((tb, LANES), qrow), pl.BlockSpec((tb, LANES), qrow)],
        [pltpu.VMEM((tb, 1), F32)] * 2 + [pltpu.VMEM((tb, LANES), F32)] * 2, side, name)


def _attn_bwd_prep(qa, att, lse, datt, name, tm=512):
    t = qa.shape[0]
    tm = min(tm, t)

    def body(qa_ref, o_ref, lse_ref, do_ref, qb_ref, doa_ref):
        j = pl.program_id(1)
        lane = lax.broadcasted_iota(jnp.int32, (tm, LANES), 1)
        first = lane < HEAD_DIM
        do = do_ref[...]
        prod = do * o_ref[...]
        lse2 = lse_ref[...]
        for hh in range(2):
            cols = slice(hh * LANES, (hh + 1) * LANES)
            delta = jnp.sum(jnp.where(first if hh == 0 else ~first, prod, 0.0), axis=1, keepdims=True)
            doh = jnp.where(first, _to_first_half(do, hh), 0.0)
            doa_ref[:, cols] = _lanes_put(doh, lane, CQ_LANE, [v.astype(F32) for v in _split3(-delta)]).astype(BF16)
            nl = [v.astype(F32) for v in _split3(-_pick_col(lse2, hh))]
            qb_ref[:, cols] = _lanes_put(qa_ref[:, cols].astype(F32), lane, LSE_LANE, nl).astype(BF16)

    wide = pl.BlockSpec((tm, 2 * LANES), lambda i, j: (i, j))
    pair = pl.BlockSpec((tm, LANES), lambda i, j: (i, j))
    shp = jax.ShapeDtypeStruct((t, N_HEADS * LANES), BF16)
    return pl.pallas_call(body, out_shape=(shp, shp), grid=(t // tm, PAIRS), in_specs=[wide, pair, pair, pair],
                          out_specs=(wide, wide), compiler_params=_cp(("arbitrary",) * 2), name=name)(qa, att, lse, datt)


def _attn_bwd(qb, ka, va, doa, name, side=None):
    t = qb.shape[0]
    tb = min(ATT_BLOCK, t)
    nb = t // tb
    q_tab, k_tab = _causal_pairs(nb, key_major=True)

    def body(q_tab_ref, k_tab_ref, q_ref, k_ref, v_ref, do_ref, bias_ref, dq_ref, dk_ref, dv_ref, dk0, dk1, dv0, dv1):
        qi, kb = q_tab_ref[pl.program_id(1)], k_tab_ref[pl.program_id(1)]
        dk_s, dv_s = (dk0, dk1), (dv0, dv1)

        @pl.when(qi == kb)
        def _():
            for ref in dk_s + dv_s:
                ref[...] = jnp.zeros(ref.shape, F32)

        def step(diagonal):
            rows = pl.ds(pl.multiple_of(qi * tb, tb), tb)
            for hh in range(2):
                cols = slice(hh * LANES, (hh + 1) * LANES)
                q, k, do = q_ref[:, cols], k_ref[:, cols], do_ref[:, cols]
                sc = _dot(q, k, NT)
                if diagonal:
                    sc = sc + bias_ref[...]
                p = jnp.exp(sc)
                ds = (p * _dot(do, v_ref[:, cols], NT)).astype(BF16)
                dv_s[hh][...] += _dot(p.astype(BF16), do, TN)
                dk_s[hh][...] += _dot(ds, q, TN)
                dq_new = _dot(ds, k, NN)

                @pl.when(kb == 0)
                def _():
                    dq_ref[rows, cols] = dq_new

                @pl.when(kb > 0)
                def _():
                    dq_ref[rows, cols] += dq_new

        @pl.when(qi == kb)
        def _():
            step(True)

        @pl.when(qi > kb)
        def _():
            step(False)

        @pl.when(qi == nb - 1)
        def _():
            for hh in range(2):
                cols = slice(hh * LANES, (hh + 1) * LANES)
                dk_ref[:, cols] = dk_s[hh][...]
                dv_ref[:, cols] = dv_s[hh][...].astype(BF16)

    qrow = lambda j, s, qt, kt: (qt[s], j)
    krow = lambda j, s, qt, kt: (kt[s], j)
    blk = (tb, 2 * LANES)
    wide = (t, N_HEADS * LANES)
    return _pair_grid_call(
        body, (q_tab, k_tab), (qb, ka, va, doa, _causal_bias(tb)),
        [pl.BlockSpec(blk, qrow), pl.BlockSpec(blk, krow), pl.BlockSpec(blk, krow), pl.BlockSpec(blk, qrow),
         pl.BlockSpec((tb, tb), lambda j, s, qt, kt: (0, 0))],
        [jax.ShapeDtypeStruct(wide, F32), jax.ShapeDtypeStruct(wide, F32), jax.ShapeDtypeStruct(wide, BF16)],
        [pl.BlockSpec((t, 2 * LANES), lambda j, s, qt, kt: (0, j)), pl.BlockSpec(blk, krow), pl.BlockSpec(blk, krow)],
        [pltpu.VMEM((tb, LANES), F32)] * 4, side, name)


def _attn_bwd_post(dqa, dka, dva, name, tm=256):
    t = dqa.shape[0]
    tm = min(tm, t)

    def body(dq_ref, dk_ref, dv_ref, o_ref, dc_ref):
        lane = lax.broadcasted_iota(jnp.int32, (tm, LANES), 1)
        first = lane < HEAD_DIM
        dc = jnp.zeros((tm, LANES), F32)
        for j in range(PAIRS):
            packed = []
            for ref, gain in ((dq_ref, HEAD_DIM ** -0.5), (dk_ref, 1.0), (dv_ref, 1.0)):
                even = ref[:, 2 * j * LANES:(2 * j + 1) * LANES].astype(F32)
                odd = ref[:, (2 * j + 1) * LANES:(2 * j + 2) * LANES].astype(F32)
                packed.append((jnp.where(first, even, pltpu.roll(odd, HEAD_DIM, 1)) * gain).astype(BF16))
                if ref is dq_ref:
                    dc = dc + jnp.where(lane == 2 * j, _pick_col(even, CQ_LANE), 0.0)
                    dc = dc + jnp.where(lane == 2 * j + 1, _pick_col(odd, CQ_LANE), 0.0)
                if ref is dk_ref:
                    dc = dc - jnp.where(lane == 2 * j, _pick_col(even, CK_LANE), 0.0)
                    dc = dc - jnp.where(lane == 2 * j + 1, _pick_col(odd, CK_LANE), 0.0)
            for part, val in enumerate(packed):
                o_ref[:, (part * PAIRS + j) * LANES:(part * PAIRS + j + 1) * LANES] = val
        dc_ref[...] = dc

    wide = _rows(tm, N_HEADS * LANES)
    return pl.pallas_call(body, out_shape=(jax.ShapeDtypeStruct((t, 3 * D_ATT), BF16), jax.ShapeDtypeStruct((t, LANES), F32)),
                          grid=(t // tm,), in_specs=[wide, wide, wide], out_specs=(_rows(tm, 3 * D_ATT), _rows(tm, LANES)),
                          compiler_params=_cp(("arbitrary",)), name=name)(dqa, dka, dva)


GELU_K = 0.7978845608028654
GELU_A = 0.044715


def _gelu(x):
    th = jnp.tanh(GELU_K * (x + GELU_A * x * x * x))
    return 0.5 * x * (1.0 + th), th


def _group_mean_matrix():
    r = jnp.arange(D_GM)[:, None] // HEAD_DIM
    s = jnp.arange(D_GM)[None, :] // HEAD_DIM
    return jnp.where(r == s, 1.0 / HEAD_DIM, 0.0).astype(BF16)


def _group_mean(x, mean_mat):
    hi = x.astype(BF16)
    lo = (x - hi.astype(F32)).astype(BF16)
    return _dot(hi, mean_mat, NN) + _dot(lo, mean_mat, NN)


def _gm_forward_parts(g, w_ref, bias, gain, mean_mat):
    gel, _ = _gelu(g)
    u, vv = gel[:, :D_GM], gel[:, D_GM:]
    mu = _group_mean(vv, mean_mat)
    d = vv - mu
    rstd = lax.rsqrt(_group_mean(d * d, mean_mat) + EPS)
    xhat = d * rstd
    vn = (xhat * gain).astype(BF16)
    first = lax.broadcasted_iota(jnp.int32, (CHUNK, LANES), 1) < HEAD_DIM
    tri = lax.broadcasted_iota(jnp.int32, (CHUNK, CHUNK), 0) >= lax.broadcasted_iota(jnp.int32, (CHUNK, CHUNK), 1)
    wm = [jnp.where(tri, w_ref[grp], 0.0).astype(BF16) for grp in range(w_ref.shape[0])]
    chunks = []
    for ck in range(g.shape[0] // CHUNK):
        parts = []
        for jp in range(D_GM // LANES):
            vp = vn[ck * CHUNK:(ck + 1) * CHUNK, jp * LANES:(jp + 1) * LANES]
            parts.append(jnp.where(first, _dot(wm[2 * jp], vp, NN), _dot(wm[2 * jp + 1], vp, NN)))
        chunks.append(jnp.concatenate(parts, axis=1) + bias)
    return u, xhat, rstd, vn, jnp.concatenate(chunks, axis=0), wm


GM_ROWS = 512


def _gmlp_fwd(zgf, w_s, bias_full, gain, mean_mat, name):
    t = zgf.shape[0]
    tm = min(GM_ROWS, t)

    def body(g_ref, w_ref, b_ref, gain_ref, mm_ref, o_ref):
        u, _, _, _, mixed, _ = _gm_forward_parts(g_ref[...], w_ref, b_ref[...], gain_ref[...], mm_ref[...])
        o_ref[...] = u * mixed

    return pl.pallas_call(body, out_shape=jax.ShapeDtypeStruct((t, D_GM), F32), grid=(t // tm,),
                          in_specs=[_rows(tm, 2 * D_GM), _full(w_s.shape), _full((CHUNK, D_GM)), _full((1, D_GM)),
                                    _full((D_GM, D_GM))],
                          out_specs=_rows(tm, D_GM), compiler_params=_cp(("arbitrary",)),
                          name=name)(zgf, w_s, bias_full, gain, mean_mat)


def _gmlp_bwd(zgf, dgm, w_s, bias_full, gain, mean_mat, name, side=None):
    t = zgf.shape[0]
    tm = min(GM_ROWS, t)

    def body(g_ref, d_ref, w_ref, b_ref, gain_ref, mm_ref, dg_ref, dw_ref, dmix_ref, dgain_ref):
        g, gain, mean_mat = g_ref[...], gain_ref[...], mm_ref[...]
        u, xhat, rstd, vn, mixed, wm = _gm_forward_parts(g, w_ref, b_ref[...], gain, mean_mat)
        dgm_v = d_ref[...]
        du = dgm_v * mixed
        dmixed = dgm_v * u
        dm_b = dmixed.astype(BF16)
        first = lax.broadcasted_iota(jnp.int32, (CHUNK, LANES), 1) < HEAD_DIM
        tri = lax.broadcasted_iota(jnp.int32, (CHUNK, CHUNK), 0) >= lax.broadcasted_iota(jnp.int32, (CHUNK, CHUNK), 1)

        @pl.when(pl.program_id(0) == 0)
        def _():
            dw_ref[...] = jnp.zeros(dw_ref.shape, F32)
            dmix_ref[...] = jnp.zeros(dmix_ref.shape, F32)
            dgain_ref[...] = jnp.zeros(dgain_ref.shape, F32)

        dw = [jnp.zeros((CHUNK, CHUNK), F32) for _ in wm]
        dmix = jnp.zeros((CHUNK, D_GM), F32)
        dvn_chunks = []
        for ck in range(tm // CHUNK):
            rows = slice(ck * CHUNK, (ck + 1) * CHUNK)
            dmix = dmix + dmixed[rows]
            dvn_parts = []
            for jp in range(D_GM // LANES):
                vp = vn[rows, jp * LANES:(jp + 1) * LANES]
                dmp = dm_b[rows, jp * LANES:(jp + 1) * LANES]
                halves = []
                for hh in range(2):
                    sel = first if hh == 0 else ~first
                    grp = 2 * jp + hh
                    dw[grp] = dw[grp] + _dot(jnp.where(sel, dmp, jnp.zeros_like(dmp)), vp, NT)
                    halves.append(_dot(wm[grp], dmp, TN))
                dvn_parts.append(jnp.where(first, halves[0], halves[1]))
            dvn_chunks.append(jnp.concatenate(dvn_parts, axis=1))
        dvn = jnp.concatenate(dvn_chunks, axis=0)
        for grp, dwg in enumerate(dw):
            dw_ref[grp] += jnp.where(tri, dwg, 0.0)
        dmix_ref[...] += dmix
        dgain_ref[...] += jnp.sum(dvn * xhat, axis=0, keepdims=True)
        dxhat = dvn * gain
        m1 = _group_mean(dxhat, mean_mat)
        m2 = _group_mean(dxhat * xhat, mean_mat)
        dvv = rstd * (dxhat - m1 - xhat * m2)
        gel, th = _gelu(g)
        dgel = 0.5 * (1.0 + th) + 0.5 * g * (1.0 - th * th) * GELU_K * (1.0 + 3.0 * GELU_A * g * g)
        dg_ref[...] = (jnp.concatenate([du, dvv], axis=1) * dgel).astype(BF16)

    grid = (t // tm,)
    body, arrays, in_specs, out_shapes, out_specs, scratch = _host(
        side, body, 0, (zgf, dgm, w_s, bias_full, gain, mean_mat),
        [_rows(tm, 2 * D_GM), _rows(tm, D_GM), _full(w_s.shape), _full((CHUNK, D_GM)), _full((1, D_GM)),
         _full((D_GM, D_GM))],
        [jax.ShapeDtypeStruct((t, 2 * D_GM), BF16), jax.ShapeDtypeStruct(w_s.shape, F32),
         jax.ShapeDtypeStruct((CHUNK, D_GM), F32), jax.ShapeDtypeStruct((1, D_GM), F32)],
        [_rows(tm, 2 * D_GM), _full(w_s.shape), _full((CHUNK, D_GM)), _full((1, D_GM))], [], grid)
    out = pl.pallas_call(body, out_shape=tuple(out_shapes), grid=grid, in_specs=in_specs, out_specs=tuple(out_specs),
                         scratch_shapes=scratch, compiler_params=_cp(("arbitrary",)), name=name)(*arrays)
    return out[:4], out[4:]


def _row_tile(r, c, budget=1 << 19):
    best = None
    for tr in range(8, r + 1, 8):
        if r % tr == 0 and tr * c <= budget:
            best = tr
    return best if best is not None else r


def _adamw(w, g, m, v, name, side=None):
    nl, r, c = w.shape
    tr = _row_tile(r, c, 1 << 18)
    c1 = 1.0 - ADAM_B1 ** ADAM_STEP
    c2 = 1.0 - ADAM_B2 ** ADAM_STEP

    def body(w_ref, g_ref, m_ref, v_ref, d_ref, mo_ref, vo_ref):
        gv = g_ref[...]
        mn = ADAM_B1 * m_ref[...] + (1.0 - ADAM_B1) * gv
        vn = ADAM_B2 * v_ref[...] + (1.0 - ADAM_B2) * jnp.square(gv)
        mo_ref[...] = mn
        vo_ref[...] = vn
        d_ref[...] = -ADAM_LR * ((mn / c1) / (jnp.sqrt(vn / c2) + ADAM_EPS) + ADAM_WD * w_ref[...])

    spec = pl.BlockSpec((None, tr, c), lambda l, i: (l, i, 0))
    shp = jax.ShapeDtypeStruct(w.shape, F32)
    grid = (nl, r // tr)
    body, arrays, in_specs, out_shapes, out_specs, scratch = _host(side, body, 0, (w, g, m, v), [spec] * 4, [shp] * 3,
                                                                    [spec] * 3, [], grid)
    out = pl.pallas_call(body, out_shape=tuple(out_shapes), grid=grid, in_specs=in_specs, out_specs=tuple(out_specs),
                         scratch_shapes=scratch, compiler_params=_cp(("arbitrary",) * 2), name=name)(*arrays)
    return out[:3], out[3:]


def _add_sibling(g, recv, c_idx, wire_dtype, name):
    nj, _, h, c = g.shape
    tr = _row_tile(h, c)

    def body(c_ref, g_ref, r_ref, o_ref):
        o_ref[...] = (g_ref[...] + r_ref[...]).astype(wire_dtype)

    grid_spec = pltpu.PrefetchScalarGridSpec(
        num_scalar_prefetch=1, grid=(nj, h // tr),
        in_specs=[pl.BlockSpec((None, None, tr, c), lambda j, i, c_ref: (j, c_ref[0], i, 0)),
                  pl.BlockSpec((None, tr, c), lambda j, i, c_ref: (j, i, 0))],
        out_specs=pl.BlockSpec((None, tr, c), lambda j, i, c_ref: (j, i, 0)))
    return pl.pallas_call(body, out_shape=jax.ShapeDtypeStruct((nj, h, c), wire_dtype), grid_spec=grid_spec,
                          compiler_params=_cp(("arbitrary",) * 2), name=name)(c_idx, g, recv)


def _add_chips(own, parts, place, name):
    _, h, c = own.shape
    tr = _row_tile(h, c)

    def body(p_ref, o_ref, a_ref, b_ref, c_ref, out_ref):
        out_ref[...] = ((o_ref[...].astype(F32) + a_ref[...].astype(F32)) + b_ref[...].astype(F32)) + c_ref[...].astype(F32)

    def other(k):
        return pl.BlockSpec((None, tr, c), lambda i, p_ref: (jnp.bitwise_xor(p_ref[0], k), i, 0))

    grid_spec = pltpu.PrefetchScalarGridSpec(
        num_scalar_prefetch=1, grid=(h // tr,),
        in_specs=[pl.BlockSpec((None, tr, c), lambda i, p_ref: (p_ref[0], i, 0)), other(1), other(2), other(3)],
        out_specs=pl.BlockSpec((None, tr, c), lambda i, p_ref: (p_ref[1], i, 0)))
    return pl.pallas_call(body, out_shape=jax.ShapeDtypeStruct((2, h, c), F32), grid_spec=grid_spec,
                          compiler_params=_cp(("arbitrary",)), name=name)(place, own, parts, parts, parts)


HBM_SPEC = pl.BlockSpec(memory_space=pltpu.HBM)


def _place():
    x, y, c = lax.axis_index("x"), lax.axis_index("y"), lax.axis_index("c")
    chips = [(1 - x, y), (x, 1 - y), (1 - x, 1 - y)]
    return x, y, c, 2 * x + y, chips, [2 * px + py for px, py in chips]


def _remote(src, dst, send_sem, recv_sem, dev):
    return pltpu.make_async_remote_copy(src_ref=src, dst_ref=dst, send_sem=send_sem, recv_sem=recv_sem,
                                        device_id=dev, device_id_type=MESH)


def _comm_call(body, arrays, out_shapes, sems, name, aliases=None):
    n = len(arrays)
    return pl.pallas_call(
        body, out_shape=tuple(out_shapes), in_specs=[HBM_SPEC] * n, out_specs=tuple([HBM_SPEC] * len(out_shapes)),
        scratch_shapes=[pltpu.SemaphoreType.DMA(s) for s in sems], input_output_aliases=aliases or {},
        compiler_params=pltpu.CompilerParams(has_side_effects=True), name=name)(*arrays)


class _SideCopies:
    def __init__(self, arrays, out_shapes, sem_shape, sends, recvs):
        self.arrays, self.out_shapes, self.sem_shape = list(arrays), list(out_shapes), sem_shape
        self.sends, self.recvs = sends, recvs

    @property
    def scratch(self):
        return [pltpu.SemaphoreType.DMA(self.sem_shape), pltpu.SemaphoreType.DMA(self.sem_shape)]

    def start(self, ins, outs, send, recv):
        for cp in self.sends(ins, outs, send, recv):
            cp.start()

    def wait(self, ins, outs, send, recv):
        for cp in self.recvs(ins, outs, send, recv):
            cp.wait_recv()
        for cp in self.sends(ins, outs, send, recv):
            cp.wait_send()

    def run(self, name):
        n = len(self.arrays)

        def body(*refs):
            ins, outs = refs[:n], refs[n:n + len(self.out_shapes)]
            send, recv = refs[-2:]
            self.start(ins, outs, send, recv)
            self.wait(ins, outs, send, recv)

        return _comm_call(body, self.arrays, self.out_shapes, [self.sem_shape] * 2, name)


def _gather_over_chips(shards):
    n = len(shards)

    def sends(ins, outs, send, recv):
        x, y, c, me, chips, _ = _place()
        over_ici = [_remote(ins[a].at[c], outs[a].at[me, c], send.at[a, k], recv.at[a, k], (px, py, c))
                    for a in range(n) for k, (px, py) in enumerate(chips)]
        return over_ici + [_remote(ins[a], outs[a].at[me], send.at[a, 3], recv.at[a, 3], (x, y, 1 - c)) for a in range(n)]

    def recvs(ins, outs, send, recv):
        x, y, c, me, _, cidx = _place()
        slots = [[outs[a].at[cidx[k], c] for k in range(3)] + [outs[a].at[me]] for a in range(n)]
        return [_remote(blk, blk, send.at[a, k], recv.at[a, k], (x, y, 1 - c))
                for a in range(n) for k, blk in enumerate(slots[a])]

    out_shapes = [jax.ShapeDtypeStruct((N_CHIPS,) + s.shape, s.dtype) for s in shards]
    return _SideCopies(shards, out_shapes, (n, 4), sends, recvs)


def _gather_finish(partial, name):
    n = len(partial)

    def body(*refs):
        part, outs = refs[:n], refs[n:2 * n]
        send, recv = refs[2 * n:]
        x, y, c, _, _, cidx = _place()
        sib = (x, y, 1 - c)
        cps = [_remote(part[a].at[cidx[k], c], outs[a].at[cidx[k], c], send.at[a, k], recv.at[a, k], sib)
               for a in range(n) for k in range(3)]
        for cp in cps:
            cp.start()
        for a in range(n):
            for k in range(3):
                blk = outs[a].at[cidx[k], 1 - c]
                _remote(blk, blk, send.at[a, k], recv.at[a, k], sib).wait_recv()
        for cp in cps:
            cp.wait_send()

    out_shapes = [jax.ShapeDtypeStruct(p.shape, p.dtype) for p in partial]
    return _comm_call(body, list(partial), out_shapes, [(n, 3), (n, 3)], name, aliases={a: a for a in range(n)})


def _sibling_halves(grads):
    n = len(grads)

    def copies(ins, outs, send, recv):
        x, y, c, _, _, _ = _place()
        return [_remote(ins[a].at[j, 1 - c], outs[a].at[j], send.at[a, j], recv.at[a, j], (x, y, 1 - c))
                for a in range(n) for j in range(N_CHIPS)]

    out_shapes = [jax.ShapeDtypeStruct((g.shape[0],) + g.shape[2:], g.dtype) for g in grads]
    return _SideCopies(grads, out_shapes, (n, N_CHIPS), copies, copies)


def _scatter_over_chips(sums):
    n = len(sums)

    def sends(ins, outs, send, recv):
        _, _, c, me, chips, cidx = _place()
        return [_remote(ins[a].at[cidx[k]], outs[a].at[me], send.at[a, k], recv.at[a, k], (px, py, c))
                for a in range(n) for k, (px, py) in enumerate(chips)]

    def recvs(ins, outs, send, recv):
        x, y, c, _, _, cidx = _place()
        return [_remote(outs[a].at[cidx[k]], outs[a].at[cidx[k]], send.at[a, k], recv.at[a, k], (x, y, 1 - c))
                for a in range(n) for k in range(3)]

    return _SideCopies(sums, [jax.ShapeDtypeStruct(s.shape, s.dtype) for s in sums], (n, 3), sends, recvs)


def _exchange_halves(halves, name):
    n = len(halves)

    def body(*refs):
        ins, bufs = refs[:n], refs[n:2 * n]
        send, recv = refs[2 * n:]
        x, y, c, _, _, _ = _place()
        sib = (x, y, 1 - c)
        cps = []
        for a in range(n):
            cp = _remote(ins[a].at[c], bufs[a].at[c], send.at[a], recv.at[a], sib)
            cp.start()
            cps.append(cp)
        for a in range(n):
            blk = bufs[a].at[1 - c]
            _remote(blk, blk, send.at[a], recv.at[a], sib).wait_recv()
        for cp in cps:
            cp.wait_send()

    out_shapes = [jax.ShapeDtypeStruct(s.shape, s.dtype) for s in halves]
    return _comm_call(body, halves, out_shapes, [(n,), (n,)], name, aliases={a: a for a in range(n)})


def _gather_chips(slices):
    n = len(slices)

    def sends(ins, outs, send, recv):
        x, y, c, me, chips, _ = _place()
        return [_remote(ins[a], outs[a].at[me], send.at[a, k], recv.at[a, k], dev)
                for a in range(n) for k, dev in enumerate([(px, py, c) for px, py in chips] + [(x, y, 1 - c)])]

    def recvs(ins, outs, send, recv):
        x, y, c, me, _, cidx = _place()
        return [_remote(outs[a].at[slot], outs[a].at[slot], send.at[a, k], recv.at[a, k], (x, y, 1 - c))
                for a in range(n) for k, slot in enumerate(cidx + [me])]

    out_shapes = [jax.ShapeDtypeStruct((N_CHIPS,) + s.shape, s.dtype) for s in slices]
    return _SideCopies(slices, out_shapes, (n, 4), sends, recvs)


def _rs_siblings(grads):
    return _sibling_halves([g.reshape(g.shape[0], 2, g.shape[1] // 2, g.shape[2]) for g in grads])


def _rs_chips(siblings, recv, c_idx, wire_dtype, tag):
    wire = wire_dtype if isinstance(wire_dtype, list) else [wire_dtype] * len(recv)
    return _scatter_over_chips([_add_sibling(g, r, c_idx, w, "rs_add_sibling_" + tag)
                                for g, r, w in zip(siblings.arrays, recv, wire)])


def _rs_end(scatter, parts, place, tag):
    halves = [_add_chips(s, p, place, "rs_add_chips_" + tag) for s, p in zip(scatter.arrays, parts)]
    both = _exchange_halves(halves, "rs_halves_" + tag)
    return [b.reshape(b.shape[0] * b.shape[1], b.shape[2]) for b in both]


SMALL_ORDER = ("gm_w_s", "mix_pre_norm", "mix_post_norm", "mix_out_norm", "ffn_pre_norm", "ffn_post_norm", "ple_norm",
               "gm_v_norm", "gm_b_s", "b_forget")
SMALL_ROWS_MULTIPLE = 64
SMALL_BLOCK = 8 * LANES


def _chip_columns(pieces, width):
    out = []
    for j in range(N_CHIPS):
        lo, hi, parts, off = j * width, (j + 1) * width, [], 0
        for piece in pieces:
            a, b = max(lo, off), min(hi, off + piece.shape[1])
            if a < b:
                parts.append(piece[:, a - off:b - off])
            off += piece.shape[1]
        out.append(parts[0] if len(parts) == 1 else jnp.concatenate(parts, axis=1))
    return jnp.stack(out)


def _columns(sliced, lo, hi):
    width, parts = sliced.shape[2], []
    for j in range(N_CHIPS):
        a, b = max(lo, j * width), min(hi, (j + 1) * width)
        if a < b:
            parts.append(sliced[j][:, a - j * width:b - j * width])
    return parts[0] if len(parts) == 1 else jnp.concatenate(parts, axis=1)


def _pack_small(parts):
    blocks = []
    for nme in SMALL_ORDER:
        v = parts[nme].reshape(-1)
        pad = (-v.shape[0]) % SMALL_BLOCK
        blocks.append((jnp.pad(v, (0, pad)) if pad else v).reshape(-1, LANES))
    rows = sum(b.shape[0] for b in blocks)
    if rows % SMALL_ROWS_MULTIPLE:
        blocks.append(jnp.zeros((SMALL_ROWS_MULTIPLE - rows % SMALL_ROWS_MULTIPLE, LANES), F32))
    return jnp.concatenate(blocks, axis=0)


def _unpack_small(packed, shapes):
    out, row = {}, 0
    for nme in SMALL_ORDER:
        size = 1
        for s in shapes[nme]:
            size *= s
        rows = -(-size // SMALL_BLOCK) * (SMALL_BLOCK // LANES)
        block = packed[row:row + rows]
        out[nme] = (block if size % SMALL_BLOCK == 0 else block.reshape(-1)[:size]).reshape(shapes[nme])
        row += rows
    return out


def kernel(x, p, mix_pre_norm, mix_post_norm, w_in, b_forget, gm_v_norm, gm_w_s, gm_b_s, mix_out_norm, w_out, ffn_pre_norm, ffn_post_norm, w_ffn_in, w_ffn_out, w_ple, ple_norm, w_ple_gate, loss_target, m_mix_pre_norm, m_mix_post_norm, m_w_in, m_b_forget, m_gm_v_norm, m_gm_w_s, m_gm_b_s, m_mix_out_norm, m_w_out, m_ffn_pre_norm, m_ffn_post_norm, m_w_ffn_in, m_w_ffn_out, m_w_ple, m_ple_norm, m_w_ple_gate, v_mix_pre_norm, v_mix_post_norm, v_w_in, v_b_forget, v_gm_v_norm, v_gm_w_s, v_gm_b_s, v_mix_out_norm, v_w_out, v_ffn_pre_norm, v_ffn_post_norm, v_w_ffn_in, v_w_ffn_out, v_w_ple, v_ple_norm, v_w_ple_gate):
    weights = dict(mix_pre_norm=mix_pre_norm, mix_post_norm=mix_post_norm, w_in=w_in, b_forget=b_forget,
                   gm_v_norm=gm_v_norm, gm_w_s=gm_w_s, gm_b_s=gm_b_s, mix_out_norm=mix_out_norm, w_out=w_out,
                   ffn_pre_norm=ffn_pre_norm, ffn_post_norm=ffn_post_norm, w_ffn_in=w_ffn_in, w_ffn_out=w_ffn_out,
                   w_ple=w_ple, ple_norm=ple_norm, w_ple_gate=w_ple_gate)
    mom_m = dict(mix_pre_norm=m_mix_pre_norm, mix_post_norm=m_mix_post_norm, w_in=m_w_in, b_forget=m_b_forget,
                 gm_v_norm=m_gm_v_norm, gm_w_s=m_gm_w_s, gm_b_s=m_gm_b_s, mix_out_norm=m_mix_out_norm, w_out=m_w_out,
                 ffn_pre_norm=m_ffn_pre_norm, ffn_post_norm=m_ffn_post_norm, w_ffn_in=m_w_ffn_in,
                 w_ffn_out=m_w_ffn_out, w_ple=m_w_ple, ple_norm=m_ple_norm, w_ple_gate=m_w_ple_gate)
    mom_v = dict(mix_pre_norm=v_mix_pre_norm, mix_post_norm=v_mix_post_norm, w_in=v_w_in, b_forget=v_b_forget,
                 gm_v_norm=v_gm_v_norm, gm_w_s=v_gm_w_s, gm_b_s=v_gm_b_s, mix_out_norm=v_mix_out_norm, w_out=v_w_out,
                 ffn_pre_norm=v_ffn_pre_norm, ffn_post_norm=v_ffn_post_norm, w_ffn_in=v_w_ffn_in,
                 w_ffn_out=v_w_ffn_out, w_ple=v_w_ple, ple_norm=v_ple_norm, w_ple_gate=v_w_ple_gate)
    big = ("w_in", "w_out", "w_ffn_in", "w_ffn_out", "w_ple", "w_ple_gate")
    depth = w_in.shape[0]
    t, d = x.shape[1], x.shape[2]
    d_ff = w_ffn_out.shape[1] * N_CHIPS
    c_idx = lax.axis_index("c").astype(jnp.int32).reshape(1)
    place = jnp.stack([2 * lax.axis_index("x") + lax.axis_index("y"), lax.axis_index("c")]).astype(jnp.int32)
    h = x[0]
    target = loss_target[0]
    mean_mat = _group_mean_matrix()

    def row(a, i):
        return a[i].reshape(1, -1)

    def shards_of(i):
        shards = [weights[nme][i].astype(BF16) for nme in big]
        return [s.reshape(2, s.shape[0] // 2, s.shape[1]) for s in shards]

    saved = []
    hn = _norm_cast(h, row(mix_pre_norm, 0), "norm_first")
    by_cols = lambda g: g.transpose(1, 0, 2).reshape(g.shape[1], N_CHIPS * g.shape[2])
    by_rows = lambda g: g.reshape(N_CHIPS * g.shape[1], g.shape[2])
    whole = lambda g: g.reshape(N_CHIPS, g.shape[2] * 2, g.shape[3])
    gather = _gather_over_chips(shards_of(0)[:1])
    w_in_g = _gather_finish(gather.run("gather_first"), "gather_finish_first")[0]
    for i in range(depth):
        w_in_c = whole(w_in_g)
        w_qkv = _columns(w_in_c, 0, 3 * D_ATT)
        w_gf = jnp.concatenate([_columns(w_in_c, 3 * D_ATT + N_HEADS, N_CHIPS * w_in_c.shape[2]),
                                _columns(w_in_c, 3 * D_ATT, 3 * D_ATT + N_HEADS),
                                jnp.zeros((d, LANES - N_HEADS), BF16)], axis=1)
        b_pad = jnp.pad(b_forget[i], (0, LANES - N_HEADS)).reshape(1, LANES)
        bias_full = jnp.repeat(gm_b_s[i].T, HEAD_DIM, axis=1)
        gain_v = row(gm_v_norm, i)

        qkv = _mm(hn, w_qkv, "nn", BF16, "mm_qkv")
        zgf = _mm(hn, w_gf, "nn", F32, "mm_gf", tn_cap=384)
        qa, ka, va = _attn_prep(qkv, _gate_fwd(zgf, b_pad, "gate_fwd"), "attn_prep")
        last = i + 1 == depth
        gather = _gather_over_chips(shards_of(i)[1:] + ([] if last else shards_of(i + 1)[:1]))
        (att, lse), partial = _attn_fwd(qa, ka, va, "attn_fwd_last" if last else "attn_fwd", gather)
        gathered = _gather_finish(partial, "gather_finish_last" if last else "gather_finish")
        w_out_f, w_fi_f, w_fo_f = by_rows(whole(gathered[0])), by_cols(whole(gathered[1])), by_rows(whole(gathered[2]))
        w_ple_f, w_pg_f = by_cols(whole(gathered[3])), by_rows(whole(gathered[4]))
        w_in_g = None if last else gathered[5]
        gm = _gmlp_fwd(zgf, gm_w_s[i], bias_full, gain_v, mean_mat, "gmlp_fwd")
        mixed = _mixnorm_fwd(att, gm, row(mix_out_norm, i), "mixnorm_fwd")
        o, h1, hn2 = _resid_norm(h, _Prod((mixed, w_out_f, "nn")), row(mix_post_norm, i), row(ffn_pre_norm, i),
                                 "out_resid")
        s, ab = _ffn_in(hn2, w_fi_f, "ffn_in")
        o2, h2, hr = _resid_norm(h1, _Prod((s, w_fo_f, "nn")), row(ffn_post_norm, i), None, "ffn_out_resid")
        g_next = row(mix_pre_norm, i + 1) if i + 1 < depth else row(mix_pre_norm, 0)
        pe, gl, h3, hn_next = _ple_fwd(h2, _Prod((p[i, 0], w_ple_f, "nn")), _Prod((hr, w_pg_f, "nn")),
                                       row(ple_norm, i), g_next, "ple_fwd")
        saved.append(dict(h=h, hn=hn, qa=qa, ka=ka, va=va, zgf=zgf, att=att, lse=lse, gm=gm, mixed=mixed,
                          o=o, h1=h1, hn2=hn2, ab=ab, s=s, o2=o2, h2=h2, hr=hr, pe=pe, gl=gl, w_qkv=w_qkv, w_gf=w_gf,
                          w_out=w_out_f, w_fi=w_fi_f, w_fo=w_fo_f, w_pg=w_pg_f, b_pad=b_pad, bias_full=bias_full,
                          gain_v=gain_v))
        h, hn = h3, hn_next

    dh, loss_blk = _loss_head(h, target, "loss_head")
    loss = lax.psum(loss_blk[0, 0], ("x", "y", "c"))

    small = {nme: [None] * depth for nme in SMALL_ORDER}
    big_grads = {nme: [None] * depth for nme in big}
    waiting = []
    for i in reversed(range(depth)):
        sv = saved[i]
        dgl, dpe, small["ple_norm"][i] = _ple_bwd(dh, sv["pe"], sv["gl"], row(ple_norm, i), "ple_bwd")
        g_pg = _mm(sv["hr"], dgl, "tn", F32, "mm_dw_ple_gate").reshape(N_CHIPS, -1, d)
        g_ple = _mm(p[i, 0], dpe, "tn", F32, "mm_dw_ple", chip_split=True)
        dh2, do2, small["ffn_post_norm"][i] = _join(dh, sv["h2"], None, _Prod((dgl, sv["w_pg"], "nt")), sv["o2"],
                                                    row(ffn_post_norm, i), "join_ple")
        dab = _ffn_bwd(do2, sv["w_fo"], sv["ab"], "ffn_bwd")
        g_fo = _mm(sv["s"], do2, "tn", F32, "mm_dw_ffn_out", tm=256).reshape(N_CHIPS, -1, d)
        g_fi = _mm_halves_tn(sv["hn2"], dab, "mm_dw_ffn_in")
        dh1, do, small["ffn_pre_norm"][i], small["mix_post_norm"][i] = _join(
            dh2, sv["h1"], row(ffn_pre_norm, i), _Prod((dab, sv["w_fi"], "nt", 0), (dab, sv["w_fi"], "nt", 1)), sv["o"],
            row(mix_post_norm, i), "join_ffn", tm=256)
        g_out = _mm(sv["mixed"], do, "tn", F32, "mm_dw_out").reshape(N_CHIPS, -1, d)
        datt, dgm, small["mix_out_norm"][i] = _mixnorm_bwd(sv["att"], sv["gm"], row(mix_out_norm, i),
                                                          _Prod((do, sv["w_out"], "nt")), "mixnorm_bwd")
        batch = waiting + [("w_out", i, g_out), ("w_ffn_in", i, g_fi), ("w_ffn_out", i, g_fo), ("w_ple", i, g_ple),
                           ("w_ple_gate", i, g_pg)]
        tag = "layer" if waiting else "top"
        siblings = _rs_siblings([g for _, _, g in batch])
        (dg, small["gm_w_s"][i], dmix_sum, small["gm_v_norm"][i]), recv = _gmlp_bwd(
            sv["zgf"], dgm, gm_w_s[i], sv["bias_full"], sv["gain_v"], mean_mat, "gmlp_bwd_" + tag, siblings)
        small["gm_b_s"][i] = dmix_sum.reshape(CHUNK, N_HEADS, HEAD_DIM).sum(-1).T
        qb, doa = _attn_bwd_prep(sv["qa"], sv["att"], sv["lse"], datt, "attn_bwd_prep")
        scatter = _rs_chips(siblings, recv, c_idx, BF16, tag)
        (dqa, dka, dva), parts = _attn_bwd(qb, sv["ka"], sv["va"], doa, "attn_bwd_" + tag, scatter)
        for (nme, layer, _), g in zip(batch, _rs_end(scatter, parts, place, tag)):
            big_grads[nme][layer] = g
        dqkv, dc = _attn_bwd_post(dqa, dka, dva, "attn_bwd_post")
        dfl, db = _gate_bwd(dc, sv["zgf"], sv["b_pad"], "gate_bwd")
        small["b_forget"][i] = db[0, :N_HEADS]
        dgf = jnp.concatenate([dg, dfl], axis=1)
        g_qkv = _mm(sv["hn"], dqkv, "tn", F32, "mm_dw_qkv")
        g_gf = _mm(sv["hn"], dgf, "tn", F32, "mm_dw_gf", tn_cap=384)
        g_in = _chip_columns([g_qkv, g_gf[:, 2 * D_GM:2 * D_GM + N_HEADS], g_gf[:, :2 * D_GM]], w_in.shape[2])
        dh, small["mix_pre_norm"][i] = _join(dh1, sv["h"], row(mix_pre_norm, i),
                                             _Prod((dqkv, sv["w_qkv"], "nt"), (dgf, sv["w_gf"], "nt")), None, None, "join_mix")
        waiting = [("w_in", i, g_in)]
    grad_x = dh.reshape(1, t, d)

    grads, deltas, new_m, new_v = {}, {}, {}, {}

    def update(nme, side=None):
        grads[nme] = jnp.stack(big_grads[nme]).reshape(weights[nme].shape)
        (deltas[nme], new_m[nme], new_v[nme]), moved = _adamw(weights[nme], grads[nme], mom_m[nme], mom_v[nme],
                                                             "adamw_" + nme, side)
        return moved

    small_shapes = {nme: weights[nme].shape for nme in SMALL_ORDER}
    small_part = _pack_small({nme: jnp.stack([g.reshape(small_shapes[nme][1:]) for g in small[nme]])
                              for nme in SMALL_ORDER})
    rows_small = small_part.shape[0]
    siblings = _rs_siblings([waiting[0][2], small_part.reshape(N_CHIPS, rows_small // N_CHIPS, LANES)])
    scatter = _rs_chips(siblings, siblings.run("rs_sibling_tail"), c_idx, [BF16, F32], "tail")
    big_grads["w_in"][0], small_slice = _rs_end(scatter, update("w_ffn_in", scatter), place, "tail")
    small_all = update("w_ffn_out", _gather_chips([small_slice]))[0].reshape(1, rows_small, LANES)
    for nme in ("w_in", "w_out", "w_ple", "w_ple_gate"):
        update(nme)
    (sd, sm, sv_), _ = _adamw(_pack_small({n_: weights[n_] for n_ in SMALL_ORDER})[None], small_all,
                              _pack_small({n_: mom_m[n_] for n_ in SMALL_ORDER})[None],
                              _pack_small({n_: mom_v[n_] for n_ in SMALL_ORDER})[None], "adamw_small")
    grads.update(_unpack_small(small_all[0], small_shapes))
    deltas.update(_unpack_small(sd[0], small_shapes))
    new_m.update(_unpack_small(sm[0], small_shapes))
    new_v.update(_unpack_small(sv_[0], small_shapes))

    order = ("mix_pre_norm", "mix_post_norm", "w_in", "b_forget", "gm_v_norm", "gm_w_s", "gm_b_s", "mix_out_norm",
             "w_out", "ffn_pre_norm", "ffn_post_norm", "w_ffn_in", "w_ffn_out", "w_ple", "ple_norm", "w_ple_gate")
    return (loss, grad_x, *[grads[n_] for n_ in order], *[deltas[n_] for n_ in order], *[new_m[n_] for n_ in order],
            *[new_v[n_] for n_ in order])
```

```python
import functools

import jax
import jax.numpy as jnp
from jax import lax
from jax.experimental import pallas as pl
from jax.experimental.pallas import tpu as pltpu

F32 = jnp.float32
BF16 = jnp.bfloat16
MESH = pl.DeviceIdType.MESH
HIGHEST = lax.Precision.HIGHEST

EPS = 1e-6
NEG_INF = -1e30
N_HEADS = 8
HEAD_DIM = 64
D_ATT = N_HEADS * HEAD_DIM
D_GM = 512
CHUNK = 128
LANES = 128
N_CHIPS = 4
ADAM_LR = 0.001
ADAM_B1 = 0.9
ADAM_B2 = 0.999
ADAM_EPS = 1e-08
ADAM_WD = 0.01
ADAM_STEP = 10
VMEM_LIMIT = 56 * 1024 * 1024


def _cp(sem=None):
    return pltpu.CompilerParams(dimension_semantics=sem, vmem_limit_bytes=VMEM_LIMIT)


def _full(shape):
    return pl.BlockSpec(shape, lambda *_: (0,) * len(shape))


def _rows(tm, width, col_block=0):
    return pl.BlockSpec((tm, width), lambda i: (i, col_block))


def _dot(a, b, dims, precision=None):
    return lax.dot_general(a, b, (dims, ((), ())), preferred_element_type=F32, precision=precision)


NN = ((1,), (0,))
NT = ((1,), (1,))
TN = ((0,), (0,))


def _pick(n, cap):
    best = None
    for t in range(LANES, min(n, cap) + 1, LANES):
        if n % t == 0:
            best = t
    assert best is not None, (n, cap)
    return best


def _mm(a, b, mode, out_dtype, name, tm=None, tn_cap=1024, chip_split=False):
    dims = {"nn": NN, "nt": NT, "tn": TN}[mode]
    if mode == "tn":
        k, m = a.shape
    else:
        m, k = a.shape
    n = b.shape[0] if mode == "nt" else b.shape[1]
    if tm is None:
        tm = 512 if mode == "tn" else 1024
    tm = min(tm, m)
    tn = n // N_CHIPS if chip_split else _pick(n, tn_cap)
    assert m % tm == 0 and n % tn == 0

    def body(a_ref, b_ref, o_ref):
        o_ref[...] = _dot(a_ref[...].astype(BF16), b_ref[...].astype(BF16), dims).astype(out_dtype)

    a_spec = pl.BlockSpec((k, tm), lambda i, j: (0, i)) if mode == "tn" else pl.BlockSpec((tm, k), lambda i, j: (i, 0))
    b_spec = pl.BlockSpec((tn, k), lambda i, j: (j, 0)) if mode == "nt" else pl.BlockSpec((k, tn), lambda i, j: (0, j))
    if chip_split:
        out_shape = jax.ShapeDtypeStruct((N_CHIPS, m, tn), out_dtype)
        out_spec = pl.BlockSpec((None, tm, tn), lambda i, j: (j, i, 0))
    else:
        out_shape = jax.ShapeDtypeStruct((m, n), out_dtype)
        out_spec = pl.BlockSpec((tm, tn), lambda i, j: (i, j))
    return pl.pallas_call(body, out_shape=out_shape, grid=(m // tm, n // tn), in_specs=[a_spec, b_spec],
                          out_specs=out_spec, compiler_params=_cp(("arbitrary", "arbitrary")), name=name)(a, b)


def _rms_inv(x):
    return lax.rsqrt(jnp.mean(x * x, axis=-1, keepdims=True) + EPS)


def _rms_bwd(x, gain, dy):
    inv = _rms_inv(x)
    xhat = x * inv
    dxn = dy if gain is None else dy * gain
    dx = inv * (dxn - xhat * jnp.mean(dxn * xhat, axis=-1, keepdims=True))
    return dx, dy * xhat


def _acc_rows(ref, val):
    s = jnp.sum(val, axis=0, keepdims=True)

    @pl.when(pl.program_id(0) == 0)
    def _():
        ref[...] = s

    @pl.when(pl.program_id(0) > 0)
    def _():
        ref[...] += s


class _Prod:
    def __init__(self, *terms):
        self.terms = terms


def _rowwise(fn, name, rows, consts, outs, accs=(), tm=512):
    arrays, specs, loaders = [], [], []
    t = next(x for x in rows if not isinstance(x, _Prod)).shape[0]
    tm = min(tm, t)
    for x in rows:
        pos = len(arrays)
        if isinstance(x, _Prod):
            dims = []
            for term in x.terms:
                a, b, mode = term[:3]
                if len(term) == 4:
                    f = a.shape[2]
                    specs += [pl.BlockSpec((None, tm, f), lambda i, half=term[3]: (half, i, 0)),
                              pl.BlockSpec((b.shape[0], f), lambda i, half=term[3]: (0, half))]
                else:
                    specs += [_rows(tm, a.shape[1]), _full(b.shape)]
                arrays += [a, b]
                dims.append(NT if mode == "nt" else NN)

            def load(refs, pos=pos, dims=dims):
                total = None
                for k, dm in enumerate(dims):
                    part = _dot(refs[pos + 2 * k][...].astype(BF16), refs[pos + 2 * k + 1][...], dm)
                    total = part if total is None else total + part
                return total
        else:
            arrays.append(x)
            specs.append(_rows(tm, x.shape[1]))

            def load(refs, pos=pos):
                return refs[pos][...].astype(F32)
        loaders.append(load)
    for cst in consts:
        loaders.append(lambda refs, pos=len(arrays): refs[pos][...])
        arrays.append(cst)
        specs.append(_full(cst.shape))
    n_in = len(arrays)

    def body(*refs):
        res = fn(*[ld(refs) for ld in loaders])
        out_refs = refs[n_in:]
        for k, (_, dtype) in enumerate(outs):
            out_refs[k][...] = res[k].astype(dtype)
        for k in range(len(accs)):
            _acc_rows(out_refs[len(outs) + k], res[len(outs) + k])

    out_shape = [jax.ShapeDtypeStruct((t, c), dtype) for c, dtype in outs] + [jax.ShapeDtypeStruct((1, c), F32) for c in accs]
    out_specs = [_rows(tm, c) for c, _ in outs] + [_full((1, c)) for c in accs]
    return pl.pallas_call(body, out_shape=tuple(out_shape), grid=(t // tm,), in_specs=specs, out_specs=tuple(out_specs),
                          compiler_params=_cp(("arbitrary",)), name=name)(*arrays)


def _norm_cast(h, gain, name):
    return _rowwise(lambda x, g: (x * _rms_inv(x) * g,), name, [h], [gain], [(h.shape[1], BF16)])[0]


def _resid_norm(h, o, g_post, g_next, name):
    d = h.shape[1]

    def fn(hv, ov, gp, *gn):
        h1 = hv + ov * _rms_inv(ov) * gp
        hn = h1 * _rms_inv(h1)
        return ov, h1, hn * gn[0] if gn else hn

    return _rowwise(fn, name, [h, o], [g_post] + ([] if g_next is None else [g_next]), [(d, BF16), (d, F32), (d, BF16)])


def _ple_fwd(h2, pe, gl, g_ple, g_next, name):
    d = h2.shape[1]

    def fn(hv, pv, gv, gp, gn):
        h3 = hv + jax.nn.sigmoid(gv) * (pv * _rms_inv(pv) * gp)
        return pv, gv, h3, h3 * _rms_inv(h3) * gn

    return _rowwise(fn, name, [h2, pe, gl], [g_ple, g_next], [(d, BF16), (d, BF16), (d, F32), (d, BF16)])


def _loss_head(y, target, name, tm=512):
    t, d = y.shape

    def body(y_ref, t_ref, dy_ref, loss_ref):
        diff = y_ref[...] - t_ref[...]
        dy_ref[...] = diff * (1.0 / d)
        part = 0.5 * jnp.sum(jnp.mean(diff * diff, axis=-1, keepdims=True), axis=0, keepdims=True)
        part = jnp.broadcast_to(part, (8, LANES))

        @pl.when(pl.program_id(0) == 0)
        def _():
            loss_ref[...] = part

        @pl.when(pl.program_id(0) > 0)
        def _():
            loss_ref[...] += part

    return pl.pallas_call(body, out_shape=(jax.ShapeDtypeStruct((t, d), F32), jax.ShapeDtypeStruct((8, LANES), F32)),
                          grid=(t // tm,), in_specs=[_rows(tm, d)] * 2, out_specs=(_rows(tm, d), _full((8, LANES))),
                          compiler_params=_cp(("arbitrary",)), name=name)(y, target)


def _ple_bwd(dh3, pe, gl, g_ple, name):
    d = dh3.shape[1]

    def fn(dh, pv, gv, gp):
        gate = jax.nn.sigmoid(gv)
        dpe, dg_rows = _rms_bwd(pv, gp, dh * gate)
        return dh * (pv * _rms_inv(pv) * gp) * gate * (1.0 - gate), dpe, dg_rows

    return _rowwise(fn, name, [dh3, pe, gl], [g_ple], [(d, BF16), (d, BF16)], [d])


def _join(d_res, x_a, gain_a, d_a, x_b, gain_b, name, tm=512):
    d = d_res.shape[1]
    has_ga = gain_a is not None
    has_b = x_b is not None

    def fn(*vals):
        it = iter(vals)
        dres, xa, da = next(it), next(it), next(it)
        xb = next(it) if has_b else None
        ga = next(it) if has_ga else None
        dx, dga_rows = _rms_bwd(xa, ga, da)
        dout = dres + dx
        if not has_b:
            return (dout, dga_rows) if has_ga else (dout,)
        db, dgb_rows = _rms_bwd(xb, next(it), dout)
        return (dout, db, dga_rows, dgb_rows) if has_ga else (dout, db, dgb_rows)

    rows = [d_res, x_a, d_a] + ([x_b] if has_b else [])
    consts = ([gain_a] if has_ga else []) + ([gain_b] if has_b else [])
    return _rowwise(fn, name, rows, consts, [(d, F32)] + ([(d, BF16)] if has_b else []), [d] * (has_ga + has_b), tm)


def _ffn_in(hn, w, name, tm=2048, tn=256):
    t, d = hn.shape
    f = w.shape[1] // 2
    tm = min(tm, t)
    nj = f // tn

    def body(h_ref, wa_ref, wb_ref, s_ref, ab_ref):
        h = h_ref[...]
        a = _dot(h, wa_ref[...], NN)
        b = _dot(h, wb_ref[...], NN)
        ab_ref[0] = a.astype(BF16)
        ab_ref[1] = b.astype(BF16)
        s_ref[...] = (a * jax.nn.sigmoid(a) * b).astype(BF16)

    return pl.pallas_call(
        body, out_shape=(jax.ShapeDtypeStruct((t, f), BF16), jax.ShapeDtypeStruct((2, t, f), BF16)), grid=(t // tm, nj),
        in_specs=[pl.BlockSpec((tm, d), lambda i, j: (i, 0)), pl.BlockSpec((d, tn), lambda i, j: (0, j)),
                  pl.BlockSpec((d, tn), lambda i, j: (0, nj + j))],
        out_specs=(pl.BlockSpec((tm, tn), lambda i, j: (i, j)), pl.BlockSpec((2, tm, tn), lambda i, j: (0, i, j))),
        compiler_params=_cp(("arbitrary",) * 2), name=name)(hn, w, w)


def _ffn_bwd(do, w_out, ab, name, tm=2048, tn=256):
    t, d = do.shape
    f = w_out.shape[0]
    tm = min(tm, t)

    def body(d_ref, w_ref, ab_ref, o_ref):
        ds = _dot(d_ref[...], w_ref[...], NT)
        a = ab_ref[0].astype(F32)
        sg = jax.nn.sigmoid(a)
        silu = a * sg
        o_ref[0] = (ds * ab_ref[1].astype(F32) * (sg + silu * (1.0 - sg))).astype(BF16)
        o_ref[1] = (ds * silu).astype(BF16)

    both = pl.BlockSpec((2, tm, tn), lambda i, j: (0, i, j))
    return pl.pallas_call(
        body, out_shape=jax.ShapeDtypeStruct((2, t, f), BF16), grid=(t // tm, f // tn),
        in_specs=[pl.BlockSpec((tm, d), lambda i, j: (i, 0)), pl.BlockSpec((tn, d), lambda i, j: (j, 0)), both],
        out_specs=both, compiler_params=_cp(("arbitrary",) * 2), name=name)(do, w_out, ab)


def _mm_halves_tn(h, x2, name, tm=512):
    t, d = h.shape
    f = x2.shape[2]
    tn = 2 * f // N_CHIPS

    def body(h_ref, x_ref, o_ref):
        o_ref[...] = _dot(h_ref[...], x_ref[...], TN)

    return pl.pallas_call(
        body, out_shape=jax.ShapeDtypeStruct((N_CHIPS, d, tn), F32), grid=(N_CHIPS, d // tm),
        in_specs=[pl.BlockSpec((t, tm), lambda j, i: (0, i)), pl.BlockSpec((None, t, tn), lambda j, i: (j // 2, 0, j % 2))],
        out_specs=pl.BlockSpec((None, tm, tn), lambda j, i: (j, i, 0)),
        compiler_params=_cp(("arbitrary",) * 2), name=name)(h, x2)


def _mixnorm_fwd(att, gm, g_out, name, tm=512):
    t, w = att.shape

    def body(a_ref, m_ref, g_ref, o_ref):
        a, m, g = a_ref[...], m_ref[...], g_ref[...]
        o_ref[:, :w] = (a * _rms_inv(a) * g[:, :w]).astype(BF16)
        o_ref[:, w:] = (m * _rms_inv(m) * g[:, w:]).astype(BF16)

    return pl.pallas_call(body, out_shape=jax.ShapeDtypeStruct((t, 2 * w), BF16), grid=(t // tm,),
                          in_specs=[_rows(tm, w), _rows(tm, w), _full((1, 2 * w))], out_specs=_rows(tm, 2 * w),
                          compiler_params=_cp(("arbitrary",)), name=name)(att, gm, g_out)


def _mixnorm_bwd(att, gm, g_out, dmixed, name):
    w = att.shape[1]

    def fn(a, m, d, g):
        da, dga = _rms_bwd(a, g[:, :w], d[:, :w])
        dm, dgm = _rms_bwd(m, g[:, w:], d[:, w:])
        return da, dm, jnp.concatenate([dga, dgm], axis=1)

    return _rowwise(fn, name, [att, gm, dmixed], [g_out], [(w, F32), (w, F32)], [2 * w])


SCAN_BLOCK = 256


def _gate_fwd(zgf, b_pad, name):
    t = zgf.shape[0]
    fcol = zgf.shape[1] // LANES - 1
    nb = t // SCAN_BLOCK

    def body(f_ref, b_ref, c_ref):
        r = lax.broadcasted_iota(jnp.int32, (SCAN_BLOCK, SCAN_BLOCK), 0)
        s = lax.broadcasted_iota(jnp.int32, (SCAN_BLOCK, SCAN_BLOCK), 1)
        tril = (r >= s).astype(F32)
        head = lax.broadcasted_iota(jnp.int32, (SCAN_BLOCK, LANES), 1) < N_HEADS
        carry = jnp.zeros((1, LANES), F32)
        for blk in range(nb):
            rows = pl.ds(blk * SCAN_BLOCK, SCAN_BLOCK)
            x = f_ref[rows, :] + b_ref[...]
            lf = jnp.minimum(x, 0.0) - jnp.log1p(jnp.exp(-jnp.abs(x)))
            lf = jnp.where(head, lf, 0.0)
            cs = _dot(tril, lf, NN, HIGHEST) + carry
            c_ref[rows, :] = cs
            carry = carry + jnp.sum(lf, axis=0, keepdims=True)

    return pl.pallas_call(body, out_shape=jax.ShapeDtypeStruct((t, LANES), F32),
                          grid=(1,), in_specs=[pl.BlockSpec((t, LANES), lambda i: (0, fcol)), _full((1, LANES))],
                          out_specs=_full((t, LANES)), compiler_params=_cp(("arbitrary",)),
                          name=name)(zgf, b_pad)


def _gate_bwd(dc, zgf, b_pad, name):
    t = zgf.shape[0]
    fcol = zgf.shape[1] // LANES - 1
    nb = t // SCAN_BLOCK

    def body(dc_ref, f_ref, b_ref, dfl_ref, db_ref):
        r = lax.broadcasted_iota(jnp.int32, (SCAN_BLOCK, SCAN_BLOCK), 0)
        s = lax.broadcasted_iota(jnp.int32, (SCAN_BLOCK, SCAN_BLOCK), 1)
        triu = (s >= r).astype(F32)
        head = lax.broadcasted_iota(jnp.int32, (SCAN_BLOCK, LANES), 1) < N_HEADS
        carry = jnp.zeros((1, LANES), F32)
        db = jnp.zeros((1, LANES), F32)
        for blk in reversed(range(nb)):
            rows = pl.ds(blk * SCAN_BLOCK, SCAN_BLOCK)
            dc = dc_ref[rows, :]
            dlf = _dot(triu, dc, NN, HIGHEST) + carry
            x = f_ref[rows, :] + b_ref[...]
            dfl = jnp.where(head, dlf * jax.nn.sigmoid(-x), 0.0)
            dfl_ref[rows, :] = dfl.astype(BF16)
            db = db + jnp.sum(dfl, axis=0, keepdims=True)
            carry = carry + jnp.sum(dc, axis=0, keepdims=True)
        db_ref[...] = db

    return pl.pallas_call(body, out_shape=(jax.ShapeDtypeStruct((t, LANES), BF16), jax.ShapeDtypeStruct((1, LANES), F32)),
                          grid=(1,), in_specs=[_full((t, LANES)), pl.BlockSpec((t, LANES), lambda i: (0, fcol)),
                                               _full((1, LANES))],
                          out_specs=(_full((t, LANES)), _full((1, LANES))), compiler_params=_cp(("arbitrary",)),
                          name=name)(dc, zgf, b_pad)


ATT_BLOCK = 512
PAIRS = N_HEADS // 2
CQ_LANE = HEAD_DIM
CK_LANE = HEAD_DIM + 3
LSE_LANE = HEAD_DIM + 6


def _pick_col(x, idx):
    lane = lax.broadcasted_iota(jnp.int32, x.shape, 1)
    return jnp.sum(jnp.where(lane == idx, x, 0.0), axis=1, keepdims=True)


def _split3(x):
    hi = x.astype(BF16)
    r1 = x - hi.astype(F32)
    mid = r1.astype(BF16)
    lo = (r1 - mid.astype(F32)).astype(BF16)
    return hi, mid, lo


def _lanes_put(base, lane, start, vals):
    out = base
    for n, v in enumerate(vals):
        out = jnp.where(lane == start + n, v, out)
    return out


def _to_first_half(x, hh):
    return x if hh == 0 else pltpu.roll(x, HEAD_DIM, 1)


def _attn_prep(qkv, c, name, tm=512):
    t = qkv.shape[0]
    tm = min(tm, t)

    def body(q_ref, k_ref, v_ref, c_ref, qa_ref, ka_ref, va_ref):
        j = pl.program_id(1)
        lane = lax.broadcasted_iota(jnp.int32, (tm, LANES), 1)
        first = lane < HEAD_DIM
        q2, k2, v2 = q_ref[...].astype(F32), k_ref[...].astype(F32), v_ref[...].astype(F32)
        cc = c_ref[...]
        one = jnp.ones((tm, 1), F32)
        for hh in range(2):
            chi, cmid, clo = [v.astype(F32) for v in _split3(_pick_col(cc, 2 * j + hh))]
            qh = jnp.where(first, _to_first_half(q2, hh) * (HEAD_DIM ** -0.5), 0.0)
            kh = jnp.where(first, _to_first_half(k2, hh), 0.0)
            vh = jnp.where(first, _to_first_half(v2, hh), 0.0)
            qa = _lanes_put(qh, lane, CQ_LANE, [chi, cmid, clo, one, one, one])
            ka = _lanes_put(kh, lane, CQ_LANE, [one, one, one, -chi, -cmid, -clo, one, one, one])
            va = _lanes_put(vh, lane, CQ_LANE, [one, one, one])
            cols = slice(hh * LANES, (hh + 1) * LANES)
            qa_ref[:, cols] = qa.astype(BF16)
            ka_ref[:, cols] = ka.astype(BF16)
            va_ref[:, cols] = va.astype(BF16)

    blk = lambda off: pl.BlockSpec((tm, LANES), lambda i, j: (i, off + j))
    out = pl.BlockSpec((tm, 2 * LANES), lambda i, j: (i, j))
    shp = jax.ShapeDtypeStruct((t, N_HEADS * LANES), BF16)
    return pl.pallas_call(body, out_shape=(shp, shp, shp), grid=(t // tm, PAIRS),
                          in_specs=[blk(0), blk(PAIRS), blk(2 * PAIRS), pl.BlockSpec((tm, LANES), lambda i, j: (i, 0))],
                          out_specs=(out, out, out), compiler_params=_cp(("arbitrary",) * 2), name=name)(qkv, qkv, qkv, c)


def _causal_block(tb):
    return lax.broadcasted_iota(jnp.int32, (tb, tb), 0) >= lax.broadcasted_iota(jnp.int32, (tb, tb), 1)


def _causal_pairs(nb, key_major):
    pairs = [(q, k) for q in range(nb) for k in range(q + 1)]
    if key_major:
        pairs.sort(key=lambda qk: (qk[1], qk[0]))
    return (jnp.array([q for q, _ in pairs], jnp.int32), jnp.array([k for _, k in pairs], jnp.int32))


def _host(side, body, n_lead, arrays, in_specs, out_shapes, out_specs, scratch, grid):
    if side is None:
        return body, tuple(arrays), list(in_specs), list(out_shapes), list(out_specs), list(scratch)
    n_in, n_out, s_in, s_out = len(arrays), len(out_shapes), len(side.arrays), len(side.out_shapes)

    def hosted(*refs):
        ins_end = n_lead + n_in + s_in
        side_in, side_out = refs[n_lead + n_in:ins_end], refs[ins_end + n_out:ins_end + n_out + s_out]
        send, recv = refs[-2:]
        ids = [pl.program_id(ax) for ax in range(len(grid))]
        first, last = ids[0] == 0, ids[0] == grid[0] - 1
        for ax in range(1, len(grid)):
            first, last = first & (ids[ax] == 0), last & (ids[ax] == grid[ax] - 1)

        @pl.when(first)
        def _():
            side.start(side_in, side_out, send, recv)

        body(*refs[:n_lead + n_in], *refs[ins_end:ins_end + n_out], *refs[ins_end + n_out + s_out:-2])

        @pl.when(last)
        def _():
            side.wait(side_in, side_out, send, recv)

    return (hosted, tuple(arrays) + tuple(side.arrays), list(in_specs) + [HBM_SPEC] * s_in,
            list(out_shapes) + side.out_shapes, list(out_specs) + [HBM_SPEC] * s_out, list(scratch) + side.scratch)


def _pair_grid_call(body, tables, arrays, in_specs, out_shapes, out_specs, scratch, side, name):
    grid = (PAIRS, tables[0].shape[0])
    n_out = len(out_shapes)
    body, arrays, in_specs, out_shapes, out_specs, scratch = _host(side, body, 2, arrays, in_specs, out_shapes,
                                                                    out_specs, scratch, grid)
    grid_spec = pltpu.PrefetchScalarGridSpec(num_scalar_prefetch=2, grid=grid, in_specs=in_specs,
                                             out_specs=tuple(out_specs), scratch_shapes=scratch)
    out = pl.pallas_call(body, out_shape=tuple(out_shapes), grid_spec=grid_spec,
                         compiler_params=_cp(("arbitrary",) * 2), name=name)(*tables, *arrays)
    return out[:n_out], out[n_out:]


def _attn_fwd(qa, ka, va, name, side=None):
    t = qa.shape[0]
    tb = min(ATT_BLOCK, t)
    q_tab, k_tab = _causal_pairs(t // tb, key_major=False)

    def body(q_tab_ref, k_tab_ref, q_ref, k_ref, v_ref, o_ref, lse_ref, m0, m1, acc0, acc1):
        qi, kb = q_tab_ref[pl.program_id(1)], k_tab_ref[pl.program_id(1)]
        m_s, acc_s = (m0, m1), (acc0, acc1)

        @pl.when(kb == 0)
        def _():
            for hh in range(2):
                m_s[hh][...] = jnp.full(m_s[hh].shape, NEG_INF, F32)
                acc_s[hh][...] = jnp.zeros(acc_s[hh].shape, F32)

        def step(diagonal):
            for hh in range(2):
                cols = slice(hh * LANES, (hh + 1) * LANES)
                sc = _dot(q_ref[:, cols], k_ref[:, cols], NT)
                if diagonal:
                    sc = jnp.where(_causal_block(tb), sc, NEG_INF)
                m_prev = m_s[hh][...]
                m_new = jnp.maximum(m_prev, jnp.max(sc, axis=1, keepdims=True))
                p = jnp.exp(sc - m_new)
                acc_s[hh][...] = jnp.exp(m_prev - m_new) * acc_s[hh][...] + _dot(p.astype(BF16), v_ref[:, cols], NN)
                m_s[hh][...] = m_new

        @pl.when(kb < qi)
        def _():
            step(False)

        @pl.when(kb == qi)
        def _():
            step(True)
            lane = lax.broadcasted_iota(jnp.int32, (tb, LANES), 1)
            outs, lses = [], []
            for hh in range(2):
                acc = acc_s[hh][...]
                l = _pick_col(acc, CQ_LANE)
                outs.append(acc / l)
                lses.append(m_s[hh][...] + jnp.log(l))
            o_ref[...] = jnp.where(lane < HEAD_DIM, outs[0], pltpu.roll(outs[1], HEAD_DIM, 1))
            lse_ref[...] = jnp.where(lane == 0, lses[0], jnp.where(lane == 1, lses[1], 0.0))

    qrow = lambda j, s, qt, kt: (qt[s], j)
    krow = lambda j, s, qt, kt: (kt[s], j)
    return _pair_grid_call(
        body, (q_tab, k_tab), (qa, ka, va),
        [pl.BlockSpec((tb, 2 * LANES), qrow), pl.BlockSpec((tb, 2 * LANES), krow), pl.BlockSpec((tb, 2 * LANES), krow)],
        [jax.ShapeDtypeStruct((t, D_ATT), F32), jax.ShapeDtypeStruct((t, PAIRS * LANES), F32)],
        [pl.BlockSpec((tb, LANES), qrow), pl.BlockSpec((tb, LANES), qrow)],
        [pltpu.VMEM((tb, 1), F32)] * 2 + [pltpu.VMEM((tb, LANES), F32)] * 2, side, name)


def _attn_bwd_prep(qa, att, lse, datt, name, tm=512):
    t = qa.shape[0]
    tm = min(tm, t)

    def body(qa_ref, o_ref, lse_ref, do_ref, qb_ref, doa_ref):
        lane = lax.broadcasted_iota(jnp.int32, (tm, LANES), 1)
        first = lane < HEAD_DIM
        do = do_ref[...]
        prod = do * o_ref[...]
        lse2 = lse_ref[...]
        for hh in range(2):
            cols = slice(hh * LANES, (hh + 1) * LANES)
            delta = jnp.sum(jnp.where(first if hh == 0 else ~first, prod, 0.0), axis=1, keepdims=True)
            doh = jnp.where(first, _to_first_half(do, hh), 0.0)
            doa_ref[:, cols] = _lanes_put(doh, lane, CQ_LANE, [v.astype(F32) for v in _split3(-delta)]).astype(BF16)
            nl = [v.astype(F32) for v in _split3(-_pick_col(lse2, hh))]
            qb_ref[:, cols] = _lanes_put(qa_ref[:, cols].astype(F32), lane, LSE_LANE, nl).astype(BF16)

    wide = pl.BlockSpec((tm, 2 * LANES), lambda i, j: (i, j))
    pair = pl.BlockSpec((tm, LANES), lambda i, j: (i, j))
    shp = jax.ShapeDtypeStruct((t, N_HEADS * LANES), BF16)
    return pl.pallas_call(body, out_shape=(shp, shp), grid=(t // tm, PAIRS), in_specs=[wide, pair, pair, pair],
                          out_specs=(wide, wide), compiler_params=_cp(("arbitrary",) * 2), name=name)(qa, att, lse, datt)


def _attn_bwd(qb, ka, va, doa, name, side=None):
    t = qb.shape[0]
    tb = min(ATT_BLOCK, t)
    nb = t // tb
    q_tab, k_tab = _causal_pairs(nb, key_major=True)

    def body(q_tab_ref, k_tab_ref, q_ref, k_ref, v_ref, do_ref, dq_ref, dk_ref, dv_ref, dk0, dk1, dv0, dv1):
        qi, kb = q_tab_ref[pl.program_id(1)], k_tab_ref[pl.program_id(1)]
        dk_s, dv_s = (dk0, dk1), (dv0, dv1)

        @pl.when(qi == kb)
        def _():
            for ref in dk_s + dv_s:
                ref[...] = jnp.zeros(ref.shape, F32)

        def step(diagonal):
            rows = pl.ds(pl.multiple_of(qi * tb, tb), tb)
            for hh in range(2):
                cols = slice(hh * LANES, (hh + 1) * LANES)
                q, k, do = q_ref[:, cols], k_ref[:, cols], do_ref[:, cols]
                sc = _dot(q, k, NT)
                if diagonal:
                    sc = jnp.where(_causal_block(tb), sc, NEG_INF)
                p = jnp.exp(sc)
                ds = (p * _dot(do, v_ref[:, cols], NT)).astype(BF16)
                dv_s[hh][...] += _dot(p.astype(BF16), do, TN)
                dk_s[hh][...] += _dot(ds, q, TN)
                dq_new = _dot(ds, k, NN)

                @pl.when(kb == 0)
                def _():
                    dq_ref[rows, cols] = dq_new

                @pl.when(kb > 0)
                def _():
                    dq_ref[rows, cols] += dq_new

        @pl.when(qi == kb)
        def _():
            step(True)

        @pl.when(qi > kb)
        def _():
            step(False)

        @pl.when(qi == nb - 1)
        def _():
            for hh in range(2):
                cols = slice(hh * LANES, (hh + 1) * LANES)
                dk_ref[:, cols] = dk_s[hh][...]
                dv_ref[:, cols] = dv_s[hh][...].astype(BF16)

    qrow = lambda j, s, qt, kt: (qt[s], j)
    krow = lambda j, s, qt, kt: (kt[s], j)
    blk = (tb, 2 * LANES)
    wide = (t, N_HEADS * LANES)
    return _pair_grid_call(
        body, (q_tab, k_tab), (qb, ka, va, doa),
        [pl.BlockSpec(blk, qrow), pl.BlockSpec(blk, krow), pl.BlockSpec(blk, krow), pl.BlockSpec(blk, qrow)],
        [jax.ShapeDtypeStruct(wide, F32), jax.ShapeDtypeStruct(wide, F32), jax.ShapeDtypeStruct(wide, BF16)],
        [pl.BlockSpec((t, 2 * LANES), lambda j, s, qt, kt: (0, j)), pl.BlockSpec(blk, krow), pl.BlockSpec(blk, krow)],
        [pltpu.VMEM((tb, LANES), F32)] * 4, side, name)


def _attn_bwd_post(dqa, dka, dva, name, tm=256):
    t = dqa.shape[0]
    tm = min(tm, t)

    def body(dq_ref, dk_ref, dv_ref, o_ref, dc_ref):
        lane = lax.broadcasted_iota(jnp.int32, (tm, LANES), 1)
        first = lane < HEAD_DIM
        dc = jnp.zeros((tm, LANES), F32)
        for j in range(PAIRS):
            packed = []
            for ref, gain in ((dq_ref, HEAD_DIM ** -0.5), (dk_ref, 1.0), (dv_ref, 1.0)):
                even = ref[:, 2 * j * LANES:(2 * j + 1) * LANES].astype(F32)
                odd = ref[:, (2 * j + 1) * LANES:(2 * j + 2) * LANES].astype(F32)
                packed.append((jnp.where(first, even, pltpu.roll(odd, HEAD_DIM, 1)) * gain).astype(BF16))
                if ref is dq_ref:
                    dc = dc + jnp.where(lane == 2 * j, _pick_col(even, CQ_LANE), 0.0)
                    dc = dc + jnp.where(lane == 2 * j + 1, _pick_col(odd, CQ_LANE), 0.0)
                if ref is dk_ref:
                    dc = dc - jnp.where(lane == 2 * j, _pick_col(even, CK_LANE), 0.0)
                    dc = dc - jnp.where(lane == 2 * j + 1, _pick_col(odd, CK_LANE), 0.0)
            for part, val in enumerate(packed):
                o_ref[:, (part * PAIRS + j) * LANES:(part * PAIRS + j + 1) * LANES] = val
        dc_ref[...] = dc

    wide = _rows(tm, N_HEADS * LANES)
    return pl.pallas_call(body, out_shape=(jax.ShapeDtypeStruct((t, 3 * D_ATT), BF16), jax.ShapeDtypeStruct((t, LANES), F32)),
                          grid=(t // tm,), in_specs=[wide, wide, wide], out_specs=(_rows(tm, 3 * D_ATT), _rows(tm, LANES)),
                          compiler_params=_cp(("arbitrary",)), name=name)(dqa, dka, dva)


GELU_K = 0.7978845608028654
GELU_A = 0.044715


def _gelu(x):
    th = jnp.tanh(GELU_K * (x + GELU_A * x * x * x))
    return 0.5 * x * (1.0 + th), th


def _group_mean_matrix():
    r = jnp.arange(D_GM)[:, None] // HEAD_DIM
    s = jnp.arange(D_GM)[None, :] // HEAD_DIM
    return jnp.where(r == s, 1.0 / HEAD_DIM, 0.0).astype(BF16)


def _group_mean(x, mean_mat):
    hi = x.astype(BF16)
    lo = (x - hi.astype(F32)).astype(BF16)
    return _dot(hi, mean_mat, NN) + _dot(lo, mean_mat, NN)


def _gm_forward_parts(g, w_ref, bias, gain, mean_mat):
    gel, _ = _gelu(g)
    u, vv = gel[:, :D_GM], gel[:, D_GM:]
    mu = _group_mean(vv, mean_mat)
    d = vv - mu
    rstd = lax.rsqrt(_group_mean(d * d, mean_mat) + EPS)
    xhat = d * rstd
    vn = (xhat * gain).astype(BF16)
    first = lax.broadcasted_iota(jnp.int32, (CHUNK, LANES), 1) < HEAD_DIM
    tri = lax.broadcasted_iota(jnp.int32, (CHUNK, CHUNK), 0) >= lax.broadcasted_iota(jnp.int32, (CHUNK, CHUNK), 1)
    wm = [jnp.where(tri, w_ref[grp], 0.0).astype(BF16) for grp in range(w_ref.shape[0])]
    chunks = []
    for ck in range(g.shape[0] // CHUNK):
        parts = []
        for jp in range(D_GM // LANES):
            vp = vn[ck * CHUNK:(ck + 1) * CHUNK, jp * LANES:(jp + 1) * LANES]
            parts.append(jnp.where(first, _dot(wm[2 * jp], vp, NN), _dot(wm[2 * jp + 1], vp, NN)))
        chunks.append(jnp.concatenate(parts, axis=1) + bias)
    return u, xhat, rstd, vn, jnp.concatenate(chunks, axis=0), wm


GM_ROWS = 512


def _gmlp_fwd(zgf, w_s, bias_full, gain, mean_mat, name):
    t = zgf.shape[0]
    tm = min(GM_ROWS, t)

    def body(g_ref, w_ref, b_ref, gain_ref, mm_ref, o_ref):
        u, _, _, _, mixed, _ = _gm_forward_parts(g_ref[...], w_ref, b_ref[...], gain_ref[...], mm_ref[...])
        o_ref[...] = u * mixed

    return pl.pallas_call(body, out_shape=jax.ShapeDtypeStruct((t, D_GM), F32), grid=(t // tm,),
                          in_specs=[_rows(tm, 2 * D_GM), _full(w_s.shape), _full((CHUNK, D_GM)), _full((1, D_GM)),
                                    _full((D_GM, D_GM))],
                          out_specs=_rows(tm, D_GM), compiler_params=_cp(("arbitrary",)),
                          name=name)(zgf, w_s, bias_full, gain, mean_mat)


def _gmlp_bwd(zgf, dgm, w_s, bias_full, gain, mean_mat, name, side=None):
    t = zgf.shape[0]
    tm = min(GM_ROWS, t)

    def body(g_ref, d_ref, w_ref, b_ref, gain_ref, mm_ref, dg_ref, dw_ref, dmix_ref, dgain_ref):
        g, gain, mean_mat = g_ref[...], gain_ref[...], mm_ref[...]
        u, xhat, rstd, vn, mixed, wm = _gm_forward_parts(g, w_ref, b_ref[...], gain, mean_mat)
        dgm_v = d_ref[...]
        du = dgm_v * mixed
        dmixed = dgm_v * u
        dm_b = dmixed.astype(BF16)
        first = lax.broadcasted_iota(jnp.int32, (CHUNK, LANES), 1) < HEAD_DIM
        tri = lax.broadcasted_iota(jnp.int32, (CHUNK, CHUNK), 0) >= lax.broadcasted_iota(jnp.int32, (CHUNK, CHUNK), 1)

        @pl.when(pl.program_id(0) == 0)
        def _():
            dw_ref[...] = jnp.zeros(dw_ref.shape, F32)
            dmix_ref[...] = jnp.zeros(dmix_ref.shape, F32)
            dgain_ref[...] = jnp.zeros(dgain_ref.shape, F32)

        dw = [jnp.zeros((CHUNK, CHUNK), F32) for _ in wm]
        dmix = jnp.zeros((CHUNK, D_GM), F32)
        dvn_chunks = []
        for ck in range(tm // CHUNK):
            rows = slice(ck * CHUNK, (ck + 1) * CHUNK)
            dmix = dmix + dmixed[rows]
            dvn_parts = []
            for jp in range(D_GM // LANES):
                vp = vn[rows, jp * LANES:(jp + 1) * LANES]
                dmp = dm_b[rows, jp * LANES:(jp + 1) * LANES]
                halves = []
                for hh in range(2):
                    sel = first if hh == 0 else ~first
                    grp = 2 * jp + hh
                    dw[grp] = dw[grp] + _dot(jnp.where(sel, dmp, jnp.zeros_like(dmp)), vp, NT)
                    halves.append(_dot(wm[grp], dmp, TN))
                dvn_parts.append(jnp.where(first, halves[0], halves[1]))
            dvn_chunks.append(jnp.concatenate(dvn_parts, axis=1))
        dvn = jnp.concatenate(dvn_chunks, axis=0)
        for grp, dwg in enumerate(dw):
            dw_ref[grp] += jnp.where(tri, dwg, 0.0)
        dmix_ref[...] += dmix
        dgain_ref[...] += jnp.sum(dvn * xhat, axis=0, keepdims=True)
        dxhat = dvn * gain
        m1 = _group_mean(dxhat, mean_mat)
        m2 = _group_mean(dxhat * xhat, mean_mat)
        dvv = rstd * (dxhat - m1 - xhat * m2)
        gel, th = _gelu(g)
        dgel = 0.5 * (1.0 + th) + 0.5 * g * (1.0 - th * th) * GELU_K * (1.0 + 3.0 * GELU_A * g * g)
        dg_ref[...] = (jnp.concatenate([du, dvv], axis=1) * dgel).astype(BF16)

    grid = (t // tm,)
    body, arrays, in_specs, out_shapes, out_specs, scratch = _host(
        side, body, 0, (zgf, dgm, w_s, bias_full, gain, mean_mat),
        [_rows(tm, 2 * D_GM), _rows(tm, D_GM), _full(w_s.shape), _full((CHUNK, D_GM)), _full((1, D_GM)),
         _full((D_GM, D_GM))],
        [jax.ShapeDtypeStruct((t, 2 * D_GM), BF16), jax.ShapeDtypeStruct(w_s.shape, F32),
         jax.ShapeDtypeStruct((CHUNK, D_GM), F32), jax.ShapeDtypeStruct((1, D_GM), F32)],
        [_rows(tm, 2 * D_GM), _full(w_s.shape), _full((CHUNK, D_GM)), _full((1, D_GM))], [], grid)
    out = pl.pallas_call(body, out_shape=tuple(out_shapes), grid=grid, in_specs=in_specs, out_specs=tuple(out_specs),
                         scratch_shapes=scratch, compiler_params=_cp(("arbitrary",)), name=name)(*arrays)
    return out[:4], out[4:]


def _row_tile(r, c, budget=1 << 19):
    best = None
    for tr in range(8, r + 1, 8):
        if r % tr == 0 and tr * c <= budget:
            best = tr
    return best if best is not None else r


def _adamw(w, g, m, v, name):
    nl, r, c = w.shape
    tr = _row_tile(r, c, 1 << 18)
    c1 = 1.0 - ADAM_B1 ** ADAM_STEP
    c2 = 1.0 - ADAM_B2 ** ADAM_STEP

    def body(w_ref, g_ref, m_ref, v_ref, d_ref, mo_ref, vo_ref):
        gv = g_ref[...]
        mn = ADAM_B1 * m_ref[...] + (1.0 - ADAM_B1) * gv
        vn = ADAM_B2 * v_ref[...] + (1.0 - ADAM_B2) * jnp.square(gv)
        mo_ref[...] = mn
        vo_ref[...] = vn
        d_ref[...] = -ADAM_LR * ((mn / c1) / (jnp.sqrt(vn / c2) + ADAM_EPS) + ADAM_WD * w_ref[...])

    spec = pl.BlockSpec((None, tr, c), lambda l, i: (l, i, 0))
    shp = jax.ShapeDtypeStruct(w.shape, F32)
    return pl.pallas_call(body, out_shape=(shp, shp, shp), grid=(nl, r // tr), in_specs=[spec] * 4,
                          out_specs=(spec, spec, spec), compiler_params=_cp(("arbitrary",) * 2), name=name)(w, g, m, v)


def _add_sibling(g, recv, c_idx, wire_dtype, name):
    nj, _, h, c = g.shape
    tr = _row_tile(h, c)

    def body(c_ref, g_ref, r_ref, o_ref):
        o_ref[...] = (g_ref[...] + r_ref[...]).astype(wire_dtype)

    grid_spec = pltpu.PrefetchScalarGridSpec(
        num_scalar_prefetch=1, grid=(nj, h // tr),
        in_specs=[pl.BlockSpec((None, None, tr, c), lambda j, i, c_ref: (j, c_ref[0], i, 0)),
                  pl.BlockSpec((None, tr, c), lambda j, i, c_ref: (j, i, 0))],
        out_specs=pl.BlockSpec((None, tr, c), lambda j, i, c_ref: (j, i, 0)))
    return pl.pallas_call(body, out_shape=jax.ShapeDtypeStruct((nj, h, c), wire_dtype), grid_spec=grid_spec,
                          compiler_params=_cp(("arbitrary",) * 2), name=name)(c_idx, g, recv)


def _add_chips(own, parts, place, name):
    _, h, c = own.shape
    tr = _row_tile(h, c)

    def body(p_ref, o_ref, a_ref, b_ref, c_ref, out_ref):
        out_ref[...] = ((o_ref[...].astype(F32) + a_ref[...].astype(F32)) + b_ref[...].astype(F32)) + c_ref[...].astype(F32)

    def other(k):
        return pl.BlockSpec((None, tr, c), lambda i, p_ref: (jnp.bitwise_xor(p_ref[0], k), i, 0))

    grid_spec = pltpu.PrefetchScalarGridSpec(
        num_scalar_prefetch=1, grid=(h // tr,),
        in_specs=[pl.BlockSpec((None, tr, c), lambda i, p_ref: (p_ref[0], i, 0)), other(1), other(2), other(3)],
        out_specs=pl.BlockSpec((None, tr, c), lambda i, p_ref: (p_ref[1], i, 0)))
    return pl.pallas_call(body, out_shape=jax.ShapeDtypeStruct((2, h, c), F32), grid_spec=grid_spec,
                          compiler_params=_cp(("arbitrary",)), name=name)(place, own, parts, parts, parts)


HBM_SPEC = pl.BlockSpec(memory_space=pltpu.HBM)


def _place():
    x, y, c = lax.axis_index("x"), lax.axis_index("y"), lax.axis_index("c")
    chips = [(1 - x, y), (x, 1 - y), (1 - x, 1 - y)]
    return x, y, c, 2 * x + y, chips, [2 * px + py for px, py in chips]


def _remote(src, dst, send_sem, recv_sem, dev):
    return pltpu.make_async_remote_copy(src_ref=src, dst_ref=dst, send_sem=send_sem, recv_sem=recv_sem,
                                        device_id=dev, device_id_type=MESH)


def _comm_call(body, arrays, out_shapes, sems, name, aliases=None):
    n = len(arrays)
    return pl.pallas_call(
        body, out_shape=tuple(out_shapes), in_specs=[HBM_SPEC] * n, out_specs=tuple([HBM_SPEC] * len(out_shapes)),
        scratch_shapes=[pltpu.SemaphoreType.DMA(s) for s in sems], input_output_aliases=aliases or {},
        compiler_params=pltpu.CompilerParams(has_side_effects=True), name=name)(*arrays)


class _SideCopies:
    def __init__(self, arrays, out_shapes, sem_shape, sends, recvs):
        self.arrays, self.out_shapes, self.sem_shape = list(arrays), list(out_shapes), sem_shape
        self.sends, self.recvs = sends, recvs

    @property
    def scratch(self):
        return [pltpu.SemaphoreType.DMA(self.sem_shape), pltpu.SemaphoreType.DMA(self.sem_shape)]

    def start(self, ins, outs, send, recv):
        for cp in self.sends(ins, outs, send, recv):
            cp.start()

    def wait(self, ins, outs, send, recv):
        for cp in self.recvs(ins, outs, send, recv):
            cp.wait_recv()
        for cp in self.sends(ins, outs, send, recv):
            cp.wait_send()

    def run(self, name):
        n = len(self.arrays)

        def body(*refs):
            ins, outs = refs[:n], refs[n:n + len(self.out_shapes)]
            send, recv = refs[-2:]
            self.start(ins, outs, send, recv)
            self.wait(ins, outs, send, recv)

        return _comm_call(body, self.arrays, self.out_shapes, [self.sem_shape] * 2, name)


def _gather_over_chips(shards):
    n = len(shards)

    def sends(ins, outs, send, recv):
        x, y, c, me, chips, _ = _place()
        over_ici = [_remote(ins[a].at[c], outs[a].at[me, c], send.at[a, k], recv.at[a, k], (px, py, c))
                    for a in range(n) for k, (px, py) in enumerate(chips)]
        return over_ici + [_remote(ins[a], outs[a].at[me], send.at[a, 3], recv.at[a, 3], (x, y, 1 - c)) for a in range(n)]

    def recvs(ins, outs, send, recv):
        x, y, c, me, _, cidx = _place()
        slots = [[outs[a].at[cidx[k], c] for k in range(3)] + [outs[a].at[me]] for a in range(n)]
        return [_remote(blk, blk, send.at[a, k], recv.at[a, k], (x, y, 1 - c))
                for a in range(n) for k, blk in enumerate(slots[a])]

    out_shapes = [jax.ShapeDtypeStruct((N_CHIPS,) + s.shape, s.dtype) for s in shards]
    return _SideCopies(shards, out_shapes, (n, 4), sends, recvs)


def _gather_finish(partial, name):
    n = len(partial)

    def body(*refs):
        part, outs = refs[:n], refs[n:2 * n]
        send, recv = refs[2 * n:]
        x, y, c, _, _, cidx = _place()
        sib = (x, y, 1 - c)
        cps = [_remote(part[a].at[cidx[k], c], outs[a].at[cidx[k], c], send.at[a, k], recv.at[a, k], sib)
               for a in range(n) for k in range(3)]
        for cp in cps:
            cp.start()
        for a in range(n):
            for k in range(3):
                blk = outs[a].at[cidx[k], 1 - c]
                _remote(blk, blk, send.at[a, k], recv.at[a, k], sib).wait_recv()
        for cp in cps:
            cp.wait_send()

    out_shapes = [jax.ShapeDtypeStruct(p.shape, p.dtype) for p in partial]
    return _comm_call(body, list(partial), out_shapes, [(n, 3), (n, 3)], name, aliases={a: a for a in range(n)})


def _sibling_halves(grads):
    n = len(grads)

    def copies(ins, outs, send, recv):
        x, y, c, _, _, _ = _place()
        return [_remote(ins[a].at[j, 1 - c], outs[a].at[j], send.at[a, j], recv.at[a, j], (x, y, 1 - c))
                for a in range(n) for j in range(N_CHIPS)]

    out_shapes = [jax.ShapeDtypeStruct((g.shape[0],) + g.shape[2:], g.dtype) for g in grads]
    return _SideCopies(grads, out_shapes, (n, N_CHIPS), copies, copies)


def _scatter_over_chips(sums):
    n = len(sums)

    def sends(ins, outs, send, recv):
        _, _, c, me, chips, cidx = _place()
        return [_remote(ins[a].at[cidx[k]], outs[a].at[me], send.at[a, k], recv.at[a, k], (px, py, c))
                for a in range(n) for k, (px, py) in enumerate(chips)]

    def recvs(ins, outs, send, recv):
        x, y, c, _, _, cidx = _place()
        return [_remote(outs[a].at[cidx[k]], outs[a].at[cidx[k]], send.at[a, k], recv.at[a, k], (x, y, 1 - c))
                for a in range(n) for k in range(3)]

    return _SideCopies(sums, [jax.ShapeDtypeStruct(s.shape, s.dtype) for s in sums], (n, 3), sends, recvs)


def _exchange_halves(halves, name):
    n = len(halves)

    def body(*refs):
        ins, bufs = refs[:n], refs[n:2 * n]
        send, recv = refs[2 * n:]
        x, y, c, _, _, _ = _place()
        sib = (x, y, 1 - c)
        cps = []
        for a in range(n):
            cp = _remote(ins[a].at[c], bufs[a].at[c], send.at[a], recv.at[a], sib)
            cp.start()
            cps.append(cp)
        for a in range(n):
            blk = bufs[a].at[1 - c]
            _remote(blk, blk, send.at[a], recv.at[a], sib).wait_recv()
        for cp in cps:
            cp.wait_send()

    out_shapes = [jax.ShapeDtypeStruct(s.shape, s.dtype) for s in halves]
    return _comm_call(body, halves, out_shapes, [(n,), (n,)], name, aliases={a: a for a in range(n)})


def _gather_chips(slices):
    n = len(slices)

    def sends(ins, outs, send, recv):
        x, y, c, me, chips, _ = _place()
        return [_remote(ins[a], outs[a].at[me], send.at[a, k], recv.at[a, k], dev)
                for a in range(n) for k, dev in enumerate([(px, py, c) for px, py in chips] + [(x, y, 1 - c)])]

    def recvs(ins, outs, send, recv):
        x, y, c, me, _, cidx = _place()
        return [_remote(outs[a].at[slot], outs[a].at[slot], send.at[a, k], recv.at[a, k], (x, y, 1 - c))
                for a in range(n) for k, slot in enumerate(cidx + [me])]

    out_shapes = [jax.ShapeDtypeStruct((N_CHIPS,) + s.shape, s.dtype) for s in slices]
    return _SideCopies(slices, out_shapes, (n, 4), sends, recvs)


def _rs_siblings(grads):
    return _sibling_halves([g.reshape(g.shape[0], 2, g.shape[1] // 2, g.shape[2]) for g in grads])


def _rs_chips(siblings, recv, c_idx, wire_dtype, tag):
    wire = wire_dtype if isinstance(wire_dtype, list) else [wire_dtype] * len(recv)
    return _scatter_over_chips([_add_sibling(g, r, c_idx, w, "rs_add_sibling_" + tag)
                                for g, r, w in zip(siblings.arrays, recv, wire)])


def _rs_end(scatter, parts, place, tag):
    halves = [_add_chips(s, p, place, "rs_add_chips_" + tag) for s, p in zip(scatter.arrays, parts)]
    both = _exchange_halves(halves, "rs_halves_" + tag)
    return [b.reshape(b.shape[0] * b.shape[1], b.shape[2]) for b in both]


SMALL_ORDER = ("gm_w_s", "mix_pre_norm", "mix_post_norm", "mix_out_norm", "ffn_pre_norm", "ffn_post_norm", "ple_norm",
               "gm_v_norm", "gm_b_s", "b_forget")
SMALL_ROWS_MULTIPLE = 64
SMALL_BLOCK = 8 * LANES


def _chip_columns(pieces, width):
    out = []
    for j in range(N_CHIPS):
        lo, hi, parts, off = j * width, (j + 1) * width, [], 0
        for piece in pieces:
            a, b = max(lo, off), min(hi, off + piece.shape[1])
            if a < b:
                parts.append(piece[:, a - off:b - off])
            off += piece.shape[1]
        out.append(parts[0] if len(parts) == 1 else jnp.concatenate(parts, axis=1))
    return jnp.stack(out)


def _columns(sliced, lo, hi):
    width, parts = sliced.shape[2], []
    for j in range(N_CHIPS):
        a, b = max(lo, j * width), min(hi, (j + 1) * width)
        if a < b:
            parts.append(sliced[j][:, a - j * width:b - j * width])
    return parts[0] if len(parts) == 1 else jnp.concatenate(parts, axis=1)


def _pack_small(parts):
    blocks = []
    for nme in SMALL_ORDER:
        v = parts[nme].reshape(-1)
        pad = (-v.shape[0]) % SMALL_BLOCK
        blocks.append((jnp.pad(v, (0, pad)) if pad else v).reshape(-1, LANES))
    rows = sum(b.shape[0] for b in blocks)
    if rows % SMALL_ROWS_MULTIPLE:
        blocks.append(jnp.zeros((SMALL_ROWS_MULTIPLE - rows % SMALL_ROWS_MULTIPLE, LANES), F32))
    return jnp.concatenate(blocks, axis=0)


def _unpack_small(packed, shapes):
    out, row = {}, 0
    for nme in SMALL_ORDER:
        size = 1
        for s in shapes[nme]:
            size *= s
        rows = -(-size // SMALL_BLOCK) * (SMALL_BLOCK // LANES)
        block = packed[row:row + rows]
        out[nme] = (block if size % SMALL_BLOCK == 0 else block.reshape(-1)[:size]).reshape(shapes[nme])
        row += rows
    return out


def kernel(x, p, mix_pre_norm, mix_post_norm, w_in, b_forget, gm_v_norm, gm_w_s, gm_b_s, mix_out_norm, w_out, ffn_pre_norm, ffn_post_norm, w_ffn_in, w_ffn_out, w_ple, ple_norm, w_ple_gate, loss_target, m_mix_pre_norm, m_mix_post_norm, m_w_in, m_b_forget, m_gm_v_norm, m_gm_w_s, m_gm_b_s, m_mix_out_norm, m_w_out, m_ffn_pre_norm, m_ffn_post_norm, m_w_ffn_in, m_w_ffn_out, m_w_ple, m_ple_norm, m_w_ple_gate, v_mix_pre_norm, v_mix_post_norm, v_w_in, v_b_forget, v_gm_v_norm, v_gm_w_s, v_gm_b_s, v_mix_out_norm, v_w_out, v_ffn_pre_norm, v_ffn_post_norm, v_w_ffn_in, v_w_ffn_out, v_w_ple, v_ple_norm, v_w_ple_gate):
    weights = dict(mix_pre_norm=mix_pre_norm, mix_post_norm=mix_post_norm, w_in=w_in, b_forget=b_forget,
                   gm_v_norm=gm_v_norm, gm_w_s=gm_w_s, gm_b_s=gm_b_s, mix_out_norm=mix_out_norm, w_out=w_out,
                   ffn_pre_norm=ffn_pre_norm, ffn_post_norm=ffn_post_norm, w_ffn_in=w_ffn_in, w_ffn_out=w_ffn_out,
                   w_ple=w_ple, ple_norm=ple_norm, w_ple_gate=w_ple_gate)
    mom_m = dict(mix_pre_norm=m_mix_pre_norm, mix_post_norm=m_mix_post_norm, w_in=m_w_in, b_forget=m_b_forget,
                 gm_v_norm=m_gm_v_norm, gm_w_s=m_gm_w_s, gm_b_s=m_gm_b_s, mix_out_norm=m_mix_out_norm, w_out=m_w_out,
                 ffn_pre_norm=m_ffn_pre_norm, ffn_post_norm=m_ffn_post_norm, w_ffn_in=m_w_ffn_in,
                 w_ffn_out=m_w_ffn_out, w_ple=m_w_ple, ple_norm=m_ple_norm, w_ple_gate=m_w_ple_gate)
    mom_v = dict(mix_pre_norm=v_mix_pre_norm, mix_post_norm=v_mix_post_norm, w_in=v_w_in, b_forget=v_b_forget,
                 gm_v_norm=v_gm_v_norm, gm_w_s=v_gm_w_s, gm_b_s=v_gm_b_s, mix_out_norm=v_mix_out_norm, w_out=v_w_out,
                 ffn_pre_norm=v_ffn_pre_norm, ffn_post_norm=v_ffn_post_norm, w_ffn_in=v_w_ffn_in,
                 w_ffn_out=v_w_ffn_out, w_ple=v_w_ple, ple_norm=v_ple_norm, w_ple_gate=v_w_ple_gate)
    big = ("w_in", "w_out", "w_ffn_in", "w_ffn_out", "w_ple", "w_ple_gate")
    depth = w_in.shape[0]
    t, d = x.shape[1], x.shape[2]
    d_ff = w_ffn_out.shape[1] * N_CHIPS
    c_idx = lax.axis_index("c").astype(jnp.int32).reshape(1)
    place = jnp.stack([2 * lax.axis_index("x") + lax.axis_index("y"), lax.axis_index("c")]).astype(jnp.int32)
    h = x[0]
    target = loss_target[0]
    mean_mat = _group_mean_matrix()

    def row(a, i):
        return a[i].reshape(1, -1)

    def shards_of(i):
        shards = [weights[nme][i].astype(BF16) for nme in big]
        return [s.reshape(2, s.shape[0] // 2, s.shape[1]) for s in shards]

    saved = []
    hn = _norm_cast(h, row(mix_pre_norm, 0), "norm_first")
    by_cols = lambda g: g.transpose(1, 0, 2).reshape(g.shape[1], N_CHIPS * g.shape[2])
    by_rows = lambda g: g.reshape(N_CHIPS * g.shape[1], g.shape[2])
    whole = lambda g: g.reshape(N_CHIPS, g.shape[2] * 2, g.shape[3])
    gather = _gather_over_chips(shards_of(0)[:1])
    w_in_g = _gather_finish(gather.run("gather_first"), "gather_finish_first")[0]
    for i in range(depth):
        w_in_c = whole(w_in_g)
        w_qkv = _columns(w_in_c, 0, 3 * D_ATT)
        w_gf = jnp.concatenate([_columns(w_in_c, 3 * D_ATT + N_HEADS, N_CHIPS * w_in_c.shape[2]),
                                _columns(w_in_c, 3 * D_ATT, 3 * D_ATT + N_HEADS),
                                jnp.zeros((d, LANES - N_HEADS), BF16)], axis=1)
        b_pad = jnp.pad(b_forget[i], (0, LANES - N_HEADS)).reshape(1, LANES)
        bias_full = jnp.repeat(gm_b_s[i].T, HEAD_DIM, axis=1)
        gain_v = row(gm_v_norm, i)

        qkv = _mm(hn, w_qkv, "nn", BF16, "mm_qkv")
        zgf = _mm(hn, w_gf, "nn", F32, "mm_gf", tn_cap=384)
        qa, ka, va = _attn_prep(qkv, _gate_fwd(zgf, b_pad, "gate_fwd"), "attn_prep")
        last = i + 1 == depth
        gather = _gather_over_chips(shards_of(i)[1:] + ([] if last else shards_of(i + 1)[:1]))
        (att, lse), partial = _attn_fwd(qa, ka, va, "attn_fwd_last" if last else "attn_fwd", gather)
        gathered = _gather_finish(partial, "gather_finish_last" if last else "gather_finish")
        w_out_f, w_fi_f, w_fo_f = by_rows(whole(gathered[0])), by_cols(whole(gathered[1])), by_rows(whole(gathered[2]))
        w_ple_f, w_pg_f = by_cols(whole(gathered[3])), by_rows(whole(gathered[4]))
        w_in_g = None if last else gathered[5]
        gm = _gmlp_fwd(zgf, gm_w_s[i], bias_full, gain_v, mean_mat, "gmlp_fwd")
        mixed = _mixnorm_fwd(att, gm, row(mix_out_norm, i), "mixnorm_fwd")
        o, h1, hn2 = _resid_norm(h, _Prod((mixed, w_out_f, "nn")), row(mix_post_norm, i), row(ffn_pre_norm, i),
                                 "out_resid")
        s, ab = _ffn_in(hn2, w_fi_f, "ffn_in")
        o2, h2, hr = _resid_norm(h1, _Prod((s, w_fo_f, "nn")), row(ffn_post_norm, i), None, "ffn_out_resid")
        g_next = row(mix_pre_norm, i + 1) if i + 1 < depth else row(mix_pre_norm, 0)
        pe, gl, h3, hn_next = _ple_fwd(h2, _Prod((p[i, 0], w_ple_f, "nn")), _Prod((hr, w_pg_f, "nn")),
                                       row(ple_norm, i), g_next, "ple_fwd")
        saved.append(dict(h=h, hn=hn, qa=qa, ka=ka, va=va, zgf=zgf, att=att, lse=lse, gm=gm, mixed=mixed,
                          o=o, h1=h1, hn2=hn2, ab=ab, s=s, o2=o2, h2=h2, hr=hr, pe=pe, gl=gl, w_qkv=w_qkv, w_gf=w_gf,
                          w_out=w_out_f, w_fi=w_fi_f, w_fo=w_fo_f, w_pg=w_pg_f, b_pad=b_pad, bias_full=bias_full,
                          gain_v=gain_v))
        h, hn = h3, hn_next

    dh, loss_blk = _loss_head(h, target, "loss_head")
    loss = lax.psum(loss_blk[0, 0], ("x", "y", "c"))

    small = {nme: [None] * depth for nme in SMALL_ORDER}
    big_grads = {nme: [None] * depth for nme in big}
    waiting = []
    for i in reversed(range(depth)):
        sv = saved[i]
        dgl, dpe, small["ple_norm"][i] = _ple_bwd(dh, sv["pe"], sv["gl"], row(ple_norm, i), "ple_bwd")
        g_pg = _mm(sv["hr"], dgl, "tn", F32, "mm_dw_ple_gate").reshape(N_CHIPS, -1, d)
        g_ple = _mm(p[i, 0], dpe, "tn", F32, "mm_dw_ple", chip_split=True)
        dh2, do2, small["ffn_post_norm"][i] = _join(dh, sv["h2"], None, _Prod((dgl, sv["w_pg"], "nt")), sv["o2"],
                                                    row(ffn_post_norm, i), "join_ple")
        dab = _ffn_bwd(do2, sv["w_fo"], sv["ab"], "ffn_bwd")
        g_fo = _mm(sv["s"], do2, "tn", F32, "mm_dw_ffn_out", tm=256).reshape(N_CHIPS, -1, d)
        g_fi = _mm_halves_tn(sv["hn2"], dab, "mm_dw_ffn_in")
        dh1, do, small["ffn_pre_norm"][i], small["mix_post_norm"][i] = _join(
            dh2, sv["h1"], row(ffn_pre_norm, i), _Prod((dab, sv["w_fi"], "nt", 0), (dab, sv["w_fi"], "nt", 1)), sv["o"],
            row(mix_post_norm, i), "join_ffn", tm=256)
        g_out = _mm(sv["mixed"], do, "tn", F32, "mm_dw_out").reshape(N_CHIPS, -1, d)
        datt, dgm, small["mix_out_norm"][i] = _mixnorm_bwd(sv["att"], sv["gm"], row(mix_out_norm, i),
                                                          _Prod((do, sv["w_out"], "nt")), "mixnorm_bwd")
        batch = waiting + [("w_out", i, g_out), ("w_ffn_in", i, g_fi), ("w_ffn_out", i, g_fo), ("w_ple", i, g_ple),
                           ("w_ple_gate", i, g_pg)]
        tag = "layer" if waiting else "top"
        siblings = _rs_siblings([g for _, _, g in batch])
        (dg, small["gm_w_s"][i], dmix_sum, small["gm_v_norm"][i]), recv = _gmlp_bwd(
            sv["zgf"], dgm, gm_w_s[i], sv["bias_full"], sv["gain_v"], mean_mat, "gmlp_bwd_" + tag, siblings)
        small["gm_b_s"][i] = dmix_sum.reshape(CHUNK, N_HEADS, HEAD_DIM).sum(-1).T
        qb, doa = _attn_bwd_prep(sv["qa"], sv["att"], sv["lse"], datt, "attn_bwd_prep")
        scatter = _rs_chips(siblings, recv, c_idx, BF16, tag)
        (dqa, dka, dva), parts = _attn_bwd(qb, sv["ka"], sv["va"], doa, "attn_bwd_" + tag, scatter)
        for (nme, layer, _), g in zip(batch, _rs_end(scatter, parts, place, tag)):
            big_grads[nme][layer] = g
        dqkv, dc = _attn_bwd_post(dqa, dka, dva, "attn_bwd_post")
        dfl, db = _gate_bwd(dc, sv["zgf"], sv["b_pad"], "gate_bwd")
        small["b_forget"][i] = db[0, :N_HEADS]
        dgf = jnp.concatenate([dg, dfl], axis=1)
        g_qkv = _mm(sv["hn"], dqkv, "tn", F32, "mm_dw_qkv")
        g_gf = _mm(sv["hn"], dgf, "tn", F32, "mm_dw_gf", tn_cap=384)
        g_in = _chip_columns([g_qkv, g_gf[:, 2 * D_GM:2 * D_GM + N_HEADS], g_gf[:, :2 * D_GM]], w_in.shape[2])
        dh, small["mix_pre_norm"][i] = _join(dh1, sv["h"], row(mix_pre_norm, i),
                                             _Prod((dqkv, sv["w_qkv"], "nt"), (dgf, sv["w_gf"], "nt")), None, None, "join_mix")
        waiting = [("w_in", i, g_in)]
    grad_x = dh.reshape(1, t, d)

    small_shapes = {nme: weights[nme].shape for nme in SMALL_ORDER}
    small_part = _pack_small({nme: jnp.stack([g.reshape(small_shapes[nme][1:]) for g in small[nme]])
                              for nme in SMALL_ORDER})
    rows_small = small_part.shape[0]
    siblings = _rs_siblings([waiting[0][2], small_part.reshape(N_CHIPS, rows_small // N_CHIPS, LANES)])
    scatter = _rs_chips(siblings, siblings.run("rs_sibling_tail"), c_idx, [BF16, F32], "tail")
    big_grads["w_in"][0], small_slice = _rs_end(scatter, scatter.run("rs_chips_tail"), place, "tail")
    small_all = _gather_chips([small_slice]).run("gather_small")[0].reshape(1, rows_small, LANES)
    sd, sm, sv_ = _adamw(_pack_small({n_: weights[n_] for n_ in SMALL_ORDER})[None], small_all,
                         _pack_small({n_: mom_m[n_] for n_ in SMALL_ORDER})[None],
                         _pack_small({n_: mom_v[n_] for n_ in SMALL_ORDER})[None], "adamw_small")
    grads = _unpack_small(small_all[0], small_shapes)
    deltas = _unpack_small(sd[0], small_shapes)
    new_m = _unpack_small(sm[0], small_shapes)
    new_v = _unpack_small(sv_[0], small_shapes)

    for nme in big:
        g = jnp.stack(big_grads[nme]).reshape(weights[nme].shape)
        grads[nme] = g
        deltas[nme], new_m[nme], new_v[nme] = _adamw(weights[nme], g, mom_m[nme], mom_v[nme], "adamw_" + nme)

    order = ("mix_pre_norm", "mix_post_norm", "w_in", "b_forget", "gm_v_norm", "gm_w_s", "gm_b_s", "mix_out_norm",
             "w_out", "ffn_pre_norm", "ffn_post_norm", "w_ffn_in", "w_ffn_out", "w_ple", "ple_norm", "w_ple_gate")
    return (loss, grad_x, *[grads[n_] for n_ in order], *[deltas[n_] for n_ in order], *[new_m[n_] for n_ in order],
            *[new_v[n_] for n_ in order])
```

```python
import functools

import jax
import jax.numpy as jnp
from jax import lax
from jax.experimental import pallas as pl
from jax.experimental.pallas import tpu as pltpu

F32 = jnp.float32
BF16 = jnp.bfloat16
MESH = pl.DeviceIdType.MESH
HIGHEST = lax.Precision.HIGHEST

EPS = 1e-6
NEG_INF = -1e30
N_HEADS = 8
HEAD_DIM = 64
D_ATT = N_HEADS * HEAD_DIM
D_GM = 512
CHUNK = 128
LANES = 128
N_CHIPS = 4
ADAM_LR = 0.001
ADAM_B1 = 0.9
ADAM_B2 = 0.999
ADAM_EPS = 1e-08
ADAM_WD = 0.01
ADAM_STEP = 10
VMEM_LIMIT = 56 * 1024 * 1024


def _cp(sem=None):
    return pltpu.CompilerParams(dimension_semantics=sem, vmem_limit_bytes=VMEM_LIMIT)


def _full(shape):
    return pl.BlockSpec(shape, lambda *_: (0,) * len(shape))


def _rows(tm, width, col_block=0):
    return pl.BlockSpec((tm, width), lambda i: (i, col_block))


def _dot(a, b, dims, precision=None):
    return lax.dot_general(a, b, (dims, ((), ())), preferred_element_type=F32, precision=precision)


NN = ((1,), (0,))
NT = ((1,), (1,))
TN = ((0,), (0,))


def _pick(n, cap):
    best = None
    for t in range(LANES, min(n, cap) + 1, LANES):
        if n % t == 0:
            best = t
    assert best is not None, (n, cap)
    return best


def _mm(a, b, mode, out_dtype, name, tm=None, tn_cap=1024, chip_split=False):
    dims = {"nn": NN, "nt": NT, "tn": TN}[mode]
    if mode == "tn":
        k, m = a.shape
    else:
        m, k = a.shape
    n = b.shape[0] if mode == "nt" else b.shape[1]
    if tm is None:
        tm = 512 if mode == "tn" else 1024
    tm = min(tm, m)
    tn = n // N_CHIPS if chip_split else _pick(n, tn_cap)
    assert m % tm == 0 and n % tn == 0

    def body(a_ref, b_ref, o_ref):
        o_ref[...] = _dot(a_ref[...].astype(BF16), b_ref[...].astype(BF16), dims).astype(out_dtype)

    a_spec = pl.BlockSpec((k, tm), lambda i, j: (0, i)) if mode == "tn" else pl.BlockSpec((tm, k), lambda i, j: (i, 0))
    b_spec = pl.BlockSpec((tn, k), lambda i, j: (j, 0)) if mode == "nt" else pl.BlockSpec((k, tn), lambda i, j: (0, j))
    if chip_split:
        out_shape = jax.ShapeDtypeStruct((N_CHIPS, m, tn), out_dtype)
        out_spec = pl.BlockSpec((None, tm, tn), lambda i, j: (j, i, 0))
    else:
        out_shape = jax.ShapeDtypeStruct((m, n), out_dtype)
        out_spec = pl.BlockSpec((tm, tn), lambda i, j: (i, j))
    return pl.pallas_call(body, out_shape=out_shape, grid=(m // tm, n // tn), in_specs=[a_spec, b_spec],
                          out_specs=out_spec, compiler_params=_cp(("arbitrary", "arbitrary")), name=name)(a, b)


def _rms_inv(x):
    return lax.rsqrt(jnp.mean(x * x, axis=-1, keepdims=True) + EPS)


def _rms_bwd(x, gain, dy):
    inv = _rms_inv(x)
    xhat = x * inv
    dxn = dy if gain is None else dy * gain
    dx = inv * (dxn - xhat * jnp.mean(dxn * xhat, axis=-1, keepdims=True))
    return dx, dy * xhat


def _acc_rows(ref, val):
    s = jnp.sum(val, axis=0, keepdims=True)

    @pl.when(pl.program_id(0) == 0)
    def _():
        ref[...] = s

    @pl.when(pl.program_id(0) > 0)
    def _():
        ref[...] += s


class _Prod:
    def __init__(self, *terms):
        self.terms = terms


def _rowwise(fn, name, rows, consts, outs, accs=(), tm=512):
    arrays, specs, loaders = [], [], []
    t = next(x for x in rows if not isinstance(x, _Prod)).shape[0]
    tm = min(tm, t)
    for x in rows:
        pos = len(arrays)
        if isinstance(x, _Prod):
            dims = []
            for term in x.terms:
                a, b, mode = term[:3]
                if len(term) == 4:
                    f = a.shape[2]
                    specs += [pl.BlockSpec((None, tm, f), lambda i, half=term[3]: (half, i, 0)),
                              pl.BlockSpec((b.shape[0], f), lambda i, half=term[3]: (0, half))]
                else:
                    specs += [_rows(tm, a.shape[1]), _full(b.shape)]
                arrays += [a, b]
                dims.append(NT if mode == "nt" else NN)

            def load(refs, pos=pos, dims=dims):
                total = None
                for k, dm in enumerate(dims):
                    part = _dot(refs[pos + 2 * k][...].astype(BF16), refs[pos + 2 * k + 1][...], dm)
                    total = part if total is None else total + part
                return total
        else:
            arrays.append(x)
            specs.append(_rows(tm, x.shape[1]))

            def load(refs, pos=pos):
                return refs[pos][...].astype(F32)
        loaders.append(load)
    for cst in consts:
        loaders.append(lambda refs, pos=len(arrays): refs[pos][...])
        arrays.append(cst)
        specs.append(_full(cst.shape))
    n_in = len(arrays)

    def body(*refs):
        res = fn(*[ld(refs) for ld in loaders])
        out_refs = refs[n_in:]
        for k, (_, dtype) in enumerate(outs):
            out_refs[k][...] = res[k].astype(dtype)
        for k in range(len(accs)):
            _acc_rows(out_refs[len(outs) + k], res[len(outs) + k])

    out_shape = [jax.ShapeDtypeStruct((t, c), dtype) for c, dtype in outs] + [jax.ShapeDtypeStruct((1, c), F32) for c in accs]
    out_specs = [_rows(tm, c) for c, _ in outs] + [_full((1, c)) for c in accs]
    return pl.pallas_call(body, out_shape=tuple(out_shape), grid=(t // tm,), in_specs=specs, out_specs=tuple(out_specs),
                          compiler_params=_cp(("arbitrary",)), name=name)(*arrays)


def _norm_cast(h, gain, name):
    return _rowwise(lambda x, g: (x * _rms_inv(x) * g,), name, [h], [gain], [(h.shape[1], BF16)])[0]


def _resid_norm(h, o, g_post, g_next, name):
    d = h.shape[1]

    def fn(hv, ov, gp, *gn):
        h1 = hv + ov * _rms_inv(ov) * gp
        hn = h1 * _rms_inv(h1)
        return ov, h1, hn * gn[0] if gn else hn

    return _rowwise(fn, name, [h, o], [g_post] + ([] if g_next is None else [g_next]), [(d, BF16), (d, F32), (d, BF16)])


def _ple_fwd(h2, pe, gl, g_ple, g_next, name):
    d = h2.shape[1]

    def fn(hv, pv, gv, gp, gn):
        h3 = hv + jax.nn.sigmoid(gv) * (pv * _rms_inv(pv) * gp)
        return pv, gv, h3, h3 * _rms_inv(h3) * gn

    return _rowwise(fn, name, [h2, pe, gl], [g_ple, g_next], [(d, BF16), (d, BF16), (d, F32), (d, BF16)])


def _loss_head(y, target, name, tm=512):
    t, d = y.shape

    def body(y_ref, t_ref, dy_ref, loss_ref):
        diff = y_ref[...] - t_ref[...]
        dy_ref[...] = diff * (1.0 / d)
        part = 0.5 * jnp.sum(jnp.mean(diff * diff, axis=-1, keepdims=True), axis=0, keepdims=True)
        part = jnp.broadcast_to(part, (8, LANES))

        @pl.when(pl.program_id(0) == 0)
        def _():
            loss_ref[...] = part

        @pl.when(pl.program_id(0) > 0)
        def _():
            loss_ref[...] += part

    return pl.pallas_call(body, out_shape=(jax.ShapeDtypeStruct((t, d), F32), jax.ShapeDtypeStruct((8, LANES), F32)),
                          grid=(t // tm,), in_specs=[_rows(tm, d)] * 2, out_specs=(_rows(tm, d), _full((8, LANES))),
                          compiler_params=_cp(("arbitrary",)), name=name)(y, target)


def _ple_bwd(dh3, pe, gl, g_ple, name):
    d = dh3.shape[1]

    def fn(dh, pv, gv, gp):
        gate = jax.nn.sigmoid(gv)
        dpe, dg_rows = _rms_bwd(pv, gp, dh * gate)
        return dh * (pv * _rms_inv(pv) * gp) * gate * (1.0 - gate), dpe, dg_rows

    return _rowwise(fn, name, [dh3, pe, gl], [g_ple], [(d, BF16), (d, BF16)], [d])


def _join(d_res, x_a, gain_a, d_a, x_b, gain_b, name, tm=512):
    d = d_res.shape[1]
    has_ga = gain_a is not None
    has_b = x_b is not None

    def fn(*vals):
        it = iter(vals)
        dres, xa, da = next(it), next(it), next(it)
        xb = next(it) if has_b else None
        ga = next(it) if has_ga else None
        dx, dga_rows = _rms_bwd(xa, ga, da)
        dout = dres + dx
        if not has_b:
            return (dout, dga_rows) if has_ga else (dout,)
        db, dgb_rows = _rms_bwd(xb, next(it), dout)
        return (dout, db, dga_rows, dgb_rows) if has_ga else (dout, db, dgb_rows)

    rows = [d_res, x_a, d_a] + ([x_b] if has_b else [])
    consts = ([gain_a] if has_ga else []) + ([gain_b] if has_b else [])
    return _rowwise(fn, name, rows, consts, [(d, F32)] + ([(d, BF16)] if has_b else []), [d] * (has_ga + has_b), tm)


def _ffn_in(hn, w, name, tm=2048, tn=256):
    t, d = hn.shape
    f = w.shape[1] // 2
    tm = min(tm, t)
    nj = f // tn

    def body(h_ref, wa_ref, wb_ref, s_ref, ab_ref):
        h = h_ref[...]
        a = _dot(h, wa_ref[...], NN)
        b = _dot(h, wb_ref[...], NN)
        ab_ref[0] = a.astype(BF16)
        ab_ref[1] = b.astype(BF16)
        s_ref[...] = (a * jax.nn.sigmoid(a) * b).astype(BF16)

    return pl.pallas_call(
        body, out_shape=(jax.ShapeDtypeStruct((t, f), BF16), jax.ShapeDtypeStruct((2, t, f), BF16)), grid=(t // tm, nj),
        in_specs=[pl.BlockSpec((tm, d), lambda i, j: (i, 0)), pl.BlockSpec((d, tn), lambda i, j: (0, j)),
                  pl.BlockSpec((d, tn), lambda i, j: (0, nj + j))],
        out_specs=(pl.BlockSpec((tm, tn), lambda i, j: (i, j)), pl.BlockSpec((2, tm, tn), lambda i, j: (0, i, j))),
        compiler_params=_cp(("arbitrary",) * 2), name=name)(hn, w, w)


def _ffn_bwd(do, w_out, ab, name, tm=2048, tn=256):
    t, d = do.shape
    f = w_out.shape[0]
    tm = min(tm, t)

    def body(d_ref, w_ref, ab_ref, o_ref):
        ds = _dot(d_ref[...], w_ref[...], NT)
        a = ab_ref[0].astype(F32)
        sg = jax.nn.sigmoid(a)
        silu = a * sg
        o_ref[0] = (ds * ab_ref[1].astype(F32) * (sg + silu * (1.0 - sg))).astype(BF16)
        o_ref[1] = (ds * silu).astype(BF16)

    both = pl.BlockSpec((2, tm, tn), lambda i, j: (0, i, j))
    return pl.pallas_call(
        body, out_shape=jax.ShapeDtypeStruct((2, t, f), BF16), grid=(t // tm, f // tn),
        in_specs=[pl.BlockSpec((tm, d), lambda i, j: (i, 0)), pl.BlockSpec((tn, d), lambda i, j: (j, 0)), both],
        out_specs=both, compiler_params=_cp(("arbitrary",) * 2), name=name)(do, w_out, ab)


def _mm_halves_tn(h, x2, name, tm=512):
    t, d = h.shape
    f = x2.shape[2]
    tn = 2 * f // N_CHIPS

    def body(h_ref, x_ref, o_ref):
        o_ref[...] = _dot(h_ref[...], x_ref[...], TN)

    return pl.pallas_call(
        body, out_shape=jax.ShapeDtypeStruct((N_CHIPS, d, tn), F32), grid=(N_CHIPS, d // tm),
        in_specs=[pl.BlockSpec((t, tm), lambda j, i: (0, i)), pl.BlockSpec((None, t, tn), lambda j, i: (j // 2, 0, j % 2))],
        out_specs=pl.BlockSpec((None, tm, tn), lambda j, i: (j, i, 0)),
        compiler_params=_cp(("arbitrary",) * 2), name=name)(h, x2)


def _mixnorm_fwd(att, gm, g_out, name, tm=512):
    t, w = att.shape

    def body(a_ref, m_ref, g_ref, o_ref):
        a, m, g = a_ref[...], m_ref[...], g_ref[...]
        o_ref[:, :w] = (a * _rms_inv(a) * g[:, :w]).astype(BF16)
        o_ref[:, w:] = (m * _rms_inv(m) * g[:, w:]).astype(BF16)

    return pl.pallas_call(body, out_shape=jax.ShapeDtypeStruct((t, 2 * w), BF16), grid=(t // tm,),
                          in_specs=[_rows(tm, w), _rows(tm, w), _full((1, 2 * w))], out_specs=_rows(tm, 2 * w),
                          compiler_params=_cp(("arbitrary",)), name=name)(att, gm, g_out)


def _mixnorm_bwd(att, gm, g_out, dmixed, name):
    w = att.shape[1]

    def fn(a, m, d, g):
        da, dga = _rms_bwd(a, g[:, :w], d[:, :w])
        dm, dgm = _rms_bwd(m, g[:, w:], d[:, w:])
        return da, dm, jnp.concatenate([dga, dgm], axis=1)

    return _rowwise(fn, name, [att, gm, dmixed], [g_out], [(w, F32), (w, F32)], [2 * w])


SCAN_BLOCK = 256


def _gate_fwd(zgf, b_pad, name):
    t = zgf.shape[0]
    fcol = zgf.shape[1] // LANES - 1
    nb = t // SCAN_BLOCK

    def body(f_ref, b_ref, c_ref):
        r = lax.broadcasted_iota(jnp.int32, (SCAN_BLOCK, SCAN_BLOCK), 0)
        s = lax.broadcasted_iota(jnp.int32, (SCAN_BLOCK, SCAN_BLOCK), 1)
        tril = (r >= s).astype(F32)
        head = lax.broadcasted_iota(jnp.int32, (SCAN_BLOCK, LANES), 1) < N_HEADS
        carry = jnp.zeros((1, LANES), F32)
        for blk in range(nb):
            rows = pl.ds(blk * SCAN_BLOCK, SCAN_BLOCK)
            x = f_ref[rows, :] + b_ref[...]
            lf = jnp.minimum(x, 0.0) - jnp.log1p(jnp.exp(-jnp.abs(x)))
            lf = jnp.where(head, lf, 0.0)
            cs = _dot(tril, lf, NN, HIGHEST) + carry
            c_ref[rows, :] = cs
            carry = carry + jnp.sum(lf, axis=0, keepdims=True)

    return pl.pallas_call(body, out_shape=jax.ShapeDtypeStruct((t, LANES), F32),
                          grid=(1,), in_specs=[pl.BlockSpec((t, LANES), lambda i: (0, fcol)), _full((1, LANES))],
                          out_specs=_full((t, LANES)), compiler_params=_cp(("arbitrary",)),
                          name=name)(zgf, b_pad)


def _gate_bwd(dc, zgf, b_pad, name):
    t = zgf.shape[0]
    fcol = zgf.shape[1] // LANES - 1
    nb = t // SCAN_BLOCK

    def body(dc_ref, f_ref, b_ref, dfl_ref, db_ref):
        r = lax.broadcasted_iota(jnp.int32, (SCAN_BLOCK, SCAN_BLOCK), 0)
        s = lax.broadcasted_iota(jnp.int32, (SCAN_BLOCK, SCAN_BLOCK), 1)
        triu = (s >= r).astype(F32)
        head = lax.broadcasted_iota(jnp.int32, (SCAN_BLOCK, LANES), 1) < N_HEADS
        carry = jnp.zeros((1, LANES), F32)
        db = jnp.zeros((1, LANES), F32)
        for blk in reversed(range(nb)):
            rows = pl.ds(blk * SCAN_BLOCK, SCAN_BLOCK)
            dc = dc_ref[rows, :]
            dlf = _dot(triu, dc, NN, HIGHEST) + carry
            x = f_ref[rows, :] + b_ref[...]
            dfl = jnp.where(head, dlf * jax.nn.sigmoid(-x), 0.0)
            dfl_ref[rows, :] = dfl.astype(BF16)
            db = db + jnp.sum(dfl, axis=0, keepdims=True)
            carry = carry + jnp.sum(dc, axis=0, keepdims=True)
        db_ref[...] = db

    return pl.pallas_call(body, out_shape=(jax.ShapeDtypeStruct((t, LANES), BF16), jax.ShapeDtypeStruct((1, LANES), F32)),
                          grid=(1,), in_specs=[_full((t, LANES)), pl.BlockSpec((t, LANES), lambda i: (0, fcol)),
                                               _full((1, LANES))],
                          out_specs=(_full((t, LANES)), _full((1, LANES))), compiler_params=_cp(("arbitrary",)),
                          name=name)(dc, zgf, b_pad)


ATT_BLOCK = 512
PAIRS = N_HEADS // 2
CQ_LANE = HEAD_DIM
CK_LANE = HEAD_DIM + 3
LSE_LANE = HEAD_DIM + 6


def _pick_col(x, idx):
    lane = lax.broadcasted_iota(jnp.int32, x.shape, 1)
    return jnp.sum(jnp.where(lane == idx, x, 0.0), axis=1, keepdims=True)


def _split3(x):
    hi = x.astype(BF16)
    r1 = x - hi.astype(F32)
    mid = r1.astype(BF16)
    lo = (r1 - mid.astype(F32)).astype(BF16)
    return hi, mid, lo


def _lanes_put(base, lane, start, vals):
    out = base
    for n, v in enumerate(vals):
        out = jnp.where(lane == start + n, v, out)
    return out


def _to_first_half(x, hh):
    return x if hh == 0 else pltpu.roll(x, HEAD_DIM, 1)


def _attn_prep(qkv, c, name, tm=512):
    t = qkv.shape[0]
    tm = min(tm, t)

    def body(q_ref, k_ref, v_ref, c_ref, qa_ref, ka_ref, va_ref):
        j = pl.program_id(1)
        lane = lax.broadcasted_iota(jnp.int32, (tm, LANES), 1)
        first = lane < HEAD_DIM
        q2, k2, v2 = q_ref[...].astype(F32), k_ref[...].astype(F32), v_ref[...].astype(F32)
        cc = c_ref[...]
        one = jnp.ones((tm, 1), F32)
        for hh in range(2):
            chi, cmid, clo = [v.astype(F32) for v in _split3(_pick_col(cc, 2 * j + hh))]
            qh = jnp.where(first, _to_first_half(q2, hh) * (HEAD_DIM ** -0.5), 0.0)
            kh = jnp.where(first, _to_first_half(k2, hh), 0.0)
            vh = jnp.where(first, _to_first_half(v2, hh), 0.0)
            qa = _lanes_put(qh, lane, CQ_LANE, [chi, cmid, clo, one, one, one])
            ka = _lanes_put(kh, lane, CQ_LANE, [one, one, one, -chi, -cmid, -clo, one, one, one])
            va = _lanes_put(vh, lane, CQ_LANE, [one, one, one])
            cols = slice(hh * LANES, (hh + 1) * LANES)
            qa_ref[:, cols] = qa.astype(BF16)
            ka_ref[:, cols] = ka.astype(BF16)
            va_ref[:, cols] = va.astype(BF16)

    blk = lambda off: pl.BlockSpec((tm, LANES), lambda i, j: (i, off + j))
    out = pl.BlockSpec((tm, 2 * LANES), lambda i, j: (i, j))
    shp = jax.ShapeDtypeStruct((t, N_HEADS * LANES), BF16)
    return pl.pallas_call(body, out_shape=(shp, shp, shp), grid=(t // tm, PAIRS),
                          in_specs=[blk(0), blk(PAIRS), blk(2 * PAIRS), pl.BlockSpec((tm, LANES), lambda i, j: (i, 0))],
                          out_specs=(out, out, out), compiler_params=_cp(("arbitrary",) * 2), name=name)(qkv, qkv, qkv, c)


def _causal_block(tb):
    return lax.broadcasted_iota(jnp.int32, (tb, tb), 0) >= lax.broadcasted_iota(jnp.int32, (tb, tb), 1)


def _causal_pairs(nb, key_major):
    pairs = [(q, k) for q in range(nb) for k in range(q + 1)]
    if key_major:
        pairs.sort(key=lambda qk: (qk[1], qk[0]))
    return (jnp.array([q for q, _ in pairs], jnp.int32), jnp.array([k for _, k in pairs], jnp.int32))


def _host(side, body, n_lead, arrays, in_specs, out_shapes, out_specs, scratch, grid):
    if side is None:
        return body, tuple(arrays), list(in_specs), list(out_shapes), list(out_specs), list(scratch)
    n_in, n_out, s_in, s_out = len(arrays), len(out_shapes), len(side.arrays), len(side.out_shapes)

    def hosted(*refs):
        ins_end = n_lead + n_in + s_in
        side_in, side_out = refs[n_lead + n_in:ins_end], refs[ins_end + n_out:ins_end + n_out + s_out]
        send, recv = refs[-2:]
        ids = [pl.program_id(ax) for ax in range(len(grid))]
        first, last = ids[0] == 0, ids[0] == grid[0] - 1
        for ax in range(1, len(grid)):
            first, last = first & (ids[ax] == 0), last & (ids[ax] == grid[ax] - 1)

        @pl.when(first)
        def _():
            side.start(side_in, side_out, send, recv)

        body(*refs[:n_lead + n_in], *refs[ins_end:ins_end + n_out], *refs[ins_end + n_out + s_out:-2])

        @pl.when(last)
        def _():
            side.wait(side_in, side_out, send, recv)

    return (hosted, tuple(arrays) + tuple(side.arrays), list(in_specs) + [HBM_SPEC] * s_in,
            list(out_shapes) + side.out_shapes, list(out_specs) + [HBM_SPEC] * s_out, list(scratch) + side.scratch)


def _pair_grid_call(body, tables, arrays, in_specs, out_shapes, out_specs, scratch, side, name):
    grid = (PAIRS, tables[0].shape[0])
    n_out = len(out_shapes)
    body, arrays, in_specs, out_shapes, out_specs, scratch = _host(side, body, 2, arrays, in_specs, out_shapes,
                                                                    out_specs, scratch, grid)
    grid_spec = pltpu.PrefetchScalarGridSpec(num_scalar_prefetch=2, grid=grid, in_specs=in_specs,
                                             out_specs=tuple(out_specs), scratch_shapes=scratch)
    out = pl.pallas_call(body, out_shape=tuple(out_shapes), grid_spec=grid_spec,
                         compiler_params=_cp(("arbitrary",) * 2), name=name)(*tables, *arrays)
    return out[:n_out], out[n_out:]


def _attn_fwd(qa, ka, va, name, side=None):
    t = qa.shape[0]
    tb = min(2 * ATT_BLOCK, t)
    q_tab, k_tab = _causal_pairs(t // tb, key_major=False)

    def body(q_tab_ref, k_tab_ref, q_ref, k_ref, v_ref, o_ref, lse_ref, m0, m1, acc0, acc1):
        qi, kb = q_tab_ref[pl.program_id(1)], k_tab_ref[pl.program_id(1)]
        m_s, acc_s = (m0, m1), (acc0, acc1)

        @pl.when(kb == 0)
        def _():
            for hh in range(2):
                m_s[hh][...] = jnp.full(m_s[hh].shape, NEG_INF, F32)
                acc_s[hh][...] = jnp.zeros(acc_s[hh].shape, F32)

        def step(diagonal):
            for hh in range(2):
                cols = slice(hh * LANES, (hh + 1) * LANES)
                sc = _dot(q_ref[:, cols], k_ref[:, cols], NT)
                if diagonal:
                    sc = jnp.where(_causal_block(tb), sc, NEG_INF)
                m_prev = m_s[hh][...]
                m_new = jnp.maximum(m_prev, jnp.max(sc, axis=1, keepdims=True))
                p = jnp.exp(sc - m_new)
                acc_s[hh][...] = jnp.exp(m_prev - m_new) * acc_s[hh][...] + _dot(p.astype(BF16), v_ref[:, cols], NN)
                m_s[hh][...] = m_new

        @pl.when(kb < qi)
        def _():
            step(False)

        @pl.when(kb == qi)
        def _():
            step(True)
            lane = lax.broadcasted_iota(jnp.int32, (tb, LANES), 1)
            outs, lses = [], []
            for hh in range(2):
                acc = acc_s[hh][...]
                l = _pick_col(acc, CQ_LANE)
                outs.append(acc / l)
                lses.append(m_s[hh][...] + jnp.log(l))
            o_ref[...] = jnp.where(lane < HEAD_DIM, outs[0], pltpu.roll(outs[1], HEAD_DIM, 1))
            lse_ref[...] = jnp.where(lane == 0, lses[0], jnp.where(lane == 1, lses[1], 0.0))

    qrow = lambda j, s, qt, kt: (qt[s], j)
    krow = lambda j, s, qt, kt: (kt[s], j)
    return _pair_grid_call(
        body, (q_tab, k_tab), (qa, ka, va),
        [pl.BlockSpec((tb, 2 * LANES), qrow), pl.BlockSpec((tb, 2 * LANES), krow), pl.BlockSpec((tb, 2 * LANES), krow)],
        [jax.ShapeDtypeStruct((t, D_ATT), F32), jax.ShapeDtypeStruct((t, PAIRS * LANES), F32)],
        [pl.BlockSpec((tb, LANES), qrow), pl.BlockSpec((tb, LANES), qrow)],
        [pltpu.VMEM((tb, 1), F32)] * 2 + [pltpu.VMEM((tb, LANES), F32)] * 2, side, name)


def _attn_bwd_prep(qa, att, lse, datt, name, tm=512):
    t = qa.shape[0]
    tm = min(tm, t)

    def body(qa_ref, o_ref, lse_ref, do_ref, qb_ref, doa_ref):
        lane = lax.broadcasted_iota(jnp.int32, (tm, LANES), 1)
        first = lane < HEAD_DIM
        do = do_ref[...]
        prod = do * o_ref[...]
        lse2 = lse_ref[...]
        for hh in range(2):
            cols = slice(hh * LANES, (hh + 1) * LANES)
            delta = jnp.sum(jnp.where(first if hh == 0 else ~first, prod, 0.0), axis=1, keepdims=True)
            doh = jnp.where(first, _to_first_half(do, hh), 0.0)
            doa_ref[:, cols] = _lanes_put(doh, lane, CQ_LANE, [v.astype(F32) for v in _split3(-delta)]).astype(BF16)
            nl = [v.astype(F32) for v in _split3(-_pick_col(lse2, hh))]
            qb_ref[:, cols] = _lanes_put(qa_ref[:, cols].astype(F32), lane, LSE_LANE, nl).astype(BF16)

    wide = pl.BlockSpec((tm, 2 * LANES), lambda i, j: (i, j))
    pair = pl.BlockSpec((tm, LANES), lambda i, j: (i, j))
    shp = jax.ShapeDtypeStruct((t, N_HEADS * LANES), BF16)
    return pl.pallas_call(body, out_shape=(shp, shp), grid=(t // tm, PAIRS), in_specs=[wide, pair, pair, pair],
                          out_specs=(wide, wide), compiler_params=_cp(("arbitrary",) * 2), name=name)(qa, att, lse, datt)


def _attn_bwd(qb, ka, va, doa, name, side=None):
    t = qb.shape[0]
    tb = min(ATT_BLOCK, t)
    nb = t // tb
    q_tab, k_tab = _causal_pairs(nb, key_major=True)

    def body(q_tab_ref, k_tab_ref, q_ref, k_ref, v_ref, do_ref, dq_ref, dk_ref, dv_ref, dk0, dk1, dv0, dv1):
        qi, kb = q_tab_ref[pl.program_id(1)], k_tab_ref[pl.program_id(1)]
        dk_s, dv_s = (dk0, dk1), (dv0, dv1)

        @pl.when(qi == kb)
        def _():
            for ref in dk_s + dv_s:
                ref[...] = jnp.zeros(ref.shape, F32)

        def step(diagonal):
            rows = pl.ds(pl.multiple_of(qi * tb, tb), tb)
            for hh in range(2):
                cols = slice(hh * LANES, (hh + 1) * LANES)
                q, k, do = q_ref[:, cols], k_ref[:, cols], do_ref[:, cols]
                sc = _dot(q, k, NT)
                if diagonal:
                    sc = jnp.where(_causal_block(tb), sc, NEG_INF)
                p = jnp.exp(sc)
                ds = (p * _dot(do, v_ref[:, cols], NT)).astype(BF16)
                dv_s[hh][...] += _dot(p.astype(BF16), do, TN)
                dk_s[hh][...] += _dot(ds, q, TN)
                dq_new = _dot(ds, k, NN)

                @pl.when(kb == 0)
                def _():
                    dq_ref[rows, cols] = dq_new

                @pl.when(kb > 0)
                def _():
                    dq_ref[rows, cols] += dq_new

        @pl.when(qi == kb)
        def _():
            step(True)

        @pl.when(qi > kb)
        def _():
            step(False)

        @pl.when(qi == nb - 1)
        def _():
            for hh in range(2):
                cols = slice(hh * LANES, (hh + 1) * LANES)
                dk_ref[:, cols] = dk_s[hh][...]
                dv_ref[:, cols] = dv_s[hh][...].astype(BF16)

    qrow = lambda j, s, qt, kt: (qt[s], j)
    krow = lambda j, s, qt, kt: (kt[s], j)
    blk = (tb, 2 * LANES)
    wide = (t, N_HEADS * LANES)
    return _pair_grid_call(
        body, (q_tab, k_tab), (qb, ka, va, doa),
        [pl.BlockSpec(blk, qrow), pl.BlockSpec(blk, krow), pl.BlockSpec(blk, krow), pl.BlockSpec(blk, qrow)],
        [jax.ShapeDtypeStruct(wide, F32), jax.ShapeDtypeStruct(wide, F32), jax.ShapeDtypeStruct(wide, BF16)],
        [pl.BlockSpec((t, 2 * LANES), lambda j, s, qt, kt: (0, j)), pl.BlockSpec(blk, krow), pl.BlockSpec(blk, krow)],
        [pltpu.VMEM((tb, LANES), F32)] * 4, side, name)


def _attn_bwd_post(dqa, dka, dva, name, tm=256):
    t = dqa.shape[0]
    tm = min(tm, t)

    def body(dq_ref, dk_ref, dv_ref, o_ref, dc_ref):
        lane = lax.broadcasted_iota(jnp.int32, (tm, LANES), 1)
        first = lane < HEAD_DIM
        dc = jnp.zeros((tm, LANES), F32)
        for j in range(PAIRS):
            packed = []
            for ref, gain in ((dq_ref, HEAD_DIM ** -0.5), (dk_ref, 1.0), (dv_ref, 1.0)):
                even = ref[:, 2 * j * LANES:(2 * j + 1) * LANES].astype(F32)
                odd = ref[:, (2 * j + 1) * LANES:(2 * j + 2) * LANES].astype(F32)
                packed.append((jnp.where(first, even, pltpu.roll(odd, HEAD_DIM, 1)) * gain).astype(BF16))
                if ref is dq_ref:
                    dc = dc + jnp.where(lane == 2 * j, _pick_col(even, CQ_LANE), 0.0)
                    dc = dc + jnp.where(lane == 2 * j + 1, _pick_col(odd, CQ_LANE), 0.0)
                if ref is dk_ref:
                    dc = dc - jnp.where(lane == 2 * j, _pick_col(even, CK_LANE), 0.0)
                    dc = dc - jnp.where(lane == 2 * j + 1, _pick_col(odd, CK_LANE), 0.0)
            for part, val in enumerate(packed):
                o_ref[:, (part * PAIRS + j) * LANES:(part * PAIRS + j + 1) * LANES] = val
        dc_ref[...] = dc

    wide = _rows(tm, N_HEADS * LANES)
    return pl.pallas_call(body, out_shape=(jax.ShapeDtypeStruct((t, 3 * D_ATT), BF16), jax.ShapeDtypeStruct((t, LANES), F32)),
                          grid=(t // tm,), in_specs=[wide, wide, wide], out_specs=(_rows(tm, 3 * D_ATT), _rows(tm, LANES)),
                          compiler_params=_cp(("arbitrary",)), name=name)(dqa, dka, dva)


GELU_K = 0.7978845608028654
GELU_A = 0.044715


def _gelu(x):
    th = jnp.tanh(GELU_K * (x + GELU_A * x * x * x))
    return 0.5 * x * (1.0 + th), th


def _group_mean_matrix():
    r = jnp.arange(D_GM)[:, None] // HEAD_DIM
    s = jnp.arange(D_GM)[None, :] // HEAD_DIM
    return jnp.where(r == s, 1.0 / HEAD_DIM, 0.0).astype(BF16)


def _group_mean(x, mean_mat):
    hi = x.astype(BF16)
    lo = (x - hi.astype(F32)).astype(BF16)
    return _dot(hi, mean_mat, NN) + _dot(lo, mean_mat, NN)


def _gm_forward_parts(g, w_ref, bias, gain, mean_mat):
    gel, _ = _gelu(g)
    u, vv = gel[:, :D_GM], gel[:, D_GM:]
    mu = _group_mean(vv, mean_mat)
    d = vv - mu
    rstd = lax.rsqrt(_group_mean(d * d, mean_mat) + EPS)
    xhat = d * rstd
    vn = (xhat * gain).astype(BF16)
    first = lax.broadcasted_iota(jnp.int32, (CHUNK, LANES), 1) < HEAD_DIM
    tri = lax.broadcasted_iota(jnp.int32, (CHUNK, CHUNK), 0) >= lax.broadcasted_iota(jnp.int32, (CHUNK, CHUNK), 1)
    wm = [jnp.where(tri, w_ref[grp], 0.0).astype(BF16) for grp in range(w_ref.shape[0])]
    chunks = []
    for ck in range(g.shape[0] // CHUNK):
        parts = []
        for jp in range(D_GM // LANES):
            vp = vn[ck * CHUNK:(ck + 1) * CHUNK, jp * LANES:(jp + 1) * LANES]
            parts.append(jnp.where(first, _dot(wm[2 * jp], vp, NN), _dot(wm[2 * jp + 1], vp, NN)))
        chunks.append(jnp.concatenate(parts, axis=1) + bias)
    return u, xhat, rstd, vn, jnp.concatenate(chunks, axis=0), wm


GM_ROWS = 512


def _gmlp_fwd(zgf, w_s, bias_full, gain, mean_mat, name):
    t = zgf.shape[0]
    tm = min(GM_ROWS, t)

    def body(g_ref, w_ref, b_ref, gain_ref, mm_ref, o_ref):
        u, _, _, _, mixed, _ = _gm_forward_parts(g_ref[...], w_ref, b_ref[...], gain_ref[...], mm_ref[...])
        o_ref[...] = u * mixed

    return pl.pallas_call(body, out_shape=jax.ShapeDtypeStruct((t, D_GM), F32), grid=(t // tm,),
                          in_specs=[_rows(tm, 2 * D_GM), _full(w_s.shape), _full((CHUNK, D_GM)), _full((1, D_GM)),
                                    _full((D_GM, D_GM))],
                          out_specs=_rows(tm, D_GM), compiler_params=_cp(("arbitrary",)),
                          name=name)(zgf, w_s, bias_full, gain, mean_mat)


def _gmlp_bwd(zgf, dgm, w_s, bias_full, gain, mean_mat, name, side=None):
    t = zgf.shape[0]
    tm = min(GM_ROWS, t)

    def body(g_ref, d_ref, w_ref, b_ref, gain_ref, mm_ref, dg_ref, dw_ref, dmix_ref, dgain_ref):
        g, gain, mean_mat = g_ref[...], gain_ref[...], mm_ref[...]
        u, xhat, rstd, vn, mixed, wm = _gm_forward_parts(g, w_ref, b_ref[...], gain, mean_mat)
        dgm_v = d_ref[...]
        du = dgm_v * mixed
        dmixed = dgm_v * u
        dm_b = dmixed.astype(BF16)
        first = lax.broadcasted_iota(jnp.int32, (CHUNK, LANES), 1) < HEAD_DIM
        tri = lax.broadcasted_iota(jnp.int32, (CHUNK, CHUNK), 0) >= lax.broadcasted_iota(jnp.int32, (CHUNK, CHUNK), 1)

        @pl.when(pl.program_id(0) == 0)
        def _():
            dw_ref[...] = jnp.zeros(dw_ref.shape, F32)
            dmix_ref[...] = jnp.zeros(dmix_ref.shape, F32)
            dgain_ref[...] = jnp.zeros(dgain_ref.shape, F32)

        dw = [jnp.zeros((CHUNK, CHUNK), F32) for _ in wm]
        dmix = jnp.zeros((CHUNK, D_GM), F32)
        dvn_chunks = []
        for ck in range(tm // CHUNK):
            rows = slice(ck * CHUNK, (ck + 1) * CHUNK)
            dmix = dmix + dmixed[rows]
            dvn_parts = []
            for jp in range(D_GM // LANES):
                vp = vn[rows, jp * LANES:(jp + 1) * LANES]
                dmp = dm_b[rows, jp * LANES:(jp + 1) * LANES]
                halves = []
                for hh in range(2):
                    sel = first if hh == 0 else ~first
                    grp = 2 * jp + hh
                    dw[grp] = dw[grp] + _dot(jnp.where(sel, dmp, jnp.zeros_like(dmp)), vp, NT)
                    halves.append(_dot(wm[grp], dmp, TN))
                dvn_parts.append(jnp.where(first, halves[0], halves[1]))
            dvn_chunks.append(jnp.concatenate(dvn_parts, axis=1))
        dvn = jnp.concatenate(dvn_chunks, axis=0)
        for grp, dwg in enumerate(dw):
            dw_ref[grp] += jnp.where(tri, dwg, 0.0)
        dmix_ref[...] += dmix
        dgain_ref[...] += jnp.sum(dvn * xhat, axis=0, keepdims=True)
        dxhat = dvn * gain
        m1 = _group_mean(dxhat, mean_mat)
        m2 = _group_mean(dxhat * xhat, mean_mat)
        dvv = rstd * (dxhat - m1 - xhat * m2)
        gel, th = _gelu(g)
        dgel = 0.5 * (1.0 + th) + 0.5 * g * (1.0 - th * th) * GELU_K * (1.0 + 3.0 * GELU_A * g * g)
        dg_ref[...] = (jnp.concatenate([du, dvv], axis=1) * dgel).astype(BF16)

    grid = (t // tm,)
    body, arrays, in_specs, out_shapes, out_specs, scratch = _host(
        side, body, 0, (zgf, dgm, w_s, bias_full, gain, mean_mat),
        [_rows(tm, 2 * D_GM), _rows(tm, D_GM), _full(w_s.shape), _full((CHUNK, D_GM)), _full((1, D_GM)),
         _full((D_GM, D_GM))],
        [jax.ShapeDtypeStruct((t, 2 * D_GM), BF16), jax.ShapeDtypeStruct(w_s.shape, F32),
         jax.ShapeDtypeStruct((CHUNK, D_GM), F32), jax.ShapeDtypeStruct((1, D_GM), F32)],
        [_rows(tm, 2 * D_GM), _full(w_s.shape), _full((CHUNK, D_GM)), _full((1, D_GM))], [], grid)
    out = pl.pallas_call(body, out_shape=tuple(out_shapes), grid=grid, in_specs=in_specs, out_specs=tuple(out_specs),
                         scratch_shapes=scratch, compiler_params=_cp(("arbitrary",)), name=name)(*arrays)
    return out[:4], out[4:]


def _row_tile(r, c, budget=1 << 19):
    best = None
    for tr in range(8, r + 1, 8):
        if r % tr == 0 and tr * c <= budget:
            best = tr
    return best if best is not None else r


def _adamw(w, g, m, v, name):
    nl, r, c = w.shape
    tr = _row_tile(r, c, 1 << 18)
    c1 = 1.0 - ADAM_B1 ** ADAM_STEP
    c2 = 1.0 - ADAM_B2 ** ADAM_STEP

    def body(w_ref, g_ref, m_ref, v_ref, d_ref, mo_ref, vo_ref):
        gv = g_ref[...]
        mn = ADAM_B1 * m_ref[...] + (1.0 - ADAM_B1) * gv
        vn = ADAM_B2 * v_ref[...] + (1.0 - ADAM_B2) * jnp.square(gv)
        mo_ref[...] = mn
        vo_ref[...] = vn
        d_ref[...] = -ADAM_LR * ((mn / c1) / (jnp.sqrt(vn / c2) + ADAM_EPS) + ADAM_WD * w_ref[...])

    spec = pl.BlockSpec((None, tr, c), lambda l, i: (l, i, 0))
    shp = jax.ShapeDtypeStruct(w.shape, F32)
    return pl.pallas_call(body, out_shape=(shp, shp, shp), grid=(nl, r // tr), in_specs=[spec] * 4,
                          out_specs=(spec, spec, spec), compiler_params=_cp(("arbitrary",) * 2), name=name)(w, g, m, v)


def _add_sibling(g, recv, c_idx, wire_dtype, name):
    nj, _, h, c = g.shape
    tr = _row_tile(h, c)

    def body(c_ref, g_ref, r_ref, o_ref):
        o_ref[...] = (g_ref[...] + r_ref[...]).astype(wire_dtype)

    grid_spec = pltpu.PrefetchScalarGridSpec(
        num_scalar_prefetch=1, grid=(nj, h // tr),
        in_specs=[pl.BlockSpec((None, None, tr, c), lambda j, i, c_ref: (j, c_ref[0], i, 0)),
                  pl.BlockSpec((None, tr, c), lambda j, i, c_ref: (j, i, 0))],
        out_specs=pl.BlockSpec((None, tr, c), lambda j, i, c_ref: (j, i, 0)))
    return pl.pallas_call(body, out_shape=jax.ShapeDtypeStruct((nj, h, c), wire_dtype), grid_spec=grid_spec,
                          compiler_params=_cp(("arbitrary",) * 2), name=name)(c_idx, g, recv)


def _add_chips(own, parts, place, name):
    _, h, c = own.shape
    tr = _row_tile(h, c)

    def body(p_ref, o_ref, a_ref, b_ref, c_ref, out_ref):
        out_ref[...] = ((o_ref[...].astype(F32) + a_ref[...].astype(F32)) + b_ref[...].astype(F32)) + c_ref[...].astype(F32)

    def other(k):
        return pl.BlockSpec((None, tr, c), lambda i, p_ref: (jnp.bitwise_xor(p_ref[0], k), i, 0))

    grid_spec = pltpu.PrefetchScalarGridSpec(
        num_scalar_prefetch=1, grid=(h // tr,),
        in_specs=[pl.BlockSpec((None, tr, c), lambda i, p_ref: (p_ref[0], i, 0)), other(1), other(2), other(3)],
        out_specs=pl.BlockSpec((None, tr, c), lambda i, p_ref: (p_ref[1], i, 0)))
    return pl.pallas_call(body, out_shape=jax.ShapeDtypeStruct((2, h, c), F32), grid_spec=grid_spec,
                          compiler_params=_cp(("arbitrary",)), name=name)(place, own, parts, parts, parts)


HBM_SPEC = pl.BlockSpec(memory_space=pltpu.HBM)


def _place():
    x, y, c = lax.axis_index("x"), lax.axis_index("y"), lax.axis_index("c")
    chips = [(1 - x, y), (x, 1 - y), (1 - x, 1 - y)]
    return x, y, c, 2 * x + y, chips, [2 * px + py for px, py in chips]


def _remote(src, dst, send_sem, recv_sem, dev):
    return pltpu.make_async_remote_copy(src_ref=src, dst_ref=dst, send_sem=send_sem, recv_sem=recv_sem,
                                        device_id=dev, device_id_type=MESH)


def _comm_call(body, arrays, out_shapes, sems, name, aliases=None):
    n = len(arrays)
    return pl.pallas_call(
        body, out_shape=tuple(out_shapes), in_specs=[HBM_SPEC] * n, out_specs=tuple([HBM_SPEC] * len(out_shapes)),
        scratch_shapes=[pltpu.SemaphoreType.DMA(s) for s in sems], input_output_aliases=aliases or {},
        compiler_params=pltpu.CompilerParams(has_side_effects=True), name=name)(*arrays)


class _SideCopies:
    def __init__(self, arrays, out_shapes, sem_shape, sends, recvs):
        self.arrays, self.out_shapes, self.sem_shape = list(arrays), list(out_shapes), sem_shape
        self.sends, self.recvs = sends, recvs

    @property
    def scratch(self):
        return [pltpu.SemaphoreType.DMA(self.sem_shape), pltpu.SemaphoreType.DMA(self.sem_shape)]

    def start(self, ins, outs, send, recv):
        for cp in self.sends(ins, outs, send, recv):
            cp.start()

    def wait(self, ins, outs, send, recv):
        for cp in self.recvs(ins, outs, send, recv):
            cp.wait_recv()
        for cp in self.sends(ins, outs, send, recv):
            cp.wait_send()

    def run(self, name):
        n = len(self.arrays)

        def body(*refs):
            ins, outs = refs[:n], refs[n:n + len(self.out_shapes)]
            send, recv = refs[-2:]
            self.start(ins, outs, send, recv)
            self.wait(ins, outs, send, recv)

        return _comm_call(body, self.arrays, self.out_shapes, [self.sem_shape] * 2, name)


def _gather_over_chips(shards):
    n = len(shards)

    def sends(ins, outs, send, recv):
        x, y, c, me, chips, _ = _place()
        over_ici = [_remote(ins[a].at[c], outs[a].at[me, c], send.at[a, k], recv.at[a, k], (px, py, c))
                    for a in range(n) for k, (px, py) in enumerate(chips)]
        return over_ici + [_remote(ins[a], outs[a].at[me], send.at[a, 3], recv.at[a, 3], (x, y, 1 - c)) for a in range(n)]

    def recvs(ins, outs, send, recv):
        x, y, c, me, _, cidx = _place()
        slots = [[outs[a].at[cidx[k], c] for k in range(3)] + [outs[a].at[me]] for a in range(n)]
        return [_remote(blk, blk, send.at[a, k], recv.at[a, k], (x, y, 1 - c))
                for a in range(n) for k, blk in enumerate(slots[a])]

    out_shapes = [jax.ShapeDtypeStruct((N_CHIPS,) + s.shape, s.dtype) for s in shards]
    return _SideCopies(shards, out_shapes, (n, 4), sends, recvs)


def _gather_finish(partial, name):
    n = len(partial)

    def body(*refs):
        part, outs = refs[:n], refs[n:2 * n]
        send, recv = refs[2 * n:]
        x, y, c, _, _, cidx = _place()
        sib = (x, y, 1 - c)
        cps = [_remote(part[a].at[cidx[k], c], outs[a].at[cidx[k], c], send.at[a, k], recv.at[a, k], sib)
               for a in range(n) for k in range(3)]
        for cp in cps:
            cp.start()
        for a in range(n):
            for k in range(3):
                blk = outs[a].at[cidx[k], 1 - c]
                _remote(blk, blk, send.at[a, k], recv.at[a, k], sib).wait_recv()
        for cp in cps:
            cp.wait_send()

    out_shapes = [jax.ShapeDtypeStruct(p.shape, p.dtype) for p in partial]
    return _comm_call(body, list(partial), out_shapes, [(n, 3), (n, 3)], name, aliases={a: a for a in range(n)})


def _sibling_halves(grads):
    n = len(grads)

    def copies(ins, outs, send, recv):
        x, y, c, _, _, _ = _place()
        return [_remote(ins[a].at[j, 1 - c], outs[a].at[j], send.at[a, j], recv.at[a, j], (x, y, 1 - c))
                for a in range(n) for j in range(N_CHIPS)]

    out_shapes = [jax.ShapeDtypeStruct((g.shape[0],) + g.shape[2:], g.dtype) for g in grads]
    return _SideCopies(grads, out_shapes, (n, N_CHIPS), copies, copies)


def _scatter_over_chips(sums):
    n = len(sums)

    def sends(ins, outs, send, recv):
        _, _, c, me, chips, cidx = _place()
        return [_remote(ins[a].at[cidx[k]], outs[a].at[me], send.at[a, k], recv.at[a, k], (px, py, c))
                for a in range(n) for k, (px, py) in enumerate(chips)]

    def recvs(ins, outs, send, recv):
        x, y, c, _, _, cidx = _place()
        return [_remote(outs[a].at[cidx[k]], outs[a].at[cidx[k]], send.at[a, k], recv.at[a, k], (x, y, 1 - c))
                for a in range(n) for k in range(3)]

    return _SideCopies(sums, [jax.ShapeDtypeStruct(s.shape, s.dtype) for s in sums], (n, 3), sends, recvs)


def _exchange_halves(halves, name):
    n = len(halves)

    def body(*refs):
        ins, bufs = refs[:n], refs[n:2 * n]
        send, recv = refs[2 * n:]
        x, y, c, _, _, _ = _place()
        sib = (x, y, 1 - c)
        cps = []
        for a in range(n):
            cp = _remote(ins[a].at[c], bufs[a].at[c], send.at[a], recv.at[a], sib)
            cp.start()
            cps.append(cp)
        for a in range(n):
            blk = bufs[a].at[1 - c]
            _remote(blk, blk, send.at[a], recv.at[a], sib).wait_recv()
        for cp in cps:
            cp.wait_send()

    out_shapes = [jax.ShapeDtypeStruct(s.shape, s.dtype) for s in halves]
    return _comm_call(body, halves, out_shapes, [(n,), (n,)], name, aliases={a: a for a in range(n)})


def _gather_chips(slices):
    n = len(slices)

    def sends(ins, outs, send, recv):
        x, y, c, me, chips, _ = _place()
        return [_remote(ins[a], outs[a].at[me], send.at[a, k], recv.at[a, k], dev)
                for a in range(n) for k, dev in enumerate([(px, py, c) for px, py in chips] + [(x, y, 1 - c)])]

    def recvs(ins, outs, send, recv):
        x, y, c, me, _, cidx = _place()
        return [_remote(outs[a].at[slot], outs[a].at[slot], send.at[a, k], recv.at[a, k], (x, y, 1 - c))
                for a in range(n) for k, slot in enumerate(cidx + [me])]

    out_shapes = [jax.ShapeDtypeStruct((N_CHIPS,) + s.shape, s.dtype) for s in slices]
    return _SideCopies(slices, out_shapes, (n, 4), sends, recvs)


def _rs_siblings(grads):
    return _sibling_halves([g.reshape(g.shape[0], 2, g.shape[1] // 2, g.shape[2]) for g in grads])


def _rs_chips(siblings, recv, c_idx, wire_dtype, tag):
    wire = wire_dtype if isinstance(wire_dtype, list) else [wire_dtype] * len(recv)
    return _scatter_over_chips([_add_sibling(g, r, c_idx, w, "rs_add_sibling_" + tag)
                                for g, r, w in zip(siblings.arrays, recv, wire)])


def _rs_end(scatter, parts, place, tag):
    halves = [_add_chips(s, p, place, "rs_add_chips_" + tag) for s, p in zip(scatter.arrays, parts)]
    both = _exchange_halves(halves, "rs_halves_" + tag)
    return [b.reshape(b.shape[0] * b.shape[1], b.shape[2]) for b in both]


SMALL_ORDER = ("gm_w_s", "mix_pre_norm", "mix_post_norm", "mix_out_norm", "ffn_pre_norm", "ffn_post_norm", "ple_norm",
               "gm_v_norm", "gm_b_s", "b_forget")
SMALL_ROWS_MULTIPLE = 64
SMALL_BLOCK = 8 * LANES


def _chip_columns(pieces, width):
    out = []
    for j in range(N_CHIPS):
        lo, hi, parts, off = j * width, (j + 1) * width, [], 0
        for piece in pieces:
            a, b = max(lo, off), min(hi, off + piece.shape[1])
            if a < b:
                parts.append(piece[:, a - off:b - off])
            off += piece.shape[1]
        out.append(parts[0] if len(parts) == 1 else jnp.concatenate(parts, axis=1))
    return jnp.stack(out)


def _columns(sliced, lo, hi):
    width, parts = sliced.shape[2], []
    for j in range(N_CHIPS):
        a, b = max(lo, j * width), min(hi, (j + 1) * width)
        if a < b:
            parts.append(sliced[j][:, a - j * width:b - j * width])
    return parts[0] if len(parts) == 1 else jnp.concatenate(parts, axis=1)


def _pack_small(parts):
    blocks = []
    for nme in SMALL_ORDER:
        v = parts[nme].reshape(-1)
        pad = (-v.shape[0]) % SMALL_BLOCK
        blocks.append((jnp.pad(v, (0, pad)) if pad else v).reshape(-1, LANES))
    rows = sum(b.shape[0] for b in blocks)
    if rows % SMALL_ROWS_MULTIPLE:
        blocks.append(jnp.zeros((SMALL_ROWS_MULTIPLE - rows % SMALL_ROWS_MULTIPLE, LANES), F32))
    return jnp.concatenate(blocks, axis=0)


def _unpack_small(packed, shapes):
    out, row = {}, 0
    for nme in SMALL_ORDER:
        size = 1
        for s in shapes[nme]:
            size *= s
        rows = -(-size // SMALL_BLOCK) * (SMALL_BLOCK // LANES)
        block = packed[row:row + rows]
        out[nme] = (block if size % SMALL_BLOCK == 0 else block.reshape(-1)[:size]).reshape(shapes[nme])
        row += rows
    return out


def kernel(x, p, mix_pre_norm, mix_post_norm, w_in, b_forget, gm_v_norm, gm_w_s, gm_b_s, mix_out_norm, w_out, ffn_pre_norm, ffn_post_norm, w_ffn_in, w_ffn_out, w_ple, ple_norm, w_ple_gate, loss_target, m_mix_pre_norm, m_mix_post_norm, m_w_in, m_b_forget, m_gm_v_norm, m_gm_w_s, m_gm_b_s, m_mix_out_norm, m_w_out, m_ffn_pre_norm, m_ffn_post_norm, m_w_ffn_in, m_w_ffn_out, m_w_ple, m_ple_norm, m_w_ple_gate, v_mix_pre_norm, v_mix_post_norm, v_w_in, v_b_forget, v_gm_v_norm, v_gm_w_s, v_gm_b_s, v_mix_out_norm, v_w_out, v_ffn_pre_norm, v_ffn_post_norm, v_w_ffn_in, v_w_ffn_out, v_w_ple, v_ple_norm, v_w_ple_gate):
    weights = dict(mix_pre_norm=mix_pre_norm, mix_post_norm=mix_post_norm, w_in=w_in, b_forget=b_forget,
                   gm_v_norm=gm_v_norm, gm_w_s=gm_w_s, gm_b_s=gm_b_s, mix_out_norm=mix_out_norm, w_out=w_out,
                   ffn_pre_norm=ffn_pre_norm, ffn_post_norm=ffn_post_norm, w_ffn_in=w_ffn_in, w_ffn_out=w_ffn_out,
                   w_ple=w_ple, ple_norm=ple_norm, w_ple_gate=w_ple_gate)
    mom_m = dict(mix_pre_norm=m_mix_pre_norm, mix_post_norm=m_mix_post_norm, w_in=m_w_in, b_forget=m_b_forget,
                 gm_v_norm=m_gm_v_norm, gm_w_s=m_gm_w_s, gm_b_s=m_gm_b_s, mix_out_norm=m_mix_out_norm, w_out=m_w_out,
                 ffn_pre_norm=m_ffn_pre_norm, ffn_post_norm=m_ffn_post_norm, w_ffn_in=m_w_ffn_in,
                 w_ffn_out=m_w_ffn_out, w_ple=m_w_ple, ple_norm=m_ple_norm, w_ple_gate=m_w_ple_gate)
    mom_v = dict(mix_pre_norm=v_mix_pre_norm, mix_post_norm=v_mix_post_norm, w_in=v_w_in, b_forget=v_b_forget,
                 gm_v_norm=v_gm_v_norm, gm_w_s=v_gm_w_s, gm_b_s=v_gm_b_s, mix_out_norm=v_mix_out_norm, w_out=v_w_out,
                 ffn_pre_norm=v_ffn_pre_norm, ffn_post_norm=v_ffn_post_norm, w_ffn_in=v_w_ffn_in,
                 w_ffn_out=v_w_ffn_out, w_ple=v_w_ple, ple_norm=v_ple_norm, w_ple_gate=v_w_ple_gate)
    big = ("w_in", "w_out", "w_ffn_in", "w_ffn_out", "w_ple", "w_ple_gate")
    depth = w_in.shape[0]
    t, d = x.shape[1], x.shape[2]
    d_ff = w_ffn_out.shape[1] * N_CHIPS
    c_idx = lax.axis_index("c").astype(jnp.int32).reshape(1)
    place = jnp.stack([2 * lax.axis_index("x") + lax.axis_index("y"), lax.axis_index("c")]).astype(jnp.int32)
    h = x[0]
    target = loss_target[0]
    mean_mat = _group_mean_matrix()

    def row(a, i):
        return a[i].reshape(1, -1)

    def shards_of(i):
        shards = [weights[nme][i].astype(BF16) for nme in big]
        return [s.reshape(2, s.shape[0] // 2, s.shape[1]) for s in shards]

    saved = []
    hn = _norm_cast(h, row(mix_pre_norm, 0), "norm_first")
    by_cols = lambda g: g.transpose(1, 0, 2).reshape(g.shape[1], N_CHIPS * g.shape[2])
    by_rows = lambda g: g.reshape(N_CHIPS * g.shape[1], g.shape[2])
    whole = lambda g: g.reshape(N_CHIPS, g.shape[2] * 2, g.shape[3])
    gather = _gather_over_chips(shards_of(0)[:1])
    w_in_g = _gather_finish(gather.run("gather_first"), "gather_finish_first")[0]
    for i in range(depth):
        w_in_c = whole(w_in_g)
        w_qkv = _columns(w_in_c, 0, 3 * D_ATT)
        w_gf = jnp.concatenate([_columns(w_in_c, 3 * D_ATT + N_HEADS, N_CHIPS * w_in_c.shape[2]),
                                _columns(w_in_c, 3 * D_ATT, 3 * D_ATT + N_HEADS),
                                jnp.zeros((d, LANES - N_HEADS), BF16)], axis=1)
        b_pad = jnp.pad(b_forget[i], (0, LANES - N_HEADS)).reshape(1, LANES)
        bias_full = jnp.repeat(gm_b_s[i].T, HEAD_DIM, axis=1)
        gain_v = row(gm_v_norm, i)

        qkv = _mm(hn, w_qkv, "nn", BF16, "mm_qkv")
        zgf = _mm(hn, w_gf, "nn", F32, "mm_gf", tn_cap=384)
        qa, ka, va = _attn_prep(qkv, _gate_fwd(zgf, b_pad, "gate_fwd"), "attn_prep")
        last = i + 1 == depth
        gather = _gather_over_chips(shards_of(i)[1:] + ([] if last else shards_of(i + 1)[:1]))
        (att, lse), partial = _attn_fwd(qa, ka, va, "attn_fwd_last" if last else "attn_fwd", gather)
        gathered = _gather_finish(partial, "gather_finish_last" if last else "gather_finish")
        w_out_f, w_fi_f, w_fo_f = by_rows(whole(gathered[0])), by_cols(whole(gathered[1])), by_rows(whole(gathered[2]))
        w_ple_f, w_pg_f = by_cols(whole(gathered[3])), by_rows(whole(gathered[4]))
        w_in_g = None if last else gathered[5]
        gm = _gmlp_fwd(zgf, gm_w_s[i], bias_full, gain_v, mean_mat, "gmlp_fwd")
        mixed = _mixnorm_fwd(att, gm, row(mix_out_norm, i), "mixnorm_fwd")
        o, h1, hn2 = _resid_norm(h, _Prod((mixed, w_out_f, "nn")), row(mix_post_norm, i), row(ffn_pre_norm, i),
                                 "out_resid")
        s, ab = _ffn_in(hn2, w_fi_f, "ffn_in")
        o2, h2, hr = _resid_norm(h1, _Prod((s, w_fo_f, "nn")), row(ffn_post_norm, i), None, "ffn_out_resid")
        g_next = row(mix_pre_norm, i + 1) if i + 1 < depth else row(mix_pre_norm, 0)
        pe, gl, h3, hn_next = _ple_fwd(h2, _Prod((p[i, 0], w_ple_f, "nn")), _Prod((hr, w_pg_f, "nn")),
                                       row(ple_norm, i), g_next, "ple_fwd")
        saved.append(dict(h=h, hn=hn, qa=qa, ka=ka, va=va, zgf=zgf, att=att, lse=lse, gm=gm, mixed=mixed,
                          o=o, h1=h1, hn2=hn2, ab=ab, s=s, o2=o2, h2=h2, hr=hr, pe=pe, gl=gl, w_qkv=w_qkv, w_gf=w_gf,
                          w_out=w_out_f, w_fi=w_fi_f, w_fo=w_fo_f, w_pg=w_pg_f, b_pad=b_pad, bias_full=bias_full,
                          gain_v=gain_v))
        h, hn = h3, hn_next

    dh, loss_blk = _loss_head(h, target, "loss_head")
    loss = lax.psum(loss_blk[0, 0], ("x", "y", "c"))

    small = {nme: [None] * depth for nme in SMALL_ORDER}
    big_grads = {nme: [None] * depth for nme in big}
    waiting = []
    for i in reversed(range(depth)):
        sv = saved[i]
        dgl, dpe, small["ple_norm"][i] = _ple_bwd(dh, sv["pe"], sv["gl"], row(ple_norm, i), "ple_bwd")
        g_pg = _mm(sv["hr"], dgl, "tn", F32, "mm_dw_ple_gate").reshape(N_CHIPS, -1, d)
        g_ple = _mm(p[i, 0], dpe, "tn", F32, "mm_dw_ple", chip_split=True)
        dh2, do2, small["ffn_post_norm"][i] = _join(dh, sv["h2"], None, _Prod((dgl, sv["w_pg"], "nt")), sv["o2"],
                                                    row(ffn_post_norm, i), "join_ple")
        dab = _ffn_bwd(do2, sv["w_fo"], sv["ab"], "ffn_bwd")
        g_fo = _mm(sv["s"], do2, "tn", F32, "mm_dw_ffn_out", tm=256).reshape(N_CHIPS, -1, d)
        g_fi = _mm_halves_tn(sv["hn2"], dab, "mm_dw_ffn_in")
        dh1, do, small["ffn_pre_norm"][i], small["mix_post_norm"][i] = _join(
            dh2, sv["h1"], row(ffn_pre_norm, i), _Prod((dab, sv["w_fi"], "nt", 0), (dab, sv["w_fi"], "nt", 1)), sv["o"],
            row(mix_post_norm, i), "join_ffn", tm=256)
        g_out = _mm(sv["mixed"], do, "tn", F32, "mm_dw_out").reshape(N_CHIPS, -1, d)
        datt, dgm, small["mix_out_norm"][i] = _mixnorm_bwd(sv["att"], sv["gm"], row(mix_out_norm, i),
                                                          _Prod((do, sv["w_out"], "nt")), "mixnorm_bwd")
        batch = waiting + [("w_out", i, g_out), ("w_ffn_in", i, g_fi), ("w_ffn_out", i, g_fo), ("w_ple", i, g_ple),
                           ("w_ple_gate", i, g_pg)]
        tag = "layer" if waiting else "top"
        siblings = _rs_siblings([g for _, _, g in batch])
        (dg, small["gm_w_s"][i], dmix_sum, small["gm_v_norm"][i]), recv = _gmlp_bwd(
            sv["zgf"], dgm, gm_w_s[i], sv["bias_full"], sv["gain_v"], mean_mat, "gmlp_bwd_" + tag, siblings)
        small["gm_b_s"][i] = dmix_sum.reshape(CHUNK, N_HEADS, HEAD_DIM).sum(-1).T
        qb, doa = _attn_bwd_prep(sv["qa"], sv["att"], sv["lse"], datt, "attn_bwd_prep")
        scatter = _rs_chips(siblings, recv, c_idx, BF16, tag)
        (dqa, dka, dva), parts = _attn_bwd(qb, sv["ka"], sv["va"], doa, "attn_bwd_" + tag, scatter)
        for (nme, layer, _), g in zip(batch, _rs_end(scatter, parts, place, tag)):
            big_grads[nme][layer] = g
        dqkv, dc = _attn_bwd_post(dqa, dka, dva, "attn_bwd_post")
        dfl, db = _gate_bwd(dc, sv["zgf"], sv["b_pad"], "gate_bwd")
        small["b_forget"][i] = db[0, :N_HEADS]
        dgf = jnp.concatenate([dg, dfl], axis=1)
        g_qkv = _mm(sv["hn"], dqkv, "tn", F32, "mm_dw_qkv")
        g_gf = _mm(sv["hn"], dgf, "tn", F32, "mm_dw_gf", tn_cap=384)
        g_in = _chip_columns([g_qkv, g_gf[:, 2 * D_GM:2 * D_GM + N_HEADS], g_gf[:, :2 * D_GM]], w_in.shape[2])
        dh, small["mix_pre_norm"][i] = _join(dh1, sv["h"], row(mix_pre_norm, i),
                                             _Prod((dqkv, sv["w_qkv"], "nt"), (dgf, sv["w_gf"], "nt")), None, None, "join_mix")
        waiting = [("w_in", i, g_in)]
    grad_x = dh.reshape(1, t, d)

    small_shapes = {nme: weights[nme].shape for nme in SMALL_ORDER}
    small_part = _pack_small({nme: jnp.stack([g.reshape(small_shapes[nme][1:]) for g in small[nme]])
                              for nme in SMALL_ORDER})
    rows_small = small_part.shape[0]
    siblings = _rs_siblings([waiting[0][2], small_part.reshape(N_CHIPS, rows_small // N_CHIPS, LANES)])
    scatter = _rs_chips(siblings, siblings.run("rs_sibling_tail"), c_idx, [BF16, F32], "tail")
    big_grads["w_in"][0], small_slice = _rs_end(scatter, scatter.run("rs_chips_tail"), place, "tail")
    small_all = _gather_chips([small_slice]).run("gather_small")[0].reshape(1, rows_small, LANES)
    sd, sm, sv_ = _adamw(_pack_small({n_: weights[n_] for n_ in SMALL_ORDER})[None], small_all,
                         _pack_small({n_: mom_m[n_] for n_ in SMALL_ORDER})[None],
                         _pack_small({n_: mom_v[n_] for n_ in SMALL_ORDER})[None], "adamw_small")
    grads = _unpack_small(small_all[0], small_shapes)
    deltas = _unpack_small(sd[0], small_shapes)
    new_m = _unpack_small(sm[0], small_shapes)
    new_v = _unpack_small(sv_[0], small_shapes)

    for nme in big:
        g = jnp.stack(big_grads[nme]).reshape(weights[nme].shape)
        grads[nme] = g
        deltas[nme], new_m[nme], new_v[nme] = _adamw(weights[nme], g, mom_m[nme], mom_v[nme], "adamw_" + nme)

    order = ("mix_pre_norm", "mix_post_norm", "w_in", "b_forget", "gm_v_norm", "gm_w_s", "gm_b_s", "mix_out_norm",
             "w_out", "ffn_pre_norm", "ffn_post_norm", "w_ffn_in", "w_ffn_out", "w_ple", "ple_norm", "w_ple_gate")
    return (loss, grad_x, *[grads[n_] for n_ in order], *[deltas[n_] for n_ in order], *[new_m[n_] for n_ in order],
            *[new_v[n_] for n_ in order])
```

```python
import functools

import jax
import jax.numpy as jnp
from jax import lax
from jax.experimental import pallas as pl
from jax.experimental.pallas import tpu as pltpu

F32 = jnp.float32
BF16 = jnp.bfloat16
MESH = pl.DeviceIdType.MESH
HIGHEST = lax.Precision.HIGHEST

EPS = 1e-6
NEG_INF = -1e30
N_HEADS = 8
HEAD_DIM = 64
D_ATT = N_HEADS * HEAD_DIM
D_GM = 512
CHUNK = 128
LANES = 128
N_CHIPS = 4
ADAM_LR = 0.001
ADAM_B1 = 0.9
ADAM_B2 = 0.999
ADAM_EPS = 1e-08
ADAM_WD = 0.01
ADAM_STEP = 10
VMEM_LIMIT = 56 * 1024 * 1024


def _cp(sem=None):
    return pltpu.CompilerParams(dimension_semantics=sem, vmem_limit_bytes=VMEM_LIMIT)


def _full(shape):
    return pl.BlockSpec(shape, lambda *_: (0,) * len(shape))


def _rows(tm, width, col_block=0):
    return pl.BlockSpec((tm, width), lambda i: (i, col_block))


def _dot(a, b, dims, precision=None):
    return lax.dot_general(a, b, (dims, ((), ())), preferred_element_type=F32, precision=precision)


NN = ((1,), (0,))
NT = ((1,), (1,))
TN = ((0,), (0,))


def _pick(n, cap):
    best = None
    for t in range(LANES, min(n, cap) + 1, LANES):
        if n % t == 0:
            best = t
    assert best is not None, (n, cap)
    return best


def _mm(a, b, mode, out_dtype, name, tm=None, tn_cap=1024, chip_split=False):
    dims = {"nn": NN, "nt": NT, "tn": TN}[mode]
    if mode == "tn":
        k, m = a.shape
    else:
        m, k = a.shape
    n = b.shape[0] if mode == "nt" else b.shape[1]
    if tm is None:
        tm = 512 if mode == "tn" else 1024
    tm = min(tm, m)
    tn = n // N_CHIPS if chip_split else _pick(n, tn_cap)
    assert m % tm == 0 and n % tn == 0

    def body(a_ref, b_ref, o_ref):
        o_ref[...] = _dot(a_ref[...].astype(BF16), b_ref[...].astype(BF16), dims).astype(out_dtype)

    a_spec = pl.BlockSpec((k, tm), lambda i, j: (0, i)) if mode == "tn" else pl.BlockSpec((tm, k), lambda i, j: (i, 0))
    b_spec = pl.BlockSpec((tn, k), lambda i, j: (j, 0)) if mode == "nt" else pl.BlockSpec((k, tn), lambda i, j: (0, j))
    if chip_split:
        out_shape = jax.ShapeDtypeStruct((N_CHIPS, m, tn), out_dtype)
        out_spec = pl.BlockSpec((None, tm, tn), lambda i, j: (j, i, 0))
    else:
        out_shape = jax.ShapeDtypeStruct((m, n), out_dtype)
        out_spec = pl.BlockSpec((tm, tn), lambda i, j: (i, j))
    return pl.pallas_call(body, out_shape=out_shape, grid=(m // tm, n // tn), in_specs=[a_spec, b_spec],
                          out_specs=out_spec, compiler_params=_cp(("arbitrary", "arbitrary")), name=name)(a, b)


def _rms_inv(x):
    return lax.rsqrt(jnp.mean(x * x, axis=-1, keepdims=True) + EPS)


def _rms_bwd(x, gain, dy):
    inv = _rms_inv(x)
    xhat = x * inv
    dxn = dy if gain is None else dy * gain
    dx = inv * (dxn - xhat * jnp.mean(dxn * xhat, axis=-1, keepdims=True))
    return dx, dy * xhat


def _acc_rows(ref, val):
    s = jnp.sum(val, axis=0, keepdims=True)

    @pl.when(pl.program_id(0) == 0)
    def _():
        ref[...] = s

    @pl.when(pl.program_id(0) > 0)
    def _():
        ref[...] += s


class _Prod:
    def __init__(self, *terms):
        self.terms = terms


def _rowwise(fn, name, rows, consts, outs, accs=(), tm=512):
    arrays, specs, loaders = [], [], []
    t = next(x for x in rows if not isinstance(x, _Prod)).shape[0]
    tm = min(tm, t)
    for x in rows:
        pos = len(arrays)
        if isinstance(x, _Prod):
            dims = []
            for term in x.terms:
                a, b, mode = term[:3]
                if len(term) == 4:
                    f = a.shape[2]
                    specs += [pl.BlockSpec((None, tm, f), lambda i, half=term[3]: (half, i, 0)),
                              pl.BlockSpec((b.shape[0], f), lambda i, half=term[3]: (0, half))]
                else:
                    specs += [_rows(tm, a.shape[1]), _full(b.shape)]
                arrays += [a, b]
                dims.append(NT if mode == "nt" else NN)

            def load(refs, pos=pos, dims=dims):
                total = None
                for k, dm in enumerate(dims):
                    part = _dot(refs[pos + 2 * k][...].astype(BF16), refs[pos + 2 * k + 1][...], dm)
                    total = part if total is None else total + part
                return total
        else:
            arrays.append(x)
            specs.append(_rows(tm, x.shape[1]))

            def load(refs, pos=pos):
                return refs[pos][...].astype(F32)
        loaders.append(load)
    for cst in consts:
        loaders.append(lambda refs, pos=len(arrays): refs[pos][...])
        arrays.append(cst)
        specs.append(_full(cst.shape))
    n_in = len(arrays)

    def body(*refs):
        res = fn(*[ld(refs) for ld in loaders])
        out_refs = refs[n_in:]
        for k, (_, dtype) in enumerate(outs):
            out_refs[k][...] = res[k].astype(dtype)
        for k in range(len(accs)):
            _acc_rows(out_refs[len(outs) + k], res[len(outs) + k])

    out_shape = [jax.ShapeDtypeStruct((t, c), dtype) for c, dtype in outs] + [jax.ShapeDtypeStruct((1, c), F32) for c in accs]
    out_specs = [_rows(tm, c) for c, _ in outs] + [_full((1, c)) for c in accs]
    return pl.pallas_call(body, out_shape=tuple(out_shape), grid=(t // tm,), in_specs=specs, out_specs=tuple(out_specs),
                          compiler_params=_cp(("arbitrary",)), name=name)(*arrays)


def _norm_cast(h, gain, name):
    return _rowwise(lambda x, g: (x * _rms_inv(x) * g,), name, [h], [gain], [(h.shape[1], BF16)])[0]


def _resid_norm(h, o, g_post, g_next, name):
    d = h.shape[1]

    def fn(hv, ov, gp, *gn):
        h1 = hv + ov * _rms_inv(ov) * gp
        hn = h1 * _rms_inv(h1)
        return ov, h1, hn * gn[0] if gn else hn

    return _rowwise(fn, name, [h, o], [g_post] + ([] if g_next is None else [g_next]), [(d, BF16), (d, F32), (d, BF16)])


def _ple_fwd(h2, pe, gl, g_ple, g_next, name):
    d = h2.shape[1]

    def fn(hv, pv, gv, gp, gn):
        h3 = hv + jax.nn.sigmoid(gv) * (pv * _rms_inv(pv) * gp)
        return pv, gv, h3, h3 * _rms_inv(h3) * gn

    return _rowwise(fn, name, [h2, pe, gl], [g_ple, g_next], [(d, BF16), (d, BF16), (d, F32), (d, BF16)])


def _loss_head(y, target, name, tm=512):
    t, d = y.shape

    def body(y_ref, t_ref, dy_ref, loss_ref):
        diff = y_ref[...] - t_ref[...]
        dy_ref[...] = diff * (1.0 / d)
        part = 0.5 * jnp.sum(jnp.mean(diff * diff, axis=-1, keepdims=True), axis=0, keepdims=True)
        part = jnp.broadcast_to(part, (8, LANES))

        @pl.when(pl.program_id(0) == 0)
        def _():
            loss_ref[...] = part

        @pl.when(pl.program_id(0) > 0)
        def _():
            loss_ref[...] += part

    return pl.pallas_call(body, out_shape=(jax.ShapeDtypeStruct((t, d), F32), jax.ShapeDtypeStruct((8, LANES), F32)),
                          grid=(t // tm,), in_specs=[_rows(tm, d)] * 2, out_specs=(_rows(tm, d), _full((8, LANES))),
                          compiler_params=_cp(("arbitrary",)), name=name)(y, target)


def _ple_bwd(dh3, pe, gl, g_ple, name):
    d = dh3.shape[1]

    def fn(dh, pv, gv, gp):
        gate = jax.nn.sigmoid(gv)
        dpe, dg_rows = _rms_bwd(pv, gp, dh * gate)
        return dh * (pv * _rms_inv(pv) * gp) * gate * (1.0 - gate), dpe, dg_rows

    return _rowwise(fn, name, [dh3, pe, gl], [g_ple], [(d, BF16), (d, BF16)], [d])


def _join(d_res, x_a, gain_a, d_a, x_b, gain_b, name, tm=512):
    d = d_res.shape[1]
    has_ga = gain_a is not None
    has_b = x_b is not None

    def fn(*vals):
        it = iter(vals)
        dres, xa, da = next(it), next(it), next(it)
        xb = next(it) if has_b else None
        ga = next(it) if has_ga else None
        dx, dga_rows = _rms_bwd(xa, ga, da)
        dout = dres + dx
        if not has_b:
            return (dout, dga_rows) if has_ga else (dout,)
        db, dgb_rows = _rms_bwd(xb, next(it), dout)
        return (dout, db, dga_rows, dgb_rows) if has_ga else (dout, db, dgb_rows)

    rows = [d_res, x_a, d_a] + ([x_b] if has_b else [])
    consts = ([gain_a] if has_ga else []) + ([gain_b] if has_b else [])
    return _rowwise(fn, name, rows, consts, [(d, F32)] + ([(d, BF16)] if has_b else []), [d] * (has_ga + has_b), tm)


def _ffn_in(hn, w, name, tm=2048, tn=256):
    t, d = hn.shape
    f = w.shape[1] // 2
    tm = min(tm, t)
    nj = f // tn

    def body(h_ref, wa_ref, wb_ref, s_ref, ab_ref):
        h = h_ref[...]
        a = _dot(h, wa_ref[...], NN)
        b = _dot(h, wb_ref[...], NN)
        ab_ref[0] = a.astype(BF16)
        ab_ref[1] = b.astype(BF16)
        s_ref[...] = (a * jax.nn.sigmoid(a) * b).astype(BF16)

    return pl.pallas_call(
        body, out_shape=(jax.ShapeDtypeStruct((t, f), BF16), jax.ShapeDtypeStruct((2, t, f), BF16)), grid=(t // tm, nj),
        in_specs=[pl.BlockSpec((tm, d), lambda i, j: (i, 0)), pl.BlockSpec((d, tn), lambda i, j: (0, j)),
                  pl.BlockSpec((d, tn), lambda i, j: (0, nj + j))],
        out_specs=(pl.BlockSpec((tm, tn), lambda i, j: (i, j)), pl.BlockSpec((2, tm, tn), lambda i, j: (0, i, j))),
        compiler_params=_cp(("arbitrary",) * 2), name=name)(hn, w, w)


def _ffn_bwd(do, w_out, ab, name, tm=2048, tn=256):
    t, d = do.shape
    f = w_out.shape[0]
    tm = min(tm, t)

    def body(d_ref, w_ref, ab_ref, o_ref):
        ds = _dot(d_ref[...], w_ref[...], NT)
        a = ab_ref[0].astype(F32)
        sg = jax.nn.sigmoid(a)
        silu = a * sg
        o_ref[0] = (ds * ab_ref[1].astype(F32) * (sg + silu * (1.0 - sg))).astype(BF16)
        o_ref[1] = (ds * silu).astype(BF16)

    both = pl.BlockSpec((2, tm, tn), lambda i, j: (0, i, j))
    return pl.pallas_call(
        body, out_shape=jax.ShapeDtypeStruct((2, t, f), BF16), grid=(t // tm, f // tn),
        in_specs=[pl.BlockSpec((tm, d), lambda i, j: (i, 0)), pl.BlockSpec((tn, d), lambda i, j: (j, 0)), both],
        out_specs=both, compiler_params=_cp(("arbitrary",) * 2), name=name)(do, w_out, ab)


def _mm_halves_tn(h, x2, name, tm=512):
    t, d = h.shape
    f = x2.shape[2]
    tn = 2 * f // N_CHIPS

    def body(h_ref, x_ref, o_ref):
        o_ref[...] = _dot(h_ref[...], x_ref[...], TN)

    return pl.pallas_call(
        body, out_shape=jax.ShapeDtypeStruct((N_CHIPS, d, tn), F32), grid=(N_CHIPS, d // tm),
        in_specs=[pl.BlockSpec((t, tm), lambda j, i: (0, i)), pl.BlockSpec((None, t, tn), lambda j, i: (j // 2, 0, j % 2))],
        out_specs=pl.BlockSpec((None, tm, tn), lambda j, i: (j, i, 0)),
        compiler_params=_cp(("arbitrary",) * 2), name=name)(h, x2)


def _mixnorm_fwd(att, gm, g_out, name, tm=512):
    t, w = att.shape

    def body(a_ref, m_ref, g_ref, o_ref):
        a, m, g = a_ref[...], m_ref[...], g_ref[...]
        o_ref[:, :w] = (a * _rms_inv(a) * g[:, :w]).astype(BF16)
        o_ref[:, w:] = (m * _rms_inv(m) * g[:, w:]).astype(BF16)

    return pl.pallas_call(body, out_shape=jax.ShapeDtypeStruct((t, 2 * w), BF16), grid=(t // tm,),
                          in_specs=[_rows(tm, w), _rows(tm, w), _full((1, 2 * w))], out_specs=_rows(tm, 2 * w),
                          compiler_params=_cp(("arbitrary",)), name=name)(att, gm, g_out)


def _mixnorm_bwd(att, gm, g_out, dmixed, name):
    w = att.shape[1]

    def fn(a, m, d, g):
        da, dga = _rms_bwd(a, g[:, :w], d[:, :w])
        dm, dgm = _rms_bwd(m, g[:, w:], d[:, w:])
        return da, dm, jnp.concatenate([dga, dgm], axis=1)

    return _rowwise(fn, name, [att, gm, dmixed], [g_out], [(w, F32), (w, F32)], [2 * w])


SCAN_BLOCK = 256


def _gate_fwd(zgf, b_pad, name):
    t = zgf.shape[0]
    fcol = zgf.shape[1] // LANES - 1
    nb = t // SCAN_BLOCK

    def body(f_ref, b_ref, c_ref):
        r = lax.broadcasted_iota(jnp.int32, (SCAN_BLOCK, SCAN_BLOCK), 0)
        s = lax.broadcasted_iota(jnp.int32, (SCAN_BLOCK, SCAN_BLOCK), 1)
        tril = (r >= s).astype(F32)
        head = lax.broadcasted_iota(jnp.int32, (SCAN_BLOCK, LANES), 1) < N_HEADS
        carry = jnp.zeros((1, LANES), F32)
        for blk in range(nb):
            rows = pl.ds(blk * SCAN_BLOCK, SCAN_BLOCK)
            x = f_ref[rows, :] + b_ref[...]
            lf = jnp.minimum(x, 0.0) - jnp.log1p(jnp.exp(-jnp.abs(x)))
            lf = jnp.where(head, lf, 0.0)
            cs = _dot(tril, lf, NN, HIGHEST) + carry
            c_ref[rows, :] = cs
            carry = carry + jnp.sum(lf, axis=0, keepdims=True)

    return pl.pallas_call(body, out_shape=jax.ShapeDtypeStruct((t, LANES), F32),
                          grid=(1,), in_specs=[pl.BlockSpec((t, LANES), lambda i: (0, fcol)), _full((1, LANES))],
                          out_specs=_full((t, LANES)), compiler_params=_cp(("arbitrary",)),
                          name=name)(zgf, b_pad)


def _gate_bwd(dc, zgf, b_pad, name):
    t = zgf.shape[0]
    fcol = zgf.shape[1] // LANES - 1
    nb = t // SCAN_BLOCK

    def body(dc_ref, f_ref, b_ref, dfl_ref, db_ref):
        r = lax.broadcasted_iota(jnp.int32, (SCAN_BLOCK, SCAN_BLOCK), 0)
        s = lax.broadcasted_iota(jnp.int32, (SCAN_BLOCK, SCAN_BLOCK), 1)
        triu = (s >= r).astype(F32)
        head = lax.broadcasted_iota(jnp.int32, (SCAN_BLOCK, LANES), 1) < N_HEADS
        carry = jnp.zeros((1, LANES), F32)
        db = jnp.zeros((1, LANES), F32)
        for blk in reversed(range(nb)):
            rows = pl.ds(blk * SCAN_BLOCK, SCAN_BLOCK)
            dc = dc_ref[rows, :]
            dlf = _dot(triu, dc, NN, HIGHEST) + carry
            x = f_ref[rows, :] + b_ref[...]
            dfl = jnp.where(head, dlf * jax.nn.sigmoid(-x), 0.0)
            dfl_ref[rows, :] = dfl.astype(BF16)
            db = db + jnp.sum(dfl, axis=0, keepdims=True)
            carry = carry + jnp.sum(dc, axis=0, keepdims=True)
        db_ref[...] = db

    return pl.pallas_call(body, out_shape=(jax.ShapeDtypeStruct((t, LANES), BF16), jax.ShapeDtypeStruct((1, LANES), F32)),
                          grid=(1,), in_specs=[_full((t, LANES)), pl.BlockSpec((t, LANES), lambda i: (0, fcol)),
                                               _full((1, LANES))],
                          out_specs=(_full((t, LANES)), _full((1, LANES))), compiler_params=_cp(("arbitrary",)),
                          name=name)(dc, zgf, b_pad)


ATT_BLOCK = 512
PAIRS = N_HEADS // 2
CQ_LANE = HEAD_DIM
CK_LANE = HEAD_DIM + 3
LSE_LANE = HEAD_DIM + 6


def _pick_col(x, idx):
    lane = lax.broadcasted_iota(jnp.int32, x.shape, 1)
    return jnp.sum(jnp.where(lane == idx, x, 0.0), axis=1, keepdims=True)


def _split3(x):
    hi = x.astype(BF16)
    r1 = x - hi.astype(F32)
    mid = r1.astype(BF16)
    lo = (r1 - mid.astype(F32)).astype(BF16)
    return hi, mid, lo


def _lanes_put(base, lane, start, vals):
    out = base
    for n, v in enumerate(vals):
        out = jnp.where(lane == start + n, v, out)
    return out


def _to_first_half(x, hh):
    return x if hh == 0 else pltpu.roll(x, HEAD_DIM, 1)


def _attn_prep(qkv, c, name, tm=512):
    t = qkv.shape[0]
    tm = min(tm, t)

    def body(q_ref, k_ref, v_ref, c_ref, qa_ref, ka_ref, va_ref):
        j = pl.program_id(1)
        lane = lax.broadcasted_iota(jnp.int32, (tm, LANES), 1)
        first = lane < HEAD_DIM
        q2, k2, v2 = q_ref[...].astype(F32), k_ref[...].astype(F32), v_ref[...].astype(F32)
        cc = c_ref[...]
        one = jnp.ones((tm, 1), F32)
        for hh in range(2):
            chi, cmid, clo = [v.astype(F32) for v in _split3(_pick_col(cc, 2 * j + hh))]
            qh = jnp.where(first, _to_first_half(q2, hh) * (HEAD_DIM ** -0.5), 0.0)
            kh = jnp.where(first, _to_first_half(k2, hh), 0.0)
            vh = jnp.where(first, _to_first_half(v2, hh), 0.0)
            qa = _lanes_put(qh, lane, CQ_LANE, [chi, cmid, clo, one, one, one])
            ka = _lanes_put(kh, lane, CQ_LANE, [one, one, one, -chi, -cmid, -clo, one, one, one])
            va = _lanes_put(vh, lane, CQ_LANE, [one, one, one])
            cols = slice(hh * LANES, (hh + 1) * LANES)
            qa_ref[:, cols] = qa.astype(BF16)
            ka_ref[:, cols] = ka.astype(BF16)
            va_ref[:, cols] = va.astype(BF16)

    blk = lambda off: pl.BlockSpec((tm, LANES), lambda i, j: (i, off + j))
    out = pl.BlockSpec((tm, 2 * LANES), lambda i, j: (i, j))
    shp = jax.ShapeDtypeStruct((t, N_HEADS * LANES), BF16)
    return pl.pallas_call(body, out_shape=(shp, shp, shp), grid=(t // tm, PAIRS),
                          in_specs=[blk(0), blk(PAIRS), blk(2 * PAIRS), pl.BlockSpec((tm, LANES), lambda i, j: (i, 0))],
                          out_specs=(out, out, out), compiler_params=_cp(("arbitrary",) * 2), name=name)(qkv, qkv, qkv, c)


def _causal_block(tb):
    return lax.broadcasted_iota(jnp.int32, (tb, tb), 0) >= lax.broadcasted_iota(jnp.int32, (tb, tb), 1)


def _causal_pairs(nb, key_major):
    pairs = [(q, k) for q in range(nb) for k in range(q + 1)]
    if key_major:
        pairs.sort(key=lambda qk: (qk[1], qk[0]))
    return (jnp.array([q for q, _ in pairs], jnp.int32), jnp.array([k for _, k in pairs], jnp.int32))


def _host(side, body, n_lead, arrays, in_specs, out_shapes, out_specs, scratch, grid):
    if side is None:
        return body, tuple(arrays), list(in_specs), list(out_shapes), list(out_specs), list(scratch)
    n_in, n_out, s_in, s_out = len(arrays), len(out_shapes), len(side.arrays), len(side.out_shapes)

    def hosted(*refs):
        ins_end = n_lead + n_in + s_in
        side_in, side_out = refs[n_lead + n_in:ins_end], refs[ins_end + n_out:ins_end + n_out + s_out]
        send, recv = refs[-2:]
        ids = [pl.program_id(ax) for ax in range(len(grid))]
        first, last = ids[0] == 0, ids[0] == grid[0] - 1
        for ax in range(1, len(grid)):
            first, last = first & (ids[ax] == 0), last & (ids[ax] == grid[ax] - 1)

        @pl.when(first)
        def _():
            side.start(side_in, side_out, send, recv)

        body(*refs[:n_lead + n_in], *refs[ins_end:ins_end + n_out], *refs[ins_end + n_out + s_out:-2])

        @pl.when(last)
        def _():
            side.wait(side_in, side_out, send, recv)

    return (hosted, tuple(arrays) + tuple(side.arrays), list(in_specs) + [HBM_SPEC] * s_in,
            list(out_shapes) + side.out_shapes, list(out_specs) + [HBM_SPEC] * s_out, list(scratch) + side.scratch)


def _pair_grid_call(body, tables, arrays, in_specs, out_shapes, out_specs, scratch, side, name):
    grid = (PAIRS, tables[0].shape[0])
    n_out = len(out_shapes)
    body, arrays, in_specs, out_shapes, out_specs, scratch = _host(side, body, 2, arrays, in_specs, out_shapes,
                                                                    out_specs, scratch, grid)
    grid_spec = pltpu.PrefetchScalarGridSpec(num_scalar_prefetch=2, grid=grid, in_specs=in_specs,
                                             out_specs=tuple(out_specs), scratch_shapes=scratch)
    out = pl.pallas_call(body, out_shape=tuple(out_shapes), grid_spec=grid_spec,
                         compiler_params=_cp(("arbitrary",) * 2), name=name)(*tables, *arrays)
    return out[:n_out], out[n_out:]


def _attn_fwd(qa, ka, va, name, side=None):
    t = qa.shape[0]
    tb = min(2 * ATT_BLOCK, t)
    q_tab, k_tab = _causal_pairs(t // tb, key_major=False)

    def body(q_tab_ref, k_tab_ref, q_ref, k_ref, v_ref, o_ref, lse_ref, m0, m1, acc0, acc1):
        qi, kb = q_tab_ref[pl.program_id(1)], k_tab_ref[pl.program_id(1)]
        m_s, acc_s = (m0, m1), (acc0, acc1)

        @pl.when(kb == 0)
        def _():
            for hh in range(2):
                m_s[hh][...] = jnp.full(m_s[hh].shape, NEG_INF, F32)
                acc_s[hh][...] = jnp.zeros(acc_s[hh].shape, F32)

        def step(diagonal):
            for hh in range(2):
                cols = slice(hh * LANES, (hh + 1) * LANES)
                sc = _dot(q_ref[:, cols], k_ref[:, cols], NT)
                if diagonal:
                    sc = jnp.where(_causal_block(tb), sc, NEG_INF)
                m_prev = m_s[hh][...]
                m_new = jnp.maximum(m_prev, jnp.max(sc, axis=1, keepdims=True))
                p = jnp.exp(sc - m_new)
                acc_s[hh][...] = jnp.exp(m_prev - m_new) * acc_s[hh][...] + _dot(p.astype(BF16), v_ref[:, cols], NN)
                m_s[hh][...] = m_new

        @pl.when(kb < qi)
        def _():
            step(False)

        @pl.when(kb == qi)
        def _():
            step(True)
            lane = lax.broadcasted_iota(jnp.int32, (tb, LANES), 1)
            outs, lses = [], []
            for hh in range(2):
                acc = acc_s[hh][...]
                l = _pick_col(acc, CQ_LANE)
                outs.append(acc / l)
                lses.append(m_s[hh][...] + jnp.log(l))
            o_ref[...] = jnp.where(lane < HEAD_DIM, outs[0], pltpu.roll(outs[1], HEAD_DIM, 1))
            lse_ref[...] = jnp.where(lane == 0, lses[0], jnp.where(lane == 1, lses[1], 0.0))

    qrow = lambda j, s, qt, kt: (qt[s], j)
    krow = lambda j, s, qt, kt: (kt[s], j)
    return _pair_grid_call(
        body, (q_tab, k_tab), (qa, ka, va),
        [pl.BlockSpec((tb, 2 * LANES), qrow), pl.BlockSpec((tb, 2 * LANES), krow), pl.BlockSpec((tb, 2 * LANES), krow)],
        [jax.ShapeDtypeStruct((t, D_ATT), F32), jax.ShapeDtypeStruct((t, PAIRS * LANES), F32)],
        [pl.BlockSpec((tb, LANES), qrow), pl.BlockSpec((tb, LANES), qrow)],
        [pltpu.VMEM((tb, 1), F32)] * 2 + [pltpu.VMEM((tb, LANES), F32)] * 2, side, name)


def _attn_bwd_prep(qa, att, lse, datt, name, tm=512):
    t = qa.shape[0]
    tm = min(tm, t)

    def body(qa_ref, o_ref, lse_ref, do_ref, qb_ref, doa_ref):
        lane = lax.broadcasted_iota(jnp.int32, (tm, LANES), 1)
        first = lane < HEAD_DIM
        do = do_ref[...]
        prod = do * o_ref[...]
        lse2 = lse_ref[...]
        for hh in range(2):
            cols = slice(hh * LANES, (hh + 1) * LANES)
            delta = jnp.sum(jnp.where(first if hh == 0 else ~first, prod, 0.0), axis=1, keepdims=True)
            doh = jnp.where(first, _to_first_half(do, hh), 0.0)
            doa_ref[:, cols] = _lanes_put(doh, lane, CQ_LANE, [v.astype(F32) for v in _split3(-delta)]).astype(BF16)
            nl = [v.astype(F32) for v in _split3(-_pick_col(lse2, hh))]
            qb_ref[:, cols] = _lanes_put(qa_ref[:, cols].astype(F32), lane, LSE_LANE, nl).astype(BF16)

    wide = pl.BlockSpec((tm, 2 * LANES), lambda i, j: (i, j))
    pair = pl.BlockSpec((tm, LANES), lambda i, j: (i, j))
    shp = jax.ShapeDtypeStruct((t, N_HEADS * LANES), BF16)
    return pl.pallas_call(body, out_shape=(shp, shp), grid=(t // tm, PAIRS), in_specs=[wide, pair, pair, pair],
                          out_specs=(wide, wide), compiler_params=_cp(("arbitrary",) * 2), name=name)(qa, att, lse, datt)


def _attn_bwd(qb, ka, va, doa, name, side=None):
    t = qb.shape[0]
    tb = min(2 * ATT_BLOCK, t)
    nb = t // tb
    q_tab, k_tab = _causal_pairs(nb, key_major=True)

    def body(q_tab_ref, k_tab_ref, q_ref, k_ref, v_ref, do_ref, dq_ref, dk_ref, dv_ref, dk0, dk1, dv0, dv1):
        qi, kb = q_tab_ref[pl.program_id(1)], k_tab_ref[pl.program_id(1)]
        dk_s, dv_s = (dk0, dk1), (dv0, dv1)

        @pl.when(qi == kb)
        def _():
            for ref in dk_s + dv_s:
                ref[...] = jnp.zeros(ref.shape, F32)

        def step(diagonal):
            rows = pl.ds(pl.multiple_of(qi * tb, tb), tb)
            for hh in range(2):
                cols = slice(hh * LANES, (hh + 1) * LANES)
                q, k, do = q_ref[:, cols], k_ref[:, cols], do_ref[:, cols]
                sc = _dot(q, k, NT)
                if diagonal:
                    sc = jnp.where(_causal_block(tb), sc, NEG_INF)
                p = jnp.exp(sc)
                ds = (p * _dot(do, v_ref[:, cols], NT)).astype(BF16)
                dv_s[hh][...] += _dot(p.astype(BF16), do, TN)
                dk_s[hh][...] += _dot(ds, q, TN)
                dq_new = _dot(ds, k, NN)

                @pl.when(kb == 0)
                def _():
                    dq_ref[rows, cols] = dq_new

                @pl.when(kb > 0)
                def _():
                    dq_ref[rows, cols] += dq_new

        @pl.when(qi == kb)
        def _():
            step(True)

        @pl.when(qi > kb)
        def _():
            step(False)

        @pl.when(qi == nb - 1)
        def _():
            for hh in range(2):
                cols = slice(hh * LANES, (hh + 1) * LANES)
                dk_ref[:, cols] = dk_s[hh][...]
                dv_ref[:, cols] = dv_s[hh][...].astype(BF16)

    qrow = lambda j, s, qt, kt: (qt[s], j)
    krow = lambda j, s, qt, kt: (kt[s], j)
    blk = (tb, 2 * LANES)
    wide = (t, N_HEADS * LANES)
    return _pair_grid_call(
        body, (q_tab, k_tab), (qb, ka, va, doa),
        [pl.BlockSpec(blk, qrow), pl.BlockSpec(blk, krow), pl.BlockSpec(blk, krow), pl.BlockSpec(blk, qrow)],
        [jax.ShapeDtypeStruct(wide, F32), jax.ShapeDtypeStruct(wide, F32), jax.ShapeDtypeStruct(wide, BF16)],
        [pl.BlockSpec((t, 2 * LANES), lambda j, s, qt, kt: (0, j)), pl.BlockSpec(blk, krow), pl.BlockSpec(blk, krow)],
        [pltpu.VMEM((tb, LANES), F32)] * 4, side, name)


def _attn_bwd_post(dqa, dka, dva, name, tm=256):
    t = dqa.shape[0]
    tm = min(tm, t)

    def body(dq_ref, dk_ref, dv_ref, o_ref, dc_ref):
        lane = lax.broadcasted_iota(jnp.int32, (tm, LANES), 1)
        first = lane < HEAD_DIM
        dc = jnp.zeros((tm, LANES), F32)
        for j in range(PAIRS):
            packed = []
            for ref, gain in ((dq_ref, HEAD_DIM ** -0.5), (dk_ref, 1.0), (dv_ref, 1.0)):
                even = ref[:, 2 * j * LANES:(2 * j + 1) * LANES].astype(F32)
                odd = ref[:, (2 * j + 1) * LANES:(2 * j + 2) * LANES].astype(F32)
                packed.append((jnp.where(first, even, pltpu.roll(odd, HEAD_DIM, 1)) * gain).astype(BF16))
                if ref is dq_ref:
                    dc = dc + jnp.where(lane == 2 * j, _pick_col(even, CQ_LANE), 0.0)
                    dc = dc + jnp.where(lane == 2 * j + 1, _pick_col(odd, CQ_LANE), 0.0)
                if ref is dk_ref:
                    dc = dc - jnp.where(lane == 2 * j, _pick_col(even, CK_LANE), 0.0)
                    dc = dc - jnp.where(lane == 2 * j + 1, _pick_col(odd, CK_LANE), 0.0)
            for part, val in enumerate(packed):
                o_ref[:, (part * PAIRS + j) * LANES:(part * PAIRS + j + 1) * LANES] = val
        dc_ref[...] = dc

    wide = _rows(tm, N_HEADS * LANES)
    return pl.pallas_call(body, out_shape=(jax.ShapeDtypeStruct((t, 3 * D_ATT), BF16), jax.ShapeDtypeStruct((t, LANES), F32)),
                          grid=(t // tm,), in_specs=[wide, wide, wide], out_specs=(_rows(tm, 3 * D_ATT), _rows(tm, LANES)),
                          compiler_params=_cp(("arbitrary",)), name=name)(dqa, dka, dva)


GELU_K = 0.7978845608028654
GELU_A = 0.044715


def _gelu(x):
    th = jnp.tanh(GELU_K * (x + GELU_A * x * x * x))
    return 0.5 * x * (1.0 + th), th


def _group_mean_matrix():
    r = jnp.arange(D_GM)[:, None] // HEAD_DIM
    s = jnp.arange(D_GM)[None, :] // HEAD_DIM
    return jnp.where(r == s, 1.0 / HEAD_DIM, 0.0).astype(BF16)


def _group_mean(x, mean_mat):
    hi = x.astype(BF16)
    lo = (x - hi.astype(F32)).astype(BF16)
    return _dot(hi, mean_mat, NN) + _dot(lo, mean_mat, NN)


def _gm_forward_parts(g, w_ref, bias, gain, mean_mat):
    gel, _ = _gelu(g)
    u, vv = gel[:, :D_GM], gel[:, D_GM:]
    mu = _group_mean(vv, mean_mat)
    d = vv - mu
    rstd = lax.rsqrt(_group_mean(d * d, mean_mat) + EPS)
    xhat = d * rstd
    vn = (xhat * gain).astype(BF16)
    first = lax.broadcasted_iota(jnp.int32, (CHUNK, LANES), 1) < HEAD_DIM
    tri = lax.broadcasted_iota(jnp.int32, (CHUNK, CHUNK), 0) >= lax.broadcasted_iota(jnp.int32, (CHUNK, CHUNK), 1)
    wm = [jnp.where(tri, w_ref[grp], 0.0).astype(BF16) for grp in range(w_ref.shape[0])]
    chunks = []
    for ck in range(g.shape[0] // CHUNK):
        parts = []
        for jp in range(D_GM // LANES):
            vp = vn[ck * CHUNK:(ck + 1) * CHUNK, jp * LANES:(jp + 1) * LANES]
            parts.append(jnp.where(first, _dot(wm[2 * jp], vp, NN), _dot(wm[2 * jp + 1], vp, NN)))
        chunks.append(jnp.concatenate(parts, axis=1) + bias)
    return u, xhat, rstd, vn, jnp.concatenate(chunks, axis=0), wm


GM_ROWS = 512


def _gmlp_fwd(zgf, w_s, bias_full, gain, mean_mat, name):
    t = zgf.shape[0]
    tm = min(GM_ROWS, t)

    def body(g_ref, w_ref, b_ref, gain_ref, mm_ref, o_ref):
        u, _, _, _, mixed, _ = _gm_forward_parts(g_ref[...], w_ref, b_ref[...], gain_ref[...], mm_ref[...])
        o_ref[...] = u * mixed

    return pl.pallas_call(body, out_shape=jax.ShapeDtypeStruct((t, D_GM), F32), grid=(t // tm,),
                          in_specs=[_rows(tm, 2 * D_GM), _full(w_s.shape), _full((CHUNK, D_GM)), _full((1, D_GM)),
                                    _full((D_GM, D_GM))],
                          out_specs=_rows(tm, D_GM), compiler_params=_cp(("arbitrary",)),
                          name=name)(zgf, w_s, bias_full, gain, mean_mat)


def _gmlp_bwd(zgf, dgm, w_s, bias_full, gain, mean_mat, name, side=None):
    t = zgf.shape[0]
    tm = min(GM_ROWS, t)

    def body(g_ref, d_ref, w_ref, b_ref, gain_ref, mm_ref, dg_ref, dw_ref, dmix_ref, dgain_ref):
        g, gain, mean_mat = g_ref[...], gain_ref[...], mm_ref[...]
        u, xhat, rstd, vn, mixed, wm = _gm_forward_parts(g, w_ref, b_ref[...], gain, mean_mat)
        dgm_v = d_ref[...]
        du = dgm_v * mixed
        dmixed = dgm_v * u
        dm_b = dmixed.astype(BF16)
        first = lax.broadcasted_iota(jnp.int32, (CHUNK, LANES), 1) < HEAD_DIM
        tri = lax.broadcasted_iota(jnp.int32, (CHUNK, CHUNK), 0) >= lax.broadcasted_iota(jnp.int32, (CHUNK, CHUNK), 1)

        @pl.when(pl.program_id(0) == 0)
        def _():
            dw_ref[...] = jnp.zeros(dw_ref.shape, F32)
            dmix_ref[...] = jnp.zeros(dmix_ref.shape, F32)
            dgain_ref[...] = jnp.zeros(dgain_ref.shape, F32)

        dw = [jnp.zeros((CHUNK, CHUNK), F32) for _ in wm]
        dmix = jnp.zeros((CHUNK, D_GM), F32)
        dvn_chunks = []
        for ck in range(tm // CHUNK):
            rows = slice(ck * CHUNK, (ck + 1) * CHUNK)
            dmix = dmix + dmixed[rows]
            dvn_parts = []
            for jp in range(D_GM // LANES):
                vp = vn[rows, jp * LANES:(jp + 1) * LANES]
                dmp = dm_b[rows, jp * LANES:(jp + 1) * LANES]
                halves = []
                for hh in range(2):
                    sel = first if hh == 0 else ~first
                    grp = 2 * jp + hh
                    dw[grp] = dw[grp] + _dot(jnp.where(sel, dmp, jnp.zeros_like(dmp)), vp, NT)
                    halves.append(_dot(wm[grp], dmp, TN))
                dvn_parts.append(jnp.where(first, halves[0], halves[1]))
            dvn_chunks.append(jnp.concatenate(dvn_parts, axis=1))
        dvn = jnp.concatenate(dvn_chunks, axis=0)
        for grp, dwg in enumerate(dw):
            dw_ref[grp] += jnp.where(tri, dwg, 0.0)
        dmix_ref[...] += dmix
        dgain_ref[...] += jnp.sum(dvn * xhat, axis=0, keepdims=True)
        dxhat = dvn * gain
        m1 = _group_mean(dxhat, mean_mat)
        m2 = _group_mean(dxhat * xhat, mean_mat)
        dvv = rstd * (dxhat - m1 - xhat * m2)
        gel, th = _gelu(g)
        dgel = 0.5 * (1.0 + th) + 0.5 * g * (1.0 - th * th) * GELU_K * (1.0 + 3.0 * GELU_A * g * g)
        dg_ref[...] = (jnp.concatenate([du, dvv], axis=1) * dgel).astype(BF16)

    grid = (t // tm,)
    body, arrays, in_specs, out_shapes, out_specs, scratch = _host(
        side, body, 0, (zgf, dgm, w_s, bias_full, gain, mean_mat),
        [_rows(tm, 2 * D_GM), _rows(tm, D_GM), _full(w_s.shape), _full((CHUNK, D_GM)), _full((1, D_GM)),
         _full((D_GM, D_GM))],
        [jax.ShapeDtypeStruct((t, 2 * D_GM), BF16), jax.ShapeDtypeStruct(w_s.shape, F32),
         jax.ShapeDtypeStruct((CHUNK, D_GM), F32), jax.ShapeDtypeStruct((1, D_GM), F32)],
        [_rows(tm, 2 * D_GM), _full(w_s.shape), _full((CHUNK, D_GM)), _full((1, D_GM))], [], grid)
    out = pl.pallas_call(body, out_shape=tuple(out_shapes), grid=grid, in_specs=in_specs, out_specs=tuple(out_specs),
                         scratch_shapes=scratch, compiler_params=_cp(("arbitrary",)), name=name)(*arrays)
    return out[:4], out[4:]


def _row_tile(r, c, budget=1 << 19):
    best = None
    for tr in range(8, r + 1, 8):
        if r % tr == 0 and tr * c <= budget:
            best = tr
    return best if best is not None else r


def _adamw(w, g, m, v, name):
    nl, r, c = w.shape
    tr = _row_tile(r, c, 1 << 18)
    c1 = 1.0 - ADAM_B1 ** ADAM_STEP
    c2 = 1.0 - ADAM_B2 ** ADAM_STEP

    def body(w_ref, g_ref, m_ref, v_ref, d_ref, mo_ref, vo_ref):
        gv = g_ref[...]
        mn = ADAM_B1 * m_ref[...] + (1.0 - ADAM_B1) * gv
        vn = ADAM_B2 * v_ref[...] + (1.0 - ADAM_B2) * jnp.square(gv)
        mo_ref[...] = mn
        vo_ref[...] = vn
        d_ref[...] = -ADAM_LR * ((mn / c1) / (jnp.sqrt(vn / c2) + ADAM_EPS) + ADAM_WD * w_ref[...])

    spec = pl.BlockSpec((None, tr, c), lambda l, i: (l, i, 0))
    shp = jax.ShapeDtypeStruct(w.shape, F32)
    return pl.pallas_call(body, out_shape=(shp, shp, shp), grid=(nl, r // tr), in_specs=[spec] * 4,
                          out_specs=(spec, spec, spec), compiler_params=_cp(("arbitrary",) * 2), name=name)(w, g, m, v)


def _add_sibling(g, recv, c_idx, wire_dtype, name):
    nj, _, h, c = g.shape
    tr = _row_tile(h, c)

    def body(c_ref, g_ref, r_ref, o_ref):
        o_ref[...] = (g_ref[...] + r_ref[...]).astype(wire_dtype)

    grid_spec = pltpu.PrefetchScalarGridSpec(
        num_scalar_prefetch=1, grid=(nj, h // tr),
        in_specs=[pl.BlockSpec((None, None, tr, c), lambda j, i, c_ref: (j, c_ref[0], i, 0)),
                  pl.BlockSpec((None, tr, c), lambda j, i, c_ref: (j, i, 0))],
        out_specs=pl.BlockSpec((None, tr, c), lambda j, i, c_ref: (j, i, 0)))
    return pl.pallas_call(body, out_shape=jax.ShapeDtypeStruct((nj, h, c), wire_dtype), grid_spec=grid_spec,
                          compiler_params=_cp(("arbitrary",) * 2), name=name)(c_idx, g, recv)


def _add_chips(own, parts, place, name):
    _, h, c = own.shape
    tr = _row_tile(h, c)

    def body(p_ref, o_ref, a_ref, b_ref, c_ref, out_ref):
        out_ref[...] = ((o_ref[...].astype(F32) + a_ref[...].astype(F32)) + b_ref[...].astype(F32)) + c_ref[...].astype(F32)

    def other(k):
        return pl.BlockSpec((None, tr, c), lambda i, p_ref: (jnp.bitwise_xor(p_ref[0], k), i, 0))

    grid_spec = pltpu.PrefetchScalarGridSpec(
        num_scalar_prefetch=1, grid=(h // tr,),
        in_specs=[pl.BlockSpec((None, tr, c), lambda i, p_ref: (p_ref[0], i, 0)), other(1), other(2), other(3)],
        out_specs=pl.BlockSpec((None, tr, c), lambda i, p_ref: (p_ref[1], i, 0)))
    return pl.pallas_call(body, out_shape=jax.ShapeDtypeStruct((2, h, c), F32), grid_spec=grid_spec,
                          compiler_params=_cp(("arbitrary",)), name=name)(place, own, parts, parts, parts)


HBM_SPEC = pl.BlockSpec(memory_space=pltpu.HBM)


def _place():
    x, y, c = lax.axis_index("x"), lax.axis_index("y"), lax.axis_index("c")
    chips = [(1 - x, y), (x, 1 - y), (1 - x, 1 - y)]
    return x, y, c, 2 * x + y, chips, [2 * px + py for px, py in chips]


def _remote(src, dst, send_sem, recv_sem, dev):
    return pltpu.make_async_remote_copy(src_ref=src, dst_ref=dst, send_sem=send_sem, recv_sem=recv_sem,
                                        device_id=dev, device_id_type=MESH)


def _comm_call(body, arrays, out_shapes, sems, name, aliases=None):
    n = len(arrays)
    return pl.pallas_call(
        body, out_shape=tuple(out_shapes), in_specs=[HBM_SPEC] * n, out_specs=tuple([HBM_SPEC] * len(out_shapes)),
        scratch_shapes=[pltpu.SemaphoreType.DMA(s) for s in sems], input_output_aliases=aliases or {},
        compiler_params=pltpu.CompilerParams(has_side_effects=True), name=name)(*arrays)


class _SideCopies:
    def __init__(self, arrays, out_shapes, sem_shape, sends, recvs):
        self.arrays, self.out_shapes, self.sem_shape = list(arrays), list(out_shapes), sem_shape
        self.sends, self.recvs = sends, recvs

    @property
    def scratch(self):
        return [pltpu.SemaphoreType.DMA(self.sem_shape), pltpu.SemaphoreType.DMA(self.sem_shape)]

    def start(self, ins, outs, send, recv):
        for cp in self.sends(ins, outs, send, recv):
            cp.start()

    def wait(self, ins, outs, send, recv):
        for cp in self.recvs(ins, outs, send, recv):
            cp.wait_recv()
        for cp in self.sends(ins, outs, send, recv):
            cp.wait_send()

    def run(self, name):
        n = len(self.arrays)

        def body(*refs):
            ins, outs = refs[:n], refs[n:n + len(self.out_shapes)]
            send, recv = refs[-2:]
            self.start(ins, outs, send, recv)
            self.wait(ins, outs, send, recv)

        return _comm_call(body, self.arrays, self.out_shapes, [self.sem_shape] * 2, name)


def _gather_over_chips(shards):
    n = len(shards)

    def sends(ins, outs, send, recv):
        x, y, c, me, chips, _ = _place()
        over_ici = [_remote(ins[a].at[c], outs[a].at[me, c], send.at[a, k], recv.at[a, k], (px, py, c))
                    for a in range(n) for k, (px, py) in enumerate(chips)]
        return over_ici + [_remote(ins[a], outs[a].at[me], send.at[a, 3], recv.at[a, 3], (x, y, 1 - c)) for a in range(n)]

    def recvs(ins, outs, send, recv):
        x, y, c, me, _, cidx = _place()
        slots = [[outs[a].at[cidx[k], c] for k in range(3)] + [outs[a].at[me]] for a in range(n)]
        return [_remote(blk, blk, send.at[a, k], recv.at[a, k], (x, y, 1 - c))
                for a in range(n) for k, blk in enumerate(slots[a])]

    out_shapes = [jax.ShapeDtypeStruct((N_CHIPS,) + s.shape, s.dtype) for s in shards]
    return _SideCopies(shards, out_shapes, (n, 4), sends, recvs)


def _gather_finish(partial, name):
    n = len(partial)

    def body(*refs):
        part, outs = refs[:n], refs[n:2 * n]
        send, recv = refs[2 * n:]
        x, y, c, _, _, cidx = _place()
        sib = (x, y, 1 - c)
        cps = [_remote(part[a].at[cidx[k], c], outs[a].at[cidx[k], c], send.at[a, k], recv.at[a, k], sib)
               for a in range(n) for k in range(3)]
        for cp in cps:
            cp.start()
        for a in range(n):
            for k in range(3):
                blk = outs[a].at[cidx[k], 1 - c]
                _remote(blk, blk, send.at[a, k], recv.at[a, k], sib).wait_recv()
        for cp in cps:
            cp.wait_send()

    out_shapes = [jax.ShapeDtypeStruct(p.shape, p.dtype) for p in partial]
    return _comm_call(body, list(partial), out_shapes, [(n, 3), (n, 3)], name, aliases={a: a for a in range(n)})


def _sibling_halves(grads):
    n = len(grads)

    def copies(ins, outs, send, recv):
        x, y, c, _, _, _ = _place()
        return [_remote(ins[a].at[j, 1 - c], outs[a].at[j], send.at[a, j], recv.at[a, j], (x, y, 1 - c))
                for a in range(n) for j in range(N_CHIPS)]

    out_shapes = [jax.ShapeDtypeStruct((g.shape[0],) + g.shape[2:], g.dtype) for g in grads]
    return _SideCopies(grads, out_shapes, (n, N_CHIPS), copies, copies)


def _scatter_over_chips(sums):
    n = len(sums)

    def sends(ins, outs, send, recv):
        _, _, c, me, chips, cidx = _place()
        return [_remote(ins[a].at[cidx[k]], outs[a].at[me], send.at[a, k], recv.at[a, k], (px, py, c))
                for a in range(n) for k, (px, py) in enumerate(chips)]

    def recvs(ins, outs, send, recv):
        x, y, c, _, _, cidx = _place()
        return [_remote(outs[a].at[cidx[k]], outs[a].at[cidx[k]], send.at[a, k], recv.at[a, k], (x, y, 1 - c))
                for a in range(n) for k in range(3)]

    return _SideCopies(sums, [jax.ShapeDtypeStruct(s.shape, s.dtype) for s in sums], (n, 3), sends, recvs)


def _exchange_halves(halves, name):
    n = len(halves)

    def body(*refs):
        ins, bufs = refs[:n], refs[n:2 * n]
        send, recv = refs[2 * n:]
        x, y, c, _, _, _ = _place()
        sib = (x, y, 1 - c)
        cps = []
        for a in range(n):
            cp = _remote(ins[a].at[c], bufs[a].at[c], send.at[a], recv.at[a], sib)
            cp.start()
            cps.append(cp)
        for a in range(n):
            blk = bufs[a].at[1 - c]
            _remote(blk, blk, send.at[a], recv.at[a], sib).wait_recv()
        for cp in cps:
            cp.wait_send()

    out_shapes = [jax.ShapeDtypeStruct(s.shape, s.dtype) for s in halves]
    return _comm_call(body, halves, out_shapes, [(n,), (n,)], name, aliases={a: a for a in range(n)})


def _gather_chips(slices):
    n = len(slices)

    def sends(ins, outs, send, recv):
        x, y, c, me, chips, _ = _place()
        return [_remote(ins[a], outs[a].at[me], send.at[a, k], recv.at[a, k], dev)
                for a in range(n) for k, dev in enumerate([(px, py, c) for px, py in chips] + [(x, y, 1 - c)])]

    def recvs(ins, outs, send, recv):
        x, y, c, me, _, cidx = _place()
        return [_remote(outs[a].at[slot], outs[a].at[slot], send.at[a, k], recv.at[a, k], (x, y, 1 - c))
                for a in range(n) for k, slot in enumerate(cidx + [me])]

    out_shapes = [jax.ShapeDtypeStruct((N_CHIPS,) + s.shape, s.dtype) for s in slices]
    return _SideCopies(slices, out_shapes, (n, 4), sends, recvs)


def _rs_siblings(grads):
    return _sibling_halves([g.reshape(g.shape[0], 2, g.shape[1] // 2, g.shape[2]) for g in grads])


def _rs_chips(siblings, recv, c_idx, wire_dtype, tag):
    wire = wire_dtype if isinstance(wire_dtype, list) else [wire_dtype] * len(recv)
    return _scatter_over_chips([_add_sibling(g, r, c_idx, w, "rs_add_sibling_" + tag)
                                for g, r, w in zip(siblings.arrays, recv, wire)])


def _rs_end(scatter, parts, place, tag):
    halves = [_add_chips(s, p, place, "rs_add_chips_" + tag) for s, p in zip(scatter.arrays, parts)]
    both = _exchange_halves(halves, "rs_halves_" + tag)
    return [b.reshape(b.shape[0] * b.shape[1], b.shape[2]) for b in both]


SMALL_ORDER = ("gm_w_s", "mix_pre_norm", "mix_post_norm", "mix_out_norm", "ffn_pre_norm", "ffn_post_norm", "ple_norm",
               "gm_v_norm", "gm_b_s", "b_forget")
SMALL_ROWS_MULTIPLE = 64
SMALL_BLOCK = 8 * LANES


def _chip_columns(pieces, width):
    out = []
    for j in range(N_CHIPS):
        lo, hi, parts, off = j * width, (j + 1) * width, [], 0
        for piece in pieces:
            a, b = max(lo, off), min(hi, off + piece.shape[1])
            if a < b:
                parts.append(piece[:, a - off:b - off])
            off += piece.shape[1]
        out.append(parts[0] if len(parts) == 1 else jnp.concatenate(parts, axis=1))
    return jnp.stack(out)


def _columns(sliced, lo, hi):
    width, parts = sliced.shape[2], []
    for j in range(N_CHIPS):
        a, b = max(lo, j * width), min(hi, (j + 1) * width)
        if a < b:
            parts.append(sliced[j][:, a - j * width:b - j * width])
    return parts[0] if len(parts) == 1 else jnp.concatenate(parts, axis=1)


def _pack_small(parts):
    blocks = []
    for nme in SMALL_ORDER:
        v = parts[nme].reshape(-1)
        pad = (-v.shape[0]) % SMALL_BLOCK
        blocks.append((jnp.pad(v, (0, pad)) if pad else v).reshape(-1, LANES))
    rows = sum(b.shape[0] for b in blocks)
    if rows % SMALL_ROWS_MULTIPLE:
        blocks.append(jnp.zeros((SMALL_ROWS_MULTIPLE - rows % SMALL_ROWS_MULTIPLE, LANES), F32))
    return jnp.concatenate(blocks, axis=0)


def _unpack_small(packed, shapes):
    out, row = {}, 0
    for nme in SMALL_ORDER:
        size = 1
        for s in shapes[nme]:
            size *= s
        rows = -(-size // SMALL_BLOCK) * (SMALL_BLOCK // LANES)
        block = packed[row:row + rows]
        out[nme] = (block if size % SMALL_BLOCK == 0 else block.reshape(-1)[:size]).reshape(shapes[nme])
        row += rows
    return out


def kernel(x, p, mix_pre_norm, mix_post_norm, w_in, b_forget, gm_v_norm, gm_w_s, gm_b_s, mix_out_norm, w_out, ffn_pre_norm, ffn_post_norm, w_ffn_in, w_ffn_out, w_ple, ple_norm, w_ple_gate, loss_target, m_mix_pre_norm, m_mix_post_norm, m_w_in, m_b_forget, m_gm_v_norm, m_gm_w_s, m_gm_b_s, m_mix_out_norm, m_w_out, m_ffn_pre_norm, m_ffn_post_norm, m_w_ffn_in, m_w_ffn_out, m_w_ple, m_ple_norm, m_w_ple_gate, v_mix_pre_norm, v_mix_post_norm, v_w_in, v_b_forget, v_gm_v_norm, v_gm_w_s, v_gm_b_s, v_mix_out_norm, v_w_out, v_ffn_pre_norm, v_ffn_post_norm, v_w_ffn_in, v_w_ffn_out, v_w_ple, v_ple_norm, v_w_ple_gate):
    weights = dict(mix_pre_norm=mix_pre_norm, mix_post_norm=mix_post_norm, w_in=w_in, b_forget=b_forget,
                   gm_v_norm=gm_v_norm, gm_w_s=gm_w_s, gm_b_s=gm_b_s, mix_out_norm=mix_out_norm, w_out=w_out,
                   ffn_pre_norm=ffn_pre_norm, ffn_post_norm=ffn_post_norm, w_ffn_in=w_ffn_in, w_ffn_out=w_ffn_out,
                   w_ple=w_ple, ple_norm=ple_norm, w_ple_gate=w_ple_gate)
    mom_m = dict(mix_pre_norm=m_mix_pre_norm, mix_post_norm=m_mix_post_norm, w_in=m_w_in, b_forget=m_b_forget,
                 gm_v_norm=m_gm_v_norm, gm_w_s=m_gm_w_s, gm_b_s=m_gm_b_s, mix_out_norm=m_mix_out_norm, w_out=m_w_out,
                 ffn_pre_norm=m_ffn_pre_norm, ffn_post_norm=m_ffn_post_norm, w_ffn_in=m_w_ffn_in,
                 w_ffn_out=m_w_ffn_out, w_ple=m_w_ple, ple_norm=m_ple_norm, w_ple_gate=m_w_ple_gate)
    mom_v = dict(mix_pre_norm=v_mix_pre_norm, mix_post_norm=v_mix_post_norm, w_in=v_w_in, b_forget=v_b_forget,
                 gm_v_norm=v_gm_v_norm, gm_w_s=v_gm_w_s, gm_b_s=v_gm_b_s, mix_out_norm=v_mix_out_norm, w_out=v_w_out,
                 ffn_pre_norm=v_ffn_pre_norm, ffn_post_norm=v_ffn_post_norm, w_ffn_in=v_w_ffn_in,
                 w_ffn_out=v_w_ffn_out, w_ple=v_w_ple, ple_norm=v_ple_norm, w_ple_gate=v_w_ple_gate)
    big = ("w_in", "w_out", "w_ffn_in", "w_ffn_out", "w_ple", "w_ple_gate")
    depth = w_in.shape[0]
    t, d = x.shape[1], x.shape[2]
    d_ff = w_ffn_out.shape[1] * N_CHIPS
    c_idx = lax.axis_index("c").astype(jnp.int32).reshape(1)
    place = jnp.stack([2 * lax.axis_index("x") + lax.axis_index("y"), lax.axis_index("c")]).astype(jnp.int32)
    h = x[0]
    target = loss_target[0]
    mean_mat = _group_mean_matrix()

    def row(a, i):
        return a[i].reshape(1, -1)

    def shards_of(i):
        shards = [weights[nme][i].astype(BF16) for nme in big]
        return [s.reshape(2, s.shape[0] // 2, s.shape[1]) for s in shards]

    saved = []
    hn = _norm_cast(h, row(mix_pre_norm, 0), "norm_first")
    by_cols = lambda g: g.transpose(1, 0, 2).reshape(g.shape[1], N_CHIPS * g.shape[2])
    by_rows = lambda g: g.reshape(N_CHIPS * g.shape[1], g.shape[2])
    whole = lambda g: g.reshape(N_CHIPS, g.shape[2] * 2, g.shape[3])
    gather = _gather_over_chips(shards_of(0)[:1])
    w_in_g = _gather_finish(gather.run("gather_first"), "gather_finish_first")[0]
    for i in range(depth):
        w_in_c = whole(w_in_g)
        w_qkv = _columns(w_in_c, 0, 3 * D_ATT)
        w_gf = jnp.concatenate([_columns(w_in_c, 3 * D_ATT + N_HEADS, N_CHIPS * w_in_c.shape[2]),
                                _columns(w_in_c, 3 * D_ATT, 3 * D_ATT + N_HEADS),
                                jnp.zeros((d, LANES - N_HEADS), BF16)], axis=1)
        b_pad = jnp.pad(b_forget[i], (0, LANES - N_HEADS)).reshape(1, LANES)
        bias_full = jnp.repeat(gm_b_s[i].T, HEAD_DIM, axis=1)
        gain_v = row(gm_v_norm, i)

        qkv = _mm(hn, w_qkv, "nn", BF16, "mm_qkv")
        zgf = _mm(hn, w_gf, "nn", F32, "mm_gf", tn_cap=384)
        qa, ka, va = _attn_prep(qkv, _gate_fwd(zgf, b_pad, "gate_fwd"), "attn_prep")
        last = i + 1 == depth
        gather = _gather_over_chips(shards_of(i)[1:] + ([] if last else shards_of(i + 1)[:1]))
        (att, lse), partial = _attn_fwd(qa, ka, va, "attn_fwd_last" if last else "attn_fwd", gather)
        gathered = _gather_finish(partial, "gather_finish_last" if last else "gather_finish")
        w_out_f, w_fi_f, w_fo_f = by_rows(whole(gathered[0])), by_cols(whole(gathered[1])), by_rows(whole(gathered[2]))
        w_ple_f, w_pg_f = by_cols(whole(gathered[3])), by_rows(whole(gathered[4]))
        w_in_g = None if last else gathered[5]
        gm = _gmlp_fwd(zgf, gm_w_s[i], bias_full, gain_v, mean_mat, "gmlp_fwd")
        mixed = _mixnorm_fwd(att, gm, row(mix_out_norm, i), "mixnorm_fwd")
        o, h1, hn2 = _resid_norm(h, _Prod((mixed, w_out_f, "nn")), row(mix_post_norm, i), row(ffn_pre_norm, i),
                                 "out_resid")
        s, ab = _ffn_in(hn2, w_fi_f, "ffn_in")
        o2, h2, hr = _resid_norm(h1, _Prod((s, w_fo_f, "nn")), row(ffn_post_norm, i), None, "ffn_out_resid")
        g_next = row(mix_pre_norm, i + 1) if i + 1 < depth else row(mix_pre_norm, 0)
        pe, gl, h3, hn_next = _ple_fwd(h2, _Prod((p[i, 0], w_ple_f, "nn")), _Prod((hr, w_pg_f, "nn")),
                                       row(ple_norm, i), g_next, "ple_fwd")
        saved.append(dict(h=h, hn=hn, qa=qa, ka=ka, va=va, zgf=zgf, att=att, lse=lse, gm=gm, mixed=mixed,
                          o=o, h1=h1, hn2=hn2, ab=ab, s=s, o2=o2, h2=h2, hr=hr, pe=pe, gl=gl, w_qkv=w_qkv, w_gf=w_gf,
                          w_out=w_out_f, w_fi=w_fi_f, w_fo=w_fo_f, w_pg=w_pg_f, b_pad=b_pad, bias_full=bias_full,
                          gain_v=gain_v))
        h, hn = h3, hn_next

    dh, loss_blk = _loss_head(h, target, "loss_head")
    loss = lax.psum(loss_blk[0, 0], ("x", "y", "c"))

    small = {nme: [None] * depth for nme in SMALL_ORDER}
    big_grads = {nme: [None] * depth for nme in big}
    waiting = []
    for i in reversed(range(depth)):
        sv = saved[i]
        dgl, dpe, small["ple_norm"][i] = _ple_bwd(dh, sv["pe"], sv["gl"], row(ple_norm, i), "ple_bwd")
        g_pg = _mm(sv["hr"], dgl, "tn", F32, "mm_dw_ple_gate").reshape(N_CHIPS, -1, d)
        g_ple = _mm(p[i, 0], dpe, "tn", F32, "mm_dw_ple", chip_split=True)
        dh2, do2, small["ffn_post_norm"][i] = _join(dh, sv["h2"], None, _Prod((dgl, sv["w_pg"], "nt")), sv["o2"],
                                                    row(ffn_post_norm, i), "join_ple")
        dab = _ffn_bwd(do2, sv["w_fo"], sv["ab"], "ffn_bwd")
        g_fo = _mm(sv["s"], do2, "tn", F32, "mm_dw_ffn_out", tm=256).reshape(N_CHIPS, -1, d)
        g_fi = _mm_halves_tn(sv["hn2"], dab, "mm_dw_ffn_in")
        dh1, do, small["ffn_pre_norm"][i], small["mix_post_norm"][i] = _join(
            dh2, sv["h1"], row(ffn_pre_norm, i), _Prod((dab, sv["w_fi"], "nt", 0), (dab, sv["w_fi"], "nt", 1)), sv["o"],
            row(mix_post_norm, i), "join_ffn", tm=256)
        g_out = _mm(sv["mixed"], do, "tn", F32, "mm_dw_out").reshape(N_CHIPS, -1, d)
        datt, dgm, small["mix_out_norm"][i] = _mixnorm_bwd(sv["att"], sv["gm"], row(mix_out_norm, i),
                                                          _Prod((do, sv["w_out"], "nt")), "mixnorm_bwd")
        batch = waiting + [("w_out", i, g_out), ("w_ffn_in", i, g_fi), ("w_ffn_out", i, g_fo), ("w_ple", i, g_ple),
                           ("w_ple_gate", i, g_pg)]
        tag = "layer" if waiting else "top"
        siblings = _rs_siblings([g for _, _, g in batch])
        (dg, small["gm_w_s"][i], dmix_sum, small["gm_v_norm"][i]), recv = _gmlp_bwd(
            sv["zgf"], dgm, gm_w_s[i], sv["bias_full"], sv["gain_v"], mean_mat, "gmlp_bwd_" + tag, siblings)
        small["gm_b_s"][i] = dmix_sum.reshape(CHUNK, N_HEADS, HEAD_DIM).sum(-1).T
        qb, doa = _attn_bwd_prep(sv["qa"], sv["att"], sv["lse"], datt, "attn_bwd_prep")
        scatter = _rs_chips(siblings, recv, c_idx, BF16, tag)
        (dqa, dka, dva), parts = _attn_bwd(qb, sv["ka"], sv["va"], doa, "attn_bwd_" + tag, scatter)
        for (nme, layer, _), g in zip(batch, _rs_end(scatter, parts, place, tag)):
            big_grads[nme][layer] = g
        dqkv, dc = _attn_bwd_post(dqa, dka, dva, "attn_bwd_post")
        dfl, db = _gate_bwd(dc, sv["zgf"], sv["b_pad"], "gate_bwd")
        small["b_forget"][i] = db[0, :N_HEADS]
        dgf = jnp.concatenate([dg, dfl], axis=1)
        g_qkv = _mm(sv["hn"], dqkv, "tn", F32, "mm_dw_qkv")
        g_gf = _mm(sv["hn"], dgf, "tn", F32, "mm_dw_gf", tn_cap=384)
        g_in = _chip_columns([g_qkv, g_gf[:, 2 * D_GM:2 * D_GM + N_HEADS], g_gf[:, :2 * D_GM]], w_in.shape[2])
        dh, small["mix_pre_norm"][i] = _join(dh1, sv["h"], row(mix_pre_norm, i),
                                             _Prod((dqkv, sv["w_qkv"], "nt"), (dgf, sv["w_gf"], "nt")), None, None, "join_mix")
        waiting = [("w_in", i, g_in)]
    grad_x = dh.reshape(1, t, d)

    small_shapes = {nme: weights[nme].shape for nme in SMALL_ORDER}
    small_part = _pack_small({nme: jnp.stack([g.reshape(small_shapes[nme][1:]) for g in small[nme]])
                              for nme in SMALL_ORDER})
    rows_small = small_part.shape[0]
    siblings = _rs_siblings([waiting[0][2], small_part.reshape(N_CHIPS, rows_small // N_CHIPS, LANES)])
    scatter = _rs_chips(siblings, siblings.run("rs_sibling_tail"), c_idx, [BF16, F32], "tail")
    big_grads["w_in"][0], small_slice = _rs_end(scatter, scatter.run("rs_chips_tail"), place, "tail")
    small_all = _gather_chips([small_slice]).run("gather_small")[0].reshape(1, rows_small, LANES)
    sd, sm, sv_ = _adamw(_pack_small({n_: weights[n_] for n_ in SMALL_ORDER})[None], small_all,
                         _pack_small({n_: mom_m[n_] for n_ in SMALL_ORDER})[None],
                         _pack_small({n_: mom_v[n_] for n_ in SMALL_ORDER})[None], "adamw_small")
    grads = _unpack_small(small_all[0], small_shapes)
    deltas = _unpack_small(sd[0], small_shapes)
    new_m = _unpack_small(sm[0], small_shapes)
    new_v = _unpack_small(sv_[0], small_shapes)

    for nme in big:
        g = jnp.stack(big_grads[nme]).reshape(weights[nme].shape)
        grads[nme] = g
        deltas[nme], new_m[nme], new_v[nme] = _adamw(weights[nme], g, mom_m[nme], mom_v[nme], "adamw_" + nme)

    order = ("mix_pre_norm", "mix_post_norm", "w_in", "b_forget", "gm_v_norm", "gm_w_s", "gm_b_s", "mix_out_norm",
             "w_out", "ffn_pre_norm", "ffn_post_norm", "w_ffn_in", "w_ffn_out", "w_ple", "ple_norm", "w_ple_gate")
    return (loss, grad_x, *[grads[n_] for n_ in order], *[deltas[n_] for n_ in order], *[new_m[n_] for n_ in order],
            *[new_v[n_] for n_ in order])
```
